```python
import math
import jax
import jax.numpy as jnp
from jax import lax
import numpy as np

D_MODEL = 1024
BATCH = 2
SEQ = 8192
DEPTH = 2

GRID_W = 64
CTX_LEN = 256
EPS = 1e-6
LB_FLOOR = 1e-30
F32 = jnp.float32
CHUNK = 64
Q_BLOCK = 128

GDN_HEADS = 4
GDN_DK = 128
GDN_DV = 128
GDN_KEY = GDN_HEADS * GDN_DK
GDN_VAL = GDN_HEADS * GDN_DV
CONV_K = 3

MLA_HEADS = 4
MLA_Q_RANK = 384
MLA_KV_RANK = 256
MLA_NOPE = 128
MLA_ROPE = 64
MLA_V = 128
MLA_SCALE = (MLA_NOPE + MLA_ROPE) ** -0.5
ROPE_BASE = 10000.0
ROPE_NF = MLA_ROPE // 4

HG_HEADS = 4
HG_DK = 128
HG_DV = 128
HG_KEY = HG_HEADS * HG_DK
HG_VAL = HG_HEADS * HG_DV

N_BRANCH = 3
BRANCH_W = GDN_VAL
D_FF = 2816
N_MOD = 9

IN_SIZES = (GDN_KEY, GDN_KEY, GDN_VAL, GDN_VAL, 2 * GDN_HEADS, 2 * GDN_HEADS,
            MLA_Q_RANK, MLA_KV_RANK, MLA_ROPE,
            HG_KEY, 2 * HG_KEY, HG_VAL, HG_VAL,
            N_BRANCH * D_MODEL)
IN_COLS = sum(IN_SIZES)

kernel_name = 'hybrid_gdn_mla_hgrn2_diffusion_block'


def rms_norm(x, w):
    xf = x.astype(F32)
    y = xf * lax.rsqrt(jnp.mean(xf * xf, axis=-1, keepdims=True) + EPS)
    return (y * w.astype(F32)).astype(x.dtype)


def l2_normalize(x):
    return x * lax.rsqrt(jnp.sum(x * x, axis=-1, keepdims=True) + EPS)


def modulate(h, shift, scale):
    return h * (1 + scale) + shift


def split_heads(t, n_heads):
    b, s, _ = t.shape
    return t.reshape(b, s, n_heads, -1).transpose(0, 2, 1, 3)


def merge_heads(t):
    b, h, s, d = t.shape
    return t.transpose(0, 2, 1, 3).reshape(b, s, h * d)


def swiglu(h, w_i, w_o):
    gate, up = jnp.split(h @ w_i, 2, axis=-1)
    return (jax.nn.silu(gate) * up) @ w_o


def ffn_sublayer(x, mod, pre_w, post_w, w_i, w_o):
    shift, scale, gate = mod
    h = modulate(rms_norm(x, pre_w), shift, scale)
    return x + 0.5 * gate * rms_norm(swiglu(h, w_i, w_o), post_w)


def short_conv(x, w):
    pad = CONV_K // 2
    t = x.shape[1]
    xp = jnp.pad(x, ((0, 0), (pad, pad), (0, 0)))
    return jax.nn.silu(sum(xp[:, j:j + t] * w[j] for j in range(CONV_K)))


def to_chunks(t):
    b, h, n = t.shape[:3]
    t = t.reshape(b, h, n // CHUNK, CHUNK, *t.shape[3:])
    return jnp.moveaxis(t, 2, 0)


def from_chunks(t):
    t = jnp.moveaxis(t, 0, 2)
    return t.reshape(t.shape[0], t.shape[1], -1, *t.shape[4:])


def masked_decay(diff, mask):
    return jnp.where(mask, jnp.exp(jnp.where(mask, diff, 0.0)), 0.0)


def gdn_chunk_scan(s0, q, k, v, g, beta):
    incl = jnp.tril(jnp.ones((CHUNK, CHUNK), bool))
    strict = jnp.tril(jnp.ones((CHUNK, CHUNK), bool), -1)
    eye = jnp.eye(CHUNK, dtype=F32)

    def step(s, inp):
        qc, kc, vc, gc, bc = inp
        gcum = jnp.cumsum(gc, axis=-1)
        dmask = masked_decay(gcum[..., :, None] - gcum[..., None, :], incl)
        kb = kc * bc[..., None]
        a = jnp.where(strict, jnp.einsum('bhid,bhjd->bhij', kb, kc) * dmask, 0.0)
        m = eye + a
        u = lax.linalg.triangular_solve(m, vc * bc[..., None], left_side=True, lower=True, unit_diagonal=True)
        w = lax.linalg.triangular_solve(m, kb * jnp.exp(gcum)[..., None], left_side=True, lower=True, unit_diagonal=True)
        v_new = u - jnp.einsum('bhck,bhkv->bhcv', w, s)
        attn = jnp.einsum('bhid,bhjd->bhij', qc, kc) * dmask
        o = (jnp.einsum('bhck,bhkv->bhcv', qc * jnp.exp(gcum)[..., None], s)
             + jnp.einsum('bhij,bhjv->bhiv', attn, v_new))
        g_last = gcum[..., -1:]
        s = (s * jnp.exp(g_last)[..., None]
             + jnp.einsum('bhck,bhcv->bhkv', kc * jnp.exp(g_last - gcum)[..., None], v_new))
        return s, o

    s, o = lax.scan(step, s0, tuple(to_chunks(t) for t in (q, k, v, g, beta)))
    return from_chunks(o), s


def gla_chunk_scan(s0, q, k, v, g):
    incl = jnp.tril(jnp.ones((CHUNK, CHUNK), bool))[:, :, None]

    def step(s, inp):
        qc, kc, vc, gc = inp
        gcum = jnp.cumsum(gc, axis=-2)
        rel = masked_decay(gcum[..., :, None, :] - gcum[..., None, :, :], incl)
        attn = jnp.einsum('bhik,bhjk,bhijk->bhij', qc, kc, rel)
        o = (jnp.einsum('bhck,bhkv->bhcv', qc * jnp.exp(gcum), s)
             + jnp.einsum('bhij,bhjv->bhiv', attn, vc))
        g_last = gcum[..., -1:, :]
        s = (s * jnp.exp(g_last)[..., 0, :, None]
             + jnp.einsum('bhck,bhcv->bhkv', kc * jnp.exp(g_last - gcum), vc))
        return s, o

    s, o = lax.scan(step, s0, tuple(to_chunks(t) for t in (q, k, v, g)))
    return from_chunks(o), s


def bidirectional_scan(scan_fn, s0, ctx_fwd, lat_fwd, ctx_bwd, lat_bwd):
    flip = lambda t: jnp.flip(t, axis=2)
    oc_f, sc_f = scan_fn(s0, *ctx_fwd)
    ol_f, _ = scan_fn(sc_f, *lat_fwd)
    oc_b, sc_b = scan_fn(s0, *[flip(t) for t in ctx_bwd])
    ol_b, _ = scan_fn(sc_b, *[flip(t) for t in lat_bwd])
    return oc_f + flip(oc_b), ol_f + flip(ol_b)


def gdn_branch(p_ctx, p_lat, conv_w, a_log, dt_bias, norm_w):
    def prep(q, k, v, a, b):
        bsz, t = q.shape[:2]
        qkv = short_conv(jnp.concatenate([q, k, v], axis=-1), conv_w).astype(F32)
        q, k, v = jnp.split(qkv, [GDN_KEY, 2 * GDN_KEY], axis=-1)
        q = l2_normalize(split_heads(q, GDN_HEADS)) * GDN_DK ** -0.5
        k = l2_normalize(split_heads(k, GDN_HEADS))
        v = split_heads(v, GDN_HEADS)
        a = a.astype(F32).reshape(bsz, t, 2, GDN_HEADS)
        b = b.astype(F32).reshape(bsz, t, 2, GDN_HEADS)
        g = -jnp.exp(a_log.astype(F32)) * jax.nn.softplus(a + dt_bias.astype(F32))
        beta = jax.nn.sigmoid(b)
        g = jnp.transpose(g, (2, 0, 3, 1))
        beta = jnp.transpose(beta, (2, 0, 3, 1))
        return (q, k, v, g[0], beta[0]), (q, k, v, g[1], beta[1])

    c_fwd, c_bwd = prep(p_ctx[0], p_ctx[1], p_ctx[2], p_ctx[4], p_ctx[5])
    l_fwd, l_bwd = prep(p_lat[0], p_lat[1], p_lat[2], p_lat[4], p_lat[5])
    s0 = jnp.zeros((p_lat[0].shape[0], GDN_HEADS, GDN_DK, GDN_DV), F32)
    oc, ol = bidirectional_scan(gdn_chunk_scan, s0, c_fwd, l_fwd, c_bwd, l_bwd)

    def readout(o, gate):
        y = rms_norm(o, norm_w) * jax.nn.silu(split_heads(gate, GDN_HEADS).astype(F32))
        return merge_heads(y).astype(gate.dtype)

    return readout(oc, p_ctx[3]), readout(ol, p_lat[3])


def axial_rope_tables(n):
    rows = n // GRID_W
    row = jnp.repeat(jnp.arange(rows, dtype=F32), GRID_W)
    col = jnp.tile(jnp.arange(GRID_W, dtype=F32), rows)
    inv = ROPE_BASE ** (-jnp.arange(ROPE_NF, dtype=F32) / ROPE_NF)
    ang = jnp.stack([row[:, None] * inv, col[:, None] * inv], axis=1)
    return jnp.cos(ang), jnp.sin(ang)


def apply_axial_rope(x, cos, sin):
    xs = x.astype(F32).reshape(*x.shape[:-1], 2, 2, ROPE_NF)
    x1, x2 = xs[..., 0, :], xs[..., 1, :]
    c, s = cos[:, None], sin[:, None]
    out = jnp.stack([x1 * c - x2 * s, x2 * c + x1 * s], axis=-2)
    return out.reshape(x.shape).astype(x.dtype)


def softmax_attention(q, k, v):
    s = jnp.einsum('bqhd,bkhd->bhqk', q, k).astype(F32) * MLA_SCALE
    p = jax.nn.softmax(s, axis=-1).astype(v.dtype)
    o = jnp.einsum('bhqk,bkhd->bqhd', p, v)
    return o.reshape(o.shape[0], o.shape[1], -1)


def mla_branch(p_ctx, p_lat, q_norm_w, kv_norm_w, w_q_b, w_kv_b, cos, sin):
    def qkv(qa, kva, kr, rotate):
        bsz, t = qa.shape[:2]
        q = (rms_norm(qa, q_norm_w) @ w_q_b).reshape(bsz, t, MLA_HEADS, MLA_NOPE + MLA_ROPE)
        kv = (rms_norm(kva, kv_norm_w) @ w_kv_b).reshape(bsz, t, MLA_HEADS, MLA_NOPE + MLA_V)
        q_nope, q_rope = q[..., :MLA_NOPE], q[..., MLA_NOPE:]
        k_nope, v = kv[..., :MLA_NOPE], kv[..., MLA_NOPE:]
        k_rope = kr[:, :, None, :]
        if rotate:
            q_rope = apply_axial_rope(q_rope, cos, sin)
            k_rope = apply_axial_rope(k_rope, cos, sin)
        q = jnp.concatenate([q_nope, q_rope], axis=-1)
        k = jnp.concatenate([k_nope, jnp.broadcast_to(k_rope, (bsz, t, MLA_HEADS, MLA_ROPE))], axis=-1)
        return q, k, v

    qc, kc, vc = qkv(*p_ctx, False)
    ql, kl, vl = qkv(*p_lat, True)
    y_ctx = softmax_attention(qc, kc, vc)
    k_all = jnp.concatenate([kc, kl], axis=1)
    v_all = jnp.concatenate([vc, vl], axis=1)
    bsz, n = ql.shape[:2]
    qb = jnp.moveaxis(ql.reshape(bsz, n // Q_BLOCK, Q_BLOCK, MLA_HEADS, -1), 1, 0)
    ob = lax.map(lambda qblk: softmax_attention(qblk, k_all, v_all), qb)
    y_lat = jnp.moveaxis(ob, 0, 1).reshape(bsz, n, MLA_HEADS * MLA_V)
    return y_ctx, y_lat


def hgrn2_branch(p_ctx, p_lat, lb, norm_w):
    lb = lb.astype(F32)
    log_lb = jnp.log(jnp.maximum(lb, LB_FLOOR))
    log_1m_lb = jnp.log1p(-lb)

    def prep(q, f, i):
        bsz, t = q.shape[:2]
        f = f.astype(F32).reshape(bsz, t, 2, HG_KEY)
        log_f = jnp.logaddexp(log_lb, log_1m_lb + jax.nn.log_sigmoid(f))
        k = (1 - lb) * jax.nn.sigmoid(-f)
        q = split_heads(q.astype(F32), HG_HEADS) * HG_DK ** -0.5
        v = split_heads(i.astype(F32), HG_HEADS)
        return tuple((q, split_heads(k[:, :, d], HG_HEADS), v, split_heads(log_f[:, :, d], HG_HEADS))
                     for d in range(2))

    c_fwd, c_bwd = prep(p_ctx[0], p_ctx[1], p_ctx[2])
    l_fwd, l_bwd = prep(p_lat[0], p_lat[1], p_lat[2])
    s0 = jnp.zeros((p_lat[0].shape[0], HG_HEADS, HG_DK, HG_DV), F32)
    oc, ol = bidirectional_scan(gla_chunk_scan, s0, c_fwd, l_fwd, c_bwd, l_bwd)

    def readout(o, gate):
        y = rms_norm(o, norm_w) * jax.nn.silu(split_heads(gate, HG_HEADS).astype(F32))
        return merge_heads(y).astype(gate.dtype)

    return readout(oc, p_ctx[3]), readout(ol, p_lat[3])


def merge_branches(ys, gate_logits, w_branch, w_out):
    bsz, t = gate_logits.shape[:2]
    g = jax.nn.sigmoid(gate_logits.reshape(bsz, t, N_BRANCH, D_MODEL))
    m = sum(g[:, :, j] * (ys[j] @ w_branch[j]) for j in range(N_BRANCH))
    return m @ w_out


def setup_inputs(seed: int = 0) -> dict:
    key = jax.random.key(seed)
    ks = jax.random.split(key, 24)

    def dense(k, shape, fan_in):
        return jax.random.normal(k, shape, F32) * fan_in ** -0.5

    def gain(k, shape):
        return 1.0 + 0.02 * jax.random.normal(k, shape, F32)

    dt = jnp.exp(jax.random.uniform(ks[9], (DEPTH, 2, GDN_HEADS), F32, math.log(1e-3), math.log(1e-1)))
    return {
        'x': jax.random.normal(ks[0], (BATCH, SEQ, D_MODEL), F32),
        'c': jax.random.normal(ks[1], (BATCH, D_MODEL), F32),
        'ctx': jax.random.normal(ks[2], (BATCH, CTX_LEN, D_MODEL), F32),
        'c_ctx': jax.random.normal(ks[3], (D_MODEL,), F32),
        'w_ada': dense(ks[4], (DEPTH, D_MODEL, N_MOD * D_MODEL), D_MODEL),
        'b_ada': 0.01 * jax.random.normal(ks[5], (DEPTH, N_MOD * D_MODEL), F32),
        'norm_w': gain(ks[6], (DEPTH, 6, D_MODEL)),
        'ffn_w_in': dense(ks[7], (DEPTH, 2, D_MODEL, 2 * D_FF), D_MODEL),
        'ffn_w_out': dense(ks[8], (DEPTH, 2, D_FF, D_MODEL), D_FF),
        'w_in': dense(ks[10], (DEPTH, D_MODEL, IN_COLS), D_MODEL),
        'gdn_conv': dense(ks[11], (DEPTH, CONV_K, 2 * GDN_KEY + GDN_VAL), CONV_K),
        'gdn_a_log': jnp.log(jax.random.uniform(ks[12], (DEPTH, 2, GDN_HEADS), F32, 1.0, 16.0)),
        'gdn_dt_bias': dt + jnp.log(-jnp.expm1(-dt)),
        'gdn_norm': gain(ks[13], (DEPTH, GDN_DV)),
        'mla_q_norm': gain(ks[14], (DEPTH, MLA_Q_RANK)),
        'mla_kv_norm': gain(ks[15], (DEPTH, MLA_KV_RANK)),
        'mla_w_q_b': dense(ks[16], (DEPTH, MLA_Q_RANK, MLA_HEADS * (MLA_NOPE + MLA_ROPE)), MLA_Q_RANK),
        'mla_w_kv_b': dense(ks[17], (DEPTH, MLA_KV_RANK, MLA_HEADS * (MLA_NOPE + MLA_V)), MLA_KV_RANK),
        'hg_lb_logits': 0.1 * jax.random.normal(ks[18], (DEPTH, 2, HG_KEY), F32),
        'hg_norm': gain(ks[19], (DEPTH, HG_DV)),
        'w_branch': dense(ks[20], (DEPTH, N_BRANCH, BRANCH_W, D_MODEL), BRANCH_W),
        'w_out': dense(ks[21], (DEPTH, D_MODEL, D_MODEL), D_MODEL),
    }


def reference(x, c, ctx, c_ctx, w_ada, b_ada, norm_w, ffn_w_in, ffn_w_out, w_in,
              gdn_conv, gdn_a_log, gdn_dt_bias, gdn_norm,
              mla_q_norm, mla_kv_norm, mla_w_q_b, mla_w_kv_b,
              hg_lb_logits, hg_norm, w_branch, w_out):
    n = x.shape[1]
    cos, sin = axial_rope_tables(n)
    sm = jax.nn.softmax(hg_lb_logits.astype(F32), axis=0)
    lower_bounds = jnp.cumsum(sm, axis=0) - sm[0:1]
    split_pts = np.cumsum(IN_SIZES)[:-1].tolist()
    s_lat = jax.nn.silu(c)
    s_ctx = jax.nn.silu(c_ctx)
    h_lat, h_ctx = x, ctx
    for l in range(DEPTH):
        last = l == DEPTH - 1
        m_lat = jnp.split((s_lat @ w_ada[l] + b_ada[l])[:, None, :], N_MOD, axis=-1)
        m_ctx = jnp.split((s_ctx @ w_ada[l] + b_ada[l])[None, None, :], N_MOD, axis=-1)
        nw = norm_w[l]
        h_lat = ffn_sublayer(h_lat, m_lat[0:3], nw[0], nw[1], ffn_w_in[l, 0], ffn_w_out[l, 0])
        h_ctx = ffn_sublayer(h_ctx, m_ctx[0:3], nw[0], nw[1], ffn_w_in[l, 0], ffn_w_out[l, 0])
        u_lat = modulate(rms_norm(h_lat, nw[2]), m_lat[3], m_lat[4])
        u_ctx = modulate(rms_norm(h_ctx, nw[2]), m_ctx[3], m_ctx[4])
        p_lat = jnp.split(u_lat @ w_in[l], split_pts, axis=-1)
        p_ctx = jnp.split(u_ctx @ w_in[l], split_pts, axis=-1)
        ya_c, ya_l = gdn_branch(p_ctx[0:6], p_lat[0:6], gdn_conv[l], gdn_a_log[l], gdn_dt_bias[l], gdn_norm[l])
        yb_c, yb_l = mla_branch(p_ctx[6:9], p_lat[6:9], mla_q_norm[l], mla_kv_norm[l],
                                mla_w_q_b[l], mla_w_kv_b[l], cos, sin)
        yc_c, yc_l = hgrn2_branch(p_ctx[9:13], p_lat[9:13], lower_bounds[l], hg_norm[l])
        y_lat = merge_branches((ya_l, yb_l, yc_l), p_lat[13], w_branch[l], w_out[l])
        h_lat = h_lat + m_lat[5] * rms_norm(y_lat, nw[3])
        h_lat = ffn_sublayer(h_lat, m_lat[6:9], nw[4], nw[5], ffn_w_in[l, 1], ffn_w_out[l, 1])
        if not last:
            y_ctx = merge_branches((ya_c, yb_c, yc_c), p_ctx[13], w_branch[l], w_out[l])
            h_ctx = h_ctx + m_ctx[5] * rms_norm(y_ctx, nw[3])
            h_ctx = ffn_sublayer(h_ctx, m_ctx[6:9], nw[4], nw[5], ffn_w_in[l, 1], ffn_w_out[l, 1])
    return h_lat
```

```python
import functools

import jax
import jax.numpy as jnp
from jax import lax
from jax.experimental import pallas as pl
from jax.experimental.pallas import tpu as pltpu

F32 = jnp.float32
BF16 = jnp.bfloat16
EPS = 1e-6
LB_FLOOR = 1e-30
GRID_W = 64
ROPE_BASE = 10000.0

D_FF = 2816
HEADS = 4
HD = 128
ROPE = 64
NOPE = 128
Q_RANK = 384
KV_RANK = 256
CHUNK = 64
BW = HEADS * HD

TM = 512
TM_S = 256
FF_CK = 256
VMEM_LIMIT = 48 * 1024 * 1024

P_GATES = 0
P_GQKV = 3072
P_GGATE = 4608
P_HF = 5120
P_HQ = 6144
P_HI = 6656
P_HGATE = 7168
P_QA = 7680
P_GB = 8064
P_KVA = 8192
P_KR = 8448
P_GA = 8576
P_COLS = 8704

NT = (((1,), (1,)), ((), ()))
TN = (((0,), (0,)), ((), ()))


def _dot(a, b):
    return jnp.dot(a, b, preferred_element_type=F32)


def _dot_nt(a, b):
    return lax.dot_general(a, b, NT, preferred_element_type=F32)


def _dot_tn(a, b):
    return lax.dot_general(a, b, TN, preferred_element_type=F32)


def _sigmoid(x):
    return 1.0 / (1.0 + jnp.exp(-x))


def _silu(x):
    return x * _sigmoid(x)


def _softplus(x):
    return jnp.maximum(x, 0.0) + jnp.log(1.0 + jnp.exp(-jnp.abs(x)))


def _rms(x, w):
    return x * lax.rsqrt(jnp.mean(x * x, axis=-1, keepdims=True) + EPS) * w


def _resident(shape):
    zeros = (0,) * len(shape)
    return pl.BlockSpec(shape, lambda *_: zeros, pipeline_mode=pl.Buffered(1))


def _params(*sem):
    return pltpu.CompilerParams(dimension_semantics=sem, vmem_limit_bytes=VMEM_LIMIT)


def _ada_kernel(c_ref, w_ref, b_ref, o_ref):
    s = _silu(c_ref[...])
    o_ref[...] = _dot(s.astype(BF16), w_ref[...].astype(BF16)) + b_ref[...]


def _ada(cc, w_ada, b_ada):
    depth, d, nm = w_ada.shape
    tn = 1024
    return pl.pallas_call(
        _ada_kernel,
        grid=(depth, nm // tn),
        in_specs=[pl.BlockSpec((8, d), lambda l, j: (0, 0)),
                  pl.BlockSpec((None, d, tn), lambda l, j: (l, 0, j)),
                  pl.BlockSpec((None, 1, tn), lambda l, j: (l, 0, j))],
        out_specs=pl.BlockSpec((None, 8, tn), lambda l, j: (l, 0, j)),
        out_shape=jax.ShapeDtypeStruct((depth, 8, nm), F32),
        compiler_params=_params("parallel", "parallel"),
        name="ada_mod",
    )(cc, w_ada, b_ada.reshape(depth, 1, nm))


def _ffn_kernel(x_ref, sh_ref, sc_ref, gt_ref, prew_ref, postw_ref, wi_ref, wo_ref, *rest, emit_u):
    if emit_u:
        nw_ref, sh2_ref, sc2_ref, o_ref, u_ref = rest
    else:
        (o_ref,) = rest
    x = x_ref[...]
    hn = (_rms(x, prew_ref[...]) * (1.0 + sc_ref[...]) + sh_ref[...]).astype(BF16)
    acc = jnp.zeros(x.shape, F32)
    for c in range(D_FF // FF_CK):
        g = _dot(hn, wi_ref[:, c * FF_CK:(c + 1) * FF_CK])
        u = _dot(hn, wi_ref[:, D_FF + c * FF_CK:D_FF + (c + 1) * FF_CK])
        a = (_silu(g) * u).astype(BF16)
        acc = acc + _dot(a, wo_ref[c * FF_CK:(c + 1) * FF_CK, :])
    out = x + 0.5 * gt_ref[...] * _rms(acc, postw_ref[...])
    o_ref[...] = out
    if emit_u:
        u_ref[...] = (_rms(out, nw_ref[...]) * (1.0 + sc2_ref[...]) + sh2_ref[...]).astype(BF16)


def _ffn(h, mods, nw, wi, wo, n_rows, n_seq, nb, mixer_mods=None):
    d = h.shape[1]
    emit_u = mixer_mods is not None
    row = lambda i: (i, 0)
    mod = lambda i: (jnp.minimum((i * TM) // n_seq, nb), 0, 0)
    vec = pl.BlockSpec((1, d), lambda i: (0, 0))
    mspec = pl.BlockSpec((None, 1, d), mod)
    in_specs = [pl.BlockSpec((TM, d), row), mspec, mspec, mspec, vec, vec,
                _resident(wi.shape), _resident(wo.shape)]
    args = [h, mods[0], mods[1], mods[2], nw[0], nw[1], wi, wo]
    out_specs = pl.BlockSpec((TM, d), row)
    out_shape = jax.ShapeDtypeStruct((n_rows, d), F32)
    if emit_u:
        in_specs += [vec, mspec, mspec]
        args += list(mixer_mods)
        out_specs = (out_specs, pl.BlockSpec((TM, d), row))
        out_shape = (out_shape, jax.ShapeDtypeStruct((n_rows, d), BF16))
    return pl.pallas_call(
        functools.partial(_ffn_kernel, emit_u=emit_u),
        grid=(n_rows // TM,),
        in_specs=in_specs, out_specs=out_specs, out_shape=out_shape,
        compiler_params=_params("parallel"),
        name="ffn_sublayer",
    )(*args)


def _proj_kernel(x_ref, w_ref, o_ref):
    o_ref[...] = _dot(x_ref[...], w_ref[...])


def _proj(u, w):
    t, d = u.shape
    n = w.shape[1]
    tn = 512
    return pl.pallas_call(
        _proj_kernel,
        grid=(n // tn, t // TM),
        in_specs=[pl.BlockSpec((TM, d), lambda j, i: (i, 0)),
                  pl.BlockSpec((d, tn), lambda j, i: (0, j))],
        out_specs=pl.BlockSpec((TM, tn), lambda j, i: (i, j)),
        out_shape=jax.ShapeDtypeStruct((t, n), F32),
        compiler_params=_params("parallel", "parallel"),
        name="in_proj",
    )(u, w)


def _mla_prep_kernel(qa_ref, kva_ref, kr_ref, cs_ref, qn_ref, kvn_ref, wq_ref, wkv_ref,
                     q_ref, k_ref, v_ref):
    cs = cs_ref[...]
    qa = _rms(qa_ref[:, :Q_RANK], qn_ref[...]).astype(BF16)
    kva = _rms(kva_ref[...], kvn_ref[...]).astype(BF16)
    q2 = _dot(qa, wq_ref[...])
    kv2 = _dot(kva, wkv_ref[...])
    kr = kr_ref[...] * cs
    kr = (kr + pltpu.roll(kr, ROPE, 1))[:, :ROPE].astype(BF16)
    for h in range(HEADS):
        q_ref[h, :, :NOPE] = q2[:, h * NOPE:(h + 1) * NOPE].astype(BF16)
        qr = q2[:, BW + h * 128:BW + (h + 1) * 128] * cs
        q_ref[h, :, NOPE:] = (qr + pltpu.roll(qr, ROPE, 1))[:, :ROPE].astype(BF16)
        k_ref[h, :, :NOPE] = kv2[:, h * NOPE:(h + 1) * NOPE].astype(BF16)
        k_ref[h, :, NOPE:] = kr
        v_ref[h] = kv2[:, BW + h * HD:BW + (h + 1) * HD].astype(BF16)


def _mla_prep(p, cs, qn, kvn, wq, wkv, nb, n_seq, n_ctx):
    t = p.shape[0]
    tm = TM_S
    lat_tiles = nb * n_seq // tm
    per_lat = n_seq // tm
    per_ctx = n_ctx // tm

    def omap(i):
        j = i - lat_tiles
        b = jnp.where(i < lat_tiles, i // per_lat, j // per_ctx)
        blk = jnp.where(i < lat_tiles, i % per_lat, per_lat + j % per_ctx)
        return (b, 0, blk, 0)

    tk = n_seq + n_ctx
    return pl.pallas_call(
        _mla_prep_kernel,
        grid=(t // tm,),
        in_specs=[pl.BlockSpec((tm, 512), lambda i: (i, P_QA // 512)),
                  pl.BlockSpec((tm, KV_RANK), lambda i: (i, P_KVA // KV_RANK)),
                  pl.BlockSpec((tm, 128), lambda i: (i, P_KR // 128)),
                  pl.BlockSpec((tm, 128), lambda i: (i, 0)),
                  pl.BlockSpec((1, Q_RANK), lambda i: (0, 0)),
                  pl.BlockSpec((1, KV_RANK), lambda i: (0, 0)),
                  _resident(wq.shape), _resident(wkv.shape)],
        out_specs=(pl.BlockSpec((None, HEADS, tm, NOPE + ROPE), omap),
                   pl.BlockSpec((None, HEADS, tm, NOPE + ROPE), omap),
                   pl.BlockSpec((None, HEADS, tm, HD), omap)),
        out_shape=(jax.ShapeDtypeStruct((nb, HEADS, tk, NOPE + ROPE), BF16),
                   jax.ShapeDtypeStruct((nb, HEADS, tk, NOPE + ROPE), BF16),
                   jax.ShapeDtypeStruct((nb, HEADS, tk, HD), BF16)),
        compiler_params=_params("parallel"),
        name="mla_prep",
    )(p, p, p, cs, qn, kvn, wq, wkv)


def _attn_kernel(q_ref, k_ref, v_ref, *rest, kb, scale, aliased):
    o_ref = rest[-1]
    q = q_ref[...]
    tq = q.shape[0]
    nkb = k_ref.shape[0] // kb

    def body(j, carry):
        m, l, acc = carry
        start = pl.multiple_of(j * kb, kb)
        k = k_ref[pl.ds(start, kb), :]
        v = v_ref[pl.ds(start, kb), :]
        s = _dot_nt(q, k) * scale
        m_new = jnp.maximum(m, jnp.max(s, axis=-1, keepdims=True))
        alpha = jnp.exp(m - m_new)
        pr = jnp.exp(s - m_new)
        l = alpha * l + jnp.sum(pr, axis=-1, keepdims=True)
        acc = alpha * acc + _dot(pr.astype(BF16), v)
        return m_new, l, acc

    init = (jnp.full((tq, 1), -1e30, F32), jnp.zeros((tq, 1), F32), jnp.zeros((tq, HD), F32))
    _, l, acc = lax.fori_loop(0, nkb, body, init)
    o_ref[...] = acc / l


def _pick_block(n, cands):
    for c in cands:
        if n % c == 0:
            return c
    raise ValueError(f"no block size for {n}")


def _attention(q, k, v, o_prev, nb, n_seq, n_ctx, rows, ctx_only):
    scale = (NOPE + ROPE) ** -0.5
    if ctx_only:
        tq, tk = n_ctx, n_ctx
        koff = n_seq // n_ctx
        grid = (nb, HEADS, 1)
        qmap = lambda b, h, i: (b, h, koff, 0)
        kmap = lambda b, h, i: (b, h, koff, 0)
        omap = lambda b, h, i: (nb * n_seq // n_ctx + b, h)
    else:
        tq, tk = _pick_block(n_seq, (512, 256)), n_seq + n_ctx
        grid = (nb, HEADS, n_seq // tq)
        qmap = lambda b, h, i: (b, h, i, 0)
        kmap = lambda b, h, i: (b, h, 0, 0)
        omap = lambda b, h, i: (b * (n_seq // tq) + i, h)
    kb = _pick_block(tk, (768, 512, 256))
    in_specs = [pl.BlockSpec((None, None, tq, NOPE + ROPE), qmap),
                pl.BlockSpec((None, None, tk, NOPE + ROPE), kmap),
                pl.BlockSpec((None, None, tk, HD), kmap)]
    args = [q, k, v]
    aliases = {}
    if ctx_only:
        in_specs.append(pl.BlockSpec(memory_space=pl.ANY))
        args.append(o_prev)
        aliases = {3: 0}
    return pl.pallas_call(
        functools.partial(_attn_kernel, kb=kb, scale=scale, aliased=ctx_only),
        grid=grid,
        in_specs=in_specs,
        out_specs=pl.BlockSpec((tq, HD), omap),
        out_shape=jax.ShapeDtypeStruct((rows, BW), F32),
        input_output_aliases=aliases,
        compiler_params=_params("parallel", "parallel", "parallel"),
        name="mla_attn_ctx" if ctx_only else "mla_attn",
    )(*args)


def _chunk_pos(g, lat_chunks, ncl, ncc):
    is_lat = g < lat_chunks
    pos = jnp.where(is_lat, g % ncl, (g - lat_chunks) % ncc)
    last = jnp.where(is_lat, pos == ncl - 1, pos == ncc - 1)
    return pos == 0, last


def _scan_chunk(b, s, rev, lat_chunks, ncl, ncc):
    c = jnp.where(s < ncc, s, s - ncc)
    if rev:
        c = jnp.where(s < ncc, ncc - 1 - c, ncl - 1 - c)
    return jnp.where(s < ncc, lat_chunks + b * ncc + c, b * ncl + c)


def _tri_masks(n):
    ri = lax.broadcasted_iota(jnp.int32, (n, n), 0)
    ci = lax.broadcasted_iota(jnp.int32, (n, n), 1)
    return ri, ci


def _split(x):
    hi = x.astype(BF16)
    return hi, (x - hi.astype(F32)).astype(BF16)


def _dot3(a, b):
    ah, al = _split(a)
    bh, bl = _split(b)
    return _dot(ah, bh) + (_dot(ah, bl) + _dot(al, bh))


def _dot_tri(tri, x):
    hi, lo = _split(x)
    lo2 = (x - hi.astype(F32) - lo.astype(F32)).astype(BF16)
    return _dot(tri, hi) + (_dot(tri, lo) + _dot(tri, lo2))


def _unit_tri_inv(a, ri, ci, eye):
    rb, cb = ri >> 3, ci >> 3
    d8 = jnp.where(rb == cb, a, 0.0)
    x2 = _dot3(d8, d8)
    x4 = _dot3(x2, x2)
    t = eye - d8
    t = t + _dot3(t, x2)
    t = t + _dot3(t, x4)
    for _ in range(3):
        same = rb == cb
        rb, cb = rb >> 1, cb >> 1
        b = jnp.where((rb == cb) & jnp.logical_not(same), a, 0.0).astype(BF16)
        tb = t.astype(BF16)
        t = t - _dot(_dot(tb, b).astype(BF16), tb)
    return t


def _gdn_pre_kernel(x_ref, prev_ref, next_ref, a_ref, b_ref, cw_ref, alog_ref, dtb_ref,
                    u_ref, w_ref, qg_ref, kd_ref, at_ref, dec_ref, *, lat_chunks, ncl, ncc):
    first, last = _chunk_pos(pl.program_id(0), lat_chunks, ncl, ncc)
    x = x_ref[...]
    xp = jnp.where(first, 0.0, prev_ref[7:8, :])
    xn = jnp.where(last, 0.0, next_ref[0:1, :])
    row = lax.broadcasted_iota(jnp.int32, (CHUNK, 1), 0)
    x_dn = jnp.where(row == 0, xp, pltpu.roll(x, 1, 0))
    x_up = jnp.where(row == CHUNK - 1, xn, pltpu.roll(x, CHUNK - 1, 0))
    cw = cw_ref[...]
    s = _silu(x_dn * cw[0:1] + x * cw[1:2] + x_up * cw[2:3])

    lane = lax.broadcasted_iota(jnp.int32, (1, 128), 1)
    g = -jnp.exp(alog_ref[...]) * _softplus(a_ref[...] + dtb_ref[...])
    beta = _sigmoid(b_ref[...])
    ri, ci = _tri_masks(CHUNK)
    eye_b = ri == ci
    eye = eye_b.astype(F32)
    low = (ri >= ci).astype(BF16)
    upp = (ri <= ci).astype(BF16)
    gc = jnp.where(lane < HEADS, _dot_tri(low, g), _dot_tri(upp, g))
    g_last = jnp.where(lane < HEADS, gc[CHUNK - 1:CHUNK], gc[0:1])
    dec_ref[...] = jnp.exp(g_last)

    for h in range(HEADS):
        sl = slice(h * HD, (h + 1) * HD)
        qh = s[:, sl]
        kh = s[:, BW + h * HD:BW + (h + 1) * HD]
        vh = s[:, 2 * BW + h * HD:2 * BW + (h + 1) * HD]
        qn = qh * lax.rsqrt(jnp.sum(qh * qh, axis=-1, keepdims=True) + EPS) * HD ** -0.5
        kn = kh * lax.rsqrt(jnp.sum(kh * kh, axis=-1, keepdims=True) + EPS)
        knb = kn.astype(BF16)
        qk = _dot_nt(qn.astype(BF16), knb)
        kk = _dot_nt(knb, knb)
        for d in range(2):
            dh = d * HEADS + h
            gcc = gc[:, dh:dh + 1]
            bt = beta[:, dh:dh + 1]
            gl = g_last[:, dh:dh + 1]
            grow = jnp.sum(jnp.where(eye_b, gcc, 0.0), axis=0, keepdims=True)
            incl = (ri >= ci) if d == 0 else (ri <= ci)
            strict = (ri > ci) if d == 0 else (ri < ci)
            dm = jnp.where(incl, jnp.exp(jnp.where(incl, gcc - grow, 0.0)), 0.0)
            a = jnp.where(strict, bt * kk * dm, 0.0)
            t = _unit_tri_inv(a, ri, ci, eye).astype(BF16)
            eg = jnp.exp(gcc)
            rhs = jnp.concatenate([vh * bt, kn * (bt * eg)], axis=1).astype(BF16)
            uw = _dot(t, rhs)
            u_ref[d, :, sl] = uw[:, :HD]
            w_ref[d, :, sl] = uw[:, HD:].astype(BF16)
            qg_ref[d, :, sl] = (qn * eg).astype(BF16)
            kd_ref[d, :, sl] = (kn * jnp.exp(gl - gcc)).astype(BF16)
            at_ref[d, :, h * HD:h * HD + CHUNK] = (qk * dm).astype(BF16)
            at_ref[d, :, h * HD + CHUNK:(h + 1) * HD] = jnp.zeros((CHUNK, HD - CHUNK), BF16)


def _gdn_pre(p, conv_w, a_log, dt_bias, nb, n_seq, n_ctx):
    t = p.shape[0]
    nc = t // CHUNK
    lat_chunks, ncl, ncc = nb * n_seq // CHUNK, n_seq // CHUNK, n_ctx // CHUNK
    qkv_blk = P_GQKV // (3 * BW)
    pad = lambda v: jnp.pad(v.reshape(1, -1), ((0, 0), (0, 128 - v.size)))
    dir_out = lambda dt: jax.ShapeDtypeStruct((2, t, BW), dt)
    dir_spec = pl.BlockSpec((2, CHUNK, BW), lambda g: (0, g, 0))
    return pl.pallas_call(
        functools.partial(_gdn_pre_kernel, lat_chunks=lat_chunks, ncl=ncl, ncc=ncc),
        grid=(nc,),
        in_specs=[pl.BlockSpec((CHUNK, 3 * BW), lambda g: (g, qkv_blk)),
                  pl.BlockSpec((8, 3 * BW), lambda g: (jnp.maximum(g * 8 - 1, 0), qkv_blk)),
                  pl.BlockSpec((8, 3 * BW), lambda g: (jnp.minimum(g * 8 + 8, t // 8 - 1), qkv_blk)),
                  pl.BlockSpec((CHUNK, 128), lambda g: (g, P_GA // 128)),
                  pl.BlockSpec((CHUNK, 128), lambda g: (g, P_GB // 128)),
                  pl.BlockSpec((3, 3 * BW), lambda g: (0, 0)),
                  pl.BlockSpec((1, 128), lambda g: (0, 0)),
                  pl.BlockSpec((1, 128), lambda g: (0, 0))],
        out_specs=(dir_spec, dir_spec, dir_spec, dir_spec, dir_spec,
                   pl.BlockSpec((None, 1, 128), lambda g: (g, 0, 0))),
        out_shape=(dir_out(F32), dir_out(BF16), dir_out(BF16), dir_out(BF16), dir_out(BF16),
                   jax.ShapeDtypeStruct((nc, 1, 128), F32)),
        compiler_params=_params("parallel"),
        name="gdn_chunk_prep",
    )(p, p, p, p, p, conv_w, pad(a_log), pad(dt_bias))


def _gdn_scan_kernel(*refs):
    ins, (of_ref, ob_ref, s_ref) = refs[:12], refs[12:]

    @pl.when(pl.program_id(1) == 0)
    def _():
        s_ref[...] = jnp.zeros(s_ref.shape, F32)

    for d in range(2):
        u_ref, w_ref, qg_ref, kd_ref, at_ref, dec_ref = ins[6 * d:6 * d + 6]
        o_ref = of_ref if d == 0 else ob_ref
        for h in range(HEADS):
            sl = slice(h * HD, (h + 1) * HD)
            st = s_ref[d, h]
            sb = st.astype(BF16)
            v_new = (u_ref[:, sl] - _dot(w_ref[:, sl], sb)).astype(BF16)
            o_ref[:, sl] = _dot(qg_ref[:, sl], sb) + _dot(at_ref[:, h * HD:h * HD + CHUNK], v_new)
            dec = dec_ref[0:1, d * HEADS + h:d * HEADS + h + 1]
            s_ref[d, h] = st * dec + _dot_tn(kd_ref[:, sl], v_new)


def _dir_scan(kernel, arrays, dec, nb, n_seq, n_ctx, scratch, name):
    t = arrays[0].shape[1]
    lat_chunks, ncl, ncc = nb * n_seq // CHUNK, n_seq // CHUNK, n_ctx // CHUNK
    in_specs, args = [], []
    for d in range(2):
        cmap = functools.partial(_scan_chunk, rev=bool(d), lat_chunks=lat_chunks, ncl=ncl, ncc=ncc)
        for arr in arrays:
            if arr.ndim == 3:
                in_specs.append(pl.BlockSpec((None, CHUNK, BW), lambda b, s, d=d, cmap=cmap: (d, cmap(b, s), 0)))
            else:
                in_specs.append(pl.BlockSpec((CHUNK, BW), lambda b, s, cmap=cmap: (cmap(b, s), 0)))
            args.append(arr)
        if dec.ndim == 3:
            in_specs.append(pl.BlockSpec((None, 1, dec.shape[-1]), lambda b, s, cmap=cmap: (cmap(b, s), 0, 0)))
        else:
            in_specs.append(pl.BlockSpec((None, None, 1, dec.shape[-1]),
                                         lambda b, s, d=d, cmap=cmap: (d, cmap(b, s), 0, 0)))
        args.append(dec)
    fmap = functools.partial(_scan_chunk, rev=False, lat_chunks=lat_chunks, ncl=ncl, ncc=ncc)
    bmap = functools.partial(_scan_chunk, rev=True, lat_chunks=lat_chunks, ncl=ncl, ncc=ncc)
    return pl.pallas_call(
        kernel,
        grid=(nb, ncc + ncl),
        in_specs=in_specs,
        out_specs=(pl.BlockSpec((CHUNK, BW), lambda b, s: (fmap(b, s), 0)),
                   pl.BlockSpec((CHUNK, BW), lambda b, s: (bmap(b, s), 0))),
        out_shape=(jax.ShapeDtypeStruct((t, BW), F32), jax.ShapeDtypeStruct((t, BW), F32)),
        scratch_shapes=[scratch],
        compiler_params=_params("parallel", "arbitrary"),
        name=name,
    )(*args)


def _hg_pre_kernel(q_ref, f_ref, i_ref, lbl_ref, oi_ref, qg_ref, kd_ref, vb_ref, dec_ref,
                   gc_s, k_s, *, layer):
    lbl = lbl_ref[...]
    e = jnp.exp(lbl - jnp.max(lbl, axis=0, keepdims=True))
    sm = e / jnp.sum(e, axis=0, keepdims=True)
    lb_all = sm[0]
    for l in range(1, layer + 1):
        lb_all = lb_all + sm[l]
    lb_all = lb_all - sm[0]
    q = q_ref[...] * HD ** -0.5
    v = i_ref[...].astype(BF16)
    vb_ref[...] = v
    ri, ci = _tri_masks(CHUNK)
    row = lax.broadcasted_iota(jnp.int32, (CHUNK, 1), 0)
    lane = lax.broadcasted_iota(jnp.int32, (1, CHUNK), 1)
    for d in range(2):
        lb = lb_all[d:d + 1]
        f = f_ref[:, d * BW:(d + 1) * BW]
        log_sig = jnp.minimum(f, 0.0) - jnp.log(1.0 + jnp.exp(-jnp.abs(f)))
        x1 = jnp.log(jnp.maximum(lb, LB_FLOOR))
        x2 = jnp.log(1.0 - lb) + log_sig
        log_f = jnp.maximum(x1, x2) + jnp.log(1.0 + jnp.exp(-jnp.abs(x1 - x2)))
        k = (1.0 - lb) * _sigmoid(-f)
        tri = ((ri >= ci) if d == 0 else (ri <= ci)).astype(BF16)
        gc = _dot_tri(tri, log_f)
        g_last = gc[CHUNK - 1:CHUNK] if d == 0 else gc[0:1]
        dec_ref[d] = jnp.exp(g_last)
        qg_ref[d] = (q * jnp.exp(gc)).astype(BF16)
        kd_ref[d] = (k * jnp.exp(g_last - gc)).astype(BF16)
        gc_s[...] = gc
        k_s[...] = k

        def jbody(j, accs, d=d, gc=gc):
            gj = gc_s[pl.ds(j, 1), :]
            kj = k_s[pl.ds(j, 1), :]
            mask = (row >= j) if d == 0 else (row <= j)
            w = jnp.where(mask, jnp.exp(jnp.where(mask, gc - gj, 0.0)), 0.0)
            tt = q * w * kj
            out = []
            for h in range(HEADS):
                col = jnp.sum(tt[:, h * HD:(h + 1) * HD], axis=-1, keepdims=True)
                out.append(jnp.where(lane == j, col, accs[h]))
            return tuple(out)

        accs = lax.fori_loop(0, CHUNK, jbody, tuple(jnp.zeros((CHUNK, CHUNK), F32) for _ in range(HEADS)))
        for h in range(HEADS):
            sl = slice(h * HD, (h + 1) * HD)
            oi_ref[d, :, sl] = _dot(accs[h].astype(BF16), v[:, sl])


def _hg_pre(p, lb_logits, layer):
    t = p.shape[0]
    nc = t // CHUNK
    dir_out = lambda dt: jax.ShapeDtypeStruct((2, t, BW), dt)
    dir_spec = pl.BlockSpec((2, CHUNK, BW), lambda g: (0, g, 0))
    return pl.pallas_call(
        functools.partial(_hg_pre_kernel, layer=layer),
        grid=(nc,),
        in_specs=[pl.BlockSpec((CHUNK, BW), lambda g: (g, P_HQ // BW)),
                  pl.BlockSpec((CHUNK, 2 * BW), lambda g: (g, P_HF // (2 * BW))),
                  pl.BlockSpec((CHUNK, BW), lambda g: (g, P_HI // BW)),
                  pl.BlockSpec(lb_logits.shape, lambda g: (0, 0, 0))],
        out_specs=(dir_spec, dir_spec, dir_spec,
                   pl.BlockSpec((CHUNK, BW), lambda g: (g, 0)),
                   pl.BlockSpec((2, None, 1, BW), lambda g: (0, g, 0, 0))),
        out_shape=(dir_out(F32), dir_out(BF16), dir_out(BF16),
                   jax.ShapeDtypeStruct((t, BW), BF16),
                   jax.ShapeDtypeStruct((2, nc, 1, BW), F32)),
        scratch_shapes=[pltpu.VMEM((CHUNK, BW), F32), pltpu.VMEM((CHUNK, BW), F32)],
        compiler_params=_params("parallel"),
        name="hgrn2_chunk_prep",
    )(p, p, p, lb_logits)


def _hg_scan_kernel(*refs):
    ins, (of_ref, ob_ref, s_ref) = refs[:10], refs[10:]

    @pl.when(pl.program_id(1) == 0)
    def _():
        s_ref[...] = jnp.zeros(s_ref.shape, F32)

    for d in range(2):
        oi_ref, qg_ref, kd_ref, v_ref, dec_ref = ins[5 * d:5 * d + 5]
        o_ref = of_ref if d == 0 else ob_ref
        for h in range(HEADS):
            sl = slice(h * HD, (h + 1) * HD)
            st = s_ref[d, h]
            o_ref[:, sl] = oi_ref[:, sl] + _dot_nt(qg_ref[:, sl], st.astype(BF16))
            s_ref[d, h] = st * dec_ref[:, sl] + _dot_tn(v_ref[:, sl], kd_ref[:, sl])


def _merge_kernel(h_ref, gof_ref, gob_ref, gg_ref, yb_ref, hof_ref, hob_ref, hgg_ref, gates_ref,
                  gnorm_ref, hnorm_ref, gt_ref, nw_ref, wb_ref, wo_ref, o_ref):
    def readout(of_ref, ob_ref, gate_ref, norm_ref):
        o = of_ref[...] + ob_ref[...]
        gate = gate_ref[...]
        parts = []
        for h in range(HEADS):
            sl = slice(h * HD, (h + 1) * HD)
            parts.append(_rms(o[:, sl], norm_ref[...]) * _silu(gate[:, sl]))
        return jnp.concatenate(parts, axis=1).astype(BF16)

    ys = (readout(gof_ref, gob_ref, gg_ref, gnorm_ref),
          yb_ref[...].astype(BF16),
          readout(hof_ref, hob_ref, hgg_ref, hnorm_ref))
    d = h_ref.shape[1]
    m = None
    for j in range(3):
        term = _sigmoid(gates_ref[:, j * d:(j + 1) * d]) * _dot(ys[j], wb_ref[j])
        m = term if m is None else m + term
    y = _dot(m.astype(BF16), wo_ref[...])
    o_ref[...] = h_ref[...] + gt_ref[...] * _rms(y, nw_ref[...])


def _merge(h, g_of, g_ob, yb, h_of, h_ob, p, gnorm, hnorm, gate_mod, nw, wb, wo, n_rows, n_seq, nb):
    d = h.shape[1]
    tm = TM_S
    row = lambda i: (i, 0)
    bw_spec = pl.BlockSpec((tm, BW), row)
    pcol = lambda off, width: pl.BlockSpec((tm, width), lambda i: (i, off // width))
    vec = lambda n: pl.BlockSpec((1, n), lambda i: (0, 0))
    return pl.pallas_call(
        _merge_kernel,
        grid=(n_rows // tm,),
        in_specs=[pl.BlockSpec((tm, d), row), bw_spec, bw_spec, pcol(P_GGATE, BW), bw_spec,
                  bw_spec, bw_spec, pcol(P_HGATE, BW), pcol(P_GATES, 3 * d),
                  vec(HD), vec(HD),
                  pl.BlockSpec((None, 1, d), lambda i: (jnp.minimum((i * tm) // n_seq, nb), 0, 0)),
                  vec(d), _resident(wb.shape), _resident(wo.shape)],
        out_specs=pl.BlockSpec((tm, d), row),
        out_shape=jax.ShapeDtypeStruct((n_rows, d), F32),
        compiler_params=_params("parallel"),
        name="branch_merge",
    )(h, g_of, g_ob, p, yb, h_of, h_ob, p, p, gnorm, hnorm, gate_mod, nw, wb, wo)


def _pack_w_in(w):
    d = w.shape[0]
    sizes = (BW, BW, BW, BW, 2 * HEADS, 2 * HEADS, Q_RANK, KV_RANK, ROPE, BW, 2 * BW, BW, BW, 3 * d)
    offs = [0]
    for s in sizes:
        offs.append(offs[-1] + s)
    part = lambda i: w[:, offs[i]:offs[i + 1]]
    gq, gk, gv, ggate, ga, gb, qa, kva, kr, hq, hf, hi, hgate, gates = (part(i) for i in range(14))
    z = lambda n: jnp.zeros((d, n), w.dtype)
    swap = jnp.concatenate([kr[:, 16:32], kr[:, 0:16], kr[:, 48:64], kr[:, 32:48]], axis=1)
    cols = [gates, gq, gk, gv, ggate, hf, hq, hi, hgate, qa, gb, z(120), kva, kr, swap, ga, z(120)]
    out = jnp.concatenate(cols, axis=1).astype(BF16)
    assert out.shape[1] == P_COLS
    return out


def _pack_wq(w):
    w = w.reshape(Q_RANK, HEADS, NOPE + ROPE)
    nope = w[:, :, :NOPE].reshape(Q_RANK, HEADS * NOPE)
    r = w[:, :, NOPE:]
    sw = jnp.concatenate([r[..., 16:32], r[..., 0:16], r[..., 48:64], r[..., 32:48]], axis=-1)
    rope = jnp.concatenate([r, sw], axis=-1).reshape(Q_RANK, HEADS * 2 * ROPE)
    return jnp.concatenate([nope, rope], axis=1).astype(BF16)


def _pack_wkv(w):
    w = w.reshape(KV_RANK, HEADS, NOPE + HD)
    return jnp.concatenate([w[:, :, :NOPE].reshape(KV_RANK, -1), w[:, :, NOPE:].reshape(KV_RANK, -1)],
                           axis=1).astype(BF16)


def _rope_table(nb, n_seq, n_ctx):
    nf = ROPE // 4
    rows = n_seq // GRID_W
    rpos = jnp.repeat(jnp.arange(rows, dtype=F32), GRID_W)
    cpos = jnp.tile(jnp.arange(GRID_W, dtype=F32), rows)
    inv = ROPE_BASE ** (-jnp.arange(nf, dtype=F32) / nf)
    ar, ac = rpos[:, None] * inv, cpos[:, None] * inv
    cos = jnp.concatenate([jnp.cos(ar), jnp.cos(ar), jnp.cos(ac), jnp.cos(ac)], axis=1)
    sin = jnp.concatenate([-jnp.sin(ar), jnp.sin(ar), -jnp.sin(ac), jnp.sin(ac)], axis=1)
    lat = jnp.tile(jnp.concatenate([cos, sin], axis=1), (nb, 1))
    ctx = jnp.concatenate([jnp.ones((nb * n_ctx, ROPE), F32), jnp.zeros((nb * n_ctx, ROPE), F32)], axis=1)
    return jnp.concatenate([lat, ctx], axis=0)


def kernel(x, c, ctx, c_ctx, w_ada, b_ada, norm_w, ffn_w_in, ffn_w_out, w_in, gdn_conv, gdn_a_log, gdn_dt_bias, gdn_norm, mla_q_norm, mla_kv_norm, mla_w_q_b, mla_w_kv_b, hg_lb_logits, hg_norm, w_branch, w_out):
    nb, n_seq, d = x.shape
    n_ctx = ctx.shape[1]
    depth = w_ada.shape[0]
    lat_rows, rows = nb * n_seq, nb * (n_seq + n_ctx)
    assert nb + 1 <= 8 and n_seq % TM == 0 and n_ctx % TM_S == 0 and (nb * n_ctx) % TM == 0
    assert n_seq % n_ctx == 0 and lat_rows % n_ctx == 0

    cc = jnp.concatenate([c, c_ctx[None], jnp.zeros((8 - nb - 1, d), F32)], axis=0)
    mods = _ada(cc, w_ada, b_ada)[:, :nb + 1].reshape(depth, nb + 1, 9, 1, d).transpose(0, 2, 1, 3, 4)
    cs = _rope_table(nb, n_seq, n_ctx)
    h = jnp.concatenate([x.reshape(lat_rows, d), ctx.reshape(nb * n_ctx, d)], axis=0)

    for l in range(depth):
        last = l == depth - 1
        md, nw = mods[l], norm_w[l][:, None, :]
        wi = [ffn_w_in[l, j].astype(BF16) for j in range(2)]
        wo = [ffn_w_out[l, j].astype(BF16) for j in range(2)]

        h, u = _ffn(h, md[0:3], nw[0:2], wi[0], wo[0], rows, n_seq, nb, mixer_mods=(nw[2], md[3], md[4]))
        p = _proj(u, _pack_w_in(w_in[l]))

        g_ops = _gdn_pre(p, gdn_conv[l], gdn_a_log[l], gdn_dt_bias[l], nb, n_seq, n_ctx)
        g_of, g_ob = _dir_scan(_gdn_scan_kernel, g_ops[:5], g_ops[5], nb, n_seq, n_ctx,
                               pltpu.VMEM((2, HEADS, HD, HD), F32), "gdn_scan")

        hoi, hqg, hkd, hvb, hdec = _hg_pre(p, hg_lb_logits, l)
        h_of, h_ob = _dir_scan(_hg_scan_kernel, (hoi, hqg, hkd, hvb), hdec, nb, n_seq, n_ctx,
                               pltpu.VMEM((2, HEADS, HD, HD), F32), "hgrn2_scan")

        q, k, v = _mla_prep(p, cs, mla_q_norm[l][None], mla_kv_norm[l][None],
                            _pack_wq(mla_w_q_b[l]), _pack_wkv(mla_w_kv_b[l]), nb, n_seq, n_ctx)
        yb = _attention(q, k, v, None, nb, n_seq, n_ctx, rows, ctx_only=False)
        if not last:
            yb = _attention(q, k, v, yb, nb, n_seq, n_ctx, rows, ctx_only=True)

        out_rows = lat_rows if last else rows
        h = _merge(h, g_of, g_ob, yb, h_of, h_ob, p, gdn_norm[l][None], hg_norm[l][None], md[5], nw[3],
                   w_branch[l].astype(BF16), w_out[l].astype(BF16), out_rows, n_seq, nb)
        h = _ffn(h, md[6:9], nw[4:6], wi[1], wo[1], out_rows, n_seq, nb)
    return h.reshape(nb, n_seq, d)
```

```python
import functools

import jax
import jax.numpy as jnp
from jax import lax
from jax.experimental import pallas as pl
from jax.experimental.pallas import tpu as pltpu

F32 = jnp.float32
BF16 = jnp.bfloat16
EPS = 1e-6
LB_FLOOR = 1e-30
GRID_W = 64
ROPE_BASE = 10000.0

D_FF = 2816
HEADS = 4
HD = 128
ROPE = 64
NOPE = 128
Q_RANK = 384
KV_RANK = 256
CHUNK = 64
SUB = 16
BW = HEADS * HD

TM = 512
TM_S = 256
FF_CK = 256
VMEM_LIMIT = 48 * 1024 * 1024

P_GATES = 0
P_GQKV = 3072
P_GGATE = 4608
P_HF = 5120
P_HQ = 6144
P_HI = 6656
P_HGATE = 7168
P_QA = 7680
P_GB = 8064
P_KVA = 8192
P_KR = 8448
P_GA = 8576
P_COLS = 8704
PROJ_TN = 2176

NT = (((1,), (1,)), ((), ()))
TN = (((0,), (0,)), ((), ()))


def _dot(a, b):
    return jnp.dot(a, b, preferred_element_type=F32)


def _dot_nt(a, b):
    return lax.dot_general(a, b, NT, preferred_element_type=F32)


def _dot_tn(a, b):
    return lax.dot_general(a, b, TN, preferred_element_type=F32)


def _sigmoid(x):
    return 1.0 / (1.0 + jnp.exp(-x))


def _silu(x):
    return x * _sigmoid(x)


def _softplus(x):
    return jnp.maximum(x, 0.0) + jnp.log(1.0 + jnp.exp(-jnp.abs(x)))


def _rms(x, w):
    return x * lax.rsqrt(jnp.mean(x * x, axis=-1, keepdims=True) + EPS) * w


def _resident(shape):
    zeros = (0,) * len(shape)
    return pl.BlockSpec(shape, lambda *_: zeros, pipeline_mode=pl.Buffered(1))


def _params(*sem):
    return pltpu.CompilerParams(dimension_semantics=sem, vmem_limit_bytes=VMEM_LIMIT)


def _ada_kernel(c_ref, w_ref, b_ref, o_ref):
    s = _silu(c_ref[...])
    o_ref[...] = _dot(s.astype(BF16), w_ref[...].astype(BF16)) + b_ref[...]


def _ada(cc, w_ada, b_ada):
    depth, d, nm = w_ada.shape
    tn = 1024
    return pl.pallas_call(
        _ada_kernel,
        grid=(depth, nm // tn),
        in_specs=[pl.BlockSpec((8, d), lambda l, j: (0, 0)),
                  pl.BlockSpec((None, d, tn), lambda l, j: (l, 0, j)),
                  pl.BlockSpec((None, 1, tn), lambda l, j: (l, 0, j))],
        out_specs=pl.BlockSpec((None, 8, tn), lambda l, j: (l, 0, j)),
        out_shape=jax.ShapeDtypeStruct((depth, 8, nm), F32),
        compiler_params=_params("parallel", "parallel"),
        name="ada_mod",
    )(cc, w_ada, b_ada.reshape(depth, 1, nm))


def _ffn_kernel(x_ref, sh_ref, sc_ref, gt_ref, prew_ref, postw_ref, wi_ref, wo_ref, *rest, emit_u):
    if emit_u:
        nw_ref, sh2_ref, sc2_ref, o_ref, u_ref = rest
    else:
        (o_ref,) = rest
    x = x_ref[...]
    hn = (_rms(x, prew_ref[...]) * (1.0 + sc_ref[...]) + sh_ref[...]).astype(BF16)
    acc = jnp.zeros(x.shape, F32)
    for c in range(D_FF // FF_CK):
        g = _dot(hn, wi_ref[:, c * FF_CK:(c + 1) * FF_CK])
        u = _dot(hn, wi_ref[:, D_FF + c * FF_CK:D_FF + (c + 1) * FF_CK])
        a = (_silu(g) * u).astype(BF16)
        acc = acc + _dot(a, wo_ref[c * FF_CK:(c + 1) * FF_CK, :])
    out = x + 0.5 * gt_ref[...] * _rms(acc, postw_ref[...])
    o_ref[...] = out
    if emit_u:
        u_ref[...] = (_rms(out, nw_ref[...]) * (1.0 + sc2_ref[...]) + sh2_ref[...]).astype(BF16)


def _ffn(h, mods, nw, wi, wo, n_rows, n_seq, nb, mixer_mods=None):
    d = h.shape[1]
    emit_u = mixer_mods is not None
    row = lambda i: (i, 0)
    mod = lambda i: (jnp.minimum((i * TM) // n_seq, nb), 0, 0)
    vec = pl.BlockSpec((1, d), lambda i: (0, 0))
    mspec = pl.BlockSpec((None, 1, d), mod)
    in_specs = [pl.BlockSpec((TM, d), row), mspec, mspec, mspec, vec, vec,
                _resident(wi.shape), _resident(wo.shape)]
    args = [h, mods[0], mods[1], mods[2], nw[0], nw[1], wi, wo]
    out_specs = pl.BlockSpec((TM, d), row)
    out_shape = jax.ShapeDtypeStruct((n_rows, d), F32)
    if emit_u:
        in_specs += [vec, mspec, mspec]
        args += list(mixer_mods)
        out_specs = (out_specs, pl.BlockSpec((TM, d), row))
        out_shape = (out_shape, jax.ShapeDtypeStruct((n_rows, d), BF16))
    return pl.pallas_call(
        functools.partial(_ffn_kernel, emit_u=emit_u),
        grid=(n_rows // TM,),
        in_specs=in_specs, out_specs=out_specs, out_shape=out_shape,
        compiler_params=_params("parallel"),
        name="ffn_sublayer",
    )(*args)


def _proj_kernel(x_ref, w_ref, o_ref):
    o_ref[...] = _dot(x_ref[...], w_ref[pl.program_id(1)])


def _proj(u, w):
    t, d = u.shape
    nj, _, tn = w.shape
    return pl.pallas_call(
        _proj_kernel,
        grid=(t // TM, nj),
        in_specs=[pl.BlockSpec((TM, d), lambda i, j: (i, 0)), _resident(w.shape)],
        out_specs=pl.BlockSpec((TM, tn), lambda i, j: (i, j)),
        out_shape=jax.ShapeDtypeStruct((t, nj * tn), F32),
        compiler_params=_params("parallel", "arbitrary"),
        name="in_proj",
    )(u, w)


def _mla_prep_kernel(qa_ref, kva_ref, kr_ref, cs_ref, qn_ref, kvn_ref, wq_ref, wkv_ref,
                     q_ref, k_ref, v_ref):
    cs = cs_ref[...]
    qa = _rms(qa_ref[:, :Q_RANK], qn_ref[...]).astype(BF16)
    kva = _rms(kva_ref[...], kvn_ref[...]).astype(BF16)
    q2 = _dot(qa, wq_ref[...])
    kv2 = _dot(kva, wkv_ref[...])
    kr = kr_ref[...] * cs
    kr = (kr + pltpu.roll(kr, ROPE, 1))[:, :ROPE].astype(BF16)
    for h in range(HEADS):
        q_ref[h, :, :NOPE] = q2[:, h * NOPE:(h + 1) * NOPE].astype(BF16)
        qr = q2[:, BW + h * 128:BW + (h + 1) * 128] * cs
        q_ref[h, :, NOPE:] = (qr + pltpu.roll(qr, ROPE, 1))[:, :ROPE].astype(BF16)
        k_ref[h, :, :NOPE] = kv2[:, h * NOPE:(h + 1) * NOPE].astype(BF16)
        k_ref[h, :, NOPE:] = kr
        v_ref[h] = kv2[:, BW + h * HD:BW + (h + 1) * HD].astype(BF16)


def _mla_prep(p, cs, qn, kvn, wq, wkv, nb, n_seq, n_ctx):
    t = p.shape[0]
    tm = TM_S
    lat_tiles = nb * n_seq // tm
    per_lat = n_seq // tm
    per_ctx = n_ctx // tm

    def omap(i):
        j = i - lat_tiles
        b = jnp.where(i < lat_tiles, i // per_lat, j // per_ctx)
        blk = jnp.where(i < lat_tiles, i % per_lat, per_lat + j % per_ctx)
        return (b, 0, blk, 0)

    tk = n_seq + n_ctx
    return pl.pallas_call(
        _mla_prep_kernel,
        grid=(t // tm,),
        in_specs=[pl.BlockSpec((tm, 512), lambda i: (i, P_QA // 512)),
                  pl.BlockSpec((tm, KV_RANK), lambda i: (i, P_KVA // KV_RANK)),
                  pl.BlockSpec((tm, 128), lambda i: (i, P_KR // 128)),
                  pl.BlockSpec((tm, 128), lambda i: (i, 0)),
                  pl.BlockSpec((1, Q_RANK), lambda i: (0, 0)),
                  pl.BlockSpec((1, KV_RANK), lambda i: (0, 0)),
                  _resident(wq.shape), _resident(wkv.shape)],
        out_specs=(pl.BlockSpec((None, HEADS, tm, NOPE + ROPE), omap),
                   pl.BlockSpec((None, HEADS, tm, NOPE + ROPE), omap),
                   pl.BlockSpec((None, HEADS, tm, HD), omap)),
        out_shape=(jax.ShapeDtypeStruct((nb, HEADS, tk, NOPE + ROPE), BF16),
                   jax.ShapeDtypeStruct((nb, HEADS, tk, NOPE + ROPE), BF16),
                   jax.ShapeDtypeStruct((nb, HEADS, tk, HD), BF16)),
        compiler_params=_params("parallel"),
        name="mla_prep",
    )(p, p, p, cs, qn, kvn, wq, wkv)


def _attn_kernel(q_ref, k_ref, v_ref, *rest, kb, scale, aliased):
    o_ref = rest[-1]
    q = q_ref[...]
    tq = q.shape[0]
    nkb = k_ref.shape[0] // kb
    c = scale * 1.4426950408889634

    def scores(j):
        return _dot_nt(q, k_ref[pl.ds(pl.multiple_of(j * kb, kb), kb), :])

    def update(j, s, m, l, acc):
        t = s * c
        m_new = jnp.maximum(m, jnp.max(t, axis=-1, keepdims=True))
        alpha = jnp.exp2(m - m_new)
        pr = jnp.exp2(t - m_new)
        l = alpha * l + jnp.sum(pr, axis=-1, keepdims=True)
        v = v_ref[pl.ds(pl.multiple_of(j * kb, kb), kb), :]
        return m_new, l, alpha * acc + _dot(pr.astype(BF16), v)

    def body(j, carry):
        m, l, acc, s = carry
        s_next = scores(j + 1)
        return update(j, s, m, l, acc) + (s_next,)

    init = (jnp.full((tq, 1), -1e30, F32), jnp.zeros((tq, 1), F32), jnp.zeros((tq, HD), F32), scores(0))
    m, l, acc, s = lax.fori_loop(0, nkb - 1, body, init)
    _, l, acc = update(nkb - 1, s, m, l, acc)
    o_ref[...] = acc / l


def _pick_block(n, cands):
    for c in cands:
        if n % c == 0:
            return c
    raise ValueError(f"no block size for {n}")


def _attention(q, k, v, o_prev, nb, n_seq, n_ctx, rows, ctx_only):
    scale = (NOPE + ROPE) ** -0.5
    if ctx_only:
        tq, tk = n_ctx, n_ctx
        koff = n_seq // n_ctx
        grid = (nb, HEADS, 1)
        qmap = lambda b, h, i: (b, h, koff, 0)
        kmap = lambda b, h, i: (b, h, koff, 0)
        omap = lambda b, h, i: (nb * n_seq // n_ctx + b, h)
    else:
        tq, tk = _pick_block(n_seq, (512, 256)), n_seq + n_ctx
        grid = (nb, HEADS, n_seq // tq)
        qmap = lambda b, h, i: (b, h, i, 0)
        kmap = lambda b, h, i: (b, h, 0, 0)
        omap = lambda b, h, i: (b * (n_seq // tq) + i, h)
    kb = _pick_block(tk, (768, 512, 256))
    in_specs = [pl.BlockSpec((None, None, tq, NOPE + ROPE), qmap),
                pl.BlockSpec((None, None, tk, NOPE + ROPE), kmap),
                pl.BlockSpec((None, None, tk, HD), kmap)]
    args = [q, k, v]
    aliases = {}
    if ctx_only:
        in_specs.append(pl.BlockSpec(memory_space=pl.ANY))
        args.append(o_prev)
        aliases = {3: 0}
    return pl.pallas_call(
        functools.partial(_attn_kernel, kb=kb, scale=scale, aliased=ctx_only),
        grid=grid,
        in_specs=in_specs,
        out_specs=pl.BlockSpec((tq, HD), omap),
        out_shape=jax.ShapeDtypeStruct((rows, BW), F32),
        input_output_aliases=aliases,
        compiler_params=_params("parallel", "parallel", "parallel"),
        name="mla_attn_ctx" if ctx_only else "mla_attn",
    )(*args)


def _chunk_pos(g, lat_chunks, ncl, ncc):
    is_lat = g < lat_chunks
    pos = jnp.where(is_lat, g % ncl, (g - lat_chunks) % ncc)
    last = jnp.where(is_lat, pos == ncl - 1, pos == ncc - 1)
    return pos == 0, last


def _scan_chunk(b, s, rev, lat_chunks, ncl, ncc):
    c = jnp.where(s < ncc, s, s - ncc)
    if rev:
        c = jnp.where(s < ncc, ncc - 1 - c, ncl - 1 - c)
    return jnp.where(s < ncc, lat_chunks + b * ncc + c, b * ncl + c)


def _tri_masks(n):
    ri = lax.broadcasted_iota(jnp.int32, (n, n), 0)
    ci = lax.broadcasted_iota(jnp.int32, (n, n), 1)
    return ri, ci


def _split(x):
    hi = x.astype(BF16)
    return hi, (x - hi.astype(F32)).astype(BF16)


def _dot3(a, b):
    (ah, al), (bh, bl) = a, b
    return _dot(ah, bh) + (_dot(ah, bl) + _dot(al, bh))


def _dot_tri(tri, x):
    hi, lo = _split(x)
    lo2 = (x - hi.astype(F32) - lo.astype(F32)).astype(BF16)
    return _dot(tri, hi) + (_dot(tri, lo) + _dot(tri, lo2))


def _unit_tri_inv(mats, ri, ci, eye):
    rb, cb = ri >> 3, ci >> 3
    d8 = [jnp.where(rb == cb, a, 0.0) for a in mats]
    d8s = [_split(d) for d in d8]
    x2 = [_split(_dot3(d, d)) for d in d8s]
    ts = [eye - d for d in d8]
    x4 = [_split(_dot3(x, x)) for x in x2]
    ts = [t + _dot3(_split(t), x) for t, x in zip(ts, x2)]
    ts = [t + _dot3(_split(t), x) for t, x in zip(ts, x4)]
    for _ in range(3):
        same = rb == cb
        rb, cb = rb >> 1, cb >> 1
        off = (rb == cb) & jnp.logical_not(same)
        bs = [jnp.where(off, a, 0.0).astype(BF16) for a in mats]
        tb = [t.astype(BF16) for t in ts]
        ys = [_dot(t, b).astype(BF16) for t, b in zip(tb, bs)]
        ts = [t - _dot(y, tl) for t, y, tl in zip(ts, ys, tb)]
    return ts


def _gdn_pre_kernel(x_ref, prev_ref, next_ref, a_ref, b_ref, cw_ref, alog_ref, dtb_ref,
                    u_ref, w_ref, qg_ref, kd_ref, at_ref, dec_ref, *, lat_chunks, ncl, ncc):
    first, last = _chunk_pos(pl.program_id(0), lat_chunks, ncl, ncc)
    x = x_ref[...]
    xp = jnp.where(first, 0.0, prev_ref[7:8, :])
    xn = jnp.where(last, 0.0, next_ref[0:1, :])
    row = lax.broadcasted_iota(jnp.int32, (CHUNK, 1), 0)
    x_dn = jnp.where(row == 0, xp, pltpu.roll(x, 1, 0))
    x_up = jnp.where(row == CHUNK - 1, xn, pltpu.roll(x, CHUNK - 1, 0))
    cw = cw_ref[...]
    s = _silu(x_dn * cw[0:1] + x * cw[1:2] + x_up * cw[2:3])

    lane = lax.broadcasted_iota(jnp.int32, (1, 128), 1)
    g = -jnp.exp(alog_ref[...]) * _softplus(a_ref[...] + dtb_ref[...])
    beta = _sigmoid(b_ref[...])
    ri, ci = _tri_masks(CHUNK)
    eye_b = ri == ci
    eye = eye_b.astype(F32)
    low = (ri >= ci).astype(BF16)
    upp = (ri <= ci).astype(BF16)
    gc = jnp.where(lane < HEADS, _dot_tri(low, g), _dot_tri(upp, g))
    g_last = jnp.where(lane < HEADS, gc[CHUNK - 1:CHUNK], gc[0:1])
    dec_ref[...] = jnp.exp(g_last)

    qn, kn, vs, qk, kk = [], [], [], [], []
    for h in range(HEADS):
        qh = s[:, h * HD:(h + 1) * HD]
        kh = s[:, BW + h * HD:BW + (h + 1) * HD]
        vs.append(s[:, 2 * BW + h * HD:2 * BW + (h + 1) * HD])
        qn.append(qh * lax.rsqrt(jnp.sum(qh * qh, axis=-1, keepdims=True) + EPS) * HD ** -0.5)
        kn.append(kh * lax.rsqrt(jnp.sum(kh * kh, axis=-1, keepdims=True) + EPS))
        knb = kn[h].astype(BF16)
        qk.append(_dot_nt(qn[h].astype(BF16), knb))
        kk.append(_dot_nt(knb, knb))

    probs = [(d, h) for h in range(HEADS) for d in range(2)]
    mats, rhs = [], []
    for d, h in probs:
        sl = slice(h * HD, (h + 1) * HD)
        dh = d * HEADS + h
        gcc = gc[:, dh:dh + 1]
        bt = beta[:, dh:dh + 1]
        grow = jnp.sum(jnp.where(eye_b, gcc, 0.0), axis=0, keepdims=True)
        incl = (ri >= ci) if d == 0 else (ri <= ci)
        strict = (ri > ci) if d == 0 else (ri < ci)
        dm = jnp.where(incl, jnp.exp(gcc - grow), 0.0)
        eg = jnp.exp(gcc)
        mats.append(jnp.where(strict, bt * kk[h] * dm, 0.0))
        rhs.append(jnp.concatenate([vs[h] * bt, kn[h] * (bt * eg)], axis=1).astype(BF16))
        qg_ref[d, :, sl] = (qn[h] * eg).astype(BF16)
        kd_ref[d, :, sl] = (kn[h] * jnp.exp(g_last[:, dh:dh + 1] - gcc)).astype(BF16)
        at_ref[d, :, h * HD:h * HD + CHUNK] = (qk[h] * dm).astype(BF16)
        at_ref[d, :, h * HD + CHUNK:(h + 1) * HD] = jnp.zeros((CHUNK, HD - CHUNK), BF16)

    for (d, h), t, r in zip(probs, _unit_tri_inv(mats, ri, ci, eye), rhs):
        sl = slice(h * HD, (h + 1) * HD)
        uw = _dot(t.astype(BF16), r)
        u_ref[d, :, sl] = uw[:, :HD]
        w_ref[d, :, sl] = uw[:, HD:].astype(BF16)


def _gdn_pre(p, conv_w, a_log, dt_bias, nb, n_seq, n_ctx):
    t = p.shape[0]
    nc = t // CHUNK
    lat_chunks, ncl, ncc = nb * n_seq // CHUNK, n_seq // CHUNK, n_ctx // CHUNK
    qkv_blk = P_GQKV // (3 * BW)
    pad = lambda v: jnp.pad(v.reshape(1, -1), ((0, 0), (0, 128 - v.size)))
    dir_out = lambda dt: jax.ShapeDtypeStruct((2, t, BW), dt)
    dir_spec = pl.BlockSpec((2, CHUNK, BW), lambda g: (0, g, 0))
    return pl.pallas_call(
        functools.partial(_gdn_pre_kernel, lat_chunks=lat_chunks, ncl=ncl, ncc=ncc),
        grid=(nc,),
        in_specs=[pl.BlockSpec((CHUNK, 3 * BW), lambda g: (g, qkv_blk)),
                  pl.BlockSpec((8, 3 * BW), lambda g: (jnp.maximum(g * 8 - 1, 0), qkv_blk)),
                  pl.BlockSpec((8, 3 * BW), lambda g: (jnp.minimum(g * 8 + 8, t // 8 - 1), qkv_blk)),
                  pl.BlockSpec((CHUNK, 128), lambda g: (g, P_GA // 128)),
                  pl.BlockSpec((CHUNK, 128), lambda g: (g, P_GB // 128)),
                  pl.BlockSpec((3, 3 * BW), lambda g: (0, 0)),
                  pl.BlockSpec((1, 128), lambda g: (0, 0)),
                  pl.BlockSpec((1, 128), lambda g: (0, 0))],
        out_specs=(dir_spec, dir_spec, dir_spec, dir_spec, dir_spec,
                   pl.BlockSpec((None, 1, 128), lambda g: (g, 0, 0))),
        out_shape=(dir_out(F32), dir_out(BF16), dir_out(BF16), dir_out(BF16), dir_out(BF16),
                   jax.ShapeDtypeStruct((nc, 1, 128), F32)),
        compiler_params=_params("parallel"),
        name="gdn_chunk_prep",
    )(p, p, p, p, p, conv_w, pad(a_log), pad(dt_bias))


def _gdn_scan_kernel(*refs):
    ins, (of_ref, ob_ref, s_ref) = refs[:12], refs[12:]

    @pl.when(pl.program_id(1) == 0)
    def _():
        s_ref[...] = jnp.zeros(s_ref.shape, F32)

    for d in range(2):
        u_ref, w_ref, qg_ref, kd_ref, at_ref, dec_ref = ins[6 * d:6 * d + 6]
        o_ref = of_ref if d == 0 else ob_ref
        for h in range(HEADS):
            sl = slice(h * HD, (h + 1) * HD)
            st = s_ref[d, h]
            sb = st.astype(BF16)
            v_new = (u_ref[:, sl] - _dot(w_ref[:, sl], sb)).astype(BF16)
            o_ref[:, sl] = _dot(qg_ref[:, sl], sb) + _dot(at_ref[:, h * HD:h * HD + CHUNK], v_new)
            dec = dec_ref[0:1, d * HEADS + h:d * HEADS + h + 1]
            s_ref[d, h] = st * dec + _dot_tn(kd_ref[:, sl], v_new)


def _dir_scan(kernel, arrays, dec, nb, n_seq, n_ctx, scratch, name):
    t = arrays[0].shape[1]
    lat_chunks, ncl, ncc = nb * n_seq // CHUNK, n_seq // CHUNK, n_ctx // CHUNK
    in_specs, args = [], []
    for d in range(2):
        cmap = functools.partial(_scan_chunk, rev=bool(d), lat_chunks=lat_chunks, ncl=ncl, ncc=ncc)
        for arr in arrays:
            if arr.ndim == 3:
                in_specs.append(pl.BlockSpec((None, CHUNK, BW), lambda b, s, d=d, cmap=cmap: (d, cmap(b, s), 0)))
            else:
                in_specs.append(pl.BlockSpec((CHUNK, BW), lambda b, s, cmap=cmap: (cmap(b, s), 0)))
            args.append(arr)
        if dec.ndim == 3:
            in_specs.append(pl.BlockSpec((None, 1, dec.shape[-1]), lambda b, s, cmap=cmap: (cmap(b, s), 0, 0)))
        else:
            in_specs.append(pl.BlockSpec((None, None, 1, dec.shape[-1]),
                                         lambda b, s, d=d, cmap=cmap: (d, cmap(b, s), 0, 0)))
        args.append(dec)
    fmap = functools.partial(_scan_chunk, rev=False, lat_chunks=lat_chunks, ncl=ncl, ncc=ncc)
    bmap = functools.partial(_scan_chunk, rev=True, lat_chunks=lat_chunks, ncl=ncl, ncc=ncc)
    return pl.pallas_call(
        kernel,
        grid=(nb, ncc + ncl),
        in_specs=in_specs,
        out_specs=(pl.BlockSpec((CHUNK, BW), lambda b, s: (fmap(b, s), 0)),
                   pl.BlockSpec((CHUNK, BW), lambda b, s: (bmap(b, s), 0))),
        out_shape=(jax.ShapeDtypeStruct((t, BW), F32), jax.ShapeDtypeStruct((t, BW), F32)),
        scratch_shapes=[scratch],
        compiler_params=_params("parallel", "arbitrary"),
        name=name,
    )(*args)


def _hg_scores(q, k, gc, rev):
    nblk = CHUNK // SUB
    lane = lax.broadcasted_iota(jnp.int32, (1, CHUNK), 1)
    row8 = lax.broadcasted_iota(jnp.int32, (8, 1), 0)
    blk = lambda x, b: x[b * SUB:(b + 1) * SUB]
    ref_row = lambda b: b * SUB + (0 if rev else SUB - 1)
    refs = [gc[ref_row(b):ref_row(b) + 1] for b in range(nblk)]
    kt = jnp.concatenate([blk(k, b) * jnp.exp(refs[b] - blk(gc, b)) for b in range(nblk)], axis=0)
    srcs = list(range(nblk - 1, 0, -1)) if rev else list(range(nblk - 1))
    rows_of = (lambda j: slice(0, j * SUB)) if rev else (lambda j: slice((j + 1) * SUB, CHUNK))
    qt = jnp.concatenate([q[rows_of(j)] * jnp.exp(gc[rows_of(j)] - refs[j]) for j in srcs], axis=0)
    seg_off, off = {}, 0
    for j in srcs:
        seg_off[j] = off
        off += (j if rev else nblk - 1 - j) * SUB
    ktb, qtb = kt.astype(BF16), qt.astype(BF16)
    cross = [_dot_nt(qtb[:, h * HD:(h + 1) * HD], ktb[:, h * HD:(h + 1) * HD]) for h in range(HEADS)]

    acc = [[jnp.zeros((8, CHUNK), F32) for _ in range(CHUNK // 8)] for _ in range(HEADS)]
    for j in range(CHUNK):
        b, jj = j // SUB, j % SUB
        gj, kj = gc[j:j + 1], k[j:j + 1]
        for rg in range(SUB // 8):
            lo, hi = 8 * rg, 8 * rg + 7
            if (hi < jj and not rev) or (lo > jj and rev):
                continue
            r0 = b * SUB + lo
            w = jnp.exp(gc[r0:r0 + 8] - gj)
            if not ((lo >= jj and not rev) or (hi <= jj and rev)):
                w = jnp.where((row8 + lo >= jj) if not rev else (row8 + lo <= jj), w, 0.0)
            tt = q[r0:r0 + 8] * w * kj
            g8 = r0 // 8
            for h in range(HEADS):
                col = jnp.sum(tt[:, h * HD:(h + 1) * HD], axis=-1, keepdims=True)
                acc[h][g8] = jnp.where(lane == j, col, acc[h][g8])

    out = []
    for h in range(HEADS):
        groups = []
        for g8 in range(CHUNK // 8):
            r0 = 8 * g8
            bi = r0 // SUB
            val = acc[h][g8]
            for j in srcs:
                if (j < bi and not rev) or (j > bi and rev):
                    base = seg_off[j] + (r0 if rev else r0 - (j + 1) * SUB)
                    val = jnp.where((lane >= j * SUB) & (lane < (j + 1) * SUB), cross[h][base:base + 8], val)
            groups.append(val)
        out.append(jnp.concatenate(groups, axis=0))
    return out


def _hg_pre_kernel(q_ref, f_ref, i_ref, lbl_ref, oi_ref, qg_ref, kd_ref, vb_ref, dec_ref, *, layer):
    lbl = lbl_ref[...]
    e = jnp.exp(lbl - jnp.max(lbl, axis=0, keepdims=True))
    sm = e / jnp.sum(e, axis=0, keepdims=True)
    lb_all = sm[0]
    for l in range(1, layer + 1):
        lb_all = lb_all + sm[l]
    lb_all = lb_all - sm[0]
    q = q_ref[...] * HD ** -0.5
    v = i_ref[...].astype(BF16)
    vb_ref[...] = v
    ri, ci = _tri_masks(CHUNK)
    for d in range(2):
        lb = lb_all[d:d + 1]
        f = f_ref[:, d * BW:(d + 1) * BW]
        log_sig = jnp.minimum(f, 0.0) - jnp.log(1.0 + jnp.exp(-jnp.abs(f)))
        x1 = jnp.log(jnp.maximum(lb, LB_FLOOR))
        x2 = jnp.log(1.0 - lb) + log_sig
        log_f = jnp.maximum(x1, x2) + jnp.log(1.0 + jnp.exp(-jnp.abs(x1 - x2)))
        k = (1.0 - lb) * _sigmoid(-f)
        tri = ((ri >= ci) if d == 0 else (ri <= ci)).astype(BF16)
        gc = _dot_tri(tri, log_f)
        g_last = gc[CHUNK - 1:CHUNK] if d == 0 else gc[0:1]
        dec_ref[d] = jnp.exp(g_last)
        qg_ref[d] = (q * jnp.exp(gc)).astype(BF16)
        kd_ref[d] = (k * jnp.exp(g_last - gc)).astype(BF16)
        scores = _hg_scores(q, k, gc, rev=bool(d))
        for h in range(HEADS):
            sl = slice(h * HD, (h + 1) * HD)
            oi_ref[d, :, sl] = _dot(scores[h].astype(BF16), v[:, sl])


def _hg_pre(p, lb_logits, layer):
    t = p.shape[0]
    nc = t // CHUNK
    dir_out = lambda dt: jax.ShapeDtypeStruct((2, t, BW), dt)
    dir_spec = pl.BlockSpec((2, CHUNK, BW), lambda g: (0, g, 0))
    return pl.pallas_call(
        functools.partial(_hg_pre_kernel, layer=layer),
        grid=(nc,),
        in_specs=[pl.BlockSpec((CHUNK, BW), lambda g: (g, P_HQ // BW)),
                  pl.BlockSpec((CHUNK, 2 * BW), lambda g: (g, P_HF // (2 * BW))),
                  pl.BlockSpec((CHUNK, BW), lambda g: (g, P_HI // BW)),
                  pl.BlockSpec(lb_logits.shape, lambda g: (0, 0, 0))],
        out_specs=(dir_spec, dir_spec, dir_spec,
                   pl.BlockSpec((CHUNK, BW), lambda g: (g, 0)),
                   pl.BlockSpec((2, None, 1, BW), lambda g: (0, g, 0, 0))),
        out_shape=(dir_out(F32), dir_out(BF16), dir_out(BF16),
                   jax.ShapeDtypeStruct((t, BW), BF16),
                   jax.ShapeDtypeStruct((2, nc, 1, BW), F32)),
        compiler_params=_params("parallel"),
        name="hgrn2_chunk_prep",
    )(p, p, p, lb_logits)


def _hg_scan_kernel(*refs):
    ins, (of_ref, ob_ref, s_ref) = refs[:10], refs[10:]

    @pl.when(pl.program_id(1) == 0)
    def _():
        s_ref[...] = jnp.zeros(s_ref.shape, F32)

    for d in range(2):
        oi_ref, qg_ref, kd_ref, v_ref, dec_ref = ins[5 * d:5 * d + 5]
        o_ref = of_ref if d == 0 else ob_ref
        for h in range(HEADS):
            sl = slice(h * HD, (h + 1) * HD)
            st = s_ref[d, h]
            o_ref[:, sl] = oi_ref[:, sl] + _dot_nt(qg_ref[:, sl], st.astype(BF16))
            s_ref[d, h] = st * dec_ref[:, sl] + _dot_tn(v_ref[:, sl], kd_ref[:, sl])


def _merge_kernel(h_ref, gof_ref, gob_ref, gg_ref, yb_ref, hof_ref, hob_ref, hgg_ref, gates_ref,
                  gnorm_ref, hnorm_ref, gt_ref, nw_ref, wb_ref, wo_ref, o_ref):
    def readout(of_ref, ob_ref, gate_ref, norm_ref):
        o = of_ref[...] + ob_ref[...]
        gate = gate_ref[...]
        parts = []
        for h in range(HEADS):
            sl = slice(h * HD, (h + 1) * HD)
            parts.append(_rms(o[:, sl], norm_ref[...]) * _silu(gate[:, sl]))
        return jnp.concatenate(parts, axis=1).astype(BF16)

    ys = (readout(gof_ref, gob_ref, gg_ref, gnorm_ref),
          yb_ref[...].astype(BF16),
          readout(hof_ref, hob_ref, hgg_ref, hnorm_ref))
    d = h_ref.shape[1]
    m = None
    for j in range(3):
        term = _sigmoid(gates_ref[:, j * d:(j + 1) * d]) * _dot(ys[j], wb_ref[j])
        m = term if m is None else m + term
    y = _dot(m.astype(BF16), wo_ref[...])
    o_ref[...] = h_ref[...] + gt_ref[...] * _rms(y, nw_ref[...])


def _merge(h, g_of, g_ob, yb, h_of, h_ob, p, gnorm, hnorm, gate_mod, nw, wb, wo, n_rows, n_seq, nb):
    d = h.shape[1]
    tm = TM_S
    row = lambda i: (i, 0)
    bw_spec = pl.BlockSpec((tm, BW), row)
    pcol = lambda off, width: pl.BlockSpec((tm, width), lambda i: (i, off // width))
    vec = lambda n: pl.BlockSpec((1, n), lambda i: (0, 0))
    return pl.pallas_call(
        _merge_kernel,
        grid=(n_rows // tm,),
        in_specs=[pl.BlockSpec((tm, d), row), bw_spec, bw_spec, pcol(P_GGATE, BW), bw_spec,
                  bw_spec, bw_spec, pcol(P_HGATE, BW), pcol(P_GATES, 3 * d),
                  vec(HD), vec(HD),
                  pl.BlockSpec((None, 1, d), lambda i: (jnp.minimum((i * tm) // n_seq, nb), 0, 0)),
                  vec(d), _resident(wb.shape), _resident(wo.shape)],
        out_specs=pl.BlockSpec((tm, d), row),
        out_shape=jax.ShapeDtypeStruct((n_rows, d), F32),
        compiler_params=_params("parallel"),
        name="branch_merge",
    )(h, g_of, g_ob, p, yb, h_of, h_ob, p, p, gnorm, hnorm, gate_mod, nw, wb, wo)


def _pack_w_in(w):
    d = w.shape[0]
    sizes = (BW, BW, BW, BW, 2 * HEADS, 2 * HEADS, Q_RANK, KV_RANK, ROPE, BW, 2 * BW, BW, BW, 3 * d)
    offs = [0]
    for s in sizes:
        offs.append(offs[-1] + s)
    part = lambda i: w[:, offs[i]:offs[i + 1]]
    gq, gk, gv, ggate, ga, gb, qa, kva, kr, hq, hf, hi, hgate, gates = (part(i) for i in range(14))
    z = lambda n: jnp.zeros((d, n), w.dtype)
    swap = jnp.concatenate([kr[:, 16:32], kr[:, 0:16], kr[:, 48:64], kr[:, 32:48]], axis=1)
    cols = [gates, gq, gk, gv, ggate, hf, hq, hi, hgate, qa, gb, z(120), kva, kr, swap, ga, z(120)]
    out = jnp.concatenate(cols, axis=1).astype(BF16)
    assert out.shape[1] == P_COLS
    return out.reshape(d, P_COLS // PROJ_TN, PROJ_TN).transpose(1, 0, 2)


def _pack_wq(w):
    w = w.reshape(Q_RANK, HEADS, NOPE + ROPE)
    nope = w[:, :, :NOPE].reshape(Q_RANK, HEADS * NOPE)
    r = w[:, :, NOPE:]
    sw = jnp.concatenate([r[..., 16:32], r[..., 0:16], r[..., 48:64], r[..., 32:48]], axis=-1)
    rope = jnp.concatenate([r, sw], axis=-1).reshape(Q_RANK, HEADS * 2 * ROPE)
    return jnp.concatenate([nope, rope], axis=1).astype(BF16)


def _pack_wkv(w):
    w = w.reshape(KV_RANK, HEADS, NOPE + HD)
    return jnp.concatenate([w[:, :, :NOPE].reshape(KV_RANK, -1), w[:, :, NOPE:].reshape(KV_RANK, -1)],
                           axis=1).astype(BF16)


def _rope_table(nb, n_seq, n_ctx):
    nf = ROPE // 4
    rows = n_seq // GRID_W
    rpos = jnp.repeat(jnp.arange(rows, dtype=F32), GRID_W)
    cpos = jnp.tile(jnp.arange(GRID_W, dtype=F32), rows)
    inv = ROPE_BASE ** (-jnp.arange(nf, dtype=F32) / nf)
    ar, ac = rpos[:, None] * inv, cpos[:, None] * inv
    cos = jnp.concatenate([jnp.cos(ar), jnp.cos(ar), jnp.cos(ac), jnp.cos(ac)], axis=1)
    sin = jnp.concatenate([-jnp.sin(ar), jnp.sin(ar), -jnp.sin(ac), jnp.sin(ac)], axis=1)
    lat = jnp.tile(jnp.concatenate([cos, sin], axis=1), (nb, 1))
    ctx = jnp.concatenate([jnp.ones((nb * n_ctx, ROPE), F32), jnp.zeros((nb * n_ctx, ROPE), F32)], axis=1)
    return jnp.concatenate([lat, ctx], axis=0)


def kernel(x, c, ctx, c_ctx, w_ada, b_ada, norm_w, ffn_w_in, ffn_w_out, w_in, gdn_conv, gdn_a_log, gdn_dt_bias, gdn_norm, mla_q_norm, mla_kv_norm, mla_w_q_b, mla_w_kv_b, hg_lb_logits, hg_norm, w_branch, w_out):
    nb, n_seq, d = x.shape
    n_ctx = ctx.shape[1]
    depth = w_ada.shape[0]
    lat_rows, rows = nb * n_seq, nb * (n_seq + n_ctx)
    assert nb + 1 <= 8 and n_seq % TM == 0 and n_ctx % TM_S == 0 and (nb * n_ctx) % TM == 0
    assert n_seq % n_ctx == 0 and lat_rows % n_ctx == 0

    cc = jnp.concatenate([c, c_ctx[None], jnp.zeros((8 - nb - 1, d), F32)], axis=0)
    mods = _ada(cc, w_ada, b_ada)[:, :nb + 1].reshape(depth, nb + 1, 9, 1, d).transpose(0, 2, 1, 3, 4)
    cs = _rope_table(nb, n_seq, n_ctx)
    h = jnp.concatenate([x.reshape(lat_rows, d), ctx.reshape(nb * n_ctx, d)], axis=0)

    for l in range(depth):
        last = l == depth - 1
        md, nw = mods[l], norm_w[l][:, None, :]
        wi = [ffn_w_in[l, j].astype(BF16) for j in range(2)]
        wo = [ffn_w_out[l, j].astype(BF16) for j in range(2)]

        h, u = _ffn(h, md[0:3], nw[0:2], wi[0], wo[0], rows, n_seq, nb, mixer_mods=(nw[2], md[3], md[4]))
        p = _proj(u, _pack_w_in(w_in[l]))

        g_ops = _gdn_pre(p, gdn_conv[l], gdn_a_log[l], gdn_dt_bias[l], nb, n_seq, n_ctx)
        g_of, g_ob = _dir_scan(_gdn_scan_kernel, g_ops[:5], g_ops[5], nb, n_seq, n_ctx,
                               pltpu.VMEM((2, HEADS, HD, HD), F32), "gdn_scan")

        hoi, hqg, hkd, hvb, hdec = _hg_pre(p, hg_lb_logits, l)
        h_of, h_ob = _dir_scan(_hg_scan_kernel, (hoi, hqg, hkd, hvb), hdec, nb, n_seq, n_ctx,
                               pltpu.VMEM((2, HEADS, HD, HD), F32), "hgrn2_scan")

        q, k, v = _mla_prep(p, cs, mla_q_norm[l][None], mla_kv_norm[l][None],
                            _pack_wq(mla_w_q_b[l]), _pack_wkv(mla_w_kv_b[l]), nb, n_seq, n_ctx)
        yb = _attention(q, k, v, None, nb, n_seq, n_ctx, rows, ctx_only=False)
        if not last:
            yb = _attention(q, k, v, yb, nb, n_seq, n_ctx, rows, ctx_only=True)

        out_rows = lat_rows if last else rows
        h = _merge(h, g_of, g_ob, yb, h_of, h_ob, p, gdn_norm[l][None], hg_norm[l][None], md[5], nw[3],
                   w_branch[l].astype(BF16), w_out[l].astype(BF16), out_rows, n_seq, nb)
        h = _ffn(h, md[6:9], nw[4:6], wi[1], wo[1], out_rows, n_seq, nb)
    return h.reshape(nb, n_seq, d)
```

```python
import functools

import jax
import jax.numpy as jnp
from jax import lax
from jax.experimental import pallas as pl
from jax.experimental.pallas import tpu as pltpu

F32 = jnp.float32
BF16 = jnp.bfloat16
EPS = 1e-6
LB_FLOOR = 1e-30
GRID_W = 64
ROPE_BASE = 10000.0

D_FF = 2816
HEADS = 4
HD = 128
ROPE = 64
NOPE = 128
Q_RANK = 384
KV_RANK = 256
CHUNK = 64
CPS = 4
SUB = 16
BW = HEADS * HD

TM = 512
TM_S = 256
FF_CK = 256
VMEM_LIMIT = 48 * 1024 * 1024
MAX_JUMP = 64.0

P_GATES = 0
P_GQKV = 3072
P_GGATE = 4608
P_HF = 5120
P_HQ = 6144
P_HI = 6656
P_HGATE = 7168
P_QA = 7680
P_GB = 8064
P_KVA = 8192
P_KR = 8448
P_GA = 8576
P_COLS = 8704
PROJ_TN = 2176

NT = (((1,), (1,)), ((), ()))
TN = (((0,), (0,)), ((), ()))


def _dot(a, b):
    return jnp.dot(a, b, preferred_element_type=F32)


def _dot_nt(a, b):
    return lax.dot_general(a, b, NT, preferred_element_type=F32)


def _dot_tn(a, b):
    return lax.dot_general(a, b, TN, preferred_element_type=F32)


def _sigmoid(x):
    return 1.0 / (1.0 + jnp.exp(-x))


def _silu(x):
    return x * _sigmoid(x)


def _softplus(x):
    return jnp.maximum(x, 0.0) + jnp.log(1.0 + jnp.exp(-jnp.abs(x)))


def _rms(x, w):
    return x * lax.rsqrt(jnp.mean(x * x, axis=-1, keepdims=True) + EPS) * w


def _resident(shape):
    zeros = (0,) * len(shape)
    return pl.BlockSpec(shape, lambda *_: zeros, pipeline_mode=pl.Buffered(1))


def _params(*sem):
    return pltpu.CompilerParams(dimension_semantics=sem, vmem_limit_bytes=VMEM_LIMIT)


def _ada_kernel(c_ref, w_ref, b_ref, o_ref):
    s = _silu(c_ref[...])
    o_ref[...] = _dot(s.astype(BF16), w_ref[...].astype(BF16)) + b_ref[...]


def _ada(cc, w_ada, b_ada):
    depth, d, nm = w_ada.shape
    tn = 1024
    return pl.pallas_call(
        _ada_kernel,
        grid=(depth, nm // tn),
        in_specs=[pl.BlockSpec((8, d), lambda l, j: (0, 0)),
                  pl.BlockSpec((None, d, tn), lambda l, j: (l, 0, j)),
                  pl.BlockSpec((None, 1, tn), lambda l, j: (l, 0, j))],
        out_specs=pl.BlockSpec((None, 8, tn), lambda l, j: (l, 0, j)),
        out_shape=jax.ShapeDtypeStruct((depth, 8, nm), F32),
        compiler_params=_params("parallel", "parallel"),
        name="ada_mod",
    )(cc, w_ada, b_ada.reshape(depth, 1, nm))


def _ffn_kernel(x_ref, sh_ref, sc_ref, gt_ref, prew_ref, postw_ref, wi_ref, wo_ref, *rest, emit_u):
    if emit_u:
        nw_ref, sh2_ref, sc2_ref, o_ref, u_ref = rest
    else:
        (o_ref,) = rest
    x = x_ref[...]
    hn = (_rms(x, prew_ref[...]) * (1.0 + sc_ref[...]) + sh_ref[...]).astype(BF16)
    acc = jnp.zeros(x.shape, F32)
    for c in range(D_FF // FF_CK):
        g = _dot(hn, wi_ref[:, c * FF_CK:(c + 1) * FF_CK])
        u = _dot(hn, wi_ref[:, D_FF + c * FF_CK:D_FF + (c + 1) * FF_CK])
        a = (_silu(g) * u).astype(BF16)
        acc = acc + _dot(a, wo_ref[c * FF_CK:(c + 1) * FF_CK, :])
    out = x + 0.5 * gt_ref[...] * _rms(acc, postw_ref[...])
    o_ref[...] = out
    if emit_u:
        u_ref[...] = (_rms(out, nw_ref[...]) * (1.0 + sc2_ref[...]) + sh2_ref[...]).astype(BF16)


def _ffn(h, mods, nw, wi, wo, n_rows, n_seq, nb, mixer_mods=None):
    d = h.shape[1]
    emit_u = mixer_mods is not None
    row = lambda i: (i, 0)
    mod = lambda i: (jnp.minimum((i * TM) // n_seq, nb), 0, 0)
    vec = pl.BlockSpec((1, d), lambda i: (0, 0))
    mspec = pl.BlockSpec((None, 1, d), mod)
    in_specs = [pl.BlockSpec((TM, d), row), mspec, mspec, mspec, vec, vec,
                _resident(wi.shape), _resident(wo.shape)]
    args = [h, mods[0], mods[1], mods[2], nw[0], nw[1], wi, wo]
    out_specs = pl.BlockSpec((TM, d), row)
    out_shape = jax.ShapeDtypeStruct((n_rows, d), F32)
    if emit_u:
        in_specs += [vec, mspec, mspec]
        args += list(mixer_mods)
        out_specs = (out_specs, pl.BlockSpec((TM, d), row))
        out_shape = (out_shape, jax.ShapeDtypeStruct((n_rows, d), BF16))
    return pl.pallas_call(
        functools.partial(_ffn_kernel, emit_u=emit_u),
        grid=(n_rows // TM,),
        in_specs=in_specs, out_specs=out_specs, out_shape=out_shape,
        compiler_params=_params("parallel"),
        name="ffn_sublayer",
    )(*args)


def _proj_kernel(x_ref, w_ref, o_ref):
    o_ref[...] = _dot(x_ref[...], w_ref[pl.program_id(1)])


def _proj(u, w):
    t, d = u.shape
    nj, _, tn = w.shape
    return pl.pallas_call(
        _proj_kernel,
        grid=(t // TM, nj),
        in_specs=[pl.BlockSpec((TM, d), lambda i, j: (i, 0)), _resident(w.shape)],
        out_specs=pl.BlockSpec((TM, tn), lambda i, j: (i, j)),
        out_shape=jax.ShapeDtypeStruct((t, nj * tn), F32),
        compiler_params=_params("parallel", "arbitrary"),
        name="in_proj",
    )(u, w)


def _mla_prep_kernel(qa_ref, kva_ref, kr_ref, cs_ref, qn_ref, kvn_ref, wq_ref, wkv_ref,
                     q_ref, k_ref, v_ref):
    cs = cs_ref[...]
    qa = _rms(qa_ref[:, :Q_RANK], qn_ref[...]).astype(BF16)
    kva = _rms(kva_ref[...], kvn_ref[...]).astype(BF16)
    q2 = _dot(qa, wq_ref[...])
    kv2 = _dot(kva, wkv_ref[...])
    kr = kr_ref[...] * cs
    kr = (kr + pltpu.roll(kr, ROPE, 1))[:, :ROPE].astype(BF16)
    for h in range(HEADS):
        q_ref[h, :, :NOPE] = q2[:, h * NOPE:(h + 1) * NOPE].astype(BF16)
        qr = q2[:, BW + h * 128:BW + (h + 1) * 128] * cs
        q_ref[h, :, NOPE:] = (qr + pltpu.roll(qr, ROPE, 1))[:, :ROPE].astype(BF16)
        k_ref[h, :, :NOPE] = kv2[:, h * NOPE:(h + 1) * NOPE].astype(BF16)
        k_ref[h, :, NOPE:] = kr
        v_ref[h] = kv2[:, BW + h * HD:BW + (h + 1) * HD].astype(BF16)


def _mla_prep(p, cs, qn, kvn, wq, wkv, nb, n_seq, n_ctx):
    t = p.shape[0]
    tm = TM_S
    lat_tiles = nb * n_seq // tm
    per_lat = n_seq // tm
    per_ctx = n_ctx // tm

    def omap(i):
        j = i - lat_tiles
        b = jnp.where(i < lat_tiles, i // per_lat, j // per_ctx)
        blk = jnp.where(i < lat_tiles, i % per_lat, per_lat + j % per_ctx)
        return (b, 0, blk, 0)

    tk = n_seq + n_ctx
    return pl.pallas_call(
        _mla_prep_kernel,
        grid=(t // tm,),
        in_specs=[pl.BlockSpec((tm, 512), lambda i: (i, P_QA // 512)),
                  pl.BlockSpec((tm, KV_RANK), lambda i: (i, P_KVA // KV_RANK)),
                  pl.BlockSpec((tm, 128), lambda i: (i, P_KR // 128)),
                  pl.BlockSpec((tm, 128), lambda i: (i, 0)),
                  pl.BlockSpec((1, Q_RANK), lambda i: (0, 0)),
                  pl.BlockSpec((1, KV_RANK), lambda i: (0, 0)),
                  _resident(wq.shape), _resident(wkv.shape)],
        out_specs=(pl.BlockSpec((None, HEADS, tm, NOPE + ROPE), omap),
                   pl.BlockSpec((None, HEADS, tm, NOPE + ROPE), omap),
                   pl.BlockSpec((None, HEADS, tm, HD), omap)),
        out_shape=(jax.ShapeDtypeStruct((nb, HEADS, tk, NOPE + ROPE), BF16),
                   jax.ShapeDtypeStruct((nb, HEADS, tk, NOPE + ROPE), BF16),
                   jax.ShapeDtypeStruct((nb, HEADS, tk, HD), BF16)),
        compiler_params=_params("parallel"),
        name="mla_prep",
    )(p, p, p, cs, qn, kvn, wq, wkv)


def _attn_kernel(q_ref, k_ref, v_ref, *rest, first, kb, nkb, scale):
    o_ref = rest[-1]
    q = q_ref[...]
    tq = q.shape[0]
    c = scale * 1.4426950408889634

    def block(j):
        sl = pl.ds(pl.multiple_of(j * kb, kb), kb)
        return _dot_nt(q, k_ref[sl, :]) * c, v_ref[sl, :]

    def safe_update(t, v, carry):
        m, l, acc = carry
        m_new = jnp.maximum(m, jnp.max(t, axis=-1, keepdims=True))
        alpha = jnp.exp2(m - m_new)
        pr = jnp.exp2(t - m_new)
        l = alpha * l + jnp.sum(pr, axis=-1, keepdims=True)
        return m_new, l, alpha * acc + _dot(pr.astype(BF16), v)

    def fast_update(j, carry):
        m, l, acc, jump = carry
        t, v = block(j)
        pr = jnp.exp2(t - m)
        bm = jnp.max(t, axis=-1, keepdims=True)
        l = l + jnp.sum(pr, axis=-1, keepdims=True)
        acc = acc + _dot(pr.astype(BF16), v)
        m_new = jnp.maximum(m, bm)
        alpha = jnp.exp2(m - m_new)
        return m_new, l * alpha, acc * alpha, jnp.maximum(jump, bm - m)

    init = (jnp.full((tq, 1), -1e30, F32), jnp.zeros((tq, 1), F32), jnp.zeros((tq, HD), F32))
    fsl = slice(first[0], first[0] + first[1])
    start = safe_update(_dot_nt(q, k_ref[fsl, :]) * c, v_ref[fsl, :], init)
    if nkb == 0:
        o_ref[...] = start[2] / start[1]
        return
    _, l, acc, jump = lax.fori_loop(0, nkb, fast_update, start + (jnp.zeros((tq, 1), F32),),
                                    unroll=_pick_block(nkb, (4, 2, 1)))
    o_ref[...] = acc / l

    @pl.when(jnp.max(jump) > MAX_JUMP)
    def _():
        _, l2, acc2 = lax.fori_loop(0, nkb, lambda j, carry: safe_update(*block(j), carry), start)
        o_ref[...] = acc2 / l2


def _pick_block(n, cands):
    for c in cands:
        if n % c == 0:
            return c
    raise ValueError(f"no block size for {n}")


def _attention(q, k, v, o_prev, nb, n_seq, n_ctx, rows, ctx_only):
    scale = (NOPE + ROPE) ** -0.5
    if ctx_only:
        tq, tk = n_ctx, n_ctx
        koff = n_seq // n_ctx
        grid = (nb, HEADS, 1)
        qmap = lambda b, h, i: (b, h, koff, 0)
        kmap = lambda b, h, i: (b, h, koff, 0)
        omap = lambda b, h, i: (nb * n_seq // n_ctx + b, h)
        first, kb, nkb = (0, n_ctx), n_ctx, 0
    else:
        tq, tk = _pick_block(n_seq, (512, 256)), n_seq + n_ctx
        grid = (nb, HEADS, n_seq // tq)
        qmap = lambda b, h, i: (b, h, i, 0)
        kmap = lambda b, h, i: (b, h, 0, 0)
        omap = lambda b, h, i: (b * (n_seq // tq) + i, h)
        kb = _pick_block(n_seq, (1024, 512))
        first, nkb = (n_seq, n_ctx), n_seq // kb
    in_specs = [pl.BlockSpec((None, None, tq, NOPE + ROPE), qmap),
                pl.BlockSpec((None, None, tk, NOPE + ROPE), kmap),
                pl.BlockSpec((None, None, tk, HD), kmap)]
    args = [q, k, v]
    aliases = {}
    if ctx_only:
        in_specs.append(pl.BlockSpec(memory_space=pl.ANY))
        args.append(o_prev)
        aliases = {3: 0}
    return pl.pallas_call(
        functools.partial(_attn_kernel, first=first, kb=kb, nkb=nkb, scale=scale),
        grid=grid,
        in_specs=in_specs,
        out_specs=pl.BlockSpec((tq, HD), omap),
        out_shape=jax.ShapeDtypeStruct((rows, BW), F32),
        input_output_aliases=aliases,
        compiler_params=_params("parallel", "parallel", "parallel"),
        name="mla_attn_ctx" if ctx_only else "mla_attn",
    )(*args)


def _chunk_pos(g, lat_chunks, ncl, ncc):
    is_lat = g < lat_chunks
    pos = jnp.where(is_lat, g % ncl, (g - lat_chunks) % ncc)
    last = jnp.where(is_lat, pos == ncl - 1, pos == ncc - 1)
    return pos == 0, last


def _scan_chunk(b, s, rev, lat_chunks, ncl, ncc):
    c = jnp.where(s < ncc, s, s - ncc)
    if rev:
        c = jnp.where(s < ncc, ncc - 1 - c, ncl - 1 - c)
    return jnp.where(s < ncc, lat_chunks + b * ncc + c, b * ncl + c)


def _tri_masks(n):
    ri = lax.broadcasted_iota(jnp.int32, (n, n), 0)
    ci = lax.broadcasted_iota(jnp.int32, (n, n), 1)
    return ri, ci


def _split(x):
    hi = x.astype(BF16)
    return hi, (x - hi.astype(F32)).astype(BF16)


def _dot3(a, b):
    (ah, al), (bh, bl) = a, b
    return _dot(ah, bh) + (_dot(ah, bl) + _dot(al, bh))


def _dot_tri(tri, x):
    hi, lo = _split(x)
    lo2 = (x - hi.astype(F32) - lo.astype(F32)).astype(BF16)
    return _dot(tri, hi) + (_dot(tri, lo) + _dot(tri, lo2))


def _unit_tri_inv(mats, ri, ci, eye):
    rb, cb = ri >> 3, ci >> 3
    d8 = [jnp.where(rb == cb, a, 0.0) for a in mats]
    d8s = [_split(d) for d in d8]
    x2 = [_split(_dot3(d, d)) for d in d8s]
    ts = [eye - d for d in d8]
    x4 = [_split(_dot3(x, x)) for x in x2]
    ts = [t + _dot3(_split(t), x) for t, x in zip(ts, x2)]
    ts = [t + _dot3(_split(t), x) for t, x in zip(ts, x4)]
    for _ in range(3):
        same = rb == cb
        rb, cb = rb >> 1, cb >> 1
        off = (rb == cb) & jnp.logical_not(same)
        bs = [jnp.where(off, a, 0.0).astype(BF16) for a in mats]
        tb = [t.astype(BF16) for t in ts]
        ys = [_dot(t, b).astype(BF16) for t, b in zip(tb, bs)]
        ts = [t - _dot(y, tl) for t, y, tl in zip(ts, ys, tb)]
    return ts


def _gdn_pre_kernel(x_ref, prev_ref, next_ref, a_ref, b_ref, cw_ref, alog_ref, dtb_ref,
                    u_ref, w_ref, qg_ref, kd_ref, at_ref, dec_ref, *, lat_chunks, ncl, ncc):
    first, last = _chunk_pos(pl.program_id(0), lat_chunks, ncl, ncc)
    x = x_ref[...]
    xp = jnp.where(first, 0.0, prev_ref[7:8, :])
    xn = jnp.where(last, 0.0, next_ref[0:1, :])
    row = lax.broadcasted_iota(jnp.int32, (CHUNK, 1), 0)
    x_dn = jnp.where(row == 0, xp, pltpu.roll(x, 1, 0))
    x_up = jnp.where(row == CHUNK - 1, xn, pltpu.roll(x, CHUNK - 1, 0))
    cw = cw_ref[...]
    s = _silu(x_dn * cw[0:1] + x * cw[1:2] + x_up * cw[2:3])

    lane = lax.broadcasted_iota(jnp.int32, (1, 128), 1)
    g = -jnp.exp(alog_ref[...]) * _softplus(a_ref[...] + dtb_ref[...])
    beta = _sigmoid(b_ref[...])
    ri, ci = _tri_masks(CHUNK)
    eye_b = ri == ci
    eye = eye_b.astype(F32)
    low = (ri >= ci).astype(BF16)
    upp = (ri <= ci).astype(BF16)
    gc = jnp.where(lane < HEADS, _dot_tri(low, g), _dot_tri(upp, g))
    g_last = jnp.where(lane < HEADS, gc[CHUNK - 1:CHUNK], gc[0:1])
    dec_ref[...] = jnp.exp(g_last)

    qn, kn, vs, qk, kk = [], [], [], [], []
    for h in range(HEADS):
        qh = s[:, h * HD:(h + 1) * HD]
        kh = s[:, BW + h * HD:BW + (h + 1) * HD]
        vs.append(s[:, 2 * BW + h * HD:2 * BW + (h + 1) * HD])
        qn.append(qh * lax.rsqrt(jnp.sum(qh * qh, axis=-1, keepdims=True) + EPS) * HD ** -0.5)
        kn.append(kh * lax.rsqrt(jnp.sum(kh * kh, axis=-1, keepdims=True) + EPS))
        knb = kn[h].astype(BF16)
        qk.append(_dot_nt(qn[h].astype(BF16), knb))
        kk.append(_dot_nt(knb, knb))

    probs = [(d, h) for h in range(HEADS) for d in range(2)]
    mats, rhs = [], []
    for d, h in probs:
        sl = slice(h * HD, (h + 1) * HD)
        dh = d * HEADS + h
        gcc = gc[:, dh:dh + 1]
        bt = beta[:, dh:dh + 1]
        grow = jnp.sum(jnp.where(eye_b, gcc, 0.0), axis=0, keepdims=True)
        incl = (ri >= ci) if d == 0 else (ri <= ci)
        strict = (ri > ci) if d == 0 else (ri < ci)
        dm = jnp.where(incl, jnp.exp(gcc - grow), 0.0)
        eg = jnp.exp(gcc)
        mats.append(jnp.where(strict, bt * kk[h] * dm, 0.0))
        rhs.append(jnp.concatenate([vs[h] * bt, kn[h] * (bt * eg)], axis=1).astype(BF16))
        qg_ref[d, :, sl] = (qn[h] * eg).astype(BF16)
        kd_ref[d, :, sl] = (kn[h] * jnp.exp(g_last[:, dh:dh + 1] - gcc)).astype(BF16)
        at_ref[d, :, h * HD:h * HD + CHUNK] = (qk[h] * dm).astype(BF16)
        at_ref[d, :, h * HD + CHUNK:(h + 1) * HD] = jnp.zeros((CHUNK, HD - CHUNK), BF16)

    for (d, h), t, r in zip(probs, _unit_tri_inv(mats, ri, ci, eye), rhs):
        sl = slice(h * HD, (h + 1) * HD)
        uw = _dot(t.astype(BF16), r)
        u_ref[d, :, sl] = uw[:, :HD]
        w_ref[d, :, sl] = uw[:, HD:].astype(BF16)


def _gdn_pre(p, conv_w, a_log, dt_bias, nb, n_seq, n_ctx):
    t = p.shape[0]
    nc = t // CHUNK
    lat_chunks, ncl, ncc = nb * n_seq // CHUNK, n_seq // CHUNK, n_ctx // CHUNK
    qkv_blk = P_GQKV // (3 * BW)
    pad = lambda v: jnp.pad(v.reshape(1, -1), ((0, 0), (0, 128 - v.size)))
    dir_out = lambda dt: jax.ShapeDtypeStruct((2, t, BW), dt)
    dir_spec = pl.BlockSpec((2, CHUNK, BW), lambda g: (0, g, 0))
    return pl.pallas_call(
        functools.partial(_gdn_pre_kernel, lat_chunks=lat_chunks, ncl=ncl, ncc=ncc),
        grid=(nc,),
        in_specs=[pl.BlockSpec((CHUNK, 3 * BW), lambda g: (g, qkv_blk)),
                  pl.BlockSpec((8, 3 * BW), lambda g: (jnp.maximum(g * 8 - 1, 0), qkv_blk)),
                  pl.BlockSpec((8, 3 * BW), lambda g: (jnp.minimum(g * 8 + 8, t // 8 - 1), qkv_blk)),
                  pl.BlockSpec((CHUNK, 128), lambda g: (g, P_GA // 128)),
                  pl.BlockSpec((CHUNK, 128), lambda g: (g, P_GB // 128)),
                  pl.BlockSpec((3, 3 * BW), lambda g: (0, 0)),
                  pl.BlockSpec((1, 128), lambda g: (0, 0)),
                  pl.BlockSpec((1, 128), lambda g: (0, 0))],
        out_specs=(dir_spec, dir_spec, dir_spec, dir_spec, dir_spec,
                   pl.BlockSpec((None, 1, 128), lambda g: (g, 0, 0))),
        out_shape=(dir_out(F32), dir_out(BF16), dir_out(BF16), dir_out(BF16), dir_out(BF16),
                   jax.ShapeDtypeStruct((nc, 1, 128), F32)),
        compiler_params=_params("parallel"),
        name="gdn_chunk_prep",
    )(p, p, p, p, p, conv_w, pad(a_log), pad(dt_bias))


def _scan_rows(d, c):
    cc = c if d == 0 else CPS - 1 - c
    return slice(cc * CHUNK, (cc + 1) * CHUNK), cc


def _gdn_scan_kernel(*refs):
    ins, (of_ref, ob_ref, s_ref) = refs[:12], refs[12:]

    @pl.when(pl.program_id(1) == 0)
    def _():
        s_ref[...] = jnp.zeros(s_ref.shape, F32)

    chains = [(d, h) for d in range(2) for h in range(HEADS)]
    outs = (of_ref, ob_ref)
    st = [s_ref[d, h] for d, h in chains]
    for c in range(CPS):
        sb = [s.astype(BF16) for s in st]
        vn = []
        for (d, h), s in zip(chains, sb):
            u_ref, w_ref = ins[6 * d], ins[6 * d + 1]
            rows, _ = _scan_rows(d, c)
            sl = slice(h * HD, (h + 1) * HD)
            vn.append((u_ref[rows, sl] - _dot(w_ref[rows, sl], s)).astype(BF16))
        for i, (d, h) in enumerate(chains):
            qg_ref, kd_ref, at_ref, dec_ref = ins[6 * d + 2:6 * d + 6]
            rows, cc = _scan_rows(d, c)
            sl = slice(h * HD, (h + 1) * HD)
            outs[d][rows, sl] = _dot(qg_ref[rows, sl], sb[i]) + _dot(at_ref[rows, h * HD:h * HD + CHUNK], vn[i])
            dec = dec_ref[cc, 0:1, d * HEADS + h:d * HEADS + h + 1]
            st[i] = st[i] * dec + _dot_tn(kd_ref[rows, sl], vn[i])
    for (d, h), s in zip(chains, st):
        s_ref[d, h] = s


def _dir_scan(kernel, arrays, dec, nb, n_seq, n_ctx, scratch, name):
    t = arrays[0].shape[1]
    step = CPS * CHUNK
    assert n_seq % step == 0 and n_ctx % step == 0
    lat_blocks, ncl, ncc = nb * n_seq // step, n_seq // step, n_ctx // step
    in_specs, args = [], []
    for d in range(2):
        cmap = functools.partial(_scan_chunk, rev=bool(d), lat_chunks=lat_blocks, ncl=ncl, ncc=ncc)
        for arr in arrays:
            if arr.ndim == 3:
                in_specs.append(pl.BlockSpec((None, step, BW), lambda b, s, d=d, cmap=cmap: (d, cmap(b, s), 0)))
            else:
                in_specs.append(pl.BlockSpec((step, BW), lambda b, s, cmap=cmap: (cmap(b, s), 0)))
            args.append(arr)
        if dec.ndim == 3:
            in_specs.append(pl.BlockSpec((CPS, 1, dec.shape[-1]), lambda b, s, cmap=cmap: (cmap(b, s), 0, 0)))
        else:
            in_specs.append(pl.BlockSpec((None, CPS, 1, dec.shape[-1]),
                                         lambda b, s, d=d, cmap=cmap: (d, cmap(b, s), 0, 0)))
        args.append(dec)
    fmap = functools.partial(_scan_chunk, rev=False, lat_chunks=lat_blocks, ncl=ncl, ncc=ncc)
    bmap = functools.partial(_scan_chunk, rev=True, lat_chunks=lat_blocks, ncl=ncl, ncc=ncc)
    return pl.pallas_call(
        kernel,
        grid=(nb, ncc + ncl),
        in_specs=in_specs,
        out_specs=(pl.BlockSpec((step, BW), lambda b, s: (fmap(b, s), 0)),
                   pl.BlockSpec((step, BW), lambda b, s: (bmap(b, s), 0))),
        out_shape=(jax.ShapeDtypeStruct((t, BW), F32), jax.ShapeDtypeStruct((t, BW), F32)),
        scratch_shapes=[scratch],
        compiler_params=_params("parallel", "arbitrary"),
        name=name,
    )(*args)


def _hg_scores(q, k, gc, rev):
    nblk = CHUNK // SUB
    lane = lax.broadcasted_iota(jnp.int32, (1, CHUNK), 1)
    row8 = lax.broadcasted_iota(jnp.int32, (8, 1), 0)
    blk = lambda x, b: x[b * SUB:(b + 1) * SUB]
    ref_row = lambda b: b * SUB + (0 if rev else SUB - 1)
    refs = [gc[ref_row(b):ref_row(b) + 1] for b in range(nblk)]
    kt = jnp.concatenate([blk(k, b) * jnp.exp(refs[b] - blk(gc, b)) for b in range(nblk)], axis=0)
    srcs = list(range(nblk - 1, 0, -1)) if rev else list(range(nblk - 1))
    rows_of = (lambda j: slice(0, j * SUB)) if rev else (lambda j: slice((j + 1) * SUB, CHUNK))
    qt = jnp.concatenate([q[rows_of(j)] * jnp.exp(gc[rows_of(j)] - refs[j]) for j in srcs], axis=0)
    seg_off, off = {}, 0
    for j in srcs:
        seg_off[j] = off
        off += (j if rev else nblk - 1 - j) * SUB
    ktb, qtb = kt.astype(BF16), qt.astype(BF16)
    cross = [_dot_nt(qtb[:, h * HD:(h + 1) * HD], ktb[:, h * HD:(h + 1) * HD]) for h in range(HEADS)]

    acc = [[jnp.zeros((8, CHUNK), F32) for _ in range(CHUNK // 8)] for _ in range(HEADS)]
    for j in range(CHUNK):
        b, jj = j // SUB, j % SUB
        gj, kj = gc[j:j + 1], k[j:j + 1]
        for rg in range(SUB // 8):
            lo, hi = 8 * rg, 8 * rg + 7
            if (hi < jj and not rev) or (lo > jj and rev):
                continue
            r0 = b * SUB + lo
            w = jnp.exp(gc[r0:r0 + 8] - gj)
            if not ((lo >= jj and not rev) or (hi <= jj and rev)):
                w = jnp.where((row8 + lo >= jj) if not rev else (row8 + lo <= jj), w, 0.0)
            tt = q[r0:r0 + 8] * w * kj
            g8 = r0 // 8
            for h in range(HEADS):
                col = jnp.sum(tt[:, h * HD:(h + 1) * HD], axis=-1, keepdims=True)
                acc[h][g8] = jnp.where(lane == j, col, acc[h][g8])

    out = []
    for h in range(HEADS):
        groups = []
        for g8 in range(CHUNK // 8):
            r0 = 8 * g8
            bi = r0 // SUB
            val = acc[h][g8]
            for j in srcs:
                if (j < bi and not rev) or (j > bi and rev):
                    base = seg_off[j] + (r0 if rev else r0 - (j + 1) * SUB)
                    val = jnp.where((lane >= j * SUB) & (lane < (j + 1) * SUB), cross[h][base:base + 8], val)
            groups.append(val)
        out.append(jnp.concatenate(groups, axis=0))
    return out


def _hg_pre_kernel(q_ref, f_ref, i_ref, lbl_ref, oi_ref, qg_ref, kd_ref, vb_ref, dec_ref, *, layer):
    lbl = lbl_ref[...]
    e = jnp.exp(lbl - jnp.max(lbl, axis=0, keepdims=True))
    sm = e / jnp.sum(e, axis=0, keepdims=True)
    lb_all = sm[0]
    for l in range(1, layer + 1):
        lb_all = lb_all + sm[l]
    lb_all = lb_all - sm[0]
    q = q_ref[...] * HD ** -0.5
    v = i_ref[...].astype(BF16)
    vb_ref[...] = v
    ri, ci = _tri_masks(CHUNK)
    for d in range(2):
        lb = lb_all[d:d + 1]
        f = f_ref[:, d * BW:(d + 1) * BW]
        log_sig = jnp.minimum(f, 0.0) - jnp.log(1.0 + jnp.exp(-jnp.abs(f)))
        x1 = jnp.log(jnp.maximum(lb, LB_FLOOR))
        x2 = jnp.log(1.0 - lb) + log_sig
        log_f = jnp.maximum(x1, x2) + jnp.log(1.0 + jnp.exp(-jnp.abs(x1 - x2)))
        k = (1.0 - lb) * _sigmoid(-f)
        tri = ((ri >= ci) if d == 0 else (ri <= ci)).astype(BF16)
        gc = _dot_tri(tri, log_f)
        g_last = gc[CHUNK - 1:CHUNK] if d == 0 else gc[0:1]
        dec_ref[d] = jnp.exp(g_last)
        qg_ref[d] = (q * jnp.exp(gc)).astype(BF16)
        kd_ref[d] = (k * jnp.exp(g_last - gc)).astype(BF16)
        scores = _hg_scores(q, k, gc, rev=bool(d))
        for h in range(HEADS):
            sl = slice(h * HD, (h + 1) * HD)
            oi_ref[d, :, sl] = _dot(scores[h].astype(BF16), v[:, sl])


def _hg_pre(p, lb_logits, layer):
    t = p.shape[0]
    nc = t // CHUNK
    dir_out = lambda dt: jax.ShapeDtypeStruct((2, t, BW), dt)
    dir_spec = pl.BlockSpec((2, CHUNK, BW), lambda g: (0, g, 0))
    return pl.pallas_call(
        functools.partial(_hg_pre_kernel, layer=layer),
        grid=(nc,),
        in_specs=[pl.BlockSpec((CHUNK, BW), lambda g: (g, P_HQ // BW)),
                  pl.BlockSpec((CHUNK, 2 * BW), lambda g: (g, P_HF // (2 * BW))),
                  pl.BlockSpec((CHUNK, BW), lambda g: (g, P_HI // BW)),
                  pl.BlockSpec(lb_logits.shape, lambda g: (0, 0, 0))],
        out_specs=(dir_spec, dir_spec, dir_spec,
                   pl.BlockSpec((CHUNK, BW), lambda g: (g, 0)),
                   pl.BlockSpec((2, None, 1, BW), lambda g: (0, g, 0, 0))),
        out_shape=(dir_out(F32), dir_out(BF16), dir_out(BF16),
                   jax.ShapeDtypeStruct((t, BW), BF16),
                   jax.ShapeDtypeStruct((2, nc, 1, BW), F32)),
        compiler_params=_params("parallel"),
        name="hgrn2_chunk_prep",
    )(p, p, p, lb_logits)


def _hg_scan_kernel(*refs):
    ins, (of_ref, ob_ref, s_ref) = refs[:10], refs[10:]

    @pl.when(pl.program_id(1) == 0)
    def _():
        s_ref[...] = jnp.zeros(s_ref.shape, F32)

    chains = [(d, h) for d in range(2) for h in range(HEADS)]
    outs = (of_ref, ob_ref)
    incs = []
    for d, h in chains:
        kd_ref, v_ref = ins[5 * d + 2], ins[5 * d + 3]
        sl = slice(h * HD, (h + 1) * HD)
        incs.append([_dot_tn(v_ref[_scan_rows(d, c)[0], sl], kd_ref[_scan_rows(d, c)[0], sl])
                     for c in range(CPS)])
    states = []
    for i, (d, h) in enumerate(chains):
        dec_ref = ins[5 * d + 4]
        sl = slice(h * HD, (h + 1) * HD)
        st = s_ref[d, h]
        seq = []
        for c in range(CPS):
            seq.append(st.astype(BF16))
            st = st * dec_ref[_scan_rows(d, c)[1], :, sl] + incs[i][c]
        s_ref[d, h] = st
        states.append(seq)
    for i, (d, h) in enumerate(chains):
        oi_ref, qg_ref = ins[5 * d], ins[5 * d + 1]
        sl = slice(h * HD, (h + 1) * HD)
        for c in range(CPS):
            rows, _ = _scan_rows(d, c)
            outs[d][rows, sl] = oi_ref[rows, sl] + _dot_nt(qg_ref[rows, sl], states[i][c])


def _merge_kernel(h_ref, gof_ref, gob_ref, gg_ref, yb_ref, hof_ref, hob_ref, hgg_ref, gates_ref,
                  gnorm_ref, hnorm_ref, gt_ref, nw_ref, wb_ref, wo_ref, o_ref):
    def readout(of_ref, ob_ref, gate_ref, norm_ref):
        o = of_ref[...] + ob_ref[...]
        gate = gate_ref[...]
        parts = []
        for h in range(HEADS):
            sl = slice(h * HD, (h + 1) * HD)
            parts.append(_rms(o[:, sl], norm_ref[...]) * _silu(gate[:, sl]))
        return jnp.concatenate(parts, axis=1).astype(BF16)

    ys = (readout(gof_ref, gob_ref, gg_ref, gnorm_ref),
          yb_ref[...].astype(BF16),
          readout(hof_ref, hob_ref, hgg_ref, hnorm_ref))
    d = h_ref.shape[1]
    m = None
    for j in range(3):
        term = _sigmoid(gates_ref[:, j * d:(j + 1) * d]) * _dot(ys[j], wb_ref[j])
        m = term if m is None else m + term
    y = _dot(m.astype(BF16), wo_ref[...])
    o_ref[...] = h_ref[...] + gt_ref[...] * _rms(y, nw_ref[...])


def _merge(h, g_of, g_ob, yb, h_of, h_ob, p, gnorm, hnorm, gate_mod, nw, wb, wo, n_rows, n_seq, nb):
    d = h.shape[1]
    tm = TM_S
    row = lambda i: (i, 0)
    bw_spec = pl.BlockSpec((tm, BW), row)
    pcol = lambda off, width: pl.BlockSpec((tm, width), lambda i: (i, off // width))
    vec = lambda n: pl.BlockSpec((1, n), lambda i: (0, 0))
    return pl.pallas_call(
        _merge_kernel,
        grid=(n_rows // tm,),
        in_specs=[pl.BlockSpec((tm, d), row), bw_spec, bw_spec, pcol(P_GGATE, BW), bw_spec,
                  bw_spec, bw_spec, pcol(P_HGATE, BW), pcol(P_GATES, 3 * d),
                  vec(HD), vec(HD),
                  pl.BlockSpec((None, 1, d), lambda i: (jnp.minimum((i * tm) // n_seq, nb), 0, 0)),
                  vec(d), _resident(wb.shape), _resident(wo.shape)],
        out_specs=pl.BlockSpec((tm, d), row),
        out_shape=jax.ShapeDtypeStruct((n_rows, d), F32),
        compiler_params=_params("parallel"),
        name="branch_merge",
    )(h, g_of, g_ob, p, yb, h_of, h_ob, p, p, gnorm, hnorm, gate_mod, nw, wb, wo)


def _pack_w_in(w):
    d = w.shape[0]
    sizes = (BW, BW, BW, BW, 2 * HEADS, 2 * HEADS, Q_RANK, KV_RANK, ROPE, BW, 2 * BW, BW, BW, 3 * d)
    offs = [0]
    for s in sizes:
        offs.append(offs[-1] + s)
    part = lambda i: w[:, offs[i]:offs[i + 1]]
    gq, gk, gv, ggate, ga, gb, qa, kva, kr, hq, hf, hi, hgate, gates = (part(i) for i in range(14))
    z = lambda n: jnp.zeros((d, n), w.dtype)
    swap = jnp.concatenate([kr[:, 16:32], kr[:, 0:16], kr[:, 48:64], kr[:, 32:48]], axis=1)
    cols = [gates, gq, gk, gv, ggate, hf, hq, hi, hgate, qa, gb, z(120), kva, kr, swap, ga, z(120)]
    out = jnp.concatenate(cols, axis=1).astype(BF16)
    assert out.shape[1] == P_COLS
    return out.reshape(d, P_COLS // PROJ_TN, PROJ_TN).transpose(1, 0, 2)


def _pack_wq(w):
    w = w.reshape(Q_RANK, HEADS, NOPE + ROPE)
    nope = w[:, :, :NOPE].reshape(Q_RANK, HEADS * NOPE)
    r = w[:, :, NOPE:]
    sw = jnp.concatenate([r[..., 16:32], r[..., 0:16], r[..., 48:64], r[..., 32:48]], axis=-1)
    rope = jnp.concatenate([r, sw], axis=-1).reshape(Q_RANK, HEADS * 2 * ROPE)
    return jnp.concatenate([nope, rope], axis=1).astype(BF16)


def _pack_wkv(w):
    w = w.reshape(KV_RANK, HEADS, NOPE + HD)
    return jnp.concatenate([w[:, :, :NOPE].reshape(KV_RANK, -1), w[:, :, NOPE:].reshape(KV_RANK, -1)],
                           axis=1).astype(BF16)


def _rope_table(nb, n_seq, n_ctx):
    nf = ROPE // 4
    rows = n_seq // GRID_W
    rpos = jnp.repeat(jnp.arange(rows, dtype=F32), GRID_W)
    cpos = jnp.tile(jnp.arange(GRID_W, dtype=F32), rows)
    inv = ROPE_BASE ** (-jnp.arange(nf, dtype=F32) / nf)
    ar, ac = rpos[:, None] * inv, cpos[:, None] * inv
    cos = jnp.concatenate([jnp.cos(ar), jnp.cos(ar), jnp.cos(ac), jnp.cos(ac)], axis=1)
    sin = jnp.concatenate([-jnp.sin(ar), jnp.sin(ar), -jnp.sin(ac), jnp.sin(ac)], axis=1)
    lat = jnp.tile(jnp.concatenate([cos, sin], axis=1), (nb, 1))
    ctx = jnp.concatenate([jnp.ones((nb * n_ctx, ROPE), F32), jnp.zeros((nb * n_ctx, ROPE), F32)], axis=1)
    return jnp.concatenate([lat, ctx], axis=0)


def kernel(x, c, ctx, c_ctx, w_ada, b_ada, norm_w, ffn_w_in, ffn_w_out, w_in, gdn_conv, gdn_a_log, gdn_dt_bias, gdn_norm, mla_q_norm, mla_kv_norm, mla_w_q_b, mla_w_kv_b, hg_lb_logits, hg_norm, w_branch, w_out):
    nb, n_seq, d = x.shape
    n_ctx = ctx.shape[1]
    depth = w_ada.shape[0]
    lat_rows, rows = nb * n_seq, nb * (n_seq + n_ctx)
    assert nb + 1 <= 8 and n_seq % TM == 0 and n_ctx % TM_S == 0 and (nb * n_ctx) % TM == 0
    assert n_seq % n_ctx == 0 and lat_rows % n_ctx == 0

    cc = jnp.concatenate([c, c_ctx[None], jnp.zeros((8 - nb - 1, d), F32)], axis=0)
    mods = _ada(cc, w_ada, b_ada)[:, :nb + 1].reshape(depth, nb + 1, 9, 1, d).transpose(0, 2, 1, 3, 4)
    cs = _rope_table(nb, n_seq, n_ctx)
    h = jnp.concatenate([x.reshape(lat_rows, d), ctx.reshape(nb * n_ctx, d)], axis=0)

    for l in range(depth):
        last = l == depth - 1
        md, nw = mods[l], norm_w[l][:, None, :]
        wi = [ffn_w_in[l, j].astype(BF16) for j in range(2)]
        wo = [ffn_w_out[l, j].astype(BF16) for j in range(2)]

        h, u = _ffn(h, md[0:3], nw[0:2], wi[0], wo[0], rows, n_seq, nb, mixer_mods=(nw[2], md[3], md[4]))
        p = _proj(u, _pack_w_in(w_in[l]))

        g_ops = _gdn_pre(p, gdn_conv[l], gdn_a_log[l], gdn_dt_bias[l], nb, n_seq, n_ctx)
        g_of, g_ob = _dir_scan(_gdn_scan_kernel, g_ops[:5], g_ops[5], nb, n_seq, n_ctx,
                               pltpu.VMEM((2, HEADS, HD, HD), F32), "gdn_scan")

        hoi, hqg, hkd, hvb, hdec = _hg_pre(p, hg_lb_logits, l)
        h_of, h_ob = _dir_scan(_hg_scan_kernel, (hoi, hqg, hkd, hvb), hdec, nb, n_seq, n_ctx,
                               pltpu.VMEM((2, HEADS, HD, HD), F32), "hgrn2_scan")

        q, k, v = _mla_prep(p, cs, mla_q_norm[l][None], mla_kv_norm[l][None],
                            _pack_wq(mla_w_q_b[l]), _pack_wkv(mla_w_kv_b[l]), nb, n_seq, n_ctx)
        yb = _attention(q, k, v, None, nb, n_seq, n_ctx, rows, ctx_only=False)
        if not last:
            yb = _attention(q, k, v, yb, nb, n_seq, n_ctx, rows, ctx_only=True)

        out_rows = lat_rows if last else rows
        h = _merge(h, g_of, g_ob, yb, h_of, h_ob, p, gdn_norm[l][None], hg_norm[l][None], md[5], nw[3],
                   w_branch[l].astype(BF16), w_out[l].astype(BF16), out_rows, n_seq, nb)
        h = _ffn(h, md[6:9], nw[4:6], wi[1], wo[1], out_rows, n_seq, nb)
    return h.reshape(nb, n_seq, d)
```

```python
import functools

import jax
import jax.numpy as jnp
from jax import lax
from jax.experimental import pallas as pl
from jax.experimental.pallas import tpu as pltpu

F32 = jnp.float32
BF16 = jnp.bfloat16
EPS = 1e-6
LB_FLOOR = 1e-30
GRID_W = 64
ROPE_BASE = 10000.0

D_FF = 2816
HEADS = 4
HD = 128
ROPE = 64
NOPE = 128
Q_RANK = 384
KV_RANK = 256
CHUNK = 64
CPS = 4
CPP = 2
SUB = 16
BW = HEADS * HD

TM = 512
TM_S = 256
FF_CK = 256
VMEM_LIMIT = 48 * 1024 * 1024
KSUB = 256
MAX_JUMP = 64.0

P_GATES = 0
P_GQKV = 3072
P_GGATE = 4608
P_HF = 5120
P_HQ = 6144
P_HI = 6656
P_HGATE = 7168
P_QA = 7680
P_GB = 8064
P_KVA = 8192
P_KR = 8448
P_GA = 8576
P_COLS = 8704
PROJ_TN = 2176

NT = (((1,), (1,)), ((), ()))
TN = (((0,), (0,)), ((), ()))


def _dot(a, b):
    return jnp.dot(a, b, preferred_element_type=F32)


def _dot_nt(a, b):
    return lax.dot_general(a, b, NT, preferred_element_type=F32)


def _dot_tn(a, b):
    return lax.dot_general(a, b, TN, preferred_element_type=F32)


def _sigmoid(x):
    return 1.0 / (1.0 + jnp.exp(-x))


def _silu(x):
    return x * _sigmoid(x)


def _softplus(x):
    return jnp.maximum(x, 0.0) + jnp.log(1.0 + jnp.exp(-jnp.abs(x)))


def _rms(x, w):
    return x * lax.rsqrt(jnp.mean(x * x, axis=-1, keepdims=True) + EPS) * w


def _resident(shape):
    zeros = (0,) * len(shape)
    return pl.BlockSpec(shape, lambda *_: zeros, pipeline_mode=pl.Buffered(1))


def _params(*sem):
    return pltpu.CompilerParams(dimension_semantics=sem, vmem_limit_bytes=VMEM_LIMIT)


def _ada_kernel(c_ref, w_ref, b_ref, o_ref):
    s = _silu(c_ref[...])
    o_ref[...] = _dot(s.astype(BF16), w_ref[...].astype(BF16)) + b_ref[...]


def _ada(cc, w_ada, b_ada):
    depth, d, nm = w_ada.shape
    tn = 1024
    return pl.pallas_call(
        _ada_kernel,
        grid=(depth, nm // tn),
        in_specs=[pl.BlockSpec((8, d), lambda l, j: (0, 0)),
                  pl.BlockSpec((None, d, tn), lambda l, j: (l, 0, j)),
                  pl.BlockSpec((None, 1, tn), lambda l, j: (l, 0, j))],
        out_specs=pl.BlockSpec((None, 8, tn), lambda l, j: (l, 0, j)),
        out_shape=jax.ShapeDtypeStruct((depth, 8, nm), F32),
        compiler_params=_params("parallel", "parallel"),
        name="ada_mod",
    )(cc, w_ada, b_ada.reshape(depth, 1, nm))


def _ffn_kernel(x_ref, sh_ref, sc_ref, gt_ref, prew_ref, postw_ref, wi_ref, wo_ref, *rest, emit_u):
    if emit_u:
        nw_ref, sh2_ref, sc2_ref, o_ref, u_ref = rest
    else:
        (o_ref,) = rest
    x = x_ref[...]
    hn = (_rms(x, prew_ref[...]) * (1.0 + sc_ref[...]) + sh_ref[...]).astype(BF16)
    acc = jnp.zeros(x.shape, F32)
    for c in range(D_FF // FF_CK):
        g = _dot(hn, wi_ref[:, c * FF_CK:(c + 1) * FF_CK])
        u = _dot(hn, wi_ref[:, D_FF + c * FF_CK:D_FF + (c + 1) * FF_CK])
        a = (_silu(g) * u).astype(BF16)
        acc = acc + _dot(a, wo_ref[c * FF_CK:(c + 1) * FF_CK, :])
    out = x + 0.5 * gt_ref[...] * _rms(acc, postw_ref[...])
    o_ref[...] = out
    if emit_u:
        u_ref[...] = (_rms(out, nw_ref[...]) * (1.0 + sc2_ref[...]) + sh2_ref[...]).astype(BF16)


def _ffn(h, mods, nw, wi, wo, n_rows, n_seq, nb, mixer_mods=None):
    d = h.shape[1]
    emit_u = mixer_mods is not None
    row = lambda i: (i, 0)
    mod = lambda i: (jnp.minimum((i * TM) // n_seq, nb), 0, 0)
    vec = pl.BlockSpec((1, d), lambda i: (0, 0))
    mspec = pl.BlockSpec((None, 1, d), mod)
    in_specs = [pl.BlockSpec((TM, d), row), mspec, mspec, mspec, vec, vec,
                _resident(wi.shape), _resident(wo.shape)]
    args = [h, mods[0], mods[1], mods[2], nw[0], nw[1], wi, wo]
    out_specs = pl.BlockSpec((TM, d), row)
    out_shape = jax.ShapeDtypeStruct((n_rows, d), F32)
    if emit_u:
        in_specs += [vec, mspec, mspec]
        args += list(mixer_mods)
        out_specs = (out_specs, pl.BlockSpec((TM, d), row))
        out_shape = (out_shape, jax.ShapeDtypeStruct((n_rows, d), BF16))
    return pl.pallas_call(
        functools.partial(_ffn_kernel, emit_u=emit_u),
        grid=(n_rows // TM,),
        in_specs=in_specs, out_specs=out_specs, out_shape=out_shape,
        compiler_params=_params("parallel"),
        name="ffn_sublayer",
    )(*args)


def _proj_kernel(x_ref, w_ref, o_ref):
    o_ref[...] = _dot(x_ref[...], w_ref[pl.program_id(1)])


def _proj(u, w):
    t, d = u.shape
    nj, _, tn = w.shape
    return pl.pallas_call(
        _proj_kernel,
        grid=(t // TM, nj),
        in_specs=[pl.BlockSpec((TM, d), lambda i, j: (i, 0)), _resident(w.shape)],
        out_specs=pl.BlockSpec((TM, tn), lambda i, j: (i, j)),
        out_shape=jax.ShapeDtypeStruct((t, nj * tn), F32),
        compiler_params=_params("parallel", "arbitrary"),
        name="in_proj",
    )(u, w)


def _mla_prep_kernel(qa_ref, kva_ref, kr_ref, cs_ref, qn_ref, kvn_ref, wq_ref, wkv_ref,
                     q_ref, k_ref, v_ref):
    cs = cs_ref[...]
    qa = _rms(qa_ref[:, :Q_RANK], qn_ref[...]).astype(BF16)
    kva = _rms(kva_ref[...], kvn_ref[...]).astype(BF16)
    q2 = _dot(qa, wq_ref[...])
    kv2 = _dot(kva, wkv_ref[...])
    kr = kr_ref[...] * cs
    kr = (kr + pltpu.roll(kr, ROPE, 1))[:, :ROPE].astype(BF16)
    for h in range(HEADS):
        q_ref[h, :, :NOPE] = q2[:, h * NOPE:(h + 1) * NOPE].astype(BF16)
        qr = q2[:, BW + h * 128:BW + (h + 1) * 128] * cs
        q_ref[h, :, NOPE:] = (qr + pltpu.roll(qr, ROPE, 1))[:, :ROPE].astype(BF16)
        k_ref[h, :, :NOPE] = kv2[:, h * NOPE:(h + 1) * NOPE].astype(BF16)
        k_ref[h, :, NOPE:] = kr
        v_ref[h] = kv2[:, BW + h * HD:BW + (h + 1) * HD].astype(BF16)


def _mla_prep(p, cs, qn, kvn, wq, wkv, nb, n_seq, n_ctx):
    t = p.shape[0]
    tm = TM_S
    lat_tiles = nb * n_seq // tm
    per_lat = n_seq // tm
    per_ctx = n_ctx // tm

    def omap(i):
        j = i - lat_tiles
        b = jnp.where(i < lat_tiles, i // per_lat, j // per_ctx)
        blk = jnp.where(i < lat_tiles, i % per_lat, per_lat + j % per_ctx)
        return (b, 0, blk, 0)

    tk = n_seq + n_ctx
    return pl.pallas_call(
        _mla_prep_kernel,
        grid=(t // tm,),
        in_specs=[pl.BlockSpec((tm, 512), lambda i: (i, P_QA // 512)),
                  pl.BlockSpec((tm, KV_RANK), lambda i: (i, P_KVA // KV_RANK)),
                  pl.BlockSpec((tm, 128), lambda i: (i, P_KR // 128)),
                  pl.BlockSpec((tm, 128), lambda i: (i, 0)),
                  pl.BlockSpec((1, Q_RANK), lambda i: (0, 0)),
                  pl.BlockSpec((1, KV_RANK), lambda i: (0, 0)),
                  _resident(wq.shape), _resident(wkv.shape)],
        out_specs=(pl.BlockSpec((None, HEADS, tm, NOPE + ROPE), omap),
                   pl.BlockSpec((None, HEADS, tm, NOPE + ROPE), omap),
                   pl.BlockSpec((None, HEADS, tm, HD), omap)),
        out_shape=(jax.ShapeDtypeStruct((nb, HEADS, tk, NOPE + ROPE), BF16),
                   jax.ShapeDtypeStruct((nb, HEADS, tk, NOPE + ROPE), BF16),
                   jax.ShapeDtypeStruct((nb, HEADS, tk, HD), BF16)),
        compiler_params=_params("parallel"),
        name="mla_prep",
    )(p, p, p, cs, qn, kvn, wq, wkv)


def _attn_kernel(q_ref, k_ref, v_ref, *rest, first, kb, nkb, scale):
    o_ref = rest[-1]
    q = q_ref[...]
    tq = q.shape[0]
    c = scale * 1.4426950408889634

    def block(j):
        sl = pl.ds(pl.multiple_of(j * kb, kb), kb)
        return _dot_nt(q, k_ref[sl, :]) * c, v_ref[sl, :]

    def safe_update(t, v, carry):
        m, l, acc = carry
        m_new = jnp.maximum(m, jnp.max(t, axis=-1, keepdims=True))
        alpha = jnp.exp2(m - m_new)
        pr = jnp.exp2(t - m_new)
        l = alpha * l + jnp.sum(pr, axis=-1, keepdims=True)
        return m_new, l, alpha * acc + _dot(pr.astype(BF16), v)

    def fast_update(j, carry):
        m, l, acc, jump = carry
        bm = None
        for i in range(kb // KSUB):
            sl = pl.ds(pl.multiple_of(j * kb + i * KSUB, KSUB), KSUB)
            t = _dot_nt(q, k_ref[sl, :]) * c
            pr = jnp.exp2(t - m)
            tm = jnp.max(t, axis=-1, keepdims=True)
            bm = tm if bm is None else jnp.maximum(bm, tm)
            l = l + jnp.sum(pr, axis=-1, keepdims=True)
            acc = acc + _dot(pr.astype(BF16), v_ref[sl, :])
        m_new = jnp.maximum(m, bm)
        alpha = jnp.exp2(m - m_new)
        return m_new, l * alpha, acc * alpha, jnp.maximum(jump, bm - m)

    init = (jnp.full((tq, 1), -1e30, F32), jnp.zeros((tq, 1), F32), jnp.zeros((tq, HD), F32))
    fsl = slice(first[0], first[0] + first[1])
    start = safe_update(_dot_nt(q, k_ref[fsl, :]) * c, v_ref[fsl, :], init)
    if nkb == 0:
        o_ref[...] = start[2] / start[1]
        return
    _, l, acc, jump = lax.fori_loop(0, nkb, fast_update, start + (jnp.zeros((tq, 1), F32),),
                                    unroll=_pick_block(nkb, (4, 2, 1)))
    o_ref[...] = acc / l

    @pl.when(jnp.max(jump) > MAX_JUMP)
    def _():
        _, l2, acc2 = lax.fori_loop(0, nkb, lambda j, carry: safe_update(*block(j), carry), start)
        o_ref[...] = acc2 / l2


def _pick_block(n, cands):
    for c in cands:
        if n % c == 0:
            return c
    raise ValueError(f"no block size for {n}")


def _attention(q, k, v, o_prev, nb, n_seq, n_ctx, rows, ctx_only):
    scale = (NOPE + ROPE) ** -0.5
    if ctx_only:
        tq, tk = n_ctx, n_ctx
        koff = n_seq // n_ctx
        grid = (nb, HEADS, 1)
        qmap = lambda b, h, i: (b, h, koff, 0)
        kmap = lambda b, h, i: (b, h, koff, 0)
        omap = lambda b, h, i: (nb * n_seq // n_ctx + b, h)
        first, kb, nkb = (0, n_ctx), n_ctx, 0
    else:
        tq, tk = _pick_block(n_seq, (512, 256)), n_seq + n_ctx
        grid = (nb, HEADS, n_seq // tq)
        qmap = lambda b, h, i: (b, h, i, 0)
        kmap = lambda b, h, i: (b, h, 0, 0)
        omap = lambda b, h, i: (b * (n_seq // tq) + i, h)
        kb = _pick_block(n_seq, (1024, 512))
        first, nkb = (n_seq, n_ctx), n_seq // kb
    in_specs = [pl.BlockSpec((None, None, tq, NOPE + ROPE), qmap),
                pl.BlockSpec((None, None, tk, NOPE + ROPE), kmap),
                pl.BlockSpec((None, None, tk, HD), kmap)]
    args = [q, k, v]
    aliases = {}
    if ctx_only:
        in_specs.append(pl.BlockSpec(memory_space=pl.ANY))
        args.append(o_prev)
        aliases = {3: 0}
    return pl.pallas_call(
        functools.partial(_attn_kernel, first=first, kb=kb, nkb=nkb, scale=scale),
        grid=grid,
        in_specs=in_specs,
        out_specs=pl.BlockSpec((tq, HD), omap),
        out_shape=jax.ShapeDtypeStruct((rows, BW), F32),
        input_output_aliases=aliases,
        compiler_params=_params("parallel", "parallel", "parallel"),
        name="mla_attn_ctx" if ctx_only else "mla_attn",
    )(*args)


def _chunk_pos(g, lat_chunks, ncl, ncc):
    is_lat = g < lat_chunks
    pos = jnp.where(is_lat, g % ncl, (g - lat_chunks) % ncc)
    last = jnp.where(is_lat, pos == ncl - 1, pos == ncc - 1)
    return pos == 0, last


def _scan_chunk(b, s, rev, lat_chunks, ncl, ncc):
    c = jnp.where(s < ncc, s, s - ncc)
    if rev:
        c = jnp.where(s < ncc, ncc - 1 - c, ncl - 1 - c)
    return jnp.where(s < ncc, lat_chunks + b * ncc + c, b * ncl + c)


def _tri_masks(n):
    ri = lax.broadcasted_iota(jnp.int32, (n, n), 0)
    ci = lax.broadcasted_iota(jnp.int32, (n, n), 1)
    return ri, ci


def _split(x):
    hi = x.astype(BF16)
    return hi, (x - hi.astype(F32)).astype(BF16)


def _dot3(a, b):
    (ah, al), (bh, bl) = a, b
    return _dot(ah, bh) + (_dot(ah, bl) + _dot(al, bh))


def _dot_tri(tri, x):
    hi, lo = _split(x)
    lo2 = (x - hi.astype(F32) - lo.astype(F32)).astype(BF16)
    return _dot(tri, hi) + (_dot(tri, lo) + _dot(tri, lo2))


def _unit_tri_inv(mats, ri, ci, eye):
    rb, cb = ri >> 3, ci >> 3
    d8 = [jnp.where(rb == cb, a, 0.0) for a in mats]
    d8s = [_split(d) for d in d8]
    x2 = [_split(_dot3(d, d)) for d in d8s]
    ts = [eye - d for d in d8]
    x4 = [_split(_dot3(x, x)) for x in x2]
    ts = [t + _dot3(_split(t), x) for t, x in zip(ts, x2)]
    ts = [t + _dot3(_split(t), x) for t, x in zip(ts, x4)]
    for _ in range(3):
        same = rb == cb
        rb, cb = rb >> 1, cb >> 1
        off = (rb == cb) & jnp.logical_not(same)
        bs = [jnp.where(off, a, 0.0).astype(BF16) for a in mats]
        tb = [t.astype(BF16) for t in ts]
        ys = [_dot(t, b).astype(BF16) for t, b in zip(tb, bs)]
        ts = [t - _dot(y, tl) for t, y, tl in zip(ts, ys, tb)]
    return ts


def _gdn_pre_kernel(x_ref, prev_ref, next_ref, a_ref, b_ref, cw_ref, alog_ref, dtb_ref,
                    u_ref, w_ref, qg_ref, kd_ref, at_ref, dec_ref, *, lat_chunks, ncl, ncc):
    first, last = _chunk_pos(pl.program_id(0), lat_chunks, ncl, ncc)
    nrow = CPP * CHUNK
    x = x_ref[...]
    xp = jnp.where(first, 0.0, prev_ref[7:8, :])
    xn = jnp.where(last, 0.0, next_ref[0:1, :])
    row = lax.broadcasted_iota(jnp.int32, (nrow, 1), 0)
    x_dn = jnp.where(row == 0, xp, pltpu.roll(x, 1, 0))
    x_up = jnp.where(row == nrow - 1, xn, pltpu.roll(x, nrow - 1, 0))
    cw = cw_ref[...]
    s = _silu(x_dn * cw[0:1] + x * cw[1:2] + x_up * cw[2:3])

    lane = lax.broadcasted_iota(jnp.int32, (1, 128), 1)
    g = -jnp.exp(alog_ref[...]) * _softplus(a_ref[...] + dtb_ref[...])
    beta = _sigmoid(b_ref[...])
    ri, ci = _tri_masks(CHUNK)
    eye_b = ri == ci
    eye = eye_b.astype(F32)
    low = (ri >= ci).astype(BF16)
    upp = (ri <= ci).astype(BF16)
    chunk_rows = [slice(c * CHUNK, (c + 1) * CHUNK) for c in range(CPP)]
    gc, g_last = [], []
    for c, rc in enumerate(chunk_rows):
        gcc = jnp.where(lane < HEADS, _dot_tri(low, g[rc]), _dot_tri(upp, g[rc]))
        gc.append(gcc)
        g_last.append(jnp.where(lane < HEADS, gcc[CHUNK - 1:CHUNK], gcc[0:1]))
        dec_ref[c] = jnp.exp(g_last[c])

    qn, kn, vs = [], [], []
    for h in range(HEADS):
        qh = s[:, h * HD:(h + 1) * HD]
        kh = s[:, BW + h * HD:BW + (h + 1) * HD]
        vs.append(s[:, 2 * BW + h * HD:2 * BW + (h + 1) * HD])
        qn.append(qh * lax.rsqrt(jnp.sum(qh * qh, axis=-1, keepdims=True) + EPS) * HD ** -0.5)
        kn.append(kh * lax.rsqrt(jnp.sum(kh * kh, axis=-1, keepdims=True) + EPS))
    qk = [[None] * HEADS for _ in range(CPP)]
    kk = [[None] * HEADS for _ in range(CPP)]
    for c, rc in enumerate(chunk_rows):
        for h in range(HEADS):
            knb = kn[h][rc].astype(BF16)
            qk[c][h] = _dot_nt(qn[h][rc].astype(BF16), knb)
            kk[c][h] = _dot_nt(knb, knb)

    probs = [(c, d, h) for c in range(CPP) for h in range(HEADS) for d in range(2)]
    mats, rhs = [], []
    for c, d, h in probs:
        rc = chunk_rows[c]
        sl = slice(h * HD, (h + 1) * HD)
        dh = d * HEADS + h
        gcc = gc[c][:, dh:dh + 1]
        bt = beta[rc, dh:dh + 1]
        grow = jnp.sum(jnp.where(eye_b, gcc, 0.0), axis=0, keepdims=True)
        incl = (ri >= ci) if d == 0 else (ri <= ci)
        strict = (ri > ci) if d == 0 else (ri < ci)
        dm = jnp.where(incl, jnp.exp(gcc - grow), 0.0)
        eg = jnp.exp(gcc)
        knc = kn[h][rc]
        mats.append(jnp.where(strict, bt * kk[c][h] * dm, 0.0))
        rhs.append(jnp.concatenate([vs[h][rc] * bt, knc * (bt * eg)], axis=1).astype(BF16))
        qg_ref[d, rc, sl] = (qn[h][rc] * eg).astype(BF16)
        kd_ref[d, rc, sl] = (knc * jnp.exp(g_last[c][:, dh:dh + 1] - gcc)).astype(BF16)
        at_ref[d, rc, h * HD:h * HD + CHUNK] = (qk[c][h] * dm).astype(BF16)
        at_ref[d, rc, h * HD + CHUNK:(h + 1) * HD] = jnp.zeros((CHUNK, HD - CHUNK), BF16)

    for (c, d, h), t, r in zip(probs, _unit_tri_inv(mats, ri, ci, eye), rhs):
        sl = slice(h * HD, (h + 1) * HD)
        uw = _dot(t.astype(BF16), r)
        u_ref[d, chunk_rows[c], sl] = uw[:, :HD]
        w_ref[d, chunk_rows[c], sl] = uw[:, HD:].astype(BF16)


def _gdn_pre(p, conv_w, a_log, dt_bias, nb, n_seq, n_ctx):
    t = p.shape[0]
    nrow = CPP * CHUNK
    assert n_seq % nrow == 0 and n_ctx % nrow == 0
    lat_blocks, nbl, nbc = nb * n_seq // nrow, n_seq // nrow, n_ctx // nrow
    qkv_blk = P_GQKV // (3 * BW)
    pad = lambda v: jnp.pad(v.reshape(1, -1), ((0, 0), (0, 128 - v.size)))
    dir_out = lambda dt: jax.ShapeDtypeStruct((2, t, BW), dt)
    dir_spec = pl.BlockSpec((2, nrow, BW), lambda g: (0, g, 0))
    r8 = nrow // 8
    return pl.pallas_call(
        functools.partial(_gdn_pre_kernel, lat_chunks=lat_blocks, ncl=nbl, ncc=nbc),
        grid=(t // nrow,),
        in_specs=[pl.BlockSpec((nrow, 3 * BW), lambda g: (g, qkv_blk)),
                  pl.BlockSpec((8, 3 * BW), lambda g: (jnp.maximum(g * r8 - 1, 0), qkv_blk)),
                  pl.BlockSpec((8, 3 * BW), lambda g: (jnp.minimum(g * r8 + r8, t // 8 - 1), qkv_blk)),
                  pl.BlockSpec((nrow, 128), lambda g: (g, P_GA // 128)),
                  pl.BlockSpec((nrow, 128), lambda g: (g, P_GB // 128)),
                  pl.BlockSpec((3, 3 * BW), lambda g: (0, 0)),
                  pl.BlockSpec((1, 128), lambda g: (0, 0)),
                  pl.BlockSpec((1, 128), lambda g: (0, 0))],
        out_specs=(dir_spec, dir_spec, dir_spec, dir_spec, dir_spec,
                   pl.BlockSpec((CPP, 1, 128), lambda g: (g, 0, 0))),
        out_shape=(dir_out(F32), dir_out(BF16), dir_out(BF16), dir_out(BF16), dir_out(BF16),
                   jax.ShapeDtypeStruct((t // CHUNK, 1, 128), F32)),
        compiler_params=_params("parallel"),
        name="gdn_chunk_prep",
    )(p, p, p, p, p, conv_w, pad(a_log), pad(dt_bias))


def _scan_rows(d, c):
    cc = c if d == 0 else CPS - 1 - c
    return slice(cc * CHUNK, (cc + 1) * CHUNK), cc


def _gdn_scan_kernel(*refs):
    ins, (of_ref, ob_ref, s_ref) = refs[:12], refs[12:]

    @pl.when(pl.program_id(1) == 0)
    def _():
        s_ref[...] = jnp.zeros(s_ref.shape, F32)

    chains = [(d, h) for d in range(2) for h in range(HEADS)]
    outs = (of_ref, ob_ref)
    st = [s_ref[d, h] for d, h in chains]
    for c in range(CPS):
        sb = [s.astype(BF16) for s in st]
        vn = []
        for (d, h), s in zip(chains, sb):
            u_ref, w_ref = ins[6 * d], ins[6 * d + 1]
            rows, _ = _scan_rows(d, c)
            sl = slice(h * HD, (h + 1) * HD)
            vn.append((u_ref[rows, sl] - _dot(w_ref[rows, sl], s)).astype(BF16))
        for i, (d, h) in enumerate(chains):
            qg_ref, kd_ref, at_ref, dec_ref = ins[6 * d + 2:6 * d + 6]
            rows, cc = _scan_rows(d, c)
            sl = slice(h * HD, (h + 1) * HD)
            outs[d][rows, sl] = _dot(qg_ref[rows, sl], sb[i]) + _dot(at_ref[rows, h * HD:h * HD + CHUNK], vn[i])
            dec = dec_ref[cc, 0:1, d * HEADS + h:d * HEADS + h + 1]
            st[i] = st[i] * dec + _dot_tn(kd_ref[rows, sl], vn[i])
    for (d, h), s in zip(chains, st):
        s_ref[d, h] = s


def _dir_scan(kernel, arrays, dec, nb, n_seq, n_ctx, scratch, name):
    t = arrays[0].shape[1]
    step = CPS * CHUNK
    assert n_seq % step == 0 and n_ctx % step == 0
    lat_blocks, ncl, ncc = nb * n_seq // step, n_seq // step, n_ctx // step
    in_specs, args = [], []
    for d in range(2):
        cmap = functools.partial(_scan_chunk, rev=bool(d), lat_chunks=lat_blocks, ncl=ncl, ncc=ncc)
        for arr in arrays:
            if arr.ndim == 3:
                in_specs.append(pl.BlockSpec((None, step, BW), lambda b, s, d=d, cmap=cmap: (d, cmap(b, s), 0)))
            else:
                in_specs.append(pl.BlockSpec((step, BW), lambda b, s, cmap=cmap: (cmap(b, s), 0)))
            args.append(arr)
        if dec.ndim == 3:
            in_specs.append(pl.BlockSpec((CPS, 1, dec.shape[-1]), lambda b, s, cmap=cmap: (cmap(b, s), 0, 0)))
        else:
            in_specs.append(pl.BlockSpec((None, CPS, 1, dec.shape[-1]),
                                         lambda b, s, d=d, cmap=cmap: (d, cmap(b, s), 0, 0)))
        args.append(dec)
    fmap = functools.partial(_scan_chunk, rev=False, lat_chunks=lat_blocks, ncl=ncl, ncc=ncc)
    bmap = functools.partial(_scan_chunk, rev=True, lat_chunks=lat_blocks, ncl=ncl, ncc=ncc)
    return pl.pallas_call(
        kernel,
        grid=(nb, ncc + ncl),
        in_specs=in_specs,
        out_specs=(pl.BlockSpec((step, BW), lambda b, s: (fmap(b, s), 0)),
                   pl.BlockSpec((step, BW), lambda b, s: (bmap(b, s), 0))),
        out_shape=(jax.ShapeDtypeStruct((t, BW), F32), jax.ShapeDtypeStruct((t, BW), F32)),
        scratch_shapes=[scratch],
        compiler_params=_params("parallel", "arbitrary"),
        name=name,
    )(*args)


def _hg_scores(q, k, gc, rev):
    nblk = CHUNK // SUB
    lane = lax.broadcasted_iota(jnp.int32, (1, CHUNK), 1)
    row8 = lax.broadcasted_iota(jnp.int32, (8, 1), 0)
    blk = lambda x, b: x[b * SUB:(b + 1) * SUB]
    ref_row = lambda b: b * SUB + (0 if rev else SUB - 1)
    refs = [gc[ref_row(b):ref_row(b) + 1] for b in range(nblk)]
    kt = jnp.concatenate([blk(k, b) * jnp.exp(refs[b] - blk(gc, b)) for b in range(nblk)], axis=0)
    srcs = list(range(nblk - 1, 0, -1)) if rev else list(range(nblk - 1))
    rows_of = (lambda j: slice(0, j * SUB)) if rev else (lambda j: slice((j + 1) * SUB, CHUNK))
    qt = jnp.concatenate([q[rows_of(j)] * jnp.exp(gc[rows_of(j)] - refs[j]) for j in srcs], axis=0)
    seg_off, off = {}, 0
    for j in srcs:
        seg_off[j] = off
        off += (j if rev else nblk - 1 - j) * SUB
    ktb, qtb = kt.astype(BF16), qt.astype(BF16)
    cross = [_dot_nt(qtb[:, h * HD:(h + 1) * HD], ktb[:, h * HD:(h + 1) * HD]) for h in range(HEADS)]

    acc = [[jnp.zeros((8, CHUNK), F32) for _ in range(CHUNK // 8)] for _ in range(HEADS)]
    for j in range(CHUNK):
        b, jj = j // SUB, j % SUB
        gj, kj = gc[j:j + 1], k[j:j + 1]
        for rg in range(SUB // 8):
            lo, hi = 8 * rg, 8 * rg + 7
            if (hi < jj and not rev) or (lo > jj and rev):
                continue
            r0 = b * SUB + lo
            w = jnp.exp(gc[r0:r0 + 8] - gj)
            if not ((lo >= jj and not rev) or (hi <= jj and rev)):
                w = jnp.where((row8 + lo >= jj) if not rev else (row8 + lo <= jj), w, 0.0)
            tt = q[r0:r0 + 8] * w * kj
            g8 = r0 // 8
            for h in range(HEADS):
                col = jnp.sum(tt[:, h * HD:(h + 1) * HD], axis=-1, keepdims=True)
                acc[h][g8] = jnp.where(lane == j, col, acc[h][g8])

    out = []
    for h in range(HEADS):
        groups = []
        for g8 in range(CHUNK // 8):
            r0 = 8 * g8
            bi = r0 // SUB
            val = acc[h][g8]
            for j in srcs:
                if (j < bi and not rev) or (j > bi and rev):
                    base = seg_off[j] + (r0 if rev else r0 - (j + 1) * SUB)
                    val = jnp.where((lane >= j * SUB) & (lane < (j + 1) * SUB), cross[h][base:base + 8], val)
            groups.append(val)
        out.append(jnp.concatenate(groups, axis=0))
    return out


def _hg_pre_kernel(q_ref, f_ref, i_ref, lbl_ref, oi_ref, qg_ref, kd_ref, vb_ref, dec_ref, *, layer):
    lbl = lbl_ref[...]
    e = jnp.exp(lbl - jnp.max(lbl, axis=0, keepdims=True))
    sm = e / jnp.sum(e, axis=0, keepdims=True)
    lb_all = sm[0]
    for l in range(1, layer + 1):
        lb_all = lb_all + sm[l]
    lb_all = lb_all - sm[0]
    q = q_ref[...] * HD ** -0.5
    v = i_ref[...].astype(BF16)
    vb_ref[...] = v
    ri, ci = _tri_masks(CHUNK)
    for d in range(2):
        lb = lb_all[d:d + 1]
        f = f_ref[:, d * BW:(d + 1) * BW]
        log_sig = jnp.minimum(f, 0.0) - jnp.log(1.0 + jnp.exp(-jnp.abs(f)))
        x1 = jnp.log(jnp.maximum(lb, LB_FLOOR))
        x2 = jnp.log(1.0 - lb) + log_sig
        log_f = jnp.maximum(x1, x2) + jnp.log(1.0 + jnp.exp(-jnp.abs(x1 - x2)))
        k = (1.0 - lb) * _sigmoid(-f)
        tri = ((ri >= ci) if d == 0 else (ri <= ci)).astype(BF16)
        gc = _dot_tri(tri, log_f)
        g_last = gc[CHUNK - 1:CHUNK] if d == 0 else gc[0:1]
        dec_ref[d] = jnp.exp(g_last)
        qg_ref[d] = (q * jnp.exp(gc)).astype(BF16)
        kd_ref[d] = (k * jnp.exp(g_last - gc)).astype(BF16)
        scores = _hg_scores(q, k, gc, rev=bool(d))
        for h in range(HEADS):
            sl = slice(h * HD, (h + 1) * HD)
            oi_ref[d, :, sl] = _dot(scores[h].astype(BF16), v[:, sl])


def _hg_pre(p, lb_logits, layer):
    t = p.shape[0]
    nc = t // CHUNK
    dir_out = lambda dt: jax.ShapeDtypeStruct((2, t, BW), dt)
    dir_spec = pl.BlockSpec((2, CHUNK, BW), lambda g: (0, g, 0))
    return pl.pallas_call(
        functools.partial(_hg_pre_kernel, layer=layer),
        grid=(nc,),
        in_specs=[pl.BlockSpec((CHUNK, BW), lambda g: (g, P_HQ // BW)),
                  pl.BlockSpec((CHUNK, 2 * BW), lambda g: (g, P_HF // (2 * BW))),
                  pl.BlockSpec((CHUNK, BW), lambda g: (g, P_HI // BW)),
                  pl.BlockSpec(lb_logits.shape, lambda g: (0, 0, 0))],
        out_specs=(dir_spec, dir_spec, dir_spec,
                   pl.BlockSpec((CHUNK, BW), lambda g: (g, 0)),
                   pl.BlockSpec((2, None, 1, BW), lambda g: (0, g, 0, 0))),
        out_shape=(dir_out(F32), dir_out(BF16), dir_out(BF16),
                   jax.ShapeDtypeStruct((t, BW), BF16),
                   jax.ShapeDtypeStruct((2, nc, 1, BW), F32)),
        compiler_params=_params("parallel"),
        name="hgrn2_chunk_prep",
    )(p, p, p, lb_logits)


def _hg_scan_kernel(*refs):
    ins, (of_ref, ob_ref, s_ref) = refs[:10], refs[10:]

    @pl.when(pl.program_id(1) == 0)
    def _():
        s_ref[...] = jnp.zeros(s_ref.shape, F32)

    chains = [(d, h) for d in range(2) for h in range(HEADS)]
    outs = (of_ref, ob_ref)
    incs = []
    for d, h in chains:
        kd_ref, v_ref = ins[5 * d + 2], ins[5 * d + 3]
        sl = slice(h * HD, (h + 1) * HD)
        incs.append([_dot_tn(v_ref[_scan_rows(d, c)[0], sl], kd_ref[_scan_rows(d, c)[0], sl])
                     for c in range(CPS)])
    states = []
    for i, (d, h) in enumerate(chains):
        dec_ref = ins[5 * d + 4]
        sl = slice(h * HD, (h + 1) * HD)
        st = s_ref[d, h]
        seq = []
        for c in range(CPS):
            seq.append(st.astype(BF16))
            st = st * dec_ref[_scan_rows(d, c)[1], :, sl] + incs[i][c]
        s_ref[d, h] = st
        states.append(seq)
    for i, (d, h) in enumerate(chains):
        oi_ref, qg_ref = ins[5 * d], ins[5 * d + 1]
        sl = slice(h * HD, (h + 1) * HD)
        for c in range(CPS):
            rows, _ = _scan_rows(d, c)
            outs[d][rows, sl] = oi_ref[rows, sl] + _dot_nt(qg_ref[rows, sl], states[i][c])


def _merge_kernel(h_ref, gof_ref, gob_ref, gg_ref, yb_ref, hof_ref, hob_ref, hgg_ref, gates_ref,
                  gnorm_ref, hnorm_ref, gt_ref, nw_ref, wb_ref, wo_ref, o_ref):
    def readout(of_ref, ob_ref, gate_ref, norm_ref):
        o = of_ref[...] + ob_ref[...]
        gate = gate_ref[...]
        parts = []
        for h in range(HEADS):
            sl = slice(h * HD, (h + 1) * HD)
            parts.append(_rms(o[:, sl], norm_ref[...]) * _silu(gate[:, sl]))
        return jnp.concatenate(parts, axis=1).astype(BF16)

    ys = (readout(gof_ref, gob_ref, gg_ref, gnorm_ref),
          yb_ref[...].astype(BF16),
          readout(hof_ref, hob_ref, hgg_ref, hnorm_ref))
    d = h_ref.shape[1]
    m = None
    for j in range(3):
        term = _sigmoid(gates_ref[:, j * d:(j + 1) * d]) * _dot(ys[j], wb_ref[j])
        m = term if m is None else m + term
    y = _dot(m.astype(BF16), wo_ref[...])
    o_ref[...] = h_ref[...] + gt_ref[...] * _rms(y, nw_ref[...])


def _merge(h, g_of, g_ob, yb, h_of, h_ob, p, gnorm, hnorm, gate_mod, nw, wb, wo, n_rows, n_seq, nb):
    d = h.shape[1]
    tm = TM_S
    row = lambda i: (i, 0)
    bw_spec = pl.BlockSpec((tm, BW), row)
    pcol = lambda off, width: pl.BlockSpec((tm, width), lambda i: (i, off // width))
    vec = lambda n: pl.BlockSpec((1, n), lambda i: (0, 0))
    return pl.pallas_call(
        _merge_kernel,
        grid=(n_rows // tm,),
        in_specs=[pl.BlockSpec((tm, d), row), bw_spec, bw_spec, pcol(P_GGATE, BW), bw_spec,
                  bw_spec, bw_spec, pcol(P_HGATE, BW), pcol(P_GATES, 3 * d),
                  vec(HD), vec(HD),
                  pl.BlockSpec((None, 1, d), lambda i: (jnp.minimum((i * tm) // n_seq, nb), 0, 0)),
                  vec(d), _resident(wb.shape), _resident(wo.shape)],
        out_specs=pl.BlockSpec((tm, d), row),
        out_shape=jax.ShapeDtypeStruct((n_rows, d), F32),
        compiler_params=_params("parallel"),
        name="branch_merge",
    )(h, g_of, g_ob, p, yb, h_of, h_ob, p, p, gnorm, hnorm, gate_mod, nw, wb, wo)


def _pack_w_in(w):
    d = w.shape[0]
    sizes = (BW, BW, BW, BW, 2 * HEADS, 2 * HEADS, Q_RANK, KV_RANK, ROPE, BW, 2 * BW, BW, BW, 3 * d)
    offs = [0]
    for s in sizes:
        offs.append(offs[-1] + s)
    part = lambda i: w[:, offs[i]:offs[i + 1]]
    gq, gk, gv, ggate, ga, gb, qa, kva, kr, hq, hf, hi, hgate, gates = (part(i) for i in range(14))
    z = lambda n: jnp.zeros((d, n), w.dtype)
    swap = jnp.concatenate([kr[:, 16:32], kr[:, 0:16], kr[:, 48:64], kr[:, 32:48]], axis=1)
    cols = [gates, gq, gk, gv, ggate, hf, hq, hi, hgate, qa, gb, z(120), kva, kr, swap, ga, z(120)]
    out = jnp.concatenate(cols, axis=1).astype(BF16)
    assert out.shape[1] == P_COLS
    return out.reshape(d, P_COLS // PROJ_TN, PROJ_TN).transpose(1, 0, 2)


def _pack_wq(w):
    w = w.reshape(Q_RANK, HEADS, NOPE + ROPE)
    nope = w[:, :, :NOPE].reshape(Q_RANK, HEADS * NOPE)
    r = w[:, :, NOPE:]
    sw = jnp.concatenate([r[..., 16:32], r[..., 0:16], r[..., 48:64], r[..., 32:48]], axis=-1)
    rope = jnp.concatenate([r, sw], axis=-1).reshape(Q_RANK, HEADS * 2 * ROPE)
    return jnp.concatenate([nope, rope], axis=1).astype(BF16)


def _pack_wkv(w):
    w = w.reshape(KV_RANK, HEADS, NOPE + HD)
    return jnp.concatenate([w[:, :, :NOPE].reshape(KV_RANK, -1), w[:, :, NOPE:].reshape(KV_RANK, -1)],
                           axis=1).astype(BF16)


def _rope_table(nb, n_seq, n_ctx):
    nf = ROPE // 4
    rows = n_seq // GRID_W
    rpos = jnp.repeat(jnp.arange(rows, dtype=F32), GRID_W)
    cpos = jnp.tile(jnp.arange(GRID_W, dtype=F32), rows)
    inv = ROPE_BASE ** (-jnp.arange(nf, dtype=F32) / nf)
    ar, ac = rpos[:, None] * inv, cpos[:, None] * inv
    cos = jnp.concatenate([jnp.cos(ar), jnp.cos(ar), jnp.cos(ac), jnp.cos(ac)], axis=1)
    sin = jnp.concatenate([-jnp.sin(ar), jnp.sin(ar), -jnp.sin(ac), jnp.sin(ac)], axis=1)
    lat = jnp.tile(jnp.concatenate([cos, sin], axis=1), (nb, 1))
    ctx = jnp.concatenate([jnp.ones((nb * n_ctx, ROPE), F32), jnp.zeros((nb * n_ctx, ROPE), F32)], axis=1)
    return jnp.concatenate([lat, ctx], axis=0)


def kernel(x, c, ctx, c_ctx, w_ada, b_ada, norm_w, ffn_w_in, ffn_w_out, w_in, gdn_conv, gdn_a_log, gdn_dt_bias, gdn_norm, mla_q_norm, mla_kv_norm, mla_w_q_b, mla_w_kv_b, hg_lb_logits, hg_norm, w_branch, w_out):
    nb, n_seq, d = x.shape
    n_ctx = ctx.shape[1]
    depth = w_ada.shape[0]
    lat_rows, rows = nb * n_seq, nb * (n_seq + n_ctx)
    assert nb + 1 <= 8 and n_seq % TM == 0 and n_ctx % TM_S == 0 and (nb * n_ctx) % TM == 0
    assert n_seq % n_ctx == 0 and lat_rows % n_ctx == 0

    cc = jnp.concatenate([c, c_ctx[None], jnp.zeros((8 - nb - 1, d), F32)], axis=0)
    mods = _ada(cc, w_ada, b_ada)[:, :nb + 1].reshape(depth, nb + 1, 9, 1, d).transpose(0, 2, 1, 3, 4)
    cs = _rope_table(nb, n_seq, n_ctx)
    h = jnp.concatenate([x.reshape(lat_rows, d), ctx.reshape(nb * n_ctx, d)], axis=0)

    for l in range(depth):
        last = l == depth - 1
        md, nw = mods[l], norm_w[l][:, None, :]
        wi = [ffn_w_in[l, j].astype(BF16) for j in range(2)]
        wo = [ffn_w_out[l, j].astype(BF16) for j in range(2)]

        h, u = _ffn(h, md[0:3], nw[0:2], wi[0], wo[0], rows, n_seq, nb, mixer_mods=(nw[2], md[3], md[4]))
        p = _proj(u, _pack_w_in(w_in[l]))

        g_ops = _gdn_pre(p, gdn_conv[l], gdn_a_log[l], gdn_dt_bias[l], nb, n_seq, n_ctx)
        g_of, g_ob = _dir_scan(_gdn_scan_kernel, g_ops[:5], g_ops[5], nb, n_seq, n_ctx,
                               pltpu.VMEM((2, HEADS, HD, HD), F32), "gdn_scan")

        hoi, hqg, hkd, hvb, hdec = _hg_pre(p, hg_lb_logits, l)
        h_of, h_ob = _dir_scan(_hg_scan_kernel, (hoi, hqg, hkd, hvb), hdec, nb, n_seq, n_ctx,
                               pltpu.VMEM((2, HEADS, HD, HD), F32), "hgrn2_scan")

        q, k, v = _mla_prep(p, cs, mla_q_norm[l][None], mla_kv_norm[l][None],
                            _pack_wq(mla_w_q_b[l]), _pack_wkv(mla_w_kv_b[l]), nb, n_seq, n_ctx)
        yb = _attention(q, k, v, None, nb, n_seq, n_ctx, rows, ctx_only=False)
        if not last:
            yb = _attention(q, k, v, yb, nb, n_seq, n_ctx, rows, ctx_only=True)

        out_rows = lat_rows if last else rows
        h = _merge(h, g_of, g_ob, yb, h_of, h_ob, p, gdn_norm[l][None], hg_norm[l][None], md[5], nw[3],
                   w_branch[l].astype(BF16), w_out[l].astype(BF16), out_rows, n_seq, nb)
        h = _ffn(h, md[6:9], nw[4:6], wi[1], wo[1], out_rows, n_seq, nb)
    return h.reshape(nb, n_seq, d)
```

```python
import functools

import jax
import jax.numpy as jnp
from jax import lax
from jax.experimental import pallas as pl
from jax.experimental.pallas import tpu as pltpu

F32 = jnp.float32
BF16 = jnp.bfloat16
EPS = 1e-6
LB_FLOOR = 1e-30
GRID_W = 64
ROPE_BASE = 10000.0

D_FF = 2816
HEADS = 4
HD = 128
ROPE = 64
NOPE = 128
Q_RANK = 384
KV_RANK = 256
CHUNK = 64
CPS = 4
CPP = 2
HG_CPP = 2
SUB = 16
BW = HEADS * HD

TM = 512
TM_S = 256
FF_CK = 256
VMEM_LIMIT = 48 * 1024 * 1024
KSUB = 256
HG_MAX_DECAY = 60.0
MAX_JUMP = 64.0

P_GATES = 0
P_GQKV = 3072
P_GGATE = 4608
P_HF = 5120
P_HQ = 6144
P_HI = 6656
P_HGATE = 7168
P_QA = 7680
P_GB = 8064
P_KVA = 8192
P_KR = 8448
P_GA = 8576
P_COLS = 8704
PROJ_TN = 2176

NT = (((1,), (1,)), ((), ()))
TN = (((0,), (0,)), ((), ()))


def _dot(a, b):
    return jnp.dot(a, b, preferred_element_type=F32)


def _dot_nt(a, b):
    return lax.dot_general(a, b, NT, preferred_element_type=F32)


def _dot_tn(a, b):
    return lax.dot_general(a, b, TN, preferred_element_type=F32)


def _sigmoid(x):
    return 1.0 / (1.0 + jnp.exp(-x))


def _silu(x):
    return x * _sigmoid(x)


def _softplus(x):
    return jnp.maximum(x, 0.0) + jnp.log(1.0 + jnp.exp(-jnp.abs(x)))


def _rms(x, w):
    return x * lax.rsqrt(jnp.mean(x * x, axis=-1, keepdims=True) + EPS) * w


def _resident(shape):
    zeros = (0,) * len(shape)
    return pl.BlockSpec(shape, lambda *_: zeros, pipeline_mode=pl.Buffered(1))


def _params(*sem):
    return pltpu.CompilerParams(dimension_semantics=sem, vmem_limit_bytes=VMEM_LIMIT)


def _ada_kernel(c_ref, w_ref, b_ref, o_ref):
    s = _silu(c_ref[...])
    o_ref[...] = _dot(s.astype(BF16), w_ref[...].astype(BF16)) + b_ref[...]


def _ada(cc, w_ada, b_ada):
    depth, d, nm = w_ada.shape
    tn = 1024
    return pl.pallas_call(
        _ada_kernel,
        grid=(depth, nm // tn),
        in_specs=[pl.BlockSpec((8, d), lambda l, j: (0, 0)),
                  pl.BlockSpec((None, d, tn), lambda l, j: (l, 0, j)),
                  pl.BlockSpec((None, 1, tn), lambda l, j: (l, 0, j))],
        out_specs=pl.BlockSpec((None, 8, tn), lambda l, j: (l, 0, j)),
        out_shape=jax.ShapeDtypeStruct((depth, 8, nm), F32),
        compiler_params=_params("parallel", "parallel"),
        name="ada_mod",
    )(cc, w_ada, b_ada.reshape(depth, 1, nm))


def _ffn_kernel(x_ref, sh_ref, sc_ref, gt_ref, prew_ref, postw_ref, wi_ref, wo_ref, *rest, emit_u):
    if emit_u:
        nw_ref, sh2_ref, sc2_ref, o_ref, u_ref = rest
    else:
        (o_ref,) = rest
    x = x_ref[...]
    hn = (_rms(x, prew_ref[...]) * (1.0 + sc_ref[...]) + sh_ref[...]).astype(BF16)
    acc = jnp.zeros(x.shape, F32)
    for c in range(D_FF // FF_CK):
        g = _dot(hn, wi_ref[:, c * FF_CK:(c + 1) * FF_CK])
        u = _dot(hn, wi_ref[:, D_FF + c * FF_CK:D_FF + (c + 1) * FF_CK])
        a = (_silu(g) * u).astype(BF16)
        acc = acc + _dot(a, wo_ref[c * FF_CK:(c + 1) * FF_CK, :])
    out = x + 0.5 * gt_ref[...] * _rms(acc, postw_ref[...])
    o_ref[...] = out
    if emit_u:
        u_ref[...] = (_rms(out, nw_ref[...]) * (1.0 + sc2_ref[...]) + sh2_ref[...]).astype(BF16)


def _ffn(h, mods, nw, wi, wo, n_rows, n_seq, nb, mixer_mods=None):
    d = h.shape[1]
    emit_u = mixer_mods is not None
    row = lambda i: (i, 0)
    mod = lambda i: (jnp.minimum((i * TM) // n_seq, nb), 0, 0)
    vec = pl.BlockSpec((1, d), lambda i: (0, 0))
    mspec = pl.BlockSpec((None, 1, d), mod)
    in_specs = [pl.BlockSpec((TM, d), row), mspec, mspec, mspec, vec, vec,
                _resident(wi.shape), _resident(wo.shape)]
    args = [h, mods[0], mods[1], mods[2], nw[0], nw[1], wi, wo]
    out_specs = pl.BlockSpec((TM, d), row)
    out_shape = jax.ShapeDtypeStruct((n_rows, d), F32)
    if emit_u:
        in_specs += [vec, mspec, mspec]
        args += list(mixer_mods)
        out_specs = (out_specs, pl.BlockSpec((TM, d), row))
        out_shape = (out_shape, jax.ShapeDtypeStruct((n_rows, d), BF16))
    return pl.pallas_call(
        functools.partial(_ffn_kernel, emit_u=emit_u),
        grid=(n_rows // TM,),
        in_specs=in_specs, out_specs=out_specs, out_shape=out_shape,
        compiler_params=_params("parallel"),
        name="ffn_sublayer",
    )(*args)


def _proj_kernel(x_ref, w_ref, o_ref):
    o_ref[...] = _dot(x_ref[...], w_ref[pl.program_id(1)])


def _proj(u, w):
    t, d = u.shape
    nj, _, tn = w.shape
    return pl.pallas_call(
        _proj_kernel,
        grid=(t // TM, nj),
        in_specs=[pl.BlockSpec((TM, d), lambda i, j: (i, 0)), _resident(w.shape)],
        out_specs=pl.BlockSpec((TM, tn), lambda i, j: (i, j)),
        out_shape=jax.ShapeDtypeStruct((t, nj * tn), F32),
        compiler_params=_params("parallel", "arbitrary"),
        name="in_proj",
    )(u, w)


def _mla_prep_kernel(qa_ref, kva_ref, kr_ref, cs_ref, qn_ref, kvn_ref, wq_ref, wkv_ref,
                     q_ref, k_ref, v_ref):
    cs = cs_ref[...]
    qa = _rms(qa_ref[:, :Q_RANK], qn_ref[...]).astype(BF16)
    kva = _rms(kva_ref[...], kvn_ref[...]).astype(BF16)
    q2 = _dot(qa, wq_ref[...])
    kv2 = _dot(kva, wkv_ref[...])
    kr = kr_ref[...] * cs
    kr = (kr + pltpu.roll(kr, ROPE, 1))[:, :ROPE].astype(BF16)
    for h in range(HEADS):
        q_ref[h, :, :NOPE] = q2[:, h * NOPE:(h + 1) * NOPE].astype(BF16)
        qr = q2[:, BW + h * 128:BW + (h + 1) * 128] * cs
        q_ref[h, :, NOPE:] = (qr + pltpu.roll(qr, ROPE, 1))[:, :ROPE].astype(BF16)
        k_ref[h, :, :NOPE] = kv2[:, h * NOPE:(h + 1) * NOPE].astype(BF16)
        k_ref[h, :, NOPE:] = kr
        v_ref[h] = kv2[:, BW + h * HD:BW + (h + 1) * HD].astype(BF16)


def _mla_prep(p, cs, qn, kvn, wq, wkv, nb, n_seq, n_ctx):
    t = p.shape[0]
    tm = TM_S
    lat_tiles = nb * n_seq // tm
    per_lat = n_seq // tm
    per_ctx = n_ctx // tm

    def omap(i):
        j = i - lat_tiles
        b = jnp.where(i < lat_tiles, i // per_lat, j // per_ctx)
        blk = jnp.where(i < lat_tiles, i % per_lat, per_lat + j % per_ctx)
        return (b, 0, blk, 0)

    tk = n_seq + n_ctx
    return pl.pallas_call(
        _mla_prep_kernel,
        grid=(t // tm,),
        in_specs=[pl.BlockSpec((tm, 512), lambda i: (i, P_QA // 512)),
                  pl.BlockSpec((tm, KV_RANK), lambda i: (i, P_KVA // KV_RANK)),
                  pl.BlockSpec((tm, 128), lambda i: (i, P_KR // 128)),
                  pl.BlockSpec((tm, 128), lambda i: (i, 0)),
                  pl.BlockSpec((1, Q_RANK), lambda i: (0, 0)),
                  pl.BlockSpec((1, KV_RANK), lambda i: (0, 0)),
                  _resident(wq.shape), _resident(wkv.shape)],
        out_specs=(pl.BlockSpec((None, HEADS, tm, NOPE + ROPE), omap),
                   pl.BlockSpec((None, HEADS, tm, NOPE + ROPE), omap),
                   pl.BlockSpec((None, HEADS, tm, HD), omap)),
        out_shape=(jax.ShapeDtypeStruct((nb, HEADS, tk, NOPE + ROPE), BF16),
                   jax.ShapeDtypeStruct((nb, HEADS, tk, NOPE + ROPE), BF16),
                   jax.ShapeDtypeStruct((nb, HEADS, tk, HD), BF16)),
        compiler_params=_params("parallel"),
        name="mla_prep",
    )(p, p, p, cs, qn, kvn, wq, wkv)


def _attn_kernel(q_ref, k_ref, v_ref, *rest, first, kb, nkb, scale):
    o_ref = rest[-1]
    q = q_ref[...]
    tq = q.shape[0]
    c = scale * 1.4426950408889634

    def block(j):
        sl = pl.ds(pl.multiple_of(j * kb, kb), kb)
        return _dot_nt(q, k_ref[sl, :]) * c, v_ref[sl, :]

    def safe_update(t, v, carry):
        m, l, acc = carry
        m_new = jnp.maximum(m, jnp.max(t, axis=-1, keepdims=True))
        alpha = jnp.exp2(m - m_new)
        pr = jnp.exp2(t - m_new)
        l = alpha * l + jnp.sum(pr, axis=-1, keepdims=True)
        return m_new, l, alpha * acc + _dot(pr.astype(BF16), v)

    def fast_update(j, carry):
        m, l, acc, jump = carry
        bm = None
        for i in range(kb // KSUB):
            sl = pl.ds(pl.multiple_of(j * kb + i * KSUB, KSUB), KSUB)
            t = _dot_nt(q, k_ref[sl, :]) * c
            pr = jnp.exp2(t - m)
            tm = jnp.max(t, axis=-1, keepdims=True)
            bm = tm if bm is None else jnp.maximum(bm, tm)
            l = l + jnp.sum(pr, axis=-1, keepdims=True)
            acc = acc + _dot(pr.astype(BF16), v_ref[sl, :])
        m_new = jnp.maximum(m, bm)
        alpha = jnp.exp2(m - m_new)
        return m_new, l * alpha, acc * alpha, jnp.maximum(jump, bm - m)

    init = (jnp.full((tq, 1), -1e30, F32), jnp.zeros((tq, 1), F32), jnp.zeros((tq, HD), F32))
    fsl = slice(first[0], first[0] + first[1])
    start = safe_update(_dot_nt(q, k_ref[fsl, :]) * c, v_ref[fsl, :], init)
    if nkb == 0:
        o_ref[...] = start[2] / start[1]
        return
    _, l, acc, jump = lax.fori_loop(0, nkb, fast_update, start + (jnp.zeros((tq, 1), F32),),
                                    unroll=_pick_block(nkb, (4, 2, 1)))
    o_ref[...] = acc / l

    @pl.when(jnp.max(jump) > MAX_JUMP)
    def _():
        _, l2, acc2 = lax.fori_loop(0, nkb, lambda j, carry: safe_update(*block(j), carry), start)
        o_ref[...] = acc2 / l2


def _pick_block(n, cands):
    for c in cands:
        if n % c == 0:
            return c
    raise ValueError(f"no block size for {n}")


def _attention(q, k, v, o_prev, nb, n_seq, n_ctx, rows, ctx_only):
    scale = (NOPE + ROPE) ** -0.5
    if ctx_only:
        tq, tk = n_ctx, n_ctx
        koff = n_seq // n_ctx
        grid = (nb, HEADS, 1)
        qmap = lambda b, h, i: (b, h, koff, 0)
        kmap = lambda b, h, i: (b, h, koff, 0)
        omap = lambda b, h, i: (nb * n_seq // n_ctx + b, h)
        first, kb, nkb = (0, n_ctx), n_ctx, 0
    else:
        tq, tk = _pick_block(n_seq, (512, 256)), n_seq + n_ctx
        grid = (nb, HEADS, n_seq // tq)
        qmap = lambda b, h, i: (b, h, i, 0)
        kmap = lambda b, h, i: (b, h, 0, 0)
        omap = lambda b, h, i: (b * (n_seq // tq) + i, h)
        kb = _pick_block(n_seq, (1024, 512))
        first, nkb = (n_seq, n_ctx), n_seq // kb
    in_specs = [pl.BlockSpec((None, None, tq, NOPE + ROPE), qmap),
                pl.BlockSpec((None, None, tk, NOPE + ROPE), kmap),
                pl.BlockSpec((None, None, tk, HD), kmap)]
    args = [q, k, v]
    aliases = {}
    if ctx_only:
        in_specs.append(pl.BlockSpec(memory_space=pl.ANY))
        args.append(o_prev)
        aliases = {3: 0}
    return pl.pallas_call(
        functools.partial(_attn_kernel, first=first, kb=kb, nkb=nkb, scale=scale),
        grid=grid,
        in_specs=in_specs,
        out_specs=pl.BlockSpec((tq, HD), omap),
        out_shape=jax.ShapeDtypeStruct((rows, BW), F32),
        input_output_aliases=aliases,
        compiler_params=_params("parallel", "parallel", "parallel"),
        name="mla_attn_ctx" if ctx_only else "mla_attn",
    )(*args)


def _chunk_pos(g, lat_chunks, ncl, ncc):
    is_lat = g < lat_chunks
    pos = jnp.where(is_lat, g % ncl, (g - lat_chunks) % ncc)
    last = jnp.where(is_lat, pos == ncl - 1, pos == ncc - 1)
    return pos == 0, last


def _scan_chunk(b, s, rev, lat_chunks, ncl, ncc):
    c = jnp.where(s < ncc, s, s - ncc)
    if rev:
        c = jnp.where(s < ncc, ncc - 1 - c, ncl - 1 - c)
    return jnp.where(s < ncc, lat_chunks + b * ncc + c, b * ncl + c)


def _tri_masks(n):
    ri = lax.broadcasted_iota(jnp.int32, (n, n), 0)
    ci = lax.broadcasted_iota(jnp.int32, (n, n), 1)
    return ri, ci


def _split(x):
    hi = x.astype(BF16)
    return hi, (x - hi.astype(F32)).astype(BF16)


def _dot3(a, b):
    (ah, al), (bh, bl) = a, b
    return _dot(ah, bh) + (_dot(ah, bl) + _dot(al, bh))


def _dot_tri(tri, x):
    hi, lo = _split(x)
    lo2 = (x - hi.astype(F32) - lo.astype(F32)).astype(BF16)
    return _dot(tri, hi) + (_dot(tri, lo) + _dot(tri, lo2))


def _unit_tri_inv(mats, ri, ci, eye):
    rb, cb = ri >> 3, ci >> 3
    d8 = [jnp.where(rb == cb, a, 0.0) for a in mats]
    d8s = [_split(d) for d in d8]
    x2 = [_split(_dot3(d, d)) for d in d8s]
    ts = [eye - d for d in d8]
    x4 = [_split(_dot3(x, x)) for x in x2]
    ts = [t + _dot3(_split(t), x) for t, x in zip(ts, x2)]
    ts = [t + _dot3(_split(t), x) for t, x in zip(ts, x4)]
    for _ in range(3):
        same = rb == cb
        rb, cb = rb >> 1, cb >> 1
        off = (rb == cb) & jnp.logical_not(same)
        bs = [jnp.where(off, a, 0.0).astype(BF16) for a in mats]
        tb = [t.astype(BF16) for t in ts]
        ys = [_dot(t, b).astype(BF16) for t, b in zip(tb, bs)]
        ts = [t - _dot(y, tl) for t, y, tl in zip(ts, ys, tb)]
    return ts


def _gdn_pre_kernel(x_ref, prev_ref, next_ref, a_ref, b_ref, cw_ref, alog_ref, dtb_ref,
                    u_ref, w_ref, qg_ref, kd_ref, at_ref, dec_ref, *, lat_chunks, ncl, ncc):
    first, last = _chunk_pos(pl.program_id(0), lat_chunks, ncl, ncc)
    nrow = CPP * CHUNK
    x = x_ref[...]
    xp = jnp.where(first, 0.0, prev_ref[7:8, :])
    xn = jnp.where(last, 0.0, next_ref[0:1, :])
    row = lax.broadcasted_iota(jnp.int32, (nrow, 1), 0)
    x_dn = jnp.where(row == 0, xp, pltpu.roll(x, 1, 0))
    x_up = jnp.where(row == nrow - 1, xn, pltpu.roll(x, nrow - 1, 0))
    cw = cw_ref[...]
    s = _silu(x_dn * cw[0:1] + x * cw[1:2] + x_up * cw[2:3])

    lane = lax.broadcasted_iota(jnp.int32, (1, 128), 1)
    g = -jnp.exp(alog_ref[...]) * _softplus(a_ref[...] + dtb_ref[...])
    beta = _sigmoid(b_ref[...])
    ri, ci = _tri_masks(CHUNK)
    eye_b = ri == ci
    eye = eye_b.astype(F32)
    low = (ri >= ci).astype(BF16)
    upp = (ri <= ci).astype(BF16)
    chunk_rows = [slice(c * CHUNK, (c + 1) * CHUNK) for c in range(CPP)]
    gc, g_last = [], []
    for c, rc in enumerate(chunk_rows):
        gcc = jnp.where(lane < HEADS, _dot_tri(low, g[rc]), _dot_tri(upp, g[rc]))
        gc.append(gcc)
        g_last.append(jnp.where(lane < HEADS, gcc[CHUNK - 1:CHUNK], gcc[0:1]))
        dec_ref[c] = jnp.exp(g_last[c])

    qn, kn, vs = [], [], []
    for h in range(HEADS):
        qh = s[:, h * HD:(h + 1) * HD]
        kh = s[:, BW + h * HD:BW + (h + 1) * HD]
        vs.append(s[:, 2 * BW + h * HD:2 * BW + (h + 1) * HD])
        qn.append(qh * lax.rsqrt(jnp.sum(qh * qh, axis=-1, keepdims=True) + EPS) * HD ** -0.5)
        kn.append(kh * lax.rsqrt(jnp.sum(kh * kh, axis=-1, keepdims=True) + EPS))
    qk = [[None] * HEADS for _ in range(CPP)]
    kk = [[None] * HEADS for _ in range(CPP)]
    for c, rc in enumerate(chunk_rows):
        for h in range(HEADS):
            knb = kn[h][rc].astype(BF16)
            qk[c][h] = _dot_nt(qn[h][rc].astype(BF16), knb)
            kk[c][h] = _dot_nt(knb, knb)

    probs = [(c, d, h) for c in range(CPP) for h in range(HEADS) for d in range(2)]
    mats, rhs = [], []
    for c, d, h in probs:
        rc = chunk_rows[c]
        sl = slice(h * HD, (h + 1) * HD)
        dh = d * HEADS + h
        gcc = gc[c][:, dh:dh + 1]
        bt = beta[rc, dh:dh + 1]
        grow = jnp.sum(jnp.where(eye_b, gcc, 0.0), axis=0, keepdims=True)
        incl = (ri >= ci) if d == 0 else (ri <= ci)
        strict = (ri > ci) if d == 0 else (ri < ci)
        dm = jnp.where(incl, jnp.exp(gcc - grow), 0.0)
        eg = jnp.exp(gcc)
        knc = kn[h][rc]
        mats.append(jnp.where(strict, bt * kk[c][h] * dm, 0.0))
        rhs.append(jnp.concatenate([vs[h][rc] * bt, knc * (bt * eg)], axis=1).astype(BF16))
        qg_ref[d, rc, sl] = (qn[h][rc] * eg).astype(BF16)
        kd_ref[d, rc, sl] = (knc * jnp.exp(g_last[c][:, dh:dh + 1] - gcc)).astype(BF16)
        at_ref[d, rc, h * HD:h * HD + CHUNK] = (qk[c][h] * dm).astype(BF16)
        at_ref[d, rc, h * HD + CHUNK:(h + 1) * HD] = jnp.zeros((CHUNK, HD - CHUNK), BF16)

    for (c, d, h), t, r in zip(probs, _unit_tri_inv(mats, ri, ci, eye), rhs):
        sl = slice(h * HD, (h + 1) * HD)
        uw = _dot(t.astype(BF16), r)
        u_ref[d, chunk_rows[c], sl] = uw[:, :HD]
        w_ref[d, chunk_rows[c], sl] = uw[:, HD:].astype(BF16)


def _gdn_pre(p, conv_w, a_log, dt_bias, nb, n_seq, n_ctx):
    t = p.shape[0]
    nrow = CPP * CHUNK
    assert n_seq % nrow == 0 and n_ctx % nrow == 0
    lat_blocks, nbl, nbc = nb * n_seq // nrow, n_seq // nrow, n_ctx // nrow
    qkv_blk = P_GQKV // (3 * BW)
    pad = lambda v: jnp.pad(v.reshape(1, -1), ((0, 0), (0, 128 - v.size)))
    dir_out = lambda dt: jax.ShapeDtypeStruct((2, t, BW), dt)
    dir_spec = pl.BlockSpec((2, nrow, BW), lambda g: (0, g, 0))
    r8 = nrow // 8
    return pl.pallas_call(
        functools.partial(_gdn_pre_kernel, lat_chunks=lat_blocks, ncl=nbl, ncc=nbc),
        grid=(t // nrow,),
        in_specs=[pl.BlockSpec((nrow, 3 * BW), lambda g: (g, qkv_blk)),
                  pl.BlockSpec((8, 3 * BW), lambda g: (jnp.maximum(g * r8 - 1, 0), qkv_blk)),
                  pl.BlockSpec((8, 3 * BW), lambda g: (jnp.minimum(g * r8 + r8, t // 8 - 1), qkv_blk)),
                  pl.BlockSpec((nrow, 128), lambda g: (g, P_GA // 128)),
                  pl.BlockSpec((nrow, 128), lambda g: (g, P_GB // 128)),
                  pl.BlockSpec((3, 3 * BW), lambda g: (0, 0)),
                  pl.BlockSpec((1, 128), lambda g: (0, 0)),
                  pl.BlockSpec((1, 128), lambda g: (0, 0))],
        out_specs=(dir_spec, dir_spec, dir_spec, dir_spec, dir_spec,
                   pl.BlockSpec((CPP, 1, 128), lambda g: (g, 0, 0))),
        out_shape=(dir_out(F32), dir_out(BF16), dir_out(BF16), dir_out(BF16), dir_out(BF16),
                   jax.ShapeDtypeStruct((t // CHUNK, 1, 128), F32)),
        compiler_params=_params("parallel"),
        name="gdn_chunk_prep",
    )(p, p, p, p, p, conv_w, pad(a_log), pad(dt_bias))


def _scan_rows(d, c):
    cc = c if d == 0 else CPS - 1 - c
    return slice(cc * CHUNK, (cc + 1) * CHUNK), cc


def _gdn_scan_kernel(*refs):
    ins, (of_ref, ob_ref, s_ref) = refs[:12], refs[12:]

    @pl.when(pl.program_id(1) == 0)
    def _():
        s_ref[...] = jnp.zeros(s_ref.shape, F32)

    chains = [(d, h) for d in range(2) for h in range(HEADS)]
    outs = (of_ref, ob_ref)
    st = [s_ref[d, h] for d, h in chains]
    for c in range(CPS):
        sb = [s.astype(BF16) for s in st]
        vn = []
        for (d, h), s in zip(chains, sb):
            u_ref, w_ref = ins[6 * d], ins[6 * d + 1]
            rows, _ = _scan_rows(d, c)
            sl = slice(h * HD, (h + 1) * HD)
            vn.append((u_ref[rows, sl] - _dot(w_ref[rows, sl], s)).astype(BF16))
        for i, (d, h) in enumerate(chains):
            qg_ref, kd_ref, at_ref, dec_ref = ins[6 * d + 2:6 * d + 6]
            rows, cc = _scan_rows(d, c)
            sl = slice(h * HD, (h + 1) * HD)
            outs[d][rows, sl] = _dot(qg_ref[rows, sl], sb[i]) + _dot(at_ref[rows, h * HD:h * HD + CHUNK], vn[i])
            dec = dec_ref[cc, 0:1, d * HEADS + h:d * HEADS + h + 1]
            st[i] = st[i] * dec + _dot_tn(kd_ref[rows, sl], vn[i])
    for (d, h), s in zip(chains, st):
        s_ref[d, h] = s


def _dir_scan(kernel, arrays, dec, nb, n_seq, n_ctx, scratch, name):
    t = arrays[0].shape[1]
    step = CPS * CHUNK
    assert n_seq % step == 0 and n_ctx % step == 0
    lat_blocks, ncl, ncc = nb * n_seq // step, n_seq // step, n_ctx // step
    in_specs, args = [], []
    for d in range(2):
        cmap = functools.partial(_scan_chunk, rev=bool(d), lat_chunks=lat_blocks, ncl=ncl, ncc=ncc)
        for arr in arrays:
            if arr.ndim == 3:
                in_specs.append(pl.BlockSpec((None, step, BW), lambda b, s, d=d, cmap=cmap: (d, cmap(b, s), 0)))
            else:
                in_specs.append(pl.BlockSpec((step, BW), lambda b, s, cmap=cmap: (cmap(b, s), 0)))
            args.append(arr)
        if dec.ndim == 3:
            in_specs.append(pl.BlockSpec((CPS, 1, dec.shape[-1]), lambda b, s, cmap=cmap: (cmap(b, s), 0, 0)))
        else:
            in_specs.append(pl.BlockSpec((None, CPS, 1, dec.shape[-1]),
                                         lambda b, s, d=d, cmap=cmap: (d, cmap(b, s), 0, 0)))
        args.append(dec)
    fmap = functools.partial(_scan_chunk, rev=False, lat_chunks=lat_blocks, ncl=ncl, ncc=ncc)
    bmap = functools.partial(_scan_chunk, rev=True, lat_chunks=lat_blocks, ncl=ncl, ncc=ncc)
    return pl.pallas_call(
        kernel,
        grid=(nb, ncc + ncl),
        in_specs=in_specs,
        out_specs=(pl.BlockSpec((step, BW), lambda b, s: (fmap(b, s), 0)),
                   pl.BlockSpec((step, BW), lambda b, s: (bmap(b, s), 0))),
        out_shape=(jax.ShapeDtypeStruct((t, BW), F32), jax.ShapeDtypeStruct((t, BW), F32)),
        scratch_shapes=[scratch],
        compiler_params=_params("parallel", "arbitrary"),
        name=name,
    )(*args)


def _hg_block_decay(gc, rev):
    out = None
    for b in range(CHUNK // SUB):
        lo, hi = gc[b * SUB:b * SUB + 1], gc[(b + 1) * SUB - 1:(b + 1) * SUB]
        dcy = (hi - lo) if rev else (lo - hi)
        out = dcy if out is None else jnp.maximum(out, dcy)
    return out


def _hg_scores(q, k, gc, rev, exact):
    nblk = CHUNK // SUB
    lane = lax.broadcasted_iota(jnp.int32, (1, CHUNK), 1)
    row8 = lax.broadcasted_iota(jnp.int32, (8, 1), 0)
    blk = lambda x, b: x[b * SUB:(b + 1) * SUB]
    ref_row = lambda b: b * SUB + (0 if rev else SUB - 1)
    refs = [gc[ref_row(b):ref_row(b) + 1] for b in range(nblk)]
    kt = jnp.concatenate([blk(k, b) * jnp.exp2(refs[b] - blk(gc, b)) for b in range(nblk)], axis=0)
    srcs = list(range(nblk - 1, 0, -1)) if rev else list(range(nblk - 1))
    rows_of = (lambda j: slice(0, j * SUB)) if rev else (lambda j: slice((j + 1) * SUB, CHUNK))
    qt = jnp.concatenate([q[rows_of(j)] * jnp.exp2(gc[rows_of(j)] - refs[j]) for j in srcs], axis=0)
    seg_off, off = {}, 0
    for j in srcs:
        seg_off[j] = off
        off += (j if rev else nblk - 1 - j) * SUB
    ktb, qtb = kt.astype(BF16), qt.astype(BF16)
    cross = [_dot_nt(qtb[:, h * HD:(h + 1) * HD], ktb[:, h * HD:(h + 1) * HD]) for h in range(HEADS)]

    def assemble(own):
        out = []
        for h in range(HEADS):
            groups = []
            for g8 in range(CHUNK // 8):
                r0 = 8 * g8
                bi = r0 // SUB
                val = own[h][g8]
                for j in srcs:
                    if (j < bi and not rev) or (j > bi and rev):
                        base = seg_off[j] + (r0 if rev else r0 - (j + 1) * SUB)
                        val = jnp.where((lane >= j * SUB) & (lane < (j + 1) * SUB), cross[h][base:base + 8], val)
                groups.append(val)
            out.append(jnp.concatenate(groups, axis=0))
        return tuple(out)

    def own_exact():
        acc = [[jnp.zeros((8, CHUNK), F32) for _ in range(CHUNK // 8)] for _ in range(HEADS)]
        for j in range(CHUNK):
            b, jj = j // SUB, j % SUB
            gj, kj = gc[j:j + 1], k[j:j + 1]
            for rg in range(SUB // 8):
                lo, hi = 8 * rg, 8 * rg + 7
                if (hi < jj and not rev) or (lo > jj and rev):
                    continue
                r0 = b * SUB + lo
                w = jnp.exp2(gc[r0:r0 + 8] - gj)
                if not ((lo >= jj and not rev) or (hi <= jj and rev)):
                    w = jnp.where((row8 + lo >= jj) if not rev else (row8 + lo <= jj), w, 0.0)
                tt = q[r0:r0 + 8] * w * kj
                g8 = r0 // 8
                for h in range(HEADS):
                    col = jnp.sum(tt[:, h * HD:(h + 1) * HD], axis=-1, keepdims=True)
                    acc[h][g8] = jnp.where(lane == j, col, acc[h][g8])
        return assemble(acc)

    first_row = lambda b: b * SUB + (SUB - 1 if rev else 0)
    starts = [gc[first_row(b):first_row(b) + 1] for b in range(nblk)]

    def own_matmul():
        rfull = jnp.concatenate([jnp.broadcast_to(starts[b], (SUB, q.shape[1])) for b in range(nblk)], axis=0)
        qd = (q * jnp.exp2(gc - rfull)).astype(BF16)
        kd = (k * jnp.exp2(rfull - gc)).astype(BF16)
        ri, ci = _tri_masks(CHUNK)
        sh = SUB.bit_length() - 1
        keep = ((ri >> sh) == (ci >> sh)) & ((ri <= ci) if rev else (ri >= ci))
        own = []
        for h in range(HEADS):
            sd = jnp.where(keep, _dot_nt(qd[:, h * HD:(h + 1) * HD], kd[:, h * HD:(h + 1) * HD]), 0.0)
            own.append([sd[8 * g8:8 * g8 + 8] for g8 in range(CHUNK // 8)])
        return assemble(own)

    return own_exact() if exact else own_matmul()


def _hg_pre_kernel(q_ref, f_ref, i_ref, lbl_ref, oi_ref, qg_ref, kd_ref, vb_ref, dec_ref, *, layer):
    lbl = lbl_ref[...]
    e = jnp.exp(lbl - jnp.max(lbl, axis=0, keepdims=True))
    sm = e / jnp.sum(e, axis=0, keepdims=True)
    lb_all = sm[0]
    for l in range(1, layer + 1):
        lb_all = lb_all + sm[l]
    lb_all = lb_all - sm[0]
    q_all = q_ref[...] * HD ** -0.5
    v_all = i_ref[...].astype(BF16)
    vb_ref[...] = v_all
    ri, ci = _tri_masks(CHUNK)
    work, decay = [], None
    for d in range(2):
        lb = lb_all[d:d + 1]
        f = f_ref[:, d * BW:(d + 1) * BW]
        e = jnp.exp(-jnp.abs(f))
        r = 1.0 / (1.0 + e)
        er = e * r
        pos = f >= 0.0
        sig, sig_neg = jnp.where(pos, r, er), jnp.where(pos, er, r)
        log_f = jnp.log2(jnp.maximum(lb, LB_FLOOR) + (1.0 - lb) * sig)
        k_all = (1.0 - lb) * sig_neg
        tri = ((ri >= ci) if d == 0 else (ri <= ci)).astype(BF16)
        for c in range(HG_CPP):
            rc = slice(c * CHUNK, (c + 1) * CHUNK)
            q, k, v = q_all[rc], k_all[rc], v_all[rc]
            gc = _dot_tri(tri, log_f[rc])
            g_last = gc[CHUNK - 1:CHUNK] if d == 0 else gc[0:1]
            dec_ref[d, c] = jnp.exp2(g_last)
            qg_ref[d, rc] = (q * jnp.exp2(gc)).astype(BF16)
            kd_ref[d, rc] = (k * jnp.exp2(g_last - gc)).astype(BF16)
            work.append((d, rc, q, k, v, gc))
            dcy = _hg_block_decay(gc, rev=bool(d))
            decay = dcy if decay is None else jnp.maximum(decay, dcy)

    def intra(exact):
        for d, rc, q, k, v, gc in work:
            scores = _hg_scores(q, k, gc, rev=bool(d), exact=exact)
            for h in range(HEADS):
                sl = slice(h * HD, (h + 1) * HD)
                oi_ref[d, rc, sl] = _dot(scores[h].astype(BF16), v[:, sl])

    safe = jnp.max(decay) <= HG_MAX_DECAY
    pl.when(safe)(lambda: intra(False))
    pl.when(jnp.logical_not(safe))(lambda: intra(True))


def _hg_pre(p, lb_logits, layer):
    t = p.shape[0]
    nc = t // CHUNK
    nrow = HG_CPP * CHUNK
    assert t % nrow == 0
    dir_out = lambda dt: jax.ShapeDtypeStruct((2, t, BW), dt)
    dir_spec = pl.BlockSpec((2, nrow, BW), lambda g: (0, g, 0))
    return pl.pallas_call(
        functools.partial(_hg_pre_kernel, layer=layer),
        grid=(t // nrow,),
        in_specs=[pl.BlockSpec((nrow, BW), lambda g: (g, P_HQ // BW)),
                  pl.BlockSpec((nrow, 2 * BW), lambda g: (g, P_HF // (2 * BW))),
                  pl.BlockSpec((nrow, BW), lambda g: (g, P_HI // BW)),
                  pl.BlockSpec(lb_logits.shape, lambda g: (0, 0, 0))],
        out_specs=(dir_spec, dir_spec, dir_spec,
                   pl.BlockSpec((nrow, BW), lambda g: (g, 0)),
                   pl.BlockSpec((2, HG_CPP, 1, BW), lambda g: (0, g, 0, 0))),
        out_shape=(dir_out(F32), dir_out(BF16), dir_out(BF16),
                   jax.ShapeDtypeStruct((t, BW), BF16),
                   jax.ShapeDtypeStruct((2, nc, 1, BW), F32)),
        compiler_params=_params("parallel"),
        name="hgrn2_chunk_prep",
    )(p, p, p, lb_logits)


def _hg_scan_kernel(*refs):
    ins, (of_ref, ob_ref, s_ref) = refs[:10], refs[10:]

    @pl.when(pl.program_id(1) == 0)
    def _():
        s_ref[...] = jnp.zeros(s_ref.shape, F32)

    chains = [(d, h) for d in range(2) for h in range(HEADS)]
    outs = (of_ref, ob_ref)
    incs = []
    for d, h in chains:
        kd_ref, v_ref = ins[5 * d + 2], ins[5 * d + 3]
        sl = slice(h * HD, (h + 1) * HD)
        incs.append([_dot_tn(v_ref[_scan_rows(d, c)[0], sl], kd_ref[_scan_rows(d, c)[0], sl])
                     for c in range(CPS)])
    states = []
    for i, (d, h) in enumerate(chains):
        dec_ref = ins[5 * d + 4]
        sl = slice(h * HD, (h + 1) * HD)
        st = s_ref[d, h]
        seq = []
        for c in range(CPS):
            seq.append(st.astype(BF16))
            st = st * dec_ref[_scan_rows(d, c)[1], :, sl] + incs[i][c]
        s_ref[d, h] = st
        states.append(seq)
    for i, (d, h) in enumerate(chains):
        oi_ref, qg_ref = ins[5 * d], ins[5 * d + 1]
        sl = slice(h * HD, (h + 1) * HD)
        for c in range(CPS):
            rows, _ = _scan_rows(d, c)
            outs[d][rows, sl] = oi_ref[rows, sl] + _dot_nt(qg_ref[rows, sl], states[i][c])


def _merge_kernel(h_ref, gof_ref, gob_ref, gg_ref, yb_ref, hof_ref, hob_ref, hgg_ref, gates_ref,
                  gnorm_ref, hnorm_ref, gt_ref, nw_ref, wb_ref, wo_ref, o_ref):
    def readout(of_ref, ob_ref, gate_ref, norm_ref):
        o = of_ref[...] + ob_ref[...]
        gate = gate_ref[...]
        parts = []
        for h in range(HEADS):
            sl = slice(h * HD, (h + 1) * HD)
            parts.append(_rms(o[:, sl], norm_ref[...]) * _silu(gate[:, sl]))
        return jnp.concatenate(parts, axis=1).astype(BF16)

    ys = (readout(gof_ref, gob_ref, gg_ref, gnorm_ref),
          yb_ref[...].astype(BF16),
          readout(hof_ref, hob_ref, hgg_ref, hnorm_ref))
    d = h_ref.shape[1]
    m = None
    for j in range(3):
        term = _sigmoid(gates_ref[:, j * d:(j + 1) * d]) * _dot(ys[j], wb_ref[j])
        m = term if m is None else m + term
    y = _dot(m.astype(BF16), wo_ref[...])
    o_ref[...] = h_ref[...] + gt_ref[...] * _rms(y, nw_ref[...])


def _merge(h, g_of, g_ob, yb, h_of, h_ob, p, gnorm, hnorm, gate_mod, nw, wb, wo, n_rows, n_seq, nb):
    d = h.shape[1]
    tm = TM_S
    row = lambda i: (i, 0)
    bw_spec = pl.BlockSpec((tm, BW), row)
    pcol = lambda off, width: pl.BlockSpec((tm, width), lambda i: (i, off // width))
    vec = lambda n: pl.BlockSpec((1, n), lambda i: (0, 0))
    return pl.pallas_call(
        _merge_kernel,
        grid=(n_rows // tm,),
        in_specs=[pl.BlockSpec((tm, d), row), bw_spec, bw_spec, pcol(P_GGATE, BW), bw_spec,
                  bw_spec, bw_spec, pcol(P_HGATE, BW), pcol(P_GATES, 3 * d),
                  vec(HD), vec(HD),
                  pl.BlockSpec((None, 1, d), lambda i: (jnp.minimum((i * tm) // n_seq, nb), 0, 0)),
                  vec(d), _resident(wb.shape), _resident(wo.shape)],
        out_specs=pl.BlockSpec((tm, d), row),
        out_shape=jax.ShapeDtypeStruct((n_rows, d), F32),
        compiler_params=_params("parallel"),
        name="branch_merge",
    )(h, g_of, g_ob, p, yb, h_of, h_ob, p, p, gnorm, hnorm, gate_mod, nw, wb, wo)


def _pack_w_in(w):
    d = w.shape[0]
    sizes = (BW, BW, BW, BW, 2 * HEADS, 2 * HEADS, Q_RANK, KV_RANK, ROPE, BW, 2 * BW, BW, BW, 3 * d)
    offs = [0]
    for s in sizes:
        offs.append(offs[-1] + s)
    part = lambda i: w[:, offs[i]:offs[i + 1]]
    gq, gk, gv, ggate, ga, gb, qa, kva, kr, hq, hf, hi, hgate, gates = (part(i) for i in range(14))
    z = lambda n: jnp.zeros((d, n), w.dtype)
    swap = jnp.concatenate([kr[:, 16:32], kr[:, 0:16], kr[:, 48:64], kr[:, 32:48]], axis=1)
    cols = [gates, gq, gk, gv, ggate, hf, hq, hi, hgate, qa, gb, z(120), kva, kr, swap, ga, z(120)]
    out = jnp.concatenate(cols, axis=1).astype(BF16)
    assert out.shape[1] == P_COLS
    return out.reshape(d, P_COLS // PROJ_TN, PROJ_TN).transpose(1, 0, 2)


def _pack_wq(w):
    w = w.reshape(Q_RANK, HEADS, NOPE + ROPE)
    nope = w[:, :, :NOPE].reshape(Q_RANK, HEADS * NOPE)
    r = w[:, :, NOPE:]
    sw = jnp.concatenate([r[..., 16:32], r[..., 0:16], r[..., 48:64], r[..., 32:48]], axis=-1)
    rope = jnp.concatenate([r, sw], axis=-1).reshape(Q_RANK, HEADS * 2 * ROPE)
    return jnp.concatenate([nope, rope], axis=1).astype(BF16)


def _pack_wkv(w):
    w = w.reshape(KV_RANK, HEADS, NOPE + HD)
    return jnp.concatenate([w[:, :, :NOPE].reshape(KV_RANK, -1), w[:, :, NOPE:].reshape(KV_RANK, -1)],
                           axis=1).astype(BF16)


def _rope_table(nb, n_seq, n_ctx):
    nf = ROPE // 4
    rows = n_seq // GRID_W
    rpos = jnp.repeat(jnp.arange(rows, dtype=F32), GRID_W)
    cpos = jnp.tile(jnp.arange(GRID_W, dtype=F32), rows)
    inv = ROPE_BASE ** (-jnp.arange(nf, dtype=F32) / nf)
    ar, ac = rpos[:, None] * inv, cpos[:, None] * inv
    cos = jnp.concatenate([jnp.cos(ar), jnp.cos(ar), jnp.cos(ac), jnp.cos(ac)], axis=1)
    sin = jnp.concatenate([-jnp.sin(ar), jnp.sin(ar), -jnp.sin(ac), jnp.sin(ac)], axis=1)
    lat = jnp.tile(jnp.concatenate([cos, sin], axis=1), (nb, 1))
    ctx = jnp.concatenate([jnp.ones((nb * n_ctx, ROPE), F32), jnp.zeros((nb * n_ctx, ROPE), F32)], axis=1)
    return jnp.concatenate([lat, ctx], axis=0)


def kernel(x, c, ctx, c_ctx, w_ada, b_ada, norm_w, ffn_w_in, ffn_w_out, w_in, gdn_conv, gdn_a_log, gdn_dt_bias, gdn_norm, mla_q_norm, mla_kv_norm, mla_w_q_b, mla_w_kv_b, hg_lb_logits, hg_norm, w_branch, w_out):
    nb, n_seq, d = x.shape
    n_ctx = ctx.shape[1]
    depth = w_ada.shape[0]
    lat_rows, rows = nb * n_seq, nb * (n_seq + n_ctx)
    assert nb + 1 <= 8 and n_seq % TM == 0 and n_ctx % TM_S == 0 and (nb * n_ctx) % TM == 0
    assert n_seq % n_ctx == 0 and lat_rows % n_ctx == 0

    cc = jnp.concatenate([c, c_ctx[None], jnp.zeros((8 - nb - 1, d), F32)], axis=0)
    mods = _ada(cc, w_ada, b_ada)[:, :nb + 1].reshape(depth, nb + 1, 9, 1, d).transpose(0, 2, 1, 3, 4)
    cs = _rope_table(nb, n_seq, n_ctx)
    h = jnp.concatenate([x.reshape(lat_rows, d), ctx.reshape(nb * n_ctx, d)], axis=0)

    for l in range(depth):
        last = l == depth - 1
        md, nw = mods[l], norm_w[l][:, None, :]
        wi = [ffn_w_in[l, j].astype(BF16) for j in range(2)]
        wo = [ffn_w_out[l, j].astype(BF16) for j in range(2)]

        h, u = _ffn(h, md[0:3], nw[0:2], wi[0], wo[0], rows, n_seq, nb, mixer_mods=(nw[2], md[3], md[4]))
        p = _proj(u, _pack_w_in(w_in[l]))

        g_ops = _gdn_pre(p, gdn_conv[l], gdn_a_log[l], gdn_dt_bias[l], nb, n_seq, n_ctx)
        g_of, g_ob = _dir_scan(_gdn_scan_kernel, g_ops[:5], g_ops[5], nb, n_seq, n_ctx,
                               pltpu.VMEM((2, HEADS, HD, HD), F32), "gdn_scan")

        hoi, hqg, hkd, hvb, hdec = _hg_pre(p, hg_lb_logits, l)
        h_of, h_ob = _dir_scan(_hg_scan_kernel, (hoi, hqg, hkd, hvb), hdec, nb, n_seq, n_ctx,
                               pltpu.VMEM((2, HEADS, HD, HD), F32), "hgrn2_scan")

        q, k, v = _mla_prep(p, cs, mla_q_norm[l][None], mla_kv_norm[l][None],
                            _pack_wq(mla_w_q_b[l]), _pack_wkv(mla_w_kv_b[l]), nb, n_seq, n_ctx)
        yb = _attention(q, k, v, None, nb, n_seq, n_ctx, rows, ctx_only=False)
        if not last:
            yb = _attention(q, k, v, yb, nb, n_seq, n_ctx, rows, ctx_only=True)

        out_rows = lat_rows if last else rows
        h = _merge(h, g_of, g_ob, yb, h_of, h_ob, p, gdn_norm[l][None], hg_norm[l][None], md[5], nw[3],
                   w_branch[l].astype(BF16), w_out[l].astype(BF16), out_rows, n_seq, nb)
        h = _ffn(h, md[6:9], nw[4:6], wi[1], wo[1], out_rows, n_seq, nb)
    return h.reshape(nb, n_seq, d)
```

```python
import functools

import jax
import jax.numpy as jnp
from jax import lax
from jax.experimental import pallas as pl
from jax.experimental.pallas import tpu as pltpu

F32 = jnp.float32
BF16 = jnp.bfloat16
EPS = 1e-6
LB_FLOOR = 1e-30
GRID_W = 64
ROPE_BASE = 10000.0

D_FF = 2816
HEADS = 4
HD = 128
ROPE = 64
NOPE = 128
Q_RANK = 384
KV_RANK = 256
CHUNK = 64
CPS = 4
CPP = 2
HG_CPP = 2
SUB = 16
BW = HEADS * HD

TM = 512
TM_S = 256
FF_CK = 256
VMEM_LIMIT = 48 * 1024 * 1024
ATTN_LOG2_SCALE = (NOPE + ROPE) ** -0.5 * 1.4426950408889634
KSUB = 256
HG_MAX_DECAY = 60.0
MAX_JUMP = 64.0

P_GQKV = 0
P_GGATE = 1536
P_HF = 2048
P_HQ = 3072
P_HI = 3584
P_HGATE = 4096
P_GA = 4608
P_GB = 4736
P_COLS = 4864
PROJ_TN = 2432
M_QA, M_KVA, M_KR, M_COLS = 0, 512, 768, 896

NT = (((1,), (1,)), ((), ()))
TN = (((0,), (0,)), ((), ()))


def _dot(a, b):
    return jnp.dot(a, b, preferred_element_type=F32)


def _dot_nt(a, b):
    return lax.dot_general(a, b, NT, preferred_element_type=F32)


def _dot_tn(a, b):
    return lax.dot_general(a, b, TN, preferred_element_type=F32)


def _sigmoid(x):
    return 1.0 / (1.0 + jnp.exp(-x))


def _silu(x):
    return x * _sigmoid(x)


def _softplus(x):
    return jnp.maximum(x, 0.0) + jnp.log(1.0 + jnp.exp(-jnp.abs(x)))


def _rms(x, w):
    return x * lax.rsqrt(jnp.mean(x * x, axis=-1, keepdims=True) + EPS) * w


def _resident(shape):
    zeros = (0,) * len(shape)
    return pl.BlockSpec(shape, lambda *_: zeros, pipeline_mode=pl.Buffered(1))


def _params(*sem):
    return pltpu.CompilerParams(dimension_semantics=sem, vmem_limit_bytes=VMEM_LIMIT)


def _ada_kernel(c_ref, w_ref, b_ref, o_ref):
    s = _silu(c_ref[...])
    o_ref[...] = _dot(s.astype(BF16), w_ref[...].astype(BF16)) + b_ref[...]


def _ada(cc, w_ada, b_ada):
    depth, d, nm = w_ada.shape
    tn = 1024
    return pl.pallas_call(
        _ada_kernel,
        grid=(depth, nm // tn),
        in_specs=[pl.BlockSpec((8, d), lambda l, j: (0, 0)),
                  pl.BlockSpec((None, d, tn), lambda l, j: (l, 0, j)),
                  pl.BlockSpec((None, 1, tn), lambda l, j: (l, 0, j))],
        out_specs=pl.BlockSpec((None, 8, tn), lambda l, j: (l, 0, j)),
        out_shape=jax.ShapeDtypeStruct((depth, 8, nm), F32),
        compiler_params=_params("parallel", "parallel"),
        name="ada_mod",
    )(cc, w_ada, b_ada.reshape(depth, 1, nm))


def _ffn_kernel(x_ref, sh_ref, sc_ref, gt_ref, prew_ref, postw_ref, wi_ref, wo_ref, *rest, emit_u):
    if emit_u:
        nw_ref, sh2_ref, sc2_ref, o_ref, u_ref = rest
    else:
        (o_ref,) = rest
    x = x_ref[...]
    hn = (_rms(x, prew_ref[...]) * (1.0 + sc_ref[...]) + sh_ref[...]).astype(BF16)
    acc = jnp.zeros(x.shape, F32)
    for c in range(D_FF // FF_CK):
        g = _dot(hn, wi_ref[:, c * FF_CK:(c + 1) * FF_CK])
        u = _dot(hn, wi_ref[:, D_FF + c * FF_CK:D_FF + (c + 1) * FF_CK])
        a = (_silu(g) * u).astype(BF16)
        acc = acc + _dot(a, wo_ref[c * FF_CK:(c + 1) * FF_CK, :])
    out = x + 0.5 * gt_ref[...] * _rms(acc, postw_ref[...])
    o_ref[...] = out
    if emit_u:
        u_ref[...] = (_rms(out, nw_ref[...]) * (1.0 + sc2_ref[...]) + sh2_ref[...]).astype(BF16)


def _ffn(h, mods, nw, wi, wo, n_rows, n_seq, nb, mixer_mods=None):
    d = h.shape[1]
    emit_u = mixer_mods is not None
    row = lambda i: (i, 0)
    mod = lambda i: (jnp.minimum((i * TM) // n_seq, nb), 0, 0)
    vec = pl.BlockSpec((1, d), lambda i: (0, 0))
    mspec = pl.BlockSpec((None, 1, d), mod)
    in_specs = [pl.BlockSpec((TM, d), row), mspec, mspec, mspec, vec, vec,
                _resident(wi.shape), _resident(wo.shape)]
    args = [h, mods[0], mods[1], mods[2], nw[0], nw[1], wi, wo]
    out_specs = pl.BlockSpec((TM, d), row)
    out_shape = jax.ShapeDtypeStruct((n_rows, d), F32)
    if emit_u:
        in_specs += [vec, mspec, mspec]
        args += list(mixer_mods)
        out_specs = (out_specs, pl.BlockSpec((TM, d), row))
        out_shape = (out_shape, jax.ShapeDtypeStruct((n_rows, d), BF16))
    return pl.pallas_call(
        functools.partial(_ffn_kernel, emit_u=emit_u),
        grid=(n_rows // TM,),
        in_specs=in_specs, out_specs=out_specs, out_shape=out_shape,
        compiler_params=_params("parallel"),
        name="ffn_sublayer",
    )(*args)


def _proj_kernel(x_ref, w_ref, o_ref):
    o_ref[...] = _dot(x_ref[...], w_ref[pl.program_id(1)])


def _proj(u, w):
    t, d = u.shape
    nj, _, tn = w.shape
    return pl.pallas_call(
        _proj_kernel,
        grid=(t // TM, nj),
        in_specs=[pl.BlockSpec((TM, d), lambda i, j: (i, 0)), _resident(w.shape)],
        out_specs=pl.BlockSpec((TM, tn), lambda i, j: (i, j)),
        out_shape=jax.ShapeDtypeStruct((t, nj * tn), F32),
        compiler_params=_params("parallel", "arbitrary"),
        name="in_proj",
    )(u, w)


def _mla_prep_kernel(u_ref, cs_ref, qn_ref, kvn_ref, win_ref, wq_ref, wkv_ref, q_ref, k_ref, v_ref):
    cs = cs_ref[...]
    low = _dot(u_ref[...], win_ref[...])
    qa = _rms(low[:, M_QA:M_QA + Q_RANK], qn_ref[...]).astype(BF16)
    kva = _rms(low[:, M_KVA:M_KVA + KV_RANK], kvn_ref[...]).astype(BF16)
    q2 = _dot(qa, wq_ref[...])
    kv2 = _dot(kva, wkv_ref[...])
    kr = low[:, M_KR:M_KR + 2 * ROPE] * cs
    kr = (kr + pltpu.roll(kr, ROPE, 1))[:, :ROPE].astype(BF16)
    q2 = q2 * ATTN_LOG2_SCALE
    for h in range(HEADS):
        q_ref[h, :, :NOPE] = q2[:, h * NOPE:(h + 1) * NOPE].astype(BF16)
        qr = q2[:, BW + h * 128:BW + (h + 1) * 128] * cs
        q_ref[h, :, NOPE:] = (qr + pltpu.roll(qr, ROPE, 1))[:, :ROPE].astype(BF16)
        k_ref[h, :, :NOPE] = kv2[:, h * NOPE:(h + 1) * NOPE].astype(BF16)
        k_ref[h, :, NOPE:] = kr
        v_ref[h] = kv2[:, BW + h * HD:BW + (h + 1) * HD].astype(BF16)


def _mla_prep(u, cs, qn, kvn, w_low, wq, wkv, nb, n_seq, n_ctx):
    t, d = u.shape
    tm = TM_S
    lat_tiles = nb * n_seq // tm
    per_lat = n_seq // tm
    per_ctx = n_ctx // tm

    def omap(i):
        j = i - lat_tiles
        b = jnp.where(i < lat_tiles, i // per_lat, j // per_ctx)
        blk = jnp.where(i < lat_tiles, i % per_lat, per_lat + j % per_ctx)
        return (b, 0, blk, 0)

    tk = n_seq + n_ctx
    return pl.pallas_call(
        _mla_prep_kernel,
        grid=(t // tm,),
        in_specs=[pl.BlockSpec((tm, d), lambda i: (i, 0)),
                  pl.BlockSpec((tm, 128), lambda i: (i, 0)),
                  pl.BlockSpec((1, Q_RANK), lambda i: (0, 0)),
                  pl.BlockSpec((1, KV_RANK), lambda i: (0, 0)),
                  _resident(w_low.shape), _resident(wq.shape), _resident(wkv.shape)],
        out_specs=(pl.BlockSpec((None, HEADS, tm, NOPE + ROPE), omap),
                   pl.BlockSpec((None, HEADS, tm, NOPE + ROPE), omap),
                   pl.BlockSpec((None, HEADS, tm, HD), omap)),
        out_shape=(jax.ShapeDtypeStruct((nb, HEADS, tk, NOPE + ROPE), BF16),
                   jax.ShapeDtypeStruct((nb, HEADS, tk, NOPE + ROPE), BF16),
                   jax.ShapeDtypeStruct((nb, HEADS, tk, HD), BF16)),
        compiler_params=_params("parallel"),
        name="mla_prep",
    )(u, cs, qn, kvn, w_low, wq, wkv)


def _attn_kernel(q_ref, k_ref, v_ref, *rest, first, kb, nkb):
    o_ref = rest[-1]
    q = q_ref[...]
    tq = q.shape[0]

    def block(j):
        sl = pl.ds(pl.multiple_of(j * kb, kb), kb)
        return _dot_nt(q, k_ref[sl, :]), v_ref[sl, :]

    def safe_update(t, v, carry):
        m, l, acc = carry
        m_new = jnp.maximum(m, jnp.max(t, axis=-1, keepdims=True))
        alpha = jnp.exp2(m - m_new)
        pr = jnp.exp2(t - m_new)
        l = alpha * l + jnp.sum(pr, axis=-1, keepdims=True)
        return m_new, l, alpha * acc + _dot(pr.astype(BF16), v)

    def fast_update(j, carry):
        m, l, acc, jump = carry
        bm = None
        for i in range(kb // KSUB):
            sl = pl.ds(pl.multiple_of(j * kb + i * KSUB, KSUB), KSUB)
            t = _dot_nt(q, k_ref[sl, :])
            pr = jnp.exp2(t - m)
            tm = jnp.max(t, axis=-1, keepdims=True)
            bm = tm if bm is None else jnp.maximum(bm, tm)
            l = l + jnp.sum(pr, axis=-1, keepdims=True)
            acc = acc + _dot(pr.astype(BF16), v_ref[sl, :])
        m_new = jnp.maximum(m, bm)
        alpha = jnp.exp2(m - m_new)
        return m_new, l * alpha, acc * alpha, jnp.maximum(jump, bm - m)

    init = (jnp.full((tq, 1), -1e30, F32), jnp.zeros((tq, 1), F32), jnp.zeros((tq, HD), F32))
    fsl = slice(first[0], first[0] + first[1])
    start = safe_update(_dot_nt(q, k_ref[fsl, :]), v_ref[fsl, :], init)
    if nkb == 0:
        o_ref[...] = start[2] / start[1]
        return
    _, l, acc, jump = lax.fori_loop(0, nkb, fast_update, start + (jnp.zeros((tq, 1), F32),),
                                    unroll=_pick_block(nkb, (4, 2, 1)))
    o_ref[...] = acc / l

    @pl.when(jnp.max(jump) > MAX_JUMP)
    def _():
        _, l2, acc2 = lax.fori_loop(0, nkb, lambda j, carry: safe_update(*block(j), carry), start)
        o_ref[...] = acc2 / l2


def _pick_block(n, cands):
    for c in cands:
        if n % c == 0:
            return c
    raise ValueError(f"no block size for {n}")


def _attention(q, k, v, o_prev, nb, n_seq, n_ctx, rows, ctx_only):
    if ctx_only:
        tq, tk = n_ctx, n_ctx
        koff = n_seq // n_ctx
        grid = (nb, HEADS, 1)
        qmap = lambda b, h, i: (b, h, koff, 0)
        kmap = lambda b, h, i: (b, h, koff, 0)
        omap = lambda b, h, i: (nb * n_seq // n_ctx + b, h)
        first, kb, nkb = (0, n_ctx), n_ctx, 0
    else:
        tq, tk = _pick_block(n_seq, (512, 256)), n_seq + n_ctx
        grid = (nb, HEADS, n_seq // tq)
        qmap = lambda b, h, i: (b, h, i, 0)
        kmap = lambda b, h, i: (b, h, 0, 0)
        omap = lambda b, h, i: (b * (n_seq // tq) + i, h)
        kb = _pick_block(n_seq, (1024, 512))
        first, nkb = (n_seq, n_ctx), n_seq // kb
    in_specs = [pl.BlockSpec((None, None, tq, NOPE + ROPE), qmap),
                pl.BlockSpec((None, None, tk, NOPE + ROPE), kmap),
                pl.BlockSpec((None, None, tk, HD), kmap)]
    args = [q, k, v]
    aliases = {}
    if ctx_only:
        in_specs.append(pl.BlockSpec(memory_space=pl.ANY))
        args.append(o_prev)
        aliases = {3: 0}
    return pl.pallas_call(
        functools.partial(_attn_kernel, first=first, kb=kb, nkb=nkb),
        grid=grid,
        in_specs=in_specs,
        out_specs=pl.BlockSpec((tq, HD), omap),
        out_shape=jax.ShapeDtypeStruct((rows, BW), F32),
        input_output_aliases=aliases,
        compiler_params=_params("parallel", "parallel", "parallel"),
        name="mla_attn_ctx" if ctx_only else "mla_attn",
    )(*args)


def _chunk_pos(g, lat_chunks, ncl, ncc):
    is_lat = g < lat_chunks
    pos = jnp.where(is_lat, g % ncl, (g - lat_chunks) % ncc)
    last = jnp.where(is_lat, pos == ncl - 1, pos == ncc - 1)
    return pos == 0, last


def _scan_chunk(b, s, rev, lat_chunks, ncl, ncc):
    c = jnp.where(s < ncc, s, s - ncc)
    if rev:
        c = jnp.where(s < ncc, ncc - 1 - c, ncl - 1 - c)
    return jnp.where(s < ncc, lat_chunks + b * ncc + c, b * ncl + c)


def _tri_masks(n):
    ri = lax.broadcasted_iota(jnp.int32, (n, n), 0)
    ci = lax.broadcasted_iota(jnp.int32, (n, n), 1)
    return ri, ci


def _split(x):
    hi = x.astype(BF16)
    return hi, (x - hi.astype(F32)).astype(BF16)


def _dot3(a, b):
    (ah, al), (bh, bl) = a, b
    return _dot(ah, bh) + (_dot(ah, bl) + _dot(al, bh))


def _dot_tri(tri, x):
    hi, lo = _split(x)
    lo2 = (x - hi.astype(F32) - lo.astype(F32)).astype(BF16)
    return _dot(tri, hi) + (_dot(tri, lo) + _dot(tri, lo2))


def _unit_tri_inv(mats, ri, ci, eye):
    rb, cb = ri >> 3, ci >> 3
    d8 = [jnp.where(rb == cb, a, 0.0) for a in mats]
    d8s = [_split(d) for d in d8]
    x2 = [_split(_dot3(d, d)) for d in d8s]
    ts = [eye - d for d in d8]
    x4 = [_split(_dot3(x, x)) for x in x2]
    ts = [t + _dot3(_split(t), x) for t, x in zip(ts, x2)]
    ts = [t + _dot3(_split(t), x) for t, x in zip(ts, x4)]
    for _ in range(3):
        same = rb == cb
        rb, cb = rb >> 1, cb >> 1
        off = (rb == cb) & jnp.logical_not(same)
        bs = [jnp.where(off, a, 0.0).astype(BF16) for a in mats]
        tb = [t.astype(BF16) for t in ts]
        ys = [_dot(t, b).astype(BF16) for t, b in zip(tb, bs)]
        ts = [t - _dot(y, tl) for t, y, tl in zip(ts, ys, tb)]
    return ts


def _gdn_pre_kernel(x_ref, prev_ref, next_ref, a_ref, b_ref, cw_ref, alog_ref, dtb_ref,
                    u_ref, w_ref, qg_ref, kd_ref, at_ref, dec_ref, *, lat_chunks, ncl, ncc):
    first, last = _chunk_pos(pl.program_id(0), lat_chunks, ncl, ncc)
    nrow = CPP * CHUNK
    x = x_ref[...]
    xp = jnp.where(first, 0.0, prev_ref[7:8, :])
    xn = jnp.where(last, 0.0, next_ref[0:1, :])
    row = lax.broadcasted_iota(jnp.int32, (nrow, 1), 0)
    x_dn = jnp.where(row == 0, xp, pltpu.roll(x, 1, 0))
    x_up = jnp.where(row == nrow - 1, xn, pltpu.roll(x, nrow - 1, 0))
    cw = cw_ref[...]
    s = _silu(x_dn * cw[0:1] + x * cw[1:2] + x_up * cw[2:3])

    lane = lax.broadcasted_iota(jnp.int32, (1, 128), 1)
    g = -jnp.exp(alog_ref[...]) * _softplus(a_ref[...] + dtb_ref[...])
    beta = _sigmoid(b_ref[...])
    ri, ci = _tri_masks(CHUNK)
    eye_b = ri == ci
    eye = eye_b.astype(F32)
    low = (ri >= ci).astype(BF16)
    upp = (ri <= ci).astype(BF16)
    chunk_rows = [slice(c * CHUNK, (c + 1) * CHUNK) for c in range(CPP)]
    gc, g_last = [], []
    for c, rc in enumerate(chunk_rows):
        gcc = jnp.where(lane < HEADS, _dot_tri(low, g[rc]), _dot_tri(upp, g[rc]))
        gc.append(gcc)
        g_last.append(jnp.where(lane < HEADS, gcc[CHUNK - 1:CHUNK], gcc[0:1]))
        dec_ref[c] = jnp.exp(g_last[c])

    qn, kn, vs = [], [], []
    for h in range(HEADS):
        qh = s[:, h * HD:(h + 1) * HD]
        kh = s[:, BW + h * HD:BW + (h + 1) * HD]
        vs.append(s[:, 2 * BW + h * HD:2 * BW + (h + 1) * HD])
        qn.append(qh * lax.rsqrt(jnp.sum(qh * qh, axis=-1, keepdims=True) + EPS) * HD ** -0.5)
        kn.append(kh * lax.rsqrt(jnp.sum(kh * kh, axis=-1, keepdims=True) + EPS))
    qk = [[None] * HEADS for _ in range(CPP)]
    kk = [[None] * HEADS for _ in range(CPP)]
    for c, rc in enumerate(chunk_rows):
        for h in range(HEADS):
            knb = kn[h][rc].astype(BF16)
            qk[c][h] = _dot_nt(qn[h][rc].astype(BF16), knb)
            kk[c][h] = _dot_nt(knb, knb)

    probs = [(c, d, h) for c in range(CPP) for h in range(HEADS) for d in range(2)]
    mats, rhs = [], []
    for c, d, h in probs:
        rc = chunk_rows[c]
        sl = slice(h * HD, (h + 1) * HD)
        dh = d * HEADS + h
        gcc = gc[c][:, dh:dh + 1]
        bt = beta[rc, dh:dh + 1]
        grow = jnp.sum(jnp.where(eye_b, gcc, 0.0), axis=0, keepdims=True)
        incl = (ri >= ci) if d == 0 else (ri <= ci)
        strict = (ri > ci) if d == 0 else (ri < ci)
        dm = jnp.where(incl, jnp.exp(gcc - grow), 0.0)
        eg = jnp.exp(gcc)
        knc = kn[h][rc]
        mats.append(jnp.where(strict, bt * kk[c][h] * dm, 0.0))
        rhs.append(jnp.concatenate([vs[h][rc] * bt, knc * (bt * eg)], axis=1).astype(BF16))
        qg_ref[d, rc, sl] = (qn[h][rc] * eg).astype(BF16)
        kd_ref[d, rc, sl] = (knc * jnp.exp(g_last[c][:, dh:dh + 1] - gcc)).astype(BF16)
        at_ref[d, rc, h * HD:h * HD + CHUNK] = (qk[c][h] * dm).astype(BF16)
        at_ref[d, rc, h * HD + CHUNK:(h + 1) * HD] = jnp.zeros((CHUNK, HD - CHUNK), BF16)

    for (c, d, h), t, r in zip(probs, _unit_tri_inv(mats, ri, ci, eye), rhs):
        sl = slice(h * HD, (h + 1) * HD)
        uw = _dot(t.astype(BF16), r)
        u_ref[d, chunk_rows[c], sl] = uw[:, :HD]
        w_ref[d, chunk_rows[c], sl] = uw[:, HD:].astype(BF16)


def _gdn_pre(p, conv_w, a_log, dt_bias, nb, n_seq, n_ctx):
    t = p.shape[0]
    nrow = CPP * CHUNK
    assert n_seq % nrow == 0 and n_ctx % nrow == 0
    lat_blocks, nbl, nbc = nb * n_seq // nrow, n_seq // nrow, n_ctx // nrow
    qkv_blk = P_GQKV // (3 * BW)
    pad = lambda v: jnp.pad(v.reshape(1, -1), ((0, 0), (0, 128 - v.size)))
    dir_out = lambda dt: jax.ShapeDtypeStruct((2, t, BW), dt)
    dir_spec = pl.BlockSpec((2, nrow, BW), lambda g: (0, g, 0))
    r8 = nrow // 8
    return pl.pallas_call(
        functools.partial(_gdn_pre_kernel, lat_chunks=lat_blocks, ncl=nbl, ncc=nbc),
        grid=(t // nrow,),
        in_specs=[pl.BlockSpec((nrow, 3 * BW), lambda g: (g, qkv_blk)),
                  pl.BlockSpec((8, 3 * BW), lambda g: (jnp.maximum(g * r8 - 1, 0), qkv_blk)),
                  pl.BlockSpec((8, 3 * BW), lambda g: (jnp.minimum(g * r8 + r8, t // 8 - 1), qkv_blk)),
                  pl.BlockSpec((nrow, 128), lambda g: (g, P_GA // 128)),
                  pl.BlockSpec((nrow, 128), lambda g: (g, P_GB // 128)),
                  pl.BlockSpec((3, 3 * BW), lambda g: (0, 0)),
                  pl.BlockSpec((1, 128), lambda g: (0, 0)),
                  pl.BlockSpec((1, 128), lambda g: (0, 0))],
        out_specs=(dir_spec, dir_spec, dir_spec, dir_spec, dir_spec,
                   pl.BlockSpec((CPP, 1, 128), lambda g: (g, 0, 0))),
        out_shape=(dir_out(F32), dir_out(BF16), dir_out(BF16), dir_out(BF16), dir_out(BF16),
                   jax.ShapeDtypeStruct((t // CHUNK, 1, 128), F32)),
        compiler_params=_params("parallel"),
        name="gdn_chunk_prep",
    )(p, p, p, p, p, conv_w, pad(a_log), pad(dt_bias))


def _scan_rows(d, c):
    cc = c if d == 0 else CPS - 1 - c
    return slice(cc * CHUNK, (cc + 1) * CHUNK), cc


def _gdn_scan_kernel(*refs):
    ins, (of_ref, ob_ref, s_ref) = refs[:12], refs[12:]

    @pl.when(pl.program_id(1) == 0)
    def _():
        s_ref[...] = jnp.zeros(s_ref.shape, F32)

    chains = [(d, h) for d in range(2) for h in range(HEADS)]
    outs = (of_ref, ob_ref)
    st = [s_ref[d, h] for d, h in chains]
    for c in range(CPS):
        sb = [s.astype(BF16) for s in st]
        vn = []
        for (d, h), s in zip(chains, sb):
            u_ref, w_ref = ins[6 * d], ins[6 * d + 1]
            rows, _ = _scan_rows(d, c)
            sl = slice(h * HD, (h + 1) * HD)
            vn.append((u_ref[rows, sl] - _dot(w_ref[rows, sl], s)).astype(BF16))
        for i, (d, h) in enumerate(chains):
            qg_ref, kd_ref, at_ref, dec_ref = ins[6 * d + 2:6 * d + 6]
            rows, cc = _scan_rows(d, c)
            sl = slice(h * HD, (h + 1) * HD)
            outs[d][rows, sl] = _dot(qg_ref[rows, sl], sb[i]) + _dot(at_ref[rows, h * HD:h * HD + CHUNK], vn[i])
            dec = dec_ref[cc, 0:1, d * HEADS + h:d * HEADS + h + 1]
            st[i] = st[i] * dec + _dot_tn(kd_ref[rows, sl], vn[i])
    for (d, h), s in zip(chains, st):
        s_ref[d, h] = s


def _dir_scan(kernel, arrays, dec, nb, n_seq, n_ctx, scratch, name):
    t = arrays[0].shape[1]
    step = CPS * CHUNK
    assert n_seq % step == 0 and n_ctx % step == 0
    lat_blocks, ncl, ncc = nb * n_seq // step, n_seq // step, n_ctx // step
    in_specs, args = [], []
    for d in range(2):
        cmap = functools.partial(_scan_chunk, rev=bool(d), lat_chunks=lat_blocks, ncl=ncl, ncc=ncc)
        for arr in arrays:
            if arr.ndim == 3:
                in_specs.append(pl.BlockSpec((None, step, BW), lambda b, s, d=d, cmap=cmap: (d, cmap(b, s), 0)))
            else:
                in_specs.append(pl.BlockSpec((step, BW), lambda b, s, cmap=cmap: (cmap(b, s), 0)))
            args.append(arr)
        if dec.ndim == 3:
            in_specs.append(pl.BlockSpec((CPS, 1, dec.shape[-1]), lambda b, s, cmap=cmap: (cmap(b, s), 0, 0)))
        else:
            in_specs.append(pl.BlockSpec((None, CPS, 1, dec.shape[-1]),
                                         lambda b, s, d=d, cmap=cmap: (d, cmap(b, s), 0, 0)))
        args.append(dec)
    fmap = functools.partial(_scan_chunk, rev=False, lat_chunks=lat_blocks, ncl=ncl, ncc=ncc)
    bmap = functools.partial(_scan_chunk, rev=True, lat_chunks=lat_blocks, ncl=ncl, ncc=ncc)
    return pl.pallas_call(
        kernel,
        grid=(nb, ncc + ncl),
        in_specs=in_specs,
        out_specs=(pl.BlockSpec((step, BW), lambda b, s: (fmap(b, s), 0)),
                   pl.BlockSpec((step, BW), lambda b, s: (bmap(b, s), 0))),
        out_shape=(jax.ShapeDtypeStruct((t, BW), F32), jax.ShapeDtypeStruct((t, BW), F32)),
        scratch_shapes=[scratch],
        compiler_params=_params("parallel", "arbitrary"),
        name=name,
    )(*args)


def _hg_block_decay(gc, rev):
    out = None
    for b in range(CHUNK // SUB):
        lo, hi = gc[b * SUB:b * SUB + 1], gc[(b + 1) * SUB - 1:(b + 1) * SUB]
        dcy = (hi - lo) if rev else (lo - hi)
        out = dcy if out is None else jnp.maximum(out, dcy)
    return out


def _hg_scores(q, k, gc, rev, exact):
    nblk = CHUNK // SUB
    lane = lax.broadcasted_iota(jnp.int32, (1, CHUNK), 1)
    row8 = lax.broadcasted_iota(jnp.int32, (8, 1), 0)
    blk = lambda x, b: x[b * SUB:(b + 1) * SUB]
    ref_row = lambda b: b * SUB + (0 if rev else SUB - 1)
    refs = [gc[ref_row(b):ref_row(b) + 1] for b in range(nblk)]
    kt = jnp.concatenate([blk(k, b) * jnp.exp2(refs[b] - blk(gc, b)) for b in range(nblk)], axis=0)
    srcs = list(range(nblk - 1, 0, -1)) if rev else list(range(nblk - 1))
    rows_of = (lambda j: slice(0, j * SUB)) if rev else (lambda j: slice((j + 1) * SUB, CHUNK))
    qt = jnp.concatenate([q[rows_of(j)] * jnp.exp2(gc[rows_of(j)] - refs[j]) for j in srcs], axis=0)
    seg_off, off = {}, 0
    for j in srcs:
        seg_off[j] = off
        off += (j if rev else nblk - 1 - j) * SUB
    ktb, qtb = kt.astype(BF16), qt.astype(BF16)
    cross = [_dot_nt(qtb[:, h * HD:(h + 1) * HD], ktb[:, h * HD:(h + 1) * HD]) for h in range(HEADS)]

    def assemble(own):
        out = []
        for h in range(HEADS):
            groups = []
            for g8 in range(CHUNK // 8):
                r0 = 8 * g8
                bi = r0 // SUB
                val = own[h][g8]
                for j in srcs:
                    if (j < bi and not rev) or (j > bi and rev):
                        base = seg_off[j] + (r0 if rev else r0 - (j + 1) * SUB)
                        val = jnp.where((lane >= j * SUB) & (lane < (j + 1) * SUB), cross[h][base:base + 8], val)
                groups.append(val)
            out.append(jnp.concatenate(groups, axis=0))
        return tuple(out)

    def own_exact():
        acc = [[jnp.zeros((8, CHUNK), F32) for _ in range(CHUNK // 8)] for _ in range(HEADS)]
        for j in range(CHUNK):
            b, jj = j // SUB, j % SUB
            gj, kj = gc[j:j + 1], k[j:j + 1]
            for rg in range(SUB // 8):
                lo, hi = 8 * rg, 8 * rg + 7
                if (hi < jj and not rev) or (lo > jj and rev):
                    continue
                r0 = b * SUB + lo
                w = jnp.exp2(gc[r0:r0 + 8] - gj)
                if not ((lo >= jj and not rev) or (hi <= jj and rev)):
                    w = jnp.where((row8 + lo >= jj) if not rev else (row8 + lo <= jj), w, 0.0)
                tt = q[r0:r0 + 8] * w * kj
                g8 = r0 // 8
                for h in range(HEADS):
                    col = jnp.sum(tt[:, h * HD:(h + 1) * HD], axis=-1, keepdims=True)
                    acc[h][g8] = jnp.where(lane == j, col, acc[h][g8])
        return assemble(acc)

    first_row = lambda b: b * SUB + (SUB - 1 if rev else 0)
    starts = [gc[first_row(b):first_row(b) + 1] for b in range(nblk)]

    def own_matmul():
        rfull = jnp.concatenate([jnp.broadcast_to(starts[b], (SUB, q.shape[1])) for b in range(nblk)], axis=0)
        qd = (q * jnp.exp2(gc - rfull)).astype(BF16)
        kd = (k * jnp.exp2(rfull - gc)).astype(BF16)
        ri, ci = _tri_masks(CHUNK)
        sh = SUB.bit_length() - 1
        keep = ((ri >> sh) == (ci >> sh)) & ((ri <= ci) if rev else (ri >= ci))
        own = []
        for h in range(HEADS):
            sd = jnp.where(keep, _dot_nt(qd[:, h * HD:(h + 1) * HD], kd[:, h * HD:(h + 1) * HD]), 0.0)
            own.append([sd[8 * g8:8 * g8 + 8] for g8 in range(CHUNK // 8)])
        return assemble(own)

    return own_exact() if exact else own_matmul()


def _hg_pre_kernel(q_ref, f_ref, i_ref, lbl_ref, oi_ref, qg_ref, kd_ref, vb_ref, dec_ref, *, layer):
    lbl = lbl_ref[...]
    e = jnp.exp(lbl - jnp.max(lbl, axis=0, keepdims=True))
    sm = e / jnp.sum(e, axis=0, keepdims=True)
    lb_all = sm[0]
    for l in range(1, layer + 1):
        lb_all = lb_all + sm[l]
    lb_all = lb_all - sm[0]
    q_all = q_ref[...] * HD ** -0.5
    v_all = i_ref[...].astype(BF16)
    vb_ref[...] = v_all
    ri, ci = _tri_masks(CHUNK)
    work, decay = [], None
    for d in range(2):
        lb = lb_all[d:d + 1]
        f = f_ref[:, d * BW:(d + 1) * BW]
        e = jnp.exp(-jnp.abs(f))
        r = 1.0 / (1.0 + e)
        er = e * r
        pos = f >= 0.0
        sig, sig_neg = jnp.where(pos, r, er), jnp.where(pos, er, r)
        log_f = jnp.log2(jnp.maximum(lb, LB_FLOOR) + (1.0 - lb) * sig)
        k_all = (1.0 - lb) * sig_neg
        tri = ((ri >= ci) if d == 0 else (ri <= ci)).astype(BF16)
        for c in range(HG_CPP):
            rc = slice(c * CHUNK, (c + 1) * CHUNK)
            q, k, v = q_all[rc], k_all[rc], v_all[rc]
            gc = _dot_tri(tri, log_f[rc])
            g_last = gc[CHUNK - 1:CHUNK] if d == 0 else gc[0:1]
            dec_ref[d, c] = jnp.exp2(g_last)
            qg_ref[d, rc] = (q * jnp.exp2(gc)).astype(BF16)
            kd_ref[d, rc] = (k * jnp.exp2(g_last - gc)).astype(BF16)
            work.append((d, rc, q, k, v, gc))
            dcy = _hg_block_decay(gc, rev=bool(d))
            decay = dcy if decay is None else jnp.maximum(decay, dcy)

    def intra(exact):
        for d, rc, q, k, v, gc in work:
            scores = _hg_scores(q, k, gc, rev=bool(d), exact=exact)
            for h in range(HEADS):
                sl = slice(h * HD, (h + 1) * HD)
                oi_ref[d, rc, sl] = _dot(scores[h].astype(BF16), v[:, sl])

    safe = jnp.max(decay) <= HG_MAX_DECAY
    pl.when(safe)(lambda: intra(False))
    pl.when(jnp.logical_not(safe))(lambda: intra(True))


def _hg_pre(p, lb_logits, layer):
    t = p.shape[0]
    nc = t // CHUNK
    nrow = HG_CPP * CHUNK
    assert t % nrow == 0
    dir_out = lambda dt: jax.ShapeDtypeStruct((2, t, BW), dt)
    dir_spec = pl.BlockSpec((2, nrow, BW), lambda g: (0, g, 0))
    return pl.pallas_call(
        functools.partial(_hg_pre_kernel, layer=layer),
        grid=(t // nrow,),
        in_specs=[pl.BlockSpec((nrow, BW), lambda g: (g, P_HQ // BW)),
                  pl.BlockSpec((nrow, 2 * BW), lambda g: (g, P_HF // (2 * BW))),
                  pl.BlockSpec((nrow, BW), lambda g: (g, P_HI // BW)),
                  pl.BlockSpec(lb_logits.shape, lambda g: (0, 0, 0))],
        out_specs=(dir_spec, dir_spec, dir_spec,
                   pl.BlockSpec((nrow, BW), lambda g: (g, 0)),
                   pl.BlockSpec((2, HG_CPP, 1, BW), lambda g: (0, g, 0, 0))),
        out_shape=(dir_out(F32), dir_out(BF16), dir_out(BF16),
                   jax.ShapeDtypeStruct((t, BW), BF16),
                   jax.ShapeDtypeStruct((2, nc, 1, BW), F32)),
        compiler_params=_params("parallel"),
        name="hgrn2_chunk_prep",
    )(p, p, p, lb_logits)


def _hg_scan_kernel(*refs):
    ins, (of_ref, ob_ref, s_ref) = refs[:10], refs[10:]

    @pl.when(pl.program_id(1) == 0)
    def _():
        s_ref[...] = jnp.zeros(s_ref.shape, F32)

    chains = [(d, h) for d in range(2) for h in range(HEADS)]
    outs = (of_ref, ob_ref)
    incs = []
    for d, h in chains:
        kd_ref, v_ref = ins[5 * d + 2], ins[5 * d + 3]
        sl = slice(h * HD, (h + 1) * HD)
        incs.append([_dot_tn(v_ref[_scan_rows(d, c)[0], sl], kd_ref[_scan_rows(d, c)[0], sl])
                     for c in range(CPS)])
    states = []
    for i, (d, h) in enumerate(chains):
        dec_ref = ins[5 * d + 4]
        sl = slice(h * HD, (h + 1) * HD)
        st = s_ref[d, h]
        seq = []
        for c in range(CPS):
            seq.append(st.astype(BF16))
            st = st * dec_ref[_scan_rows(d, c)[1], :, sl] + incs[i][c]
        s_ref[d, h] = st
        states.append(seq)
    for i, (d, h) in enumerate(chains):
        oi_ref, qg_ref = ins[5 * d], ins[5 * d + 1]
        sl = slice(h * HD, (h + 1) * HD)
        for c in range(CPS):
            rows, _ = _scan_rows(d, c)
            outs[d][rows, sl] = oi_ref[rows, sl] + _dot_nt(qg_ref[rows, sl], states[i][c])


def _merge_kernel(h_ref, gof_ref, gob_ref, gg_ref, yb_ref, hof_ref, hob_ref, hgg_ref, u_ref,
                  gnorm_ref, hnorm_ref, gt_ref, nw_ref, wg_ref, wb_ref, wo_ref, o_ref):
    def readout(of_ref, ob_ref, gate_ref, norm_ref):
        o = of_ref[...] + ob_ref[...]
        gate = gate_ref[...]
        parts = []
        for h in range(HEADS):
            sl = slice(h * HD, (h + 1) * HD)
            parts.append(_rms(o[:, sl], norm_ref[...]) * _silu(gate[:, sl]))
        return jnp.concatenate(parts, axis=1).astype(BF16)

    ys = (readout(gof_ref, gob_ref, gg_ref, gnorm_ref),
          yb_ref[...].astype(BF16),
          readout(hof_ref, hob_ref, hgg_ref, hnorm_ref))
    d = h_ref.shape[1]
    u = u_ref[...]
    m = None
    for j in range(3):
        gate_logits = _dot(u, wg_ref[:, j * d:(j + 1) * d])
        term = _sigmoid(gate_logits) * _dot(ys[j], wb_ref[j])
        m = term if m is None else m + term
    y = _dot(m.astype(BF16), wo_ref[...])
    o_ref[...] = h_ref[...] + gt_ref[...] * _rms(y, nw_ref[...])


def _merge(h, g_of, g_ob, yb, h_of, h_ob, p, u, gnorm, hnorm, gate_mod, nw, wg, wb, wo, n_rows, n_seq, nb):
    d = h.shape[1]
    tm = TM_S
    row = lambda i: (i, 0)
    bw_spec = pl.BlockSpec((tm, BW), row)
    pcol = lambda off, width: pl.BlockSpec((tm, width), lambda i: (i, off // width))
    vec = lambda n: pl.BlockSpec((1, n), lambda i: (0, 0))
    return pl.pallas_call(
        _merge_kernel,
        grid=(n_rows // tm,),
        in_specs=[pl.BlockSpec((tm, d), row), bw_spec, bw_spec, pcol(P_GGATE, BW), bw_spec,
                  bw_spec, bw_spec, pcol(P_HGATE, BW), pl.BlockSpec((tm, d), row),
                  vec(HD), vec(HD),
                  pl.BlockSpec((None, 1, d), lambda i: (jnp.minimum((i * tm) // n_seq, nb), 0, 0)),
                  vec(d), _resident(wg.shape), _resident(wb.shape), _resident(wo.shape)],
        out_specs=pl.BlockSpec((tm, d), row),
        out_shape=jax.ShapeDtypeStruct((n_rows, d), F32),
        compiler_params=_params("parallel"),
        name="branch_merge",
    )(h, g_of, g_ob, p, yb, h_of, h_ob, p, u, gnorm, hnorm, gate_mod, nw, wg, wb, wo)


def _pack_w_in(w):
    d = w.shape[0]
    sizes = (BW, BW, BW, BW, 2 * HEADS, 2 * HEADS, Q_RANK, KV_RANK, ROPE, BW, 2 * BW, BW, BW, 3 * d)
    offs = [0]
    for s in sizes:
        offs.append(offs[-1] + s)
    part = lambda i: w[:, offs[i]:offs[i + 1]]
    gq, gk, gv, ggate, ga, gb, qa, kva, kr, hq, hf, hi, hgate, gates = (part(i) for i in range(14))
    z = lambda n: jnp.zeros((d, n), w.dtype)
    swap = jnp.concatenate([kr[:, 16:32], kr[:, 0:16], kr[:, 48:64], kr[:, 32:48]], axis=1)
    rec = jnp.concatenate([gq, gk, gv, ggate, hf, hq, hi, hgate, ga, z(120), gb, z(120)], axis=1).astype(BF16)
    mla = jnp.concatenate([qa, z(128), kva, kr, swap], axis=1).astype(BF16)
    assert rec.shape[1] == P_COLS and mla.shape[1] == M_COLS
    return rec.reshape(d, P_COLS // PROJ_TN, PROJ_TN).transpose(1, 0, 2), mla, gates.astype(BF16)


def _pack_wq(w):
    w = w.reshape(Q_RANK, HEADS, NOPE + ROPE)
    nope = w[:, :, :NOPE].reshape(Q_RANK, HEADS * NOPE)
    r = w[:, :, NOPE:]
    sw = jnp.concatenate([r[..., 16:32], r[..., 0:16], r[..., 48:64], r[..., 32:48]], axis=-1)
    rope = jnp.concatenate([r, sw], axis=-1).reshape(Q_RANK, HEADS * 2 * ROPE)
    return jnp.concatenate([nope, rope], axis=1).astype(BF16)


def _pack_wkv(w):
    w = w.reshape(KV_RANK, HEADS, NOPE + HD)
    return jnp.concatenate([w[:, :, :NOPE].reshape(KV_RANK, -1), w[:, :, NOPE:].reshape(KV_RANK, -1)],
                           axis=1).astype(BF16)


def _rope_table(nb, n_seq, n_ctx):
    nf = ROPE // 4
    rows = n_seq // GRID_W
    rpos = jnp.repeat(jnp.arange(rows, dtype=F32), GRID_W)
    cpos = jnp.tile(jnp.arange(GRID_W, dtype=F32), rows)
    inv = ROPE_BASE ** (-jnp.arange(nf, dtype=F32) / nf)
    ar, ac = rpos[:, None] * inv, cpos[:, None] * inv
    cos = jnp.concatenate([jnp.cos(ar), jnp.cos(ar), jnp.cos(ac), jnp.cos(ac)], axis=1)
    sin = jnp.concatenate([-jnp.sin(ar), jnp.sin(ar), -jnp.sin(ac), jnp.sin(ac)], axis=1)
    lat = jnp.tile(jnp.concatenate([cos, sin], axis=1), (nb, 1))
    ctx = jnp.concatenate([jnp.ones((nb * n_ctx, ROPE), F32), jnp.zeros((nb * n_ctx, ROPE), F32)], axis=1)
    return jnp.concatenate([lat, ctx], axis=0)


def kernel(x, c, ctx, c_ctx, w_ada, b_ada, norm_w, ffn_w_in, ffn_w_out, w_in, gdn_conv, gdn_a_log, gdn_dt_bias, gdn_norm, mla_q_norm, mla_kv_norm, mla_w_q_b, mla_w_kv_b, hg_lb_logits, hg_norm, w_branch, w_out):
    nb, n_seq, d = x.shape
    n_ctx = ctx.shape[1]
    depth = w_ada.shape[0]
    lat_rows, rows = nb * n_seq, nb * (n_seq + n_ctx)
    assert nb + 1 <= 8 and n_seq % TM == 0 and n_ctx % TM_S == 0 and (nb * n_ctx) % TM == 0
    assert n_seq % n_ctx == 0 and lat_rows % n_ctx == 0

    cc = jnp.concatenate([c, c_ctx[None], jnp.zeros((8 - nb - 1, d), F32)], axis=0)
    mods = _ada(cc, w_ada, b_ada)[:, :nb + 1].reshape(depth, nb + 1, 9, 1, d).transpose(0, 2, 1, 3, 4)
    cs = _rope_table(nb, n_seq, n_ctx)
    h = jnp.concatenate([x.reshape(lat_rows, d), ctx.reshape(nb * n_ctx, d)], axis=0)

    for l in range(depth):
        last = l == depth - 1
        md, nw = mods[l], norm_w[l][:, None, :]
        wi = [ffn_w_in[l, j].astype(BF16) for j in range(2)]
        wo = [ffn_w_out[l, j].astype(BF16) for j in range(2)]

        h, u = _ffn(h, md[0:3], nw[0:2], wi[0], wo[0], rows, n_seq, nb, mixer_mods=(nw[2], md[3], md[4]))
        w_rec, w_low, w_gates = _pack_w_in(w_in[l])
        p = _proj(u, w_rec)

        g_ops = _gdn_pre(p, gdn_conv[l], gdn_a_log[l], gdn_dt_bias[l], nb, n_seq, n_ctx)
        g_of, g_ob = _dir_scan(_gdn_scan_kernel, g_ops[:5], g_ops[5], nb, n_seq, n_ctx,
                               pltpu.VMEM((2, HEADS, HD, HD), F32), "gdn_scan")

        hoi, hqg, hkd, hvb, hdec = _hg_pre(p, hg_lb_logits, l)
        h_of, h_ob = _dir_scan(_hg_scan_kernel, (hoi, hqg, hkd, hvb), hdec, nb, n_seq, n_ctx,
                               pltpu.VMEM((2, HEADS, HD, HD), F32), "hgrn2_scan")

        q, k, v = _mla_prep(u, cs, mla_q_norm[l][None], mla_kv_norm[l][None], w_low,
                            _pack_wq(mla_w_q_b[l]), _pack_wkv(mla_w_kv_b[l]), nb, n_seq, n_ctx)
        yb = _attention(q, k, v, None, nb, n_seq, n_ctx, rows, ctx_only=False)
        if not last:
            yb = _attention(q, k, v, yb, nb, n_seq, n_ctx, rows, ctx_only=True)

        out_rows = lat_rows if last else rows
        h = _merge(h, g_of, g_ob, yb, h_of, h_ob, p, u, gdn_norm[l][None], hg_norm[l][None], md[5], nw[3],
                   w_gates, w_branch[l].astype(BF16), w_out[l].astype(BF16), out_rows, n_seq, nb)
        h = _ffn(h, md[6:9], nw[4:6], wi[1], wo[1], out_rows, n_seq, nb)
    return h.reshape(nb, n_seq, d)
```

```python
import functools

import jax
import jax.numpy as jnp
from jax import lax
from jax.experimental import pallas as pl
from jax.experimental.pallas import tpu as pltpu

F32 = jnp.float32
BF16 = jnp.bfloat16
EPS = 1e-6
LB_FLOOR = 1e-30
GRID_W = 64
ROPE_BASE = 10000.0

D_FF = 2816
HEADS = 4
HD = 128
ROPE = 64
NOPE = 128
Q_RANK = 384
KV_RANK = 256
CHUNK = 64
CPS = 4
CPP = 4
HG_CPP = 4
SUB = 16
BW = HEADS * HD

TM = 512
TM_S = 256
FF_CK = 256
VMEM_LIMIT = 48 * 1024 * 1024
ATTN_LOG2_SCALE = (NOPE + ROPE) ** -0.5 * 1.4426950408889634
KSUB = 256
HG_MAX_DECAY = 60.0
MAX_JUMP = 64.0

P_GQKV = 0
P_GGATE = 1536
P_HF = 2048
P_HQ = 3072
P_HI = 3584
P_HGATE = 4096
P_GA = 4608
P_GB = 4736
P_COLS = 4864
PROJ_TN = 2432
M_QA, M_KVA, M_KR, M_COLS = 0, 512, 768, 896

NT = (((1,), (1,)), ((), ()))
TN = (((0,), (0,)), ((), ()))


def _dot(a, b):
    return jnp.dot(a, b, preferred_element_type=F32)


def _dot_nt(a, b):
    return lax.dot_general(a, b, NT, preferred_element_type=F32)


def _dot_tn(a, b):
    return lax.dot_general(a, b, TN, preferred_element_type=F32)


def _sigmoid(x):
    return 1.0 / (1.0 + jnp.exp(-x))


def _silu(x):
    return x * _sigmoid(x)


def _softplus(x):
    return jnp.maximum(x, 0.0) + jnp.log(1.0 + jnp.exp(-jnp.abs(x)))


def _rms(x, w):
    return x * lax.rsqrt(jnp.mean(x * x, axis=-1, keepdims=True) + EPS) * w


def _resident(shape):
    zeros = (0,) * len(shape)
    return pl.BlockSpec(shape, lambda *_: zeros, pipeline_mode=pl.Buffered(1))


def _params(*sem):
    return pltpu.CompilerParams(dimension_semantics=sem, vmem_limit_bytes=VMEM_LIMIT)


def _ada_kernel(c_ref, w_ref, b_ref, o_ref):
    s = _silu(c_ref[...])
    o_ref[...] = _dot(s.astype(BF16), w_ref[...].astype(BF16)) + b_ref[...]


def _ada(cc, w_ada, b_ada):
    depth, d, nm = w_ada.shape
    tn = 1024
    return pl.pallas_call(
        _ada_kernel,
        grid=(depth, nm // tn),
        in_specs=[pl.BlockSpec((8, d), lambda l, j: (0, 0)),
                  pl.BlockSpec((None, d, tn), lambda l, j: (l, 0, j)),
                  pl.BlockSpec((None, 1, tn), lambda l, j: (l, 0, j))],
        out_specs=pl.BlockSpec((None, 8, tn), lambda l, j: (l, 0, j)),
        out_shape=jax.ShapeDtypeStruct((depth, 8, nm), F32),
        compiler_params=_params("parallel", "parallel"),
        name="ada_mod",
    )(cc, w_ada, b_ada.reshape(depth, 1, nm))


def _ffn_kernel(x_ref, sh_ref, sc_ref, gt_ref, prew_ref, postw_ref, wi_ref, wo_ref, *rest, emit_u):
    if emit_u:
        nw_ref, sh2_ref, sc2_ref, o_ref, u_ref = rest
    else:
        (o_ref,) = rest
    x = x_ref[...]
    hn = (_rms(x, prew_ref[...]) * (1.0 + sc_ref[...]) + sh_ref[...]).astype(BF16)
    acc = jnp.zeros(x.shape, F32)
    for c in range(D_FF // FF_CK):
        g = _dot(hn, wi_ref[:, c * FF_CK:(c + 1) * FF_CK])
        u = _dot(hn, wi_ref[:, D_FF + c * FF_CK:D_FF + (c + 1) * FF_CK])
        a = (_silu(g) * u).astype(BF16)
        acc = acc + _dot(a, wo_ref[c * FF_CK:(c + 1) * FF_CK, :])
    out = x + 0.5 * gt_ref[...] * _rms(acc, postw_ref[...])
    o_ref[...] = out
    if emit_u:
        u_ref[...] = (_rms(out, nw_ref[...]) * (1.0 + sc2_ref[...]) + sh2_ref[...]).astype(BF16)


def _ffn(h, mods, nw, wi, wo, n_rows, n_seq, nb, mixer_mods=None):
    d = h.shape[1]
    emit_u = mixer_mods is not None
    row = lambda i: (i, 0)
    mod = lambda i: (jnp.minimum((i * TM) // n_seq, nb), 0, 0)
    vec = pl.BlockSpec((1, d), lambda i: (0, 0))
    mspec = pl.BlockSpec((None, 1, d), mod)
    in_specs = [pl.BlockSpec((TM, d), row), mspec, mspec, mspec, vec, vec,
                _resident(wi.shape), _resident(wo.shape)]
    args = [h, mods[0], mods[1], mods[2], nw[0], nw[1], wi, wo]
    out_specs = pl.BlockSpec((TM, d), row)
    out_shape = jax.ShapeDtypeStruct((n_rows, d), F32)
    if emit_u:
        in_specs += [vec, mspec, mspec]
        args += list(mixer_mods)
        out_specs = (out_specs, pl.BlockSpec((TM, d), row))
        out_shape = (out_shape, jax.ShapeDtypeStruct((n_rows, d), BF16))
    return pl.pallas_call(
        functools.partial(_ffn_kernel, emit_u=emit_u),
        grid=(n_rows // TM,),
        in_specs=in_specs, out_specs=out_specs, out_shape=out_shape,
        compiler_params=_params("parallel"),
        name="ffn_sublayer",
    )(*args)


def _proj_kernel(x_ref, w_ref, o_ref):
    o_ref[...] = _dot(x_ref[...], w_ref[pl.program_id(1)])


def _proj(u, w):
    t, d = u.shape
    nj, _, tn = w.shape
    return pl.pallas_call(
        _proj_kernel,
        grid=(t // TM, nj),
        in_specs=[pl.BlockSpec((TM, d), lambda i, j: (i, 0)), _resident(w.shape)],
        out_specs=pl.BlockSpec((TM, tn), lambda i, j: (i, j)),
        out_shape=jax.ShapeDtypeStruct((t, nj * tn), F32),
        compiler_params=_params("parallel", "arbitrary"),
        name="in_proj",
    )(u, w)


def _mla_prep_kernel(u_ref, cs_ref, qn_ref, kvn_ref, win_ref, wq_ref, wkv_ref, q_ref, k_ref, v_ref):
    cs = cs_ref[...]
    low = _dot(u_ref[...], win_ref[...])
    qa = _rms(low[:, M_QA:M_QA + Q_RANK], qn_ref[...]).astype(BF16)
    kva = _rms(low[:, M_KVA:M_KVA + KV_RANK], kvn_ref[...]).astype(BF16)
    q2 = _dot(qa, wq_ref[...])
    kv2 = _dot(kva, wkv_ref[...])
    kr = low[:, M_KR:M_KR + 2 * ROPE] * cs
    kr = (kr + pltpu.roll(kr, ROPE, 1))[:, :ROPE].astype(BF16)
    q2 = q2 * ATTN_LOG2_SCALE
    for h in range(HEADS):
        q_ref[h, :, :NOPE] = q2[:, h * NOPE:(h + 1) * NOPE].astype(BF16)
        qr = q2[:, BW + h * 128:BW + (h + 1) * 128] * cs
        q_ref[h, :, NOPE:] = (qr + pltpu.roll(qr, ROPE, 1))[:, :ROPE].astype(BF16)
        k_ref[h, :, :NOPE] = kv2[:, h * NOPE:(h + 1) * NOPE].astype(BF16)
        k_ref[h, :, NOPE:] = kr
        v_ref[h] = kv2[:, BW + h * HD:BW + (h + 1) * HD].astype(BF16)


def _mla_prep(u, cs, qn, kvn, w_low, wq, wkv, nb, n_seq, n_ctx):
    t, d = u.shape
    tm = TM_S
    lat_tiles = nb * n_seq // tm
    per_lat = n_seq // tm
    per_ctx = n_ctx // tm

    def omap(i):
        j = i - lat_tiles
        b = jnp.where(i < lat_tiles, i // per_lat, j // per_ctx)
        blk = jnp.where(i < lat_tiles, i % per_lat, per_lat + j % per_ctx)
        return (b, 0, blk, 0)

    tk = n_seq + n_ctx
    return pl.pallas_call(
        _mla_prep_kernel,
        grid=(t // tm,),
        in_specs=[pl.BlockSpec((tm, d), lambda i: (i, 0)),
                  pl.BlockSpec((tm, 128), lambda i: (i, 0)),
                  pl.BlockSpec((1, Q_RANK), lambda i: (0, 0)),
                  pl.BlockSpec((1, KV_RANK), lambda i: (0, 0)),
                  _resident(w_low.shape), _resident(wq.shape), _resident(wkv.shape)],
        out_specs=(pl.BlockSpec((None, HEADS, tm, NOPE + ROPE), omap),
                   pl.BlockSpec((None, HEADS, tm, NOPE + ROPE), omap),
                   pl.BlockSpec((None, HEADS, tm, HD), omap)),
        out_shape=(jax.ShapeDtypeStruct((nb, HEADS, tk, NOPE + ROPE), BF16),
                   jax.ShapeDtypeStruct((nb, HEADS, tk, NOPE + ROPE), BF16),
                   jax.ShapeDtypeStruct((nb, HEADS, tk, HD), BF16)),
        compiler_params=_params("parallel"),
        name="mla_prep",
    )(u, cs, qn, kvn, w_low, wq, wkv)


def _attn_kernel(q_ref, k_ref, v_ref, *rest, first, kb, nkb):
    o_ref = rest[-1]
    q = q_ref[...]
    tq = q.shape[0]

    def block(j):
        sl = pl.ds(pl.multiple_of(j * kb, kb), kb)
        return _dot_nt(q, k_ref[sl, :]), v_ref[sl, :]

    def safe_update(t, v, carry):
        m, l, acc = carry
        m_new = jnp.maximum(m, jnp.max(t, axis=-1, keepdims=True))
        alpha = jnp.exp2(m - m_new)
        pr = jnp.exp2(t - m_new)
        l = alpha * l + jnp.sum(pr, axis=-1, keepdims=True)
        return m_new, l, alpha * acc + _dot(pr.astype(BF16), v)

    def fast_update(j, carry):
        m, l, acc, jump = carry
        bm = None
        for i in range(kb // KSUB):
            sl = pl.ds(pl.multiple_of(j * kb + i * KSUB, KSUB), KSUB)
            t = _dot_nt(q, k_ref[sl, :])
            pr = jnp.exp2(t - m)
            tm = jnp.max(t, axis=-1, keepdims=True)
            bm = tm if bm is None else jnp.maximum(bm, tm)
            l = l + jnp.sum(pr, axis=-1, keepdims=True)
            acc = acc + _dot(pr.astype(BF16), v_ref[sl, :])
        m_new = jnp.maximum(m, bm)
        alpha = jnp.exp2(m - m_new)
        return m_new, l * alpha, acc * alpha, jnp.maximum(jump, bm - m)

    init = (jnp.full((tq, 1), -1e30, F32), jnp.zeros((tq, 1), F32), jnp.zeros((tq, HD), F32))
    fsl = slice(first[0], first[0] + first[1])
    start = safe_update(_dot_nt(q, k_ref[fsl, :]), v_ref[fsl, :], init)
    if nkb == 0:
        o_ref[...] = start[2] / start[1]
        return
    _, l, acc, jump = lax.fori_loop(0, nkb, fast_update, start + (jnp.zeros((tq, 1), F32),),
                                    unroll=_pick_block(nkb, (4, 2, 1)))
    o_ref[...] = acc / l

    @pl.when(jnp.max(jump) > MAX_JUMP)
    def _():
        _, l2, acc2 = lax.fori_loop(0, nkb, lambda j, carry: safe_update(*block(j), carry), start)
        o_ref[...] = acc2 / l2


def _pick_block(n, cands):
    for c in cands:
        if n % c == 0:
            return c
    raise ValueError(f"no block size for {n}")


def _attention(q, k, v, o_prev, nb, n_seq, n_ctx, rows, ctx_only):
    if ctx_only:
        tq, tk = n_ctx, n_ctx
        koff = n_seq // n_ctx
        grid = (nb, HEADS, 1)
        qmap = lambda b, h, i: (b, h, koff, 0)
        kmap = lambda b, h, i: (b, h, koff, 0)
        omap = lambda b, h, i: (nb * n_seq // n_ctx + b, h)
        first, kb, nkb = (0, n_ctx), n_ctx, 0
    else:
        tq, tk = _pick_block(n_seq, (512, 256)), n_seq + n_ctx
        grid = (nb, HEADS, n_seq // tq)
        qmap = lambda b, h, i: (b, h, i, 0)
        kmap = lambda b, h, i: (b, h, 0, 0)
        omap = lambda b, h, i: (b * (n_seq // tq) + i, h)
        kb = _pick_block(n_seq, (1024, 512))
        first, nkb = (n_seq, n_ctx), n_seq // kb
    in_specs = [pl.BlockSpec((None, None, tq, NOPE + ROPE), qmap),
                pl.BlockSpec((None, None, tk, NOPE + ROPE), kmap),
                pl.BlockSpec((None, None, tk, HD), kmap)]
    args = [q, k, v]
    aliases = {}
    if ctx_only:
        in_specs.append(pl.BlockSpec(memory_space=pl.ANY))
        args.append(o_prev)
        aliases = {3: 0}
    return pl.pallas_call(
        functools.partial(_attn_kernel, first=first, kb=kb, nkb=nkb),
        grid=grid,
        in_specs=in_specs,
        out_specs=pl.BlockSpec((tq, HD), omap),
        out_shape=jax.ShapeDtypeStruct((rows, BW), F32),
        input_output_aliases=aliases,
        compiler_params=_params("parallel", "parallel", "parallel"),
        name="mla_attn_ctx" if ctx_only else "mla_attn",
    )(*args)


def _chunk_pos(g, lat_chunks, ncl, ncc):
    is_lat = g < lat_chunks
    pos = jnp.where(is_lat, g % ncl, (g - lat_chunks) % ncc)
    last = jnp.where(is_lat, pos == ncl - 1, pos == ncc - 1)
    return pos == 0, last


def _scan_chunk(b, s, rev, lat_chunks, ncl, ncc):
    c = jnp.where(s < ncc, s, s - ncc)
    if rev:
        c = jnp.where(s < ncc, ncc - 1 - c, ncl - 1 - c)
    return jnp.where(s < ncc, lat_chunks + b * ncc + c, b * ncl + c)


def _tri_masks(n):
    ri = lax.broadcasted_iota(jnp.int32, (n, n), 0)
    ci = lax.broadcasted_iota(jnp.int32, (n, n), 1)
    return ri, ci


def _split(x):
    hi = x.astype(BF16)
    return hi, (x - hi.astype(F32)).astype(BF16)


def _dot3(a, b):
    (ah, al), (bh, bl) = a, b
    return _dot(ah, bh) + (_dot(ah, bl) + _dot(al, bh))


def _dot_tri(tri, x):
    hi, lo = _split(x)
    lo2 = (x - hi.astype(F32) - lo.astype(F32)).astype(BF16)
    return _dot(tri, hi) + (_dot(tri, lo) + _dot(tri, lo2))


def _unit_tri_inv(mats, ri, ci, eye):
    rb, cb = ri >> 3, ci >> 3
    d8 = [jnp.where(rb == cb, a, 0.0) for a in mats]
    d8s = [_split(d) for d in d8]
    x2 = [_split(_dot3(d, d)) for d in d8s]
    ts = [eye - d for d in d8]
    x4 = [_split(_dot3(x, x)) for x in x2]
    ts = [t + _dot3(_split(t), x) for t, x in zip(ts, x2)]
    ts = [t + _dot3(_split(t), x) for t, x in zip(ts, x4)]
    for _ in range(3):
        same = rb == cb
        rb, cb = rb >> 1, cb >> 1
        off = (rb == cb) & jnp.logical_not(same)
        bs = [jnp.where(off, a, 0.0).astype(BF16) for a in mats]
        tb = [t.astype(BF16) for t in ts]
        ys = [_dot(t, b).astype(BF16) for t, b in zip(tb, bs)]
        ts = [t - _dot(y, tl) for t, y, tl in zip(ts, ys, tb)]
    return ts


def _gdn_pre_kernel(x_ref, prev_ref, next_ref, a_ref, b_ref, cw_ref, alog_ref, dtb_ref,
                    u_ref, w_ref, qg_ref, kd_ref, at_ref, dec_ref, *, lat_chunks, ncl, ncc):
    first, last = _chunk_pos(pl.program_id(0), lat_chunks, ncl, ncc)
    nrow = CPP * CHUNK
    x = x_ref[...]
    xp = jnp.where(first, 0.0, prev_ref[7:8, :])
    xn = jnp.where(last, 0.0, next_ref[0:1, :])
    row = lax.broadcasted_iota(jnp.int32, (nrow, 1), 0)
    x_dn = jnp.where(row == 0, xp, pltpu.roll(x, 1, 0))
    x_up = jnp.where(row == nrow - 1, xn, pltpu.roll(x, nrow - 1, 0))
    cw = cw_ref[...]
    s = _silu(x_dn * cw[0:1] + x * cw[1:2] + x_up * cw[2:3])

    lane = lax.broadcasted_iota(jnp.int32, (1, 128), 1)
    g = -jnp.exp(alog_ref[...]) * _softplus(a_ref[...] + dtb_ref[...])
    beta = _sigmoid(b_ref[...])
    ri, ci = _tri_masks(CHUNK)
    eye_b = ri == ci
    eye = eye_b.astype(F32)
    low = (ri >= ci).astype(BF16)
    upp = (ri <= ci).astype(BF16)
    chunk_rows = [slice(c * CHUNK, (c + 1) * CHUNK) for c in range(CPP)]
    gc, g_last = [], []
    for c, rc in enumerate(chunk_rows):
        gcc = jnp.where(lane < HEADS, _dot_tri(low, g[rc]), _dot_tri(upp, g[rc]))
        gc.append(gcc)
        g_last.append(jnp.where(lane < HEADS, gcc[CHUNK - 1:CHUNK], gcc[0:1]))
        dec_ref[c] = jnp.exp(g_last[c])

    qn, kn, vs = [], [], []
    for h in range(HEADS):
        qh = s[:, h * HD:(h + 1) * HD]
        kh = s[:, BW + h * HD:BW + (h + 1) * HD]
        vs.append(s[:, 2 * BW + h * HD:2 * BW + (h + 1) * HD])
        qn.append(qh * lax.rsqrt(jnp.sum(qh * qh, axis=-1, keepdims=True) + EPS) * HD ** -0.5)
        kn.append(kh * lax.rsqrt(jnp.sum(kh * kh, axis=-1, keepdims=True) + EPS))
    qk = [[None] * HEADS for _ in range(CPP)]
    kk = [[None] * HEADS for _ in range(CPP)]
    for c, rc in enumerate(chunk_rows):
        for h in range(HEADS):
            knb = kn[h][rc].astype(BF16)
            qk[c][h] = _dot_nt(qn[h][rc].astype(BF16), knb)
            kk[c][h] = _dot_nt(knb, knb)

    probs = [(c, d, h) for c in range(CPP) for h in range(HEADS) for d in range(2)]
    mats, rhs = [], []
    for c, d, h in probs:
        rc = chunk_rows[c]
        sl = slice(h * HD, (h + 1) * HD)
        dh = d * HEADS + h
        gcc = gc[c][:, dh:dh + 1]
        bt = beta[rc, dh:dh + 1]
        grow = jnp.sum(jnp.where(eye_b, gcc, 0.0), axis=0, keepdims=True)
        incl = (ri >= ci) if d == 0 else (ri <= ci)
        strict = (ri > ci) if d == 0 else (ri < ci)
        dm = jnp.where(incl, jnp.exp(gcc - grow), 0.0)
        eg = jnp.exp(gcc)
        knc = kn[h][rc]
        mats.append(jnp.where(strict, bt * kk[c][h] * dm, 0.0))
        rhs.append(jnp.concatenate([vs[h][rc] * bt, knc * (bt * eg)], axis=1).astype(BF16))
        qg_ref[d, rc, sl] = (qn[h][rc] * eg).astype(BF16)
        kd_ref[d, rc, sl] = (knc * jnp.exp(g_last[c][:, dh:dh + 1] - gcc)).astype(BF16)
        at_ref[d, rc, h * HD:h * HD + CHUNK] = (qk[c][h] * dm).astype(BF16)
        at_ref[d, rc, h * HD + CHUNK:(h + 1) * HD] = jnp.zeros((CHUNK, HD - CHUNK), BF16)

    for (c, d, h), t, r in zip(probs, _unit_tri_inv(mats, ri, ci, eye), rhs):
        sl = slice(h * HD, (h + 1) * HD)
        uw = _dot(t.astype(BF16), r)
        u_ref[d, chunk_rows[c], sl] = uw[:, :HD]
        w_ref[d, chunk_rows[c], sl] = uw[:, HD:].astype(BF16)


def _gdn_pre(p, conv_w, a_log, dt_bias, nb, n_seq, n_ctx):
    t = p.shape[0]
    nrow = CPP * CHUNK
    assert n_seq % nrow == 0 and n_ctx % nrow == 0
    lat_blocks, nbl, nbc = nb * n_seq // nrow, n_seq // nrow, n_ctx // nrow
    qkv_blk = P_GQKV // (3 * BW)
    pad = lambda v: jnp.pad(v.reshape(1, -1), ((0, 0), (0, 128 - v.size)))
    dir_out = lambda dt: jax.ShapeDtypeStruct((2, t, BW), dt)
    dir_spec = pl.BlockSpec((2, nrow, BW), lambda g: (0, g, 0))
    r8 = nrow // 8
    return pl.pallas_call(
        functools.partial(_gdn_pre_kernel, lat_chunks=lat_blocks, ncl=nbl, ncc=nbc),
        grid=(t // nrow,),
        in_specs=[pl.BlockSpec((nrow, 3 * BW), lambda g: (g, qkv_blk)),
                  pl.BlockSpec((8, 3 * BW), lambda g: (jnp.maximum(g * r8 - 1, 0), qkv_blk)),
                  pl.BlockSpec((8, 3 * BW), lambda g: (jnp.minimum(g * r8 + r8, t // 8 - 1), qkv_blk)),
                  pl.BlockSpec((nrow, 128), lambda g: (g, P_GA // 128)),
                  pl.BlockSpec((nrow, 128), lambda g: (g, P_GB // 128)),
                  pl.BlockSpec((3, 3 * BW), lambda g: (0, 0)),
                  pl.BlockSpec((1, 128), lambda g: (0, 0)),
                  pl.BlockSpec((1, 128), lambda g: (0, 0))],
        out_specs=(dir_spec, dir_spec, dir_spec, dir_spec, dir_spec,
                   pl.BlockSpec((CPP, 1, 128), lambda g: (g, 0, 0))),
        out_shape=(dir_out(F32), dir_out(BF16), dir_out(BF16), dir_out(BF16), dir_out(BF16),
                   jax.ShapeDtypeStruct((t // CHUNK, 1, 128), F32)),
        compiler_params=_params("parallel"),
        name="gdn_chunk_prep",
    )(p, p, p, p, p, conv_w, pad(a_log), pad(dt_bias))


def _scan_rows(d, c):
    cc = c if d == 0 else CPS - 1 - c
    return slice(cc * CHUNK, (cc + 1) * CHUNK), cc


def _gdn_scan_kernel(*refs):
    ins, (of_ref, ob_ref, s_ref) = refs[:12], refs[12:]

    @pl.when(pl.program_id(1) == 0)
    def _():
        s_ref[...] = jnp.zeros(s_ref.shape, F32)

    chains = [(d, h) for d in range(2) for h in range(HEADS)]
    outs = (of_ref, ob_ref)
    st = [s_ref[d, h] for d, h in chains]
    for c in range(CPS):
        sb = [s.astype(BF16) for s in st]
        vn = []
        for (d, h), s in zip(chains, sb):
            u_ref, w_ref = ins[6 * d], ins[6 * d + 1]
            rows, _ = _scan_rows(d, c)
            sl = slice(h * HD, (h + 1) * HD)
            vn.append((u_ref[rows, sl] - _dot(w_ref[rows, sl], s)).astype(BF16))
        for i, (d, h) in enumerate(chains):
            qg_ref, kd_ref, at_ref, dec_ref = ins[6 * d + 2:6 * d + 6]
            rows, cc = _scan_rows(d, c)
            sl = slice(h * HD, (h + 1) * HD)
            outs[d][rows, sl] = _dot(qg_ref[rows, sl], sb[i]) + _dot(at_ref[rows, h * HD:h * HD + CHUNK], vn[i])
            dec = dec_ref[cc, 0:1, d * HEADS + h:d * HEADS + h + 1]
            st[i] = st[i] * dec + _dot_tn(kd_ref[rows, sl], vn[i])
    for (d, h), s in zip(chains, st):
        s_ref[d, h] = s


def _dir_scan(kernel, arrays, dec, nb, n_seq, n_ctx, scratch, name):
    t = arrays[0].shape[1]
    step = CPS * CHUNK
    assert n_seq % step == 0 and n_ctx % step == 0
    lat_blocks, ncl, ncc = nb * n_seq // step, n_seq // step, n_ctx // step
    in_specs, args = [], []
    for d in range(2):
        cmap = functools.partial(_scan_chunk, rev=bool(d), lat_chunks=lat_blocks, ncl=ncl, ncc=ncc)
        for arr in arrays:
            if arr.ndim == 3:
                in_specs.append(pl.BlockSpec((None, step, BW), lambda b, s, d=d, cmap=cmap: (d, cmap(b, s), 0)))
            else:
                in_specs.append(pl.BlockSpec((step, BW), lambda b, s, cmap=cmap: (cmap(b, s), 0)))
            args.append(arr)
        if dec.ndim == 3:
            in_specs.append(pl.BlockSpec((CPS, 1, dec.shape[-1]), lambda b, s, cmap=cmap: (cmap(b, s), 0, 0)))
        else:
            in_specs.append(pl.BlockSpec((None, CPS, 1, dec.shape[-1]),
                                         lambda b, s, d=d, cmap=cmap: (d, cmap(b, s), 0, 0)))
        args.append(dec)
    fmap = functools.partial(_scan_chunk, rev=False, lat_chunks=lat_blocks, ncl=ncl, ncc=ncc)
    bmap = functools.partial(_scan_chunk, rev=True, lat_chunks=lat_blocks, ncl=ncl, ncc=ncc)
    return pl.pallas_call(
        kernel,
        grid=(nb, ncc + ncl),
        in_specs=in_specs,
        out_specs=(pl.BlockSpec((step, BW), lambda b, s: (fmap(b, s), 0)),
                   pl.BlockSpec((step, BW), lambda b, s: (bmap(b, s), 0))),
        out_shape=(jax.ShapeDtypeStruct((t, BW), F32), jax.ShapeDtypeStruct((t, BW), F32)),
        scratch_shapes=[scratch],
        compiler_params=_params("parallel", "arbitrary"),
        name=name,
    )(*args)


def _hg_block_decay(gc, rev):
    out = None
    for b in range(CHUNK // SUB):
        lo, hi = gc[b * SUB:b * SUB + 1], gc[(b + 1) * SUB - 1:(b + 1) * SUB]
        dcy = (hi - lo) if rev else (lo - hi)
        out = dcy if out is None else jnp.maximum(out, dcy)
    return out


def _hg_scores(q, k, gc, rev, exact):
    nblk = CHUNK // SUB
    lane = lax.broadcasted_iota(jnp.int32, (1, CHUNK), 1)
    row8 = lax.broadcasted_iota(jnp.int32, (8, 1), 0)
    blk = lambda x, b: x[b * SUB:(b + 1) * SUB]
    ref_row = lambda b: b * SUB + (0 if rev else SUB - 1)
    refs = [gc[ref_row(b):ref_row(b) + 1] for b in range(nblk)]
    kt = jnp.concatenate([blk(k, b) * jnp.exp2(refs[b] - blk(gc, b)) for b in range(nblk)], axis=0)
    srcs = list(range(nblk - 1, 0, -1)) if rev else list(range(nblk - 1))
    rows_of = (lambda j: slice(0, j * SUB)) if rev else (lambda j: slice((j + 1) * SUB, CHUNK))
    qt = jnp.concatenate([q[rows_of(j)] * jnp.exp2(gc[rows_of(j)] - refs[j]) for j in srcs], axis=0)
    seg_off, off = {}, 0
    for j in srcs:
        seg_off[j] = off
        off += (j if rev else nblk - 1 - j) * SUB
    ktb, qtb = kt.astype(BF16), qt.astype(BF16)
    cross = [_dot_nt(qtb[:, h * HD:(h + 1) * HD], ktb[:, h * HD:(h + 1) * HD]) for h in range(HEADS)]

    def assemble(own):
        out = []
        for h in range(HEADS):
            groups = []
            for g8 in range(CHUNK // 8):
                r0 = 8 * g8
                bi = r0 // SUB
                val = own[h][g8]
                for j in srcs:
                    if (j < bi and not rev) or (j > bi and rev):
                        base = seg_off[j] + (r0 if rev else r0 - (j + 1) * SUB)
                        val = jnp.where((lane >= j * SUB) & (lane < (j + 1) * SUB), cross[h][base:base + 8], val)
                groups.append(val)
            out.append(jnp.concatenate(groups, axis=0))
        return tuple(out)

    def own_exact():
        acc = [[jnp.zeros((8, CHUNK), F32) for _ in range(CHUNK // 8)] for _ in range(HEADS)]
        for j in range(CHUNK):
            b, jj = j // SUB, j % SUB
            gj, kj = gc[j:j + 1], k[j:j + 1]
            for rg in range(SUB // 8):
                lo, hi = 8 * rg, 8 * rg + 7
                if (hi < jj and not rev) or (lo > jj and rev):
                    continue
                r0 = b * SUB + lo
                w = jnp.exp2(gc[r0:r0 + 8] - gj)
                if not ((lo >= jj and not rev) or (hi <= jj and rev)):
                    w = jnp.where((row8 + lo >= jj) if not rev else (row8 + lo <= jj), w, 0.0)
                tt = q[r0:r0 + 8] * w * kj
                g8 = r0 // 8
                for h in range(HEADS):
                    col = jnp.sum(tt[:, h * HD:(h + 1) * HD], axis=-1, keepdims=True)
                    acc[h][g8] = jnp.where(lane == j, col, acc[h][g8])
        return assemble(acc)

    first_row = lambda b: b * SUB + (SUB - 1 if rev else 0)
    starts = [gc[first_row(b):first_row(b) + 1] for b in range(nblk)]

    def own_matmul():
        rfull = jnp.concatenate([jnp.broadcast_to(starts[b], (SUB, q.shape[1])) for b in range(nblk)], axis=0)
        qd = (q * jnp.exp2(gc - rfull)).astype(BF16)
        kd = (k * jnp.exp2(rfull - gc)).astype(BF16)
        ri, ci = _tri_masks(CHUNK)
        sh = SUB.bit_length() - 1
        keep = ((ri >> sh) == (ci >> sh)) & ((ri <= ci) if rev else (ri >= ci))
        own = []
        for h in range(HEADS):
            sd = jnp.where(keep, _dot_nt(qd[:, h * HD:(h + 1) * HD], kd[:, h * HD:(h + 1) * HD]), 0.0)
            own.append([sd[8 * g8:8 * g8 + 8] for g8 in range(CHUNK // 8)])
        return assemble(own)

    return own_exact() if exact else own_matmul()


def _hg_pre_kernel(q_ref, f_ref, i_ref, lbl_ref, oi_ref, qg_ref, kd_ref, vb_ref, dec_ref, *, layer):
    lbl = lbl_ref[...]
    e = jnp.exp(lbl - jnp.max(lbl, axis=0, keepdims=True))
    sm = e / jnp.sum(e, axis=0, keepdims=True)
    lb_all = sm[0]
    for l in range(1, layer + 1):
        lb_all = lb_all + sm[l]
    lb_all = lb_all - sm[0]
    q_all = q_ref[...] * HD ** -0.5
    v_all = i_ref[...].astype(BF16)
    vb_ref[...] = v_all
    ri, ci = _tri_masks(CHUNK)
    work, decay = [], None
    for d in range(2):
        lb = lb_all[d:d + 1]
        f = f_ref[:, d * BW:(d + 1) * BW]
        e = jnp.exp(-jnp.abs(f))
        r = 1.0 / (1.0 + e)
        er = e * r
        pos = f >= 0.0
        sig, sig_neg = jnp.where(pos, r, er), jnp.where(pos, er, r)
        log_f = jnp.log2(jnp.maximum(lb, LB_FLOOR) + (1.0 - lb) * sig)
        k_all = (1.0 - lb) * sig_neg
        tri = ((ri >= ci) if d == 0 else (ri <= ci)).astype(BF16)
        for c in range(HG_CPP):
            rc = slice(c * CHUNK, (c + 1) * CHUNK)
            q, k, v = q_all[rc], k_all[rc], v_all[rc]
            gc = _dot_tri(tri, log_f[rc])
            g_last = gc[CHUNK - 1:CHUNK] if d == 0 else gc[0:1]
            dec_ref[d, c] = jnp.exp2(g_last)
            qg_ref[d, rc] = (q * jnp.exp2(gc)).astype(BF16)
            kd_ref[d, rc] = (k * jnp.exp2(g_last - gc)).astype(BF16)
            work.append((d, rc, q, k, v, gc))
            dcy = _hg_block_decay(gc, rev=bool(d))
            decay = dcy if decay is None else jnp.maximum(decay, dcy)

    def intra(exact):
        for d, rc, q, k, v, gc in work:
            scores = _hg_scores(q, k, gc, rev=bool(d), exact=exact)
            for h in range(HEADS):
                sl = slice(h * HD, (h + 1) * HD)
                oi_ref[d, rc, sl] = _dot(scores[h].astype(BF16), v[:, sl])

    safe = jnp.max(decay) <= HG_MAX_DECAY
    pl.when(safe)(lambda: intra(False))
    pl.when(jnp.logical_not(safe))(lambda: intra(True))


def _hg_pre(p, lb_logits, layer):
    t = p.shape[0]
    nc = t // CHUNK
    nrow = HG_CPP * CHUNK
    assert t % nrow == 0
    dir_out = lambda dt: jax.ShapeDtypeStruct((2, t, BW), dt)
    dir_spec = pl.BlockSpec((2, nrow, BW), lambda g: (0, g, 0))
    return pl.pallas_call(
        functools.partial(_hg_pre_kernel, layer=layer),
        grid=(t // nrow,),
        in_specs=[pl.BlockSpec((nrow, BW), lambda g: (g, P_HQ // BW)),
                  pl.BlockSpec((nrow, 2 * BW), lambda g: (g, P_HF // (2 * BW))),
                  pl.BlockSpec((nrow, BW), lambda g: (g, P_HI // BW)),
                  pl.BlockSpec(lb_logits.shape, lambda g: (0, 0, 0))],
        out_specs=(dir_spec, dir_spec, dir_spec,
                   pl.BlockSpec((nrow, BW), lambda g: (g, 0)),
                   pl.BlockSpec((2, HG_CPP, 1, BW), lambda g: (0, g, 0, 0))),
        out_shape=(dir_out(F32), dir_out(BF16), dir_out(BF16),
                   jax.ShapeDtypeStruct((t, BW), BF16),
                   jax.ShapeDtypeStruct((2, nc, 1, BW), F32)),
        compiler_params=_params("parallel"),
        name="hgrn2_chunk_prep",
    )(p, p, p, lb_logits)


def _hg_scan_kernel(*refs):
    ins, (of_ref, ob_ref, s_ref) = refs[:10], refs[10:]

    @pl.when(pl.program_id(1) == 0)
    def _():
        s_ref[...] = jnp.zeros(s_ref.shape, F32)

    chains = [(d, h) for d in range(2) for h in range(HEADS)]
    outs = (of_ref, ob_ref)
    incs = []
    for d, h in chains:
        kd_ref, v_ref = ins[5 * d + 2], ins[5 * d + 3]
        sl = slice(h * HD, (h + 1) * HD)
        incs.append([_dot_tn(v_ref[_scan_rows(d, c)[0], sl], kd_ref[_scan_rows(d, c)[0], sl])
                     for c in range(CPS)])
    states = []
    for i, (d, h) in enumerate(chains):
        dec_ref = ins[5 * d + 4]
        sl = slice(h * HD, (h + 1) * HD)
        st = s_ref[d, h]
        seq = []
        for c in range(CPS):
            seq.append(st.astype(BF16))
            st = st * dec_ref[_scan_rows(d, c)[1], :, sl] + incs[i][c]
        s_ref[d, h] = st
        states.append(seq)
    for i, (d, h) in enumerate(chains):
        oi_ref, qg_ref = ins[5 * d], ins[5 * d + 1]
        sl = slice(h * HD, (h + 1) * HD)
        for c in range(CPS):
            rows, _ = _scan_rows(d, c)
            outs[d][rows, sl] = oi_ref[rows, sl] + _dot_nt(qg_ref[rows, sl], states[i][c])


def _merge_kernel(h_ref, gof_ref, gob_ref, gg_ref, yb_ref, hof_ref, hob_ref, hgg_ref, u_ref,
                  gnorm_ref, hnorm_ref, gt_ref, nw_ref, wg_ref, wb_ref, wo_ref, o_ref):
    def readout(of_ref, ob_ref, gate_ref, norm_ref):
        o = of_ref[...] + ob_ref[...]
        gate = gate_ref[...]
        parts = []
        for h in range(HEADS):
            sl = slice(h * HD, (h + 1) * HD)
            parts.append(_rms(o[:, sl], norm_ref[...]) * _silu(gate[:, sl]))
        return jnp.concatenate(parts, axis=1).astype(BF16)

    ys = (readout(gof_ref, gob_ref, gg_ref, gnorm_ref),
          yb_ref[...].astype(BF16),
          readout(hof_ref, hob_ref, hgg_ref, hnorm_ref))
    d = h_ref.shape[1]
    u = u_ref[...]
    m = None
    for j in range(3):
        gate_logits = _dot(u, wg_ref[:, j * d:(j + 1) * d])
        term = _sigmoid(gate_logits) * _dot(ys[j], wb_ref[j])
        m = term if m is None else m + term
    y = _dot(m.astype(BF16), wo_ref[...])
    o_ref[...] = h_ref[...] + gt_ref[...] * _rms(y, nw_ref[...])


def _merge(h, g_of, g_ob, yb, h_of, h_ob, p, u, gnorm, hnorm, gate_mod, nw, wg, wb, wo, n_rows, n_seq, nb):
    d = h.shape[1]
    tm = TM
    row = lambda i: (i, 0)
    bw_spec = pl.BlockSpec((tm, BW), row)
    pcol = lambda off, width: pl.BlockSpec((tm, width), lambda i: (i, off // width))
    vec = lambda n: pl.BlockSpec((1, n), lambda i: (0, 0))
    return pl.pallas_call(
        _merge_kernel,
        grid=(n_rows // tm,),
        in_specs=[pl.BlockSpec((tm, d), row), bw_spec, bw_spec, pcol(P_GGATE, BW), bw_spec,
                  bw_spec, bw_spec, pcol(P_HGATE, BW), pl.BlockSpec((tm, d), row),
                  vec(HD), vec(HD),
                  pl.BlockSpec((None, 1, d), lambda i: (jnp.minimum((i * tm) // n_seq, nb), 0, 0)),
                  vec(d), _resident(wg.shape), _resident(wb.shape), _resident(wo.shape)],
        out_specs=pl.BlockSpec((tm, d), row),
        out_shape=jax.ShapeDtypeStruct((n_rows, d), F32),
        compiler_params=_params("parallel"),
        name="branch_merge",
    )(h, g_of, g_ob, p, yb, h_of, h_ob, p, u, gnorm, hnorm, gate_mod, nw, wg, wb, wo)


def _pack_w_in(w):
    d = w.shape[0]
    sizes = (BW, BW, BW, BW, 2 * HEADS, 2 * HEADS, Q_RANK, KV_RANK, ROPE, BW, 2 * BW, BW, BW, 3 * d)
    offs = [0]
    for s in sizes:
        offs.append(offs[-1] + s)
    part = lambda i: w[:, offs[i]:offs[i + 1]]
    gq, gk, gv, ggate, ga, gb, qa, kva, kr, hq, hf, hi, hgate, gates = (part(i) for i in range(14))
    z = lambda n: jnp.zeros((d, n), w.dtype)
    swap = jnp.concatenate([kr[:, 16:32], kr[:, 0:16], kr[:, 48:64], kr[:, 32:48]], axis=1)
    rec = jnp.concatenate([gq, gk, gv, ggate, hf, hq, hi, hgate, ga, z(120), gb, z(120)], axis=1).astype(BF16)
    mla = jnp.concatenate([qa, z(128), kva, kr, swap], axis=1).astype(BF16)
    assert rec.shape[1] == P_COLS and mla.shape[1] == M_COLS
    return rec.reshape(d, P_COLS // PROJ_TN, PROJ_TN).transpose(1, 0, 2), mla, gates.astype(BF16)


def _pack_wq(w):
    w = w.reshape(Q_RANK, HEADS, NOPE + ROPE)
    nope = w[:, :, :NOPE].reshape(Q_RANK, HEADS * NOPE)
    r = w[:, :, NOPE:]
    sw = jnp.concatenate([r[..., 16:32], r[..., 0:16], r[..., 48:64], r[..., 32:48]], axis=-1)
    rope = jnp.concatenate([r, sw], axis=-1).reshape(Q_RANK, HEADS * 2 * ROPE)
    return jnp.concatenate([nope, rope], axis=1).astype(BF16)


def _pack_wkv(w):
    w = w.reshape(KV_RANK, HEADS, NOPE + HD)
    return jnp.concatenate([w[:, :, :NOPE].reshape(KV_RANK, -1), w[:, :, NOPE:].reshape(KV_RANK, -1)],
                           axis=1).astype(BF16)


def _rope_table(nb, n_seq, n_ctx):
    nf = ROPE // 4
    rows = n_seq // GRID_W
    rpos = jnp.repeat(jnp.arange(rows, dtype=F32), GRID_W)
    cpos = jnp.tile(jnp.arange(GRID_W, dtype=F32), rows)
    inv = ROPE_BASE ** (-jnp.arange(nf, dtype=F32) / nf)
    ar, ac = rpos[:, None] * inv, cpos[:, None] * inv
    cos = jnp.concatenate([jnp.cos(ar), jnp.cos(ar), jnp.cos(ac), jnp.cos(ac)], axis=1)
    sin = jnp.concatenate([-jnp.sin(ar), jnp.sin(ar), -jnp.sin(ac), jnp.sin(ac)], axis=1)
    lat = jnp.tile(jnp.concatenate([cos, sin], axis=1), (nb, 1))
    ctx = jnp.concatenate([jnp.ones((nb * n_ctx, ROPE), F32), jnp.zeros((nb * n_ctx, ROPE), F32)], axis=1)
    return jnp.concatenate([lat, ctx], axis=0)


def kernel(x, c, ctx, c_ctx, w_ada, b_ada, norm_w, ffn_w_in, ffn_w_out, w_in, gdn_conv, gdn_a_log, gdn_dt_bias, gdn_norm, mla_q_norm, mla_kv_norm, mla_w_q_b, mla_w_kv_b, hg_lb_logits, hg_norm, w_branch, w_out):
    nb, n_seq, d = x.shape
    n_ctx = ctx.shape[1]
    depth = w_ada.shape[0]
    lat_rows, rows = nb * n_seq, nb * (n_seq + n_ctx)
    assert nb + 1 <= 8 and n_seq % TM == 0 and n_ctx % TM_S == 0 and (nb * n_ctx) % TM == 0
    assert n_seq % n_ctx == 0 and lat_rows % n_ctx == 0

    cc = jnp.concatenate([c, c_ctx[None], jnp.zeros((8 - nb - 1, d), F32)], axis=0)
    mods = _ada(cc, w_ada, b_ada)[:, :nb + 1].reshape(depth, nb + 1, 9, 1, d).transpose(0, 2, 1, 3, 4)
    cs = _rope_table(nb, n_seq, n_ctx)
    h = jnp.concatenate([x.reshape(lat_rows, d), ctx.reshape(nb * n_ctx, d)], axis=0)

    for l in range(depth):
        last = l == depth - 1
        md, nw = mods[l], norm_w[l][:, None, :]
        wi = [ffn_w_in[l, j].astype(BF16) for j in range(2)]
        wo = [ffn_w_out[l, j].astype(BF16) for j in range(2)]

        h, u = _ffn(h, md[0:3], nw[0:2], wi[0], wo[0], rows, n_seq, nb, mixer_mods=(nw[2], md[3], md[4]))
        w_rec, w_low, w_gates = _pack_w_in(w_in[l])
        p = _proj(u, w_rec)

        g_ops = _gdn_pre(p, gdn_conv[l], gdn_a_log[l], gdn_dt_bias[l], nb, n_seq, n_ctx)
        g_of, g_ob = _dir_scan(_gdn_scan_kernel, g_ops[:5], g_ops[5], nb, n_seq, n_ctx,
                               pltpu.VMEM((2, HEADS, HD, HD), F32), "gdn_scan")

        hoi, hqg, hkd, hvb, hdec = _hg_pre(p, hg_lb_logits, l)
        h_of, h_ob = _dir_scan(_hg_scan_kernel, (hoi, hqg, hkd, hvb), hdec, nb, n_seq, n_ctx,
                               pltpu.VMEM((2, HEADS, HD, HD), F32), "hgrn2_scan")

        q, k, v = _mla_prep(u, cs, mla_q_norm[l][None], mla_kv_norm[l][None], w_low,
                            _pack_wq(mla_w_q_b[l]), _pack_wkv(mla_w_kv_b[l]), nb, n_seq, n_ctx)
        yb = _attention(q, k, v, None, nb, n_seq, n_ctx, rows, ctx_only=False)
        if not last:
            yb = _attention(q, k, v, yb, nb, n_seq, n_ctx, rows, ctx_only=True)

        out_rows = lat_rows if last else rows
        h = _merge(h, g_of, g_ob, yb, h_of, h_ob, p, u, gdn_norm[l][None], hg_norm[l][None], md[5], nw[3],
                   w_gates, w_branch[l].astype(BF16), w_out[l].astype(BF16), out_rows, n_seq, nb)
        h = _ffn(h, md[6:9], nw[4:6], wi[1], wo[1], out_rows, n_seq, nb)
    return h.reshape(nb, n_seq, d)
```

```python
import functools

import jax
import jax.numpy as jnp
from jax import lax
from jax.experimental import pallas as pl
from jax.experimental.pallas import tpu as pltpu

F32 = jnp.float32
BF16 = jnp.bfloat16
EPS = 1e-6
LB_FLOOR = 1e-30
GRID_W = 64
ROPE_BASE = 10000.0

D_FF = 2816
HEADS = 4
HD = 128
ROPE = 64
NOPE = 128
Q_RANK = 384
KV_RANK = 256
CHUNK = 64
CPS = 4
CPP = 4
HG_CPP = 4
SUB = 16
BW = HEADS * HD

TM = 512
TM_S = 256
FF_CK = 256
VMEM_LIMIT = 48 * 1024 * 1024
ATTN_LOG2_SCALE = (NOPE + ROPE) ** -0.5 * 1.4426950408889634
HPS = 2
KSUB = 256
HG_MAX_DECAY = 100.0
MAX_JUMP = 64.0

P_GQKV = 0
P_GGATE = 1536
P_HF = 2048
P_HQ = 3072
P_HI = 3584
P_HGATE = 4096
P_GA = 4608
P_GB = 4736
P_COLS = 4864
PROJ_TN = 2432
M_QA, M_KVA, M_KR, M_COLS = 0, 512, 768, 896

NT = (((1,), (1,)), ((), ()))
TN = (((0,), (0,)), ((), ()))


def _dot(a, b):
    return jnp.dot(a, b, preferred_element_type=F32)


def _dot_nt(a, b):
    return lax.dot_general(a, b, NT, preferred_element_type=F32)


def _dot_tn(a, b):
    return lax.dot_general(a, b, TN, preferred_element_type=F32)


def _sigmoid(x):
    return 1.0 / (1.0 + jnp.exp(-x))


def _silu(x):
    return x * _sigmoid(x)


def _softplus(x):
    return jnp.maximum(x, 0.0) + jnp.log(1.0 + jnp.exp(-jnp.abs(x)))


def _rms(x, w):
    return x * lax.rsqrt(jnp.mean(x * x, axis=-1, keepdims=True) + EPS) * w


def _resident(shape):
    zeros = (0,) * len(shape)
    return pl.BlockSpec(shape, lambda *_: zeros, pipeline_mode=pl.Buffered(1))


def _params(*sem):
    return pltpu.CompilerParams(dimension_semantics=sem, vmem_limit_bytes=VMEM_LIMIT)


def _ada_kernel(c_ref, w_ref, b_ref, o_ref):
    s = _silu(c_ref[...])
    o_ref[...] = _dot(s.astype(BF16), w_ref[...].astype(BF16)) + b_ref[...]


def _ada(cc, w_ada, b_ada):
    depth, d, nm = w_ada.shape
    tn = 1024
    return pl.pallas_call(
        _ada_kernel,
        grid=(depth, nm // tn),
        in_specs=[pl.BlockSpec((8, d), lambda l, j: (0, 0)),
                  pl.BlockSpec((None, d, tn), lambda l, j: (l, 0, j)),
                  pl.BlockSpec((None, 1, tn), lambda l, j: (l, 0, j))],
        out_specs=pl.BlockSpec((None, 8, tn), lambda l, j: (l, 0, j)),
        out_shape=jax.ShapeDtypeStruct((depth, 8, nm), F32),
        compiler_params=_params("parallel", "parallel"),
        name="ada_mod",
    )(cc, w_ada, b_ada.reshape(depth, 1, nm))


def _ffn_kernel(x_ref, sh_ref, sc_ref, gt_ref, prew_ref, postw_ref, wi_ref, wo_ref, *rest, emit_u):
    if emit_u:
        nw_ref, sh2_ref, sc2_ref, o_ref, u_ref = rest
    else:
        (o_ref,) = rest
    x = x_ref[...]
    hn = (_rms(x, prew_ref[...]) * (1.0 + sc_ref[...]) + sh_ref[...]).astype(BF16)
    acc = jnp.zeros(x.shape, F32)
    for c in range(D_FF // FF_CK):
        g = _dot(hn, wi_ref[:, c * FF_CK:(c + 1) * FF_CK])
        u = _dot(hn, wi_ref[:, D_FF + c * FF_CK:D_FF + (c + 1) * FF_CK])
        a = (_silu(g) * u).astype(BF16)
        acc = acc + _dot(a, wo_ref[c * FF_CK:(c + 1) * FF_CK, :])
    out = x + 0.5 * gt_ref[...] * _rms(acc, postw_ref[...])
    o_ref[...] = out
    if emit_u:
        u_ref[...] = (_rms(out, nw_ref[...]) * (1.0 + sc2_ref[...]) + sh2_ref[...]).astype(BF16)


def _ffn(h, mods, nw, wi, wo, n_rows, n_seq, nb, mixer_mods=None):
    d = h.shape[1]
    emit_u = mixer_mods is not None
    row = lambda i: (i, 0)
    mod = lambda i: (jnp.minimum((i * TM) // n_seq, nb), 0, 0)
    vec = pl.BlockSpec((1, d), lambda i: (0, 0))
    mspec = pl.BlockSpec((None, 1, d), mod)
    in_specs = [pl.BlockSpec((TM, d), row), mspec, mspec, mspec, vec, vec,
                _resident(wi.shape), _resident(wo.shape)]
    args = [h, mods[0], mods[1], mods[2], nw[0], nw[1], wi, wo]
    out_specs = pl.BlockSpec((TM, d), row)
    out_shape = jax.ShapeDtypeStruct((n_rows, d), F32)
    if emit_u:
        in_specs += [vec, mspec, mspec]
        args += list(mixer_mods)
        out_specs = (out_specs, pl.BlockSpec((TM, d), row))
        out_shape = (out_shape, jax.ShapeDtypeStruct((n_rows, d), BF16))
    return pl.pallas_call(
        functools.partial(_ffn_kernel, emit_u=emit_u),
        grid=(n_rows // TM,),
        in_specs=in_specs, out_specs=out_specs, out_shape=out_shape,
        compiler_params=_params("parallel"),
        name="ffn_sublayer",
    )(*args)


def _proj_kernel(x_ref, w_ref, o_ref):
    o_ref[...] = _dot(x_ref[...], w_ref[pl.program_id(1)])


def _proj(u, w):
    t, d = u.shape
    nj, _, tn = w.shape
    return pl.pallas_call(
        _proj_kernel,
        grid=(t // TM, nj),
        in_specs=[pl.BlockSpec((TM, d), lambda i, j: (i, 0)), _resident(w.shape)],
        out_specs=pl.BlockSpec((TM, tn), lambda i, j: (i, j)),
        out_shape=jax.ShapeDtypeStruct((t, nj * tn), F32),
        compiler_params=_params("parallel", "arbitrary"),
        name="in_proj",
    )(u, w)


def _mla_prep_kernel(u_ref, cs_ref, qn_ref, kvn_ref, win_ref, wq_ref, wkv_ref, q_ref, k_ref, v_ref):
    cs = cs_ref[...]
    low = _dot(u_ref[...], win_ref[...])
    qa = _rms(low[:, M_QA:M_QA + Q_RANK], qn_ref[...]).astype(BF16)
    kva = _rms(low[:, M_KVA:M_KVA + KV_RANK], kvn_ref[...]).astype(BF16)
    q2 = _dot(qa, wq_ref[...])
    kv2 = _dot(kva, wkv_ref[...])
    kr = low[:, M_KR:M_KR + 2 * ROPE] * cs
    kr = (kr + pltpu.roll(kr, ROPE, 1))[:, :ROPE].astype(BF16)
    q2 = q2 * ATTN_LOG2_SCALE
    for h in range(HEADS):
        q_ref[h, :, :NOPE] = q2[:, h * NOPE:(h + 1) * NOPE].astype(BF16)
        qr = q2[:, BW + h * 128:BW + (h + 1) * 128] * cs
        q_ref[h, :, NOPE:] = (qr + pltpu.roll(qr, ROPE, 1))[:, :ROPE].astype(BF16)
        k_ref[h, :, :NOPE] = kv2[:, h * NOPE:(h + 1) * NOPE].astype(BF16)
        k_ref[h, :, NOPE:] = kr
        v_ref[h] = kv2[:, BW + h * HD:BW + (h + 1) * HD].astype(BF16)


def _mla_prep(u, cs, qn, kvn, w_low, wq, wkv, nb, n_seq, n_ctx):
    t, d = u.shape
    tm = TM_S
    lat_tiles = nb * n_seq // tm
    per_lat = n_seq // tm
    per_ctx = n_ctx // tm

    def omap(i):
        j = i - lat_tiles
        b = jnp.where(i < lat_tiles, i // per_lat, j // per_ctx)
        blk = jnp.where(i < lat_tiles, i % per_lat, per_lat + j % per_ctx)
        return (b, 0, blk, 0)

    tk = n_seq + n_ctx
    return pl.pallas_call(
        _mla_prep_kernel,
        grid=(t // tm,),
        in_specs=[pl.BlockSpec((tm, d), lambda i: (i, 0)),
                  pl.BlockSpec((tm, 128), lambda i: (i, 0)),
                  pl.BlockSpec((1, Q_RANK), lambda i: (0, 0)),
                  pl.BlockSpec((1, KV_RANK), lambda i: (0, 0)),
                  _resident(w_low.shape), _resident(wq.shape), _resident(wkv.shape)],
        out_specs=(pl.BlockSpec((None, HEADS, tm, NOPE + ROPE), omap),
                   pl.BlockSpec((None, HEADS, tm, NOPE + ROPE), omap),
                   pl.BlockSpec((None, HEADS, tm, HD), omap)),
        out_shape=(jax.ShapeDtypeStruct((nb, HEADS, tk, NOPE + ROPE), BF16),
                   jax.ShapeDtypeStruct((nb, HEADS, tk, NOPE + ROPE), BF16),
                   jax.ShapeDtypeStruct((nb, HEADS, tk, HD), BF16)),
        compiler_params=_params("parallel"),
        name="mla_prep",
    )(u, cs, qn, kvn, w_low, wq, wkv)


def _attn_kernel(q_ref, k_ref, v_ref, *rest, first, kb, nkb):
    o_ref = rest[-1]
    nh, tq = q_ref.shape[0], q_ref.shape[1]
    qs = [q_ref[h] for h in range(nh)]

    def safe_update(h, sl, carry):
        m, l, acc = carry
        t = _dot_nt(qs[h], k_ref[h, sl, :])
        m_new = jnp.maximum(m, jnp.max(t, axis=-1, keepdims=True))
        alpha = jnp.exp2(m - m_new)
        pr = jnp.exp2(t - m_new)
        l = alpha * l + jnp.sum(pr, axis=-1, keepdims=True)
        return m_new, l, alpha * acc + _dot(pr.astype(BF16), v_ref[h, sl, :])

    def fast_update(j, carry):
        st = [list(c) for c in carry]
        bm = [None] * nh
        items = [(pl.ds(pl.multiple_of(j * kb + i * KSUB, KSUB), KSUB), h)
                 for i in range(kb // KSUB) for h in range(nh)]
        score = lambda it: _dot_nt(qs[it[1]], k_ref[it[1], it[0], :])
        t_next = score(items[0])
        for n, (sl, h) in enumerate(items):
            t = t_next
            if n + 1 < len(items):
                t_next = score(items[n + 1])
            pr = jnp.exp2(t - st[h][0])
            tm = jnp.max(t, axis=-1, keepdims=True)
            bm[h] = tm if bm[h] is None else jnp.maximum(bm[h], tm)
            st[h][1] = st[h][1] + jnp.sum(pr, axis=-1, keepdims=True)
            st[h][2] = st[h][2] + _dot(pr.astype(BF16), v_ref[h, sl, :])
        out = []
        for h in range(nh):
            m, l, acc, jump = st[h]
            m_new = jnp.maximum(m, bm[h])
            alpha = jnp.exp2(m - m_new)
            out.append((m_new, l * alpha, acc * alpha, jnp.maximum(jump, bm[h] - m)))
        return tuple(out)

    init = (jnp.full((tq, 1), -1e30, F32), jnp.zeros((tq, 1), F32), jnp.zeros((tq, HD), F32))
    fsl = slice(first[0], first[0] + first[1])
    start = [safe_update(h, fsl, init) for h in range(nh)]
    if nkb == 0:
        for h in range(nh):
            o_ref[:, h * HD:(h + 1) * HD] = start[h][2] / start[h][1]
        return
    done = lax.fori_loop(0, nkb, fast_update, tuple(s + (jnp.zeros((tq, 1), F32),) for s in start),
                         unroll=_pick_block(nkb, (4, 2, 1)))
    for h in range(nh):
        o_ref[:, h * HD:(h + 1) * HD] = done[h][2] / done[h][1]

    for h in range(nh):
        @pl.when(jnp.max(done[h][3]) > MAX_JUMP)
        def _(h=h):
            step = lambda j, carry: safe_update(h, pl.ds(pl.multiple_of(j * kb, kb), kb), carry)
            _, l2, acc2 = lax.fori_loop(0, nkb, step, start[h])
            o_ref[:, h * HD:(h + 1) * HD] = acc2 / l2


def _pick_block(n, cands):
    for c in cands:
        if n % c == 0:
            return c
    raise ValueError(f"no block size for {n}")


def _attention(q, k, v, o_prev, nb, n_seq, n_ctx, rows, ctx_only):
    if ctx_only:
        tq, tk = n_ctx, n_ctx
        koff = n_seq // n_ctx
        grid = (nb, HEADS // HPS, 1)
        qmap = lambda b, h, i: (b, h, koff, 0)
        kmap = lambda b, h, i: (b, h, koff, 0)
        omap = lambda b, h, i: (nb * n_seq // n_ctx + b, h)
        first, kb, nkb = (0, n_ctx), n_ctx, 0
    else:
        tq, tk = _pick_block(n_seq, (512, 256)), n_seq + n_ctx
        grid = (nb, HEADS // HPS, n_seq // tq)
        qmap = lambda b, h, i: (b, h, i, 0)
        kmap = lambda b, h, i: (b, h, 0, 0)
        omap = lambda b, h, i: (b * (n_seq // tq) + i, h)
        kb = _pick_block(n_seq, (1024, 512))
        first, nkb = (n_seq, n_ctx), n_seq // kb
    in_specs = [pl.BlockSpec((None, HPS, tq, NOPE + ROPE), qmap),
                pl.BlockSpec((None, HPS, tk, NOPE + ROPE), kmap),
                pl.BlockSpec((None, HPS, tk, HD), kmap)]
    args = [q, k, v]
    aliases = {}
    if ctx_only:
        in_specs.append(pl.BlockSpec(memory_space=pl.ANY))
        args.append(o_prev)
        aliases = {3: 0}
    return pl.pallas_call(
        functools.partial(_attn_kernel, first=first, kb=kb, nkb=nkb),
        grid=grid,
        in_specs=in_specs,
        out_specs=pl.BlockSpec((tq, HPS * HD), omap),
        out_shape=jax.ShapeDtypeStruct((rows, BW), F32),
        input_output_aliases=aliases,
        compiler_params=_params("parallel", "parallel", "parallel"),
        name="mla_attn_ctx" if ctx_only else "mla_attn",
    )(*args)


def _chunk_pos(g, lat_chunks, ncl, ncc):
    is_lat = g < lat_chunks
    pos = jnp.where(is_lat, g % ncl, (g - lat_chunks) % ncc)
    last = jnp.where(is_lat, pos == ncl - 1, pos == ncc - 1)
    return pos == 0, last


def _scan_chunk(b, s, rev, lat_chunks, ncl, ncc):
    c = jnp.where(s < ncc, s, s - ncc)
    if rev:
        c = jnp.where(s < ncc, ncc - 1 - c, ncl - 1 - c)
    return jnp.where(s < ncc, lat_chunks + b * ncc + c, b * ncl + c)


def _tri_masks(n):
    ri = lax.broadcasted_iota(jnp.int32, (n, n), 0)
    ci = lax.broadcasted_iota(jnp.int32, (n, n), 1)
    return ri, ci


def _split(x):
    hi = x.astype(BF16)
    return hi, (x - hi.astype(F32)).astype(BF16)


def _dot3(a, b):
    (ah, al), (bh, bl) = a, b
    return _dot(ah, bh) + (_dot(ah, bl) + _dot(al, bh))


def _dot_tri(tri, x):
    hi, lo = _split(x)
    lo2 = (x - hi.astype(F32) - lo.astype(F32)).astype(BF16)
    return _dot(tri, hi) + (_dot(tri, lo) + _dot(tri, lo2))


def _unit_tri_inv(mats, ri, ci, eye):
    rb, cb = ri >> 3, ci >> 3
    d8 = [jnp.where(rb == cb, a, 0.0) for a in mats]
    d8s = [_split(d) for d in d8]
    x2 = [_split(_dot3(d, d)) for d in d8s]
    ts = [eye - d for d in d8]
    x4 = [_split(_dot3(x, x)) for x in x2]
    ts = [t + _dot3(_split(t), x) for t, x in zip(ts, x2)]
    ts = [t + _dot3(_split(t), x) for t, x in zip(ts, x4)]
    for _ in range(3):
        same = rb == cb
        rb, cb = rb >> 1, cb >> 1
        off = (rb == cb) & jnp.logical_not(same)
        bs = [jnp.where(off, a, 0.0).astype(BF16) for a in mats]
        tb = [t.astype(BF16) for t in ts]
        ys = [_dot(t, b).astype(BF16) for t, b in zip(tb, bs)]
        ts = [t - _dot(y, tl) for t, y, tl in zip(ts, ys, tb)]
    return ts


def _gdn_pre_kernel(x_ref, prev_ref, next_ref, a_ref, b_ref, cw_ref, alog_ref, dtb_ref,
                    u_ref, w_ref, qg_ref, kd_ref, at_ref, dec_ref, *, lat_chunks, ncl, ncc):
    first, last = _chunk_pos(pl.program_id(0), lat_chunks, ncl, ncc)
    nrow = CPP * CHUNK
    x = x_ref[...]
    xp = jnp.where(first, 0.0, prev_ref[7:8, :])
    xn = jnp.where(last, 0.0, next_ref[0:1, :])
    row = lax.broadcasted_iota(jnp.int32, (nrow, 1), 0)
    x_dn = jnp.where(row == 0, xp, pltpu.roll(x, 1, 0))
    x_up = jnp.where(row == nrow - 1, xn, pltpu.roll(x, nrow - 1, 0))
    cw = cw_ref[...]
    s = _silu(x_dn * cw[0:1] + x * cw[1:2] + x_up * cw[2:3])

    lane = lax.broadcasted_iota(jnp.int32, (1, 128), 1)
    g = -jnp.exp(alog_ref[...]) * _softplus(a_ref[...] + dtb_ref[...])
    beta = _sigmoid(b_ref[...])
    ri, ci = _tri_masks(CHUNK)
    eye_b = ri == ci
    eye = eye_b.astype(F32)
    low = (ri >= ci).astype(BF16)
    upp = (ri <= ci).astype(BF16)
    chunk_rows = [slice(c * CHUNK, (c + 1) * CHUNK) for c in range(CPP)]
    gc, g_last = [], []
    for c, rc in enumerate(chunk_rows):
        gcc = jnp.where(lane < HEADS, _dot_tri(low, g[rc]), _dot_tri(upp, g[rc]))
        gc.append(gcc)
        g_last.append(jnp.where(lane < HEADS, gcc[CHUNK - 1:CHUNK], gcc[0:1]))
        dec_ref[c] = jnp.exp(g_last[c])

    qn, kn, vs = [], [], []
    for h in range(HEADS):
        qh = s[:, h * HD:(h + 1) * HD]
        kh = s[:, BW + h * HD:BW + (h + 1) * HD]
        vs.append(s[:, 2 * BW + h * HD:2 * BW + (h + 1) * HD])
        qn.append(qh * lax.rsqrt(jnp.sum(qh * qh, axis=-1, keepdims=True) + EPS) * HD ** -0.5)
        kn.append(kh * lax.rsqrt(jnp.sum(kh * kh, axis=-1, keepdims=True) + EPS))
    qk = [[None] * HEADS for _ in range(CPP)]
    kk = [[None] * HEADS for _ in range(CPP)]
    for c, rc in enumerate(chunk_rows):
        for h in range(HEADS):
            knb = kn[h][rc].astype(BF16)
            qk[c][h] = _dot_nt(qn[h][rc].astype(BF16), knb)
            kk[c][h] = _dot_nt(knb, knb)

    probs = [(c, d, h) for c in range(CPP) for h in range(HEADS) for d in range(2)]
    mats, rhs = [], []
    for c, d, h in probs:
        rc = chunk_rows[c]
        sl = slice(h * HD, (h + 1) * HD)
        dh = d * HEADS + h
        gcc = gc[c][:, dh:dh + 1]
        bt = beta[rc, dh:dh + 1]
        grow = jnp.sum(jnp.where(eye_b, gcc, 0.0), axis=0, keepdims=True)
        incl = (ri >= ci) if d == 0 else (ri <= ci)
        strict = (ri > ci) if d == 0 else (ri < ci)
        dm = jnp.where(incl, jnp.exp(gcc - grow), 0.0)
        eg = jnp.exp(gcc)
        knc = kn[h][rc]
        mats.append(jnp.where(strict, bt * kk[c][h] * dm, 0.0))
        rhs.append(jnp.concatenate([vs[h][rc] * bt, knc * (bt * eg)], axis=1).astype(BF16))
        qg_ref[d, rc, sl] = (qn[h][rc] * eg).astype(BF16)
        kd_ref[d, rc, sl] = (knc * jnp.exp(g_last[c][:, dh:dh + 1] - gcc)).astype(BF16)
        at_ref[d, rc, h * HD:h * HD + CHUNK] = (qk[c][h] * dm).astype(BF16)
        at_ref[d, rc, h * HD + CHUNK:(h + 1) * HD] = jnp.zeros((CHUNK, HD - CHUNK), BF16)

    for (c, d, h), t, r in zip(probs, _unit_tri_inv(mats, ri, ci, eye), rhs):
        sl = slice(h * HD, (h + 1) * HD)
        uw = _dot(t.astype(BF16), r)
        u_ref[d, chunk_rows[c], sl] = uw[:, :HD]
        w_ref[d, chunk_rows[c], sl] = uw[:, HD:].astype(BF16)


def _gdn_pre(p, conv_w, a_log, dt_bias, nb, n_seq, n_ctx):
    t = p.shape[0]
    nrow = CPP * CHUNK
    assert n_seq % nrow == 0 and n_ctx % nrow == 0
    lat_blocks, nbl, nbc = nb * n_seq // nrow, n_seq // nrow, n_ctx // nrow
    qkv_blk = P_GQKV // (3 * BW)
    pad = lambda v: jnp.pad(v.reshape(1, -1), ((0, 0), (0, 128 - v.size)))
    dir_out = lambda dt: jax.ShapeDtypeStruct((2, t, BW), dt)
    dir_spec = pl.BlockSpec((2, nrow, BW), lambda g: (0, g, 0))
    r8 = nrow // 8
    return pl.pallas_call(
        functools.partial(_gdn_pre_kernel, lat_chunks=lat_blocks, ncl=nbl, ncc=nbc),
        grid=(t // nrow,),
        in_specs=[pl.BlockSpec((nrow, 3 * BW), lambda g: (g, qkv_blk)),
                  pl.BlockSpec((8, 3 * BW), lambda g: (jnp.maximum(g * r8 - 1, 0), qkv_blk)),
                  pl.BlockSpec((8, 3 * BW), lambda g: (jnp.minimum(g * r8 + r8, t // 8 - 1), qkv_blk)),
                  pl.BlockSpec((nrow, 128), lambda g: (g, P_GA // 128)),
                  pl.BlockSpec((nrow, 128), lambda g: (g, P_GB // 128)),
                  pl.BlockSpec((3, 3 * BW), lambda g: (0, 0)),
                  pl.BlockSpec((1, 128), lambda g: (0, 0)),
                  pl.BlockSpec((1, 128), lambda g: (0, 0))],
        out_specs=(dir_spec, dir_spec, dir_spec, dir_spec, dir_spec,
                   pl.BlockSpec((CPP, 1, 128), lambda g: (g, 0, 0))),
        out_shape=(dir_out(F32), dir_out(BF16), dir_out(BF16), dir_out(BF16), dir_out(BF16),
                   jax.ShapeDtypeStruct((t // CHUNK, 1, 128), F32)),
        compiler_params=_params("parallel"),
        name="gdn_chunk_prep",
    )(p, p, p, p, p, conv_w, pad(a_log), pad(dt_bias))


def _scan_rows(d, c):
    cc = c if d == 0 else CPS - 1 - c
    return slice(cc * CHUNK, (cc + 1) * CHUNK), cc


def _gdn_scan_kernel(*refs):
    ins, (of_ref, ob_ref, s_ref) = refs[:12], refs[12:]

    @pl.when(pl.program_id(1) == 0)
    def _():
        s_ref[...] = jnp.zeros(s_ref.shape, F32)

    chains = [(d, h) for d in range(2) for h in range(HEADS)]
    outs = (of_ref, ob_ref)
    st = [s_ref[d, h] for d, h in chains]
    for c in range(CPS):
        sb = [s.astype(BF16) for s in st]
        vn = []
        for (d, h), s in zip(chains, sb):
            u_ref, w_ref = ins[6 * d], ins[6 * d + 1]
            rows, _ = _scan_rows(d, c)
            sl = slice(h * HD, (h + 1) * HD)
            vn.append((u_ref[rows, sl] - _dot(w_ref[rows, sl], s)).astype(BF16))
        for i, (d, h) in enumerate(chains):
            qg_ref, kd_ref, at_ref, dec_ref = ins[6 * d + 2:6 * d + 6]
            rows, cc = _scan_rows(d, c)
            sl = slice(h * HD, (h + 1) * HD)
            outs[d][rows, sl] = _dot(qg_ref[rows, sl], sb[i]) + _dot(at_ref[rows, h * HD:h * HD + CHUNK], vn[i])
            dec = dec_ref[cc, 0:1, d * HEADS + h:d * HEADS + h + 1]
            st[i] = st[i] * dec + _dot_tn(kd_ref[rows, sl], vn[i])
    for (d, h), s in zip(chains, st):
        s_ref[d, h] = s


def _dir_scan(kernel, arrays, dec, nb, n_seq, n_ctx, scratch, name):
    t = arrays[0].shape[1]
    step = CPS * CHUNK
    assert n_seq % step == 0 and n_ctx % step == 0
    lat_blocks, ncl, ncc = nb * n_seq // step, n_seq // step, n_ctx // step
    in_specs, args = [], []
    for d in range(2):
        cmap = functools.partial(_scan_chunk, rev=bool(d), lat_chunks=lat_blocks, ncl=ncl, ncc=ncc)
        for arr in arrays:
            if arr.ndim == 3:
                in_specs.append(pl.BlockSpec((None, step, BW), lambda b, s, d=d, cmap=cmap: (d, cmap(b, s), 0)))
            else:
                in_specs.append(pl.BlockSpec((step, BW), lambda b, s, cmap=cmap: (cmap(b, s), 0)))
            args.append(arr)
        if dec.ndim == 3:
            in_specs.append(pl.BlockSpec((CPS, 1, dec.shape[-1]), lambda b, s, cmap=cmap: (cmap(b, s), 0, 0)))
        else:
            in_specs.append(pl.BlockSpec((None, CPS, 1, dec.shape[-1]),
                                         lambda b, s, d=d, cmap=cmap: (d, cmap(b, s), 0, 0)))
        args.append(dec)
    fmap = functools.partial(_scan_chunk, rev=False, lat_chunks=lat_blocks, ncl=ncl, ncc=ncc)
    bmap = functools.partial(_scan_chunk, rev=True, lat_chunks=lat_blocks, ncl=ncl, ncc=ncc)
    return pl.pallas_call(
        kernel,
        grid=(nb, ncc + ncl),
        in_specs=in_specs,
        out_specs=(pl.BlockSpec((step, BW), lambda b, s: (fmap(b, s), 0)),
                   pl.BlockSpec((step, BW), lambda b, s: (bmap(b, s), 0))),
        out_shape=(jax.ShapeDtypeStruct((t, BW), F32), jax.ShapeDtypeStruct((t, BW), F32)),
        scratch_shapes=[scratch],
        compiler_params=_params("parallel", "arbitrary"),
        name=name,
    )(*args)


def _hg_block_decay(gc, rev):
    out = None
    for b in range(CHUNK // SUB):
        lo, hi = gc[b * SUB:b * SUB + 1], gc[(b + 1) * SUB - 1:(b + 1) * SUB]
        dcy = (hi - lo) if rev else (lo - hi)
        out = dcy if out is None else jnp.maximum(out, dcy)
    return out


def _hg_scores(q, k, gc, rev, exact):
    nblk = CHUNK // SUB
    lane = lax.broadcasted_iota(jnp.int32, (1, CHUNK), 1)
    row8 = lax.broadcasted_iota(jnp.int32, (8, 1), 0)
    blk = lambda x, b: x[b * SUB:(b + 1) * SUB]
    ref_row = lambda b: b * SUB + (0 if rev else SUB - 1)
    refs = [gc[ref_row(b):ref_row(b) + 1] for b in range(nblk)]
    kt = jnp.concatenate([blk(k, b) * jnp.exp2(refs[b] - blk(gc, b)) for b in range(nblk)], axis=0)
    srcs = list(range(nblk - 1, 0, -1)) if rev else list(range(nblk - 1))
    rows_of = (lambda j: slice(0, j * SUB)) if rev else (lambda j: slice((j + 1) * SUB, CHUNK))
    qt = jnp.concatenate([q[rows_of(j)] * jnp.exp2(gc[rows_of(j)] - refs[j]) for j in srcs], axis=0)
    seg_off, off = {}, 0
    for j in srcs:
        seg_off[j] = off
        off += (j if rev else nblk - 1 - j) * SUB
    ktb, qtb = kt.astype(BF16), qt.astype(BF16)
    cross = [_dot_nt(qtb[:, h * HD:(h + 1) * HD], ktb[:, h * HD:(h + 1) * HD]) for h in range(HEADS)]

    def assemble(own):
        out = []
        for h in range(HEADS):
            groups = []
            for g8 in range(CHUNK // 8):
                r0 = 8 * g8
                bi = r0 // SUB
                val = own[h][g8]
                for j in srcs:
                    if (j < bi and not rev) or (j > bi and rev):
                        base = seg_off[j] + (r0 if rev else r0 - (j + 1) * SUB)
                        val = jnp.where((lane >= j * SUB) & (lane < (j + 1) * SUB), cross[h][base:base + 8], val)
                groups.append(val)
            out.append(jnp.concatenate(groups, axis=0))
        return tuple(out)

    def own_exact():
        acc = [[jnp.zeros((8, CHUNK), F32) for _ in range(CHUNK // 8)] for _ in range(HEADS)]
        for j in range(CHUNK):
            b, jj = j // SUB, j % SUB
            gj, kj = gc[j:j + 1], k[j:j + 1]
            for rg in range(SUB // 8):
                lo, hi = 8 * rg, 8 * rg + 7
                if (hi < jj and not rev) or (lo > jj and rev):
                    continue
                r0 = b * SUB + lo
                w = jnp.exp2(gc[r0:r0 + 8] - gj)
                if not ((lo >= jj and not rev) or (hi <= jj and rev)):
                    w = jnp.where((row8 + lo >= jj) if not rev else (row8 + lo <= jj), w, 0.0)
                tt = q[r0:r0 + 8] * w * kj
                g8 = r0 // 8
                for h in range(HEADS):
                    col = jnp.sum(tt[:, h * HD:(h + 1) * HD], axis=-1, keepdims=True)
                    acc[h][g8] = jnp.where(lane == j, col, acc[h][g8])
        return assemble(acc)

    first_row = lambda b: b * SUB + (SUB - 1 if rev else 0)
    starts = [gc[first_row(b):first_row(b) + 1] for b in range(nblk)]

    def own_matmul():
        rfull = jnp.concatenate([jnp.broadcast_to(starts[b], (SUB, q.shape[1])) for b in range(nblk)], axis=0)
        qd = (q * jnp.exp2(gc - rfull)).astype(BF16)
        kd = (k * jnp.exp2(rfull - gc)).astype(BF16)
        ri, ci = _tri_masks(CHUNK)
        sh = SUB.bit_length() - 1
        keep = ((ri >> sh) == (ci >> sh)) & ((ri <= ci) if rev else (ri >= ci))
        own = []
        for h in range(HEADS):
            sd = jnp.where(keep, _dot_nt(qd[:, h * HD:(h + 1) * HD], kd[:, h * HD:(h + 1) * HD]), 0.0)
            own.append([sd[8 * g8:8 * g8 + 8] for g8 in range(CHUNK // 8)])
        return assemble(own)

    return own_exact() if exact else own_matmul()


def _hg_pre_kernel(q_ref, f_ref, i_ref, lbl_ref, oi_ref, qg_ref, kd_ref, vb_ref, dec_ref, *, layer):
    lbl = lbl_ref[...]
    e = jnp.exp(lbl - jnp.max(lbl, axis=0, keepdims=True))
    sm = e / jnp.sum(e, axis=0, keepdims=True)
    lb_all = sm[0]
    for l in range(1, layer + 1):
        lb_all = lb_all + sm[l]
    lb_all = lb_all - sm[0]
    q_all = q_ref[...] * HD ** -0.5
    v_all = i_ref[...].astype(BF16)
    vb_ref[...] = v_all
    ri, ci = _tri_masks(CHUNK)
    work, decay = [], None
    for d in range(2):
        lb = lb_all[d:d + 1]
        f = f_ref[:, d * BW:(d + 1) * BW]
        e = jnp.exp(-jnp.abs(f))
        r = 1.0 / (1.0 + e)
        er = e * r
        pos = f >= 0.0
        sig, sig_neg = jnp.where(pos, r, er), jnp.where(pos, er, r)
        log_f = jnp.log2(jnp.maximum(lb, LB_FLOOR) + (1.0 - lb) * sig)
        k_all = (1.0 - lb) * sig_neg
        tri = ((ri >= ci) if d == 0 else (ri <= ci)).astype(BF16)
        for c in range(HG_CPP):
            rc = slice(c * CHUNK, (c + 1) * CHUNK)
            q, k, v = q_all[rc], k_all[rc], v_all[rc]
            gc = _dot_tri(tri, log_f[rc])
            g_last = gc[CHUNK - 1:CHUNK] if d == 0 else gc[0:1]
            dec_ref[d, c] = jnp.exp2(g_last)
            qg_ref[d, rc] = (q * jnp.exp2(gc)).astype(BF16)
            kd_ref[d, rc] = (k * jnp.exp2(g_last - gc)).astype(BF16)
            work.append((d, rc, q, k, v, gc))
            dcy = _hg_block_decay(gc, rev=bool(d))
            decay = dcy if decay is None else jnp.maximum(decay, dcy)

    def intra(exact):
        for d, rc, q, k, v, gc in work:
            scores = _hg_scores(q, k, gc, rev=bool(d), exact=exact)
            for h in range(HEADS):
                sl = slice(h * HD, (h + 1) * HD)
                oi_ref[d, rc, sl] = _dot(scores[h].astype(BF16), v[:, sl])

    safe = jnp.max(decay) <= HG_MAX_DECAY
    pl.when(safe)(lambda: intra(False))
    pl.when(jnp.logical_not(safe))(lambda: intra(True))


def _hg_pre(p, lb_logits, layer):
    t = p.shape[0]
    nc = t // CHUNK
    nrow = HG_CPP * CHUNK
    assert t % nrow == 0
    dir_out = lambda dt: jax.ShapeDtypeStruct((2, t, BW), dt)
    dir_spec = pl.BlockSpec((2, nrow, BW), lambda g: (0, g, 0))
    return pl.pallas_call(
        functools.partial(_hg_pre_kernel, layer=layer),
        grid=(t // nrow,),
        in_specs=[pl.BlockSpec((nrow, BW), lambda g: (g, P_HQ // BW)),
                  pl.BlockSpec((nrow, 2 * BW), lambda g: (g, P_HF // (2 * BW))),
                  pl.BlockSpec((nrow, BW), lambda g: (g, P_HI // BW)),
                  pl.BlockSpec(lb_logits.shape, lambda g: (0, 0, 0))],
        out_specs=(dir_spec, dir_spec, dir_spec,
                   pl.BlockSpec((nrow, BW), lambda g: (g, 0)),
                   pl.BlockSpec((2, HG_CPP, 1, BW), lambda g: (0, g, 0, 0))),
        out_shape=(dir_out(F32), dir_out(BF16), dir_out(BF16),
                   jax.ShapeDtypeStruct((t, BW), BF16),
                   jax.ShapeDtypeStruct((2, nc, 1, BW), F32)),
        compiler_params=_params("parallel"),
        name="hgrn2_chunk_prep",
    )(p, p, p, lb_logits)


def _hg_scan_kernel(*refs):
    ins, (of_ref, ob_ref, s_ref) = refs[:10], refs[10:]

    @pl.when(pl.program_id(1) == 0)
    def _():
        s_ref[...] = jnp.zeros(s_ref.shape, F32)

    chains = [(d, h) for d in range(2) for h in range(HEADS)]
    outs = (of_ref, ob_ref)
    incs = []
    for d, h in chains:
        kd_ref, v_ref = ins[5 * d + 2], ins[5 * d + 3]
        sl = slice(h * HD, (h + 1) * HD)
        incs.append([_dot_tn(v_ref[_scan_rows(d, c)[0], sl], kd_ref[_scan_rows(d, c)[0], sl])
                     for c in range(CPS)])
    states = []
    for i, (d, h) in enumerate(chains):
        dec_ref = ins[5 * d + 4]
        sl = slice(h * HD, (h + 1) * HD)
        st = s_ref[d, h]
        seq = []
        for c in range(CPS):
            seq.append(st.astype(BF16))
            st = st * dec_ref[_scan_rows(d, c)[1], :, sl] + incs[i][c]
        s_ref[d, h] = st
        states.append(seq)
    for i, (d, h) in enumerate(chains):
        oi_ref, qg_ref = ins[5 * d], ins[5 * d + 1]
        sl = slice(h * HD, (h + 1) * HD)
        for c in range(CPS):
            rows, _ = _scan_rows(d, c)
            outs[d][rows, sl] = oi_ref[rows, sl] + _dot_nt(qg_ref[rows, sl], states[i][c])


def _merge_kernel(h_ref, gof_ref, gob_ref, gg_ref, yb_ref, hof_ref, hob_ref, hgg_ref, u_ref,
                  gnorm_ref, hnorm_ref, gt_ref, nw_ref, wg_ref, wb_ref, wo_ref, o_ref):
    def readout(of_ref, ob_ref, gate_ref, norm_ref):
        o = of_ref[...] + ob_ref[...]
        gate = gate_ref[...]
        parts = []
        for h in range(HEADS):
            sl = slice(h * HD, (h + 1) * HD)
            parts.append(_rms(o[:, sl], norm_ref[...]) * _silu(gate[:, sl]))
        return jnp.concatenate(parts, axis=1).astype(BF16)

    ys = (readout(gof_ref, gob_ref, gg_ref, gnorm_ref),
          yb_ref[...].astype(BF16),
          readout(hof_ref, hob_ref, hgg_ref, hnorm_ref))
    d = h_ref.shape[1]
    u = u_ref[...]
    m = None
    for j in range(3):
        gate_logits = _dot(u, wg_ref[:, j * d:(j + 1) * d])
        term = _sigmoid(gate_logits) * _dot(ys[j], wb_ref[j])
        m = term if m is None else m + term
    y = _dot(m.astype(BF16), wo_ref[...])
    o_ref[...] = h_ref[...] + gt_ref[...] * _rms(y, nw_ref[...])


def _merge(h, g_of, g_ob, yb, h_of, h_ob, p, u, gnorm, hnorm, gate_mod, nw, wg, wb, wo, n_rows, n_seq, nb):
    d = h.shape[1]
    tm = TM
    row = lambda i: (i, 0)
    bw_spec = pl.BlockSpec((tm, BW), row)
    pcol = lambda off, width: pl.BlockSpec((tm, width), lambda i: (i, off // width))
    vec = lambda n: pl.BlockSpec((1, n), lambda i: (0, 0))
    return pl.pallas_call(
        _merge_kernel,
        grid=(n_rows // tm,),
        in_specs=[pl.BlockSpec((tm, d), row), bw_spec, bw_spec, pcol(P_GGATE, BW), bw_spec,
                  bw_spec, bw_spec, pcol(P_HGATE, BW), pl.BlockSpec((tm, d), row),
                  vec(HD), vec(HD),
                  pl.BlockSpec((None, 1, d), lambda i: (jnp.minimum((i * tm) // n_seq, nb), 0, 0)),
                  vec(d), _resident(wg.shape), _resident(wb.shape), _resident(wo.shape)],
        out_specs=pl.BlockSpec((tm, d), row),
        out_shape=jax.ShapeDtypeStruct((n_rows, d), F32),
        compiler_params=_params("parallel"),
        name="branch_merge",
    )(h, g_of, g_ob, p, yb, h_of, h_ob, p, u, gnorm, hnorm, gate_mod, nw, wg, wb, wo)


def _pack_w_in(w):
    d = w.shape[0]
    sizes = (BW, BW, BW, BW, 2 * HEADS, 2 * HEADS, Q_RANK, KV_RANK, ROPE, BW, 2 * BW, BW, BW, 3 * d)
    offs = [0]
    for s in sizes:
        offs.append(offs[-1] + s)
    part = lambda i: w[:, offs[i]:offs[i + 1]]
    gq, gk, gv, ggate, ga, gb, qa, kva, kr, hq, hf, hi, hgate, gates = (part(i) for i in range(14))
    z = lambda n: jnp.zeros((d, n), w.dtype)
    swap = jnp.concatenate([kr[:, 16:32], kr[:, 0:16], kr[:, 48:64], kr[:, 32:48]], axis=1)
    rec = jnp.concatenate([gq, gk, gv, ggate, hf, hq, hi, hgate, ga, z(120), gb, z(120)], axis=1).astype(BF16)
    mla = jnp.concatenate([qa, z(128), kva, kr, swap], axis=1).astype(BF16)
    assert rec.shape[1] == P_COLS and mla.shape[1] == M_COLS
    return rec.reshape(d, P_COLS // PROJ_TN, PROJ_TN).transpose(1, 0, 2), mla, gates.astype(BF16)


def _pack_wq(w):
    w = w.reshape(Q_RANK, HEADS, NOPE + ROPE)
    nope = w[:, :, :NOPE].reshape(Q_RANK, HEADS * NOPE)
    r = w[:, :, NOPE:]
    sw = jnp.concatenate([r[..., 16:32], r[..., 0:16], r[..., 48:64], r[..., 32:48]], axis=-1)
    rope = jnp.concatenate([r, sw], axis=-1).reshape(Q_RANK, HEADS * 2 * ROPE)
    return jnp.concatenate([nope, rope], axis=1).astype(BF16)


def _pack_wkv(w):
    w = w.reshape(KV_RANK, HEADS, NOPE + HD)
    return jnp.concatenate([w[:, :, :NOPE].reshape(KV_RANK, -1), w[:, :, NOPE:].reshape(KV_RANK, -1)],
                           axis=1).astype(BF16)


def _rope_table(nb, n_seq, n_ctx):
    nf = ROPE // 4
    rows = n_seq // GRID_W
    rpos = jnp.repeat(jnp.arange(rows, dtype=F32), GRID_W)
    cpos = jnp.tile(jnp.arange(GRID_W, dtype=F32), rows)
    inv = ROPE_BASE ** (-jnp.arange(nf, dtype=F32) / nf)
    ar, ac = rpos[:, None] * inv, cpos[:, None] * inv
    cos = jnp.concatenate([jnp.cos(ar), jnp.cos(ar), jnp.cos(ac), jnp.cos(ac)], axis=1)
    sin = jnp.concatenate([-jnp.sin(ar), jnp.sin(ar), -jnp.sin(ac), jnp.sin(ac)], axis=1)
    lat = jnp.tile(jnp.concatenate([cos, sin], axis=1), (nb, 1))
    ctx = jnp.concatenate([jnp.ones((nb * n_ctx, ROPE), F32), jnp.zeros((nb * n_ctx, ROPE), F32)], axis=1)
    return jnp.concatenate([lat, ctx], axis=0)


def kernel(x, c, ctx, c_ctx, w_ada, b_ada, norm_w, ffn_w_in, ffn_w_out, w_in, gdn_conv, gdn_a_log, gdn_dt_bias, gdn_norm, mla_q_norm, mla_kv_norm, mla_w_q_b, mla_w_kv_b, hg_lb_logits, hg_norm, w_branch, w_out):
    nb, n_seq, d = x.shape
    n_ctx = ctx.shape[1]
    depth = w_ada.shape[0]
    lat_rows, rows = nb * n_seq, nb * (n_seq + n_ctx)
    assert nb + 1 <= 8 and n_seq % TM == 0 and n_ctx % TM_S == 0 and (nb * n_ctx) % TM == 0
    assert n_seq % n_ctx == 0 and lat_rows % n_ctx == 0

    cc = jnp.concatenate([c, c_ctx[None], jnp.zeros((8 - nb - 1, d), F32)], axis=0)
    mods = _ada(cc, w_ada, b_ada)[:, :nb + 1].reshape(depth, nb + 1, 9, 1, d).transpose(0, 2, 1, 3, 4)
    cs = _rope_table(nb, n_seq, n_ctx)
    h = jnp.concatenate([x.reshape(lat_rows, d), ctx.reshape(nb * n_ctx, d)], axis=0)

    for l in range(depth):
        last = l == depth - 1
        md, nw = mods[l], norm_w[l][:, None, :]
        wi = [ffn_w_in[l, j].astype(BF16) for j in range(2)]
        wo = [ffn_w_out[l, j].astype(BF16) for j in range(2)]

        h, u = _ffn(h, md[0:3], nw[0:2], wi[0], wo[0], rows, n_seq, nb, mixer_mods=(nw[2], md[3], md[4]))
        w_rec, w_low, w_gates = _pack_w_in(w_in[l])
        p = _proj(u, w_rec)

        g_ops = _gdn_pre(p, gdn_conv[l], gdn_a_log[l], gdn_dt_bias[l], nb, n_seq, n_ctx)
        g_of, g_ob = _dir_scan(_gdn_scan_kernel, g_ops[:5], g_ops[5], nb, n_seq, n_ctx,
                               pltpu.VMEM((2, HEADS, HD, HD), F32), "gdn_scan")

        hoi, hqg, hkd, hvb, hdec = _hg_pre(p, hg_lb_logits, l)
        h_of, h_ob = _dir_scan(_hg_scan_kernel, (hoi, hqg, hkd, hvb), hdec, nb, n_seq, n_ctx,
                               pltpu.VMEM((2, HEADS, HD, HD), F32), "hgrn2_scan")

        q, k, v = _mla_prep(u, cs, mla_q_norm[l][None], mla_kv_norm[l][None], w_low,
                            _pack_wq(mla_w_q_b[l]), _pack_wkv(mla_w_kv_b[l]), nb, n_seq, n_ctx)
        yb = _attention(q, k, v, None, nb, n_seq, n_ctx, rows, ctx_only=False)
        if not last:
            yb = _attention(q, k, v, yb, nb, n_seq, n_ctx, rows, ctx_only=True)

        out_rows = lat_rows if last else rows
        h = _merge(h, g_of, g_ob, yb, h_of, h_ob, p, u, gdn_norm[l][None], hg_norm[l][None], md[5], nw[3],
                   w_gates, w_branch[l].astype(BF16), w_out[l].astype(BF16), out_rows, n_seq, nb)
        h = _ffn(h, md[6:9], nw[4:6], wi[1], wo[1], out_rows, n_seq, nb)
    return h.reshape(nb, n_seq, d)
```

```python
import functools

import jax
import jax.numpy as jnp
from jax import lax
from jax.experimental import pallas as pl
from jax.experimental.pallas import tpu as pltpu

F32 = jnp.float32
BF16 = jnp.bfloat16
EPS = 1e-6
LB_FLOOR = 1e-30
GRID_W = 64
ROPE_BASE = 10000.0

D_FF = 2816
HEADS = 4
HD = 128
ROPE = 64
NOPE = 128
Q_RANK = 384
KV_RANK = 256
CHUNK = 64
CPS = 4
CPP = 4
HG_CPP = 4
SUB = 16
BW = HEADS * HD

TM = 512
TM_S = 256
FF_CK = 256
FFN_SUB = 2
VMEM_LIMIT = 48 * 1024 * 1024
FFN_VMEM_LIMIT = 56 * 1024 * 1024
ATTN_LOG2_SCALE = (NOPE + ROPE) ** -0.5 * 1.4426950408889634
HPS = 2
KSUB = 256
HG_MAX_DECAY = 100.0
MAX_JUMP = 64.0

P_GQKV = 0
P_GGATE = 1536
P_HF = 2048
P_HQ = 3072
P_HI = 3584
P_HGATE = 4096
P_GA = 4608
P_GB = 4736
P_COLS = 4864
PROJ_TN = 2432
M_QA, M_KVA, M_KR, M_COLS = 0, 512, 768, 896

NT = (((1,), (1,)), ((), ()))
TN = (((0,), (0,)), ((), ()))


def _dot(a, b):
    return jnp.dot(a, b, preferred_element_type=F32)


def _dot_nt(a, b):
    return lax.dot_general(a, b, NT, preferred_element_type=F32)


def _dot_tn(a, b):
    return lax.dot_general(a, b, TN, preferred_element_type=F32)


def _sigmoid(x):
    return 1.0 / (1.0 + jnp.exp(-x))


def _silu(x):
    return x * _sigmoid(x)


def _softplus(x):
    return jnp.maximum(x, 0.0) + jnp.log(1.0 + jnp.exp(-jnp.abs(x)))


def _rms(x, w):
    return x * lax.rsqrt(jnp.mean(x * x, axis=-1, keepdims=True) + EPS) * w


def _resident(shape):
    zeros = (0,) * len(shape)
    return pl.BlockSpec(shape, lambda *_: zeros, pipeline_mode=pl.Buffered(1))


def _params(*sem):
    return pltpu.CompilerParams(dimension_semantics=sem, vmem_limit_bytes=VMEM_LIMIT)


def _ada_kernel(c_ref, w_ref, b_ref, o_ref):
    s = _silu(c_ref[...])
    o_ref[...] = _dot(s.astype(BF16), w_ref[...].astype(BF16)) + b_ref[...]


def _ada(cc, w_ada, b_ada):
    depth, d, nm = w_ada.shape
    tn = 1024
    return pl.pallas_call(
        _ada_kernel,
        grid=(depth, nm // tn),
        in_specs=[pl.BlockSpec((8, d), lambda l, j: (0, 0)),
                  pl.BlockSpec((None, d, tn), lambda l, j: (l, 0, j)),
                  pl.BlockSpec((None, 1, tn), lambda l, j: (l, 0, j))],
        out_specs=pl.BlockSpec((None, 8, tn), lambda l, j: (l, 0, j)),
        out_shape=jax.ShapeDtypeStruct((depth, 8, nm), F32),
        compiler_params=_params("parallel", "parallel"),
        name="ada_mod",
    )(cc, w_ada, b_ada.reshape(depth, 1, nm))


def _ffn_kernel(x_ref, sh_ref, sc_ref, gt_ref, prew_ref, postw_ref, wi_ref, wo_ref, *rest, emit_u, n_sub,
                n_alias):
    rest = rest[n_alias:]
    if emit_u:
        nw_ref, sh2_ref, sc2_ref, o_ref, u_ref = rest
    else:
        (o_ref,) = rest
    n_ck = D_FF // FF_CK
    half = n_ck // 2

    def prologue(s):
        x = x_ref[s * TM:(s + 1) * TM, :]
        return (_rms(x, prew_ref[...]) * (1.0 + sc_ref[...]) + sh_ref[...]).astype(BF16)

    def chunks(hn, acc, lo, hi):
        for c in range(lo, hi):
            g = _dot(hn, wi_ref[:, c * FF_CK:(c + 1) * FF_CK])
            u = _dot(hn, wi_ref[:, D_FF + c * FF_CK:D_FF + (c + 1) * FF_CK])
            a = (_silu(g) * u).astype(BF16)
            acc = acc + _dot(a, wo_ref[c * FF_CK:(c + 1) * FF_CK, :])
        return acc

    def epilogue(s, acc):
        rs = slice(s * TM, (s + 1) * TM)
        out = x_ref[rs, :] + 0.5 * gt_ref[...] * _rms(acc, postw_ref[...])
        o_ref[rs, :] = out
        if emit_u:
            u_ref[rs, :] = (_rms(out, nw_ref[...]) * (1.0 + sc2_ref[...]) + sh2_ref[...]).astype(BF16)

    zero = jnp.zeros((TM, x_ref.shape[1]), F32)
    hn = prologue(0)
    acc = chunks(hn, zero, 0, half)
    for s in range(n_sub):
        hn_next = prologue(s + 1) if s + 1 < n_sub else None
        acc = chunks(hn, acc, half, n_ck)
        if hn_next is not None:
            acc_next = chunks(hn_next, zero, 0, half)
        epilogue(s, acc)
        if hn_next is not None:
            hn, acc = hn_next, acc_next


def _ffn(h, mods, nw, wi, wo, n_rows, n_seq, nb, mixer_mods=None):
    d = h.shape[1]
    emit_u = mixer_mods is not None
    lat_rows = nb * n_seq
    vec = pl.BlockSpec((1, d), lambda i: (0, 0))
    out_shape = [jax.ShapeDtypeStruct((n_rows, d), F32)]
    if emit_u:
        out_shape.append(jax.ShapeDtypeStruct((n_rows, d), BF16))

    def call(n_sub, n_steps, row0, prev):
        tm = n_sub * TM
        row = lambda i: (row0 // tm + i, 0)
        mspec = pl.BlockSpec((None, 1, d), lambda i: (jnp.minimum((row0 + i * tm) // n_seq, nb), 0, 0))
        in_specs = [pl.BlockSpec((tm, d), row), mspec, mspec, mspec, vec, vec,
                    _resident(wi.shape), _resident(wo.shape)]
        args = [h, mods[0], mods[1], mods[2], nw[0], nw[1], wi, wo]
        aliases = {}
        for k, arr in enumerate(prev):
            aliases[len(args)] = k
            in_specs.append(pl.BlockSpec(memory_space=pl.ANY))
            args.append(arr)
        if emit_u:
            in_specs += [vec, mspec, mspec]
            args += list(mixer_mods)
        return pl.pallas_call(
            functools.partial(_ffn_kernel, emit_u=emit_u, n_sub=n_sub, n_alias=len(prev)),
            grid=(n_steps,),
            in_specs=in_specs,
            out_specs=tuple(pl.BlockSpec((tm, d), row) for _ in out_shape),
            out_shape=tuple(out_shape),
            input_output_aliases=aliases,
            compiler_params=pltpu.CompilerParams(dimension_semantics=("parallel",),
                                                 vmem_limit_bytes=FFN_VMEM_LIMIT),
            name="ffn_sublayer" if not prev else "ffn_sublayer_ctx",
        )(*args)

    outs = call(FFN_SUB, lat_rows // (FFN_SUB * TM), 0, ())
    if n_rows > lat_rows:
        outs = call(1, (n_rows - lat_rows) // TM, lat_rows, outs)
    return outs if emit_u else outs[0]


def _proj_kernel(x_ref, w_ref, o_ref):
    o_ref[...] = _dot(x_ref[...], w_ref[pl.program_id(1)])


def _proj(u, w):
    t, d = u.shape
    nj, _, tn = w.shape
    return pl.pallas_call(
        _proj_kernel,
        grid=(t // TM, nj),
        in_specs=[pl.BlockSpec((TM, d), lambda i, j: (i, 0)), _resident(w.shape)],
        out_specs=pl.BlockSpec((TM, tn), lambda i, j: (i, j)),
        out_shape=jax.ShapeDtypeStruct((t, nj * tn), F32),
        compiler_params=_params("parallel", "arbitrary"),
        name="in_proj",
    )(u, w)


def _mla_prep_kernel(u_ref, cs_ref, qn_ref, kvn_ref, win_ref, wq_ref, wkv_ref, q_ref, k_ref, v_ref):
    cs = cs_ref[...]
    low = _dot(u_ref[...], win_ref[...])
    qa = _rms(low[:, M_QA:M_QA + Q_RANK], qn_ref[...]).astype(BF16)
    kva = _rms(low[:, M_KVA:M_KVA + KV_RANK], kvn_ref[...]).astype(BF16)
    q2 = _dot(qa, wq_ref[...])
    kv2 = _dot(kva, wkv_ref[...])
    kr = low[:, M_KR:M_KR + 2 * ROPE] * cs
    kr = (kr + pltpu.roll(kr, ROPE, 1))[:, :ROPE].astype(BF16)
    q2 = q2 * ATTN_LOG2_SCALE
    for h in range(HEADS):
        q_ref[h, :, :NOPE] = q2[:, h * NOPE:(h + 1) * NOPE].astype(BF16)
        qr = q2[:, BW + h * 128:BW + (h + 1) * 128] * cs
        q_ref[h, :, NOPE:] = (qr + pltpu.roll(qr, ROPE, 1))[:, :ROPE].astype(BF16)
        k_ref[h, :, :NOPE] = kv2[:, h * NOPE:(h + 1) * NOPE].astype(BF16)
        k_ref[h, :, NOPE:] = kr
        v_ref[h] = kv2[:, BW + h * HD:BW + (h + 1) * HD].astype(BF16)


def _mla_prep(u, cs, qn, kvn, w_low, wq, wkv, nb, n_seq, n_ctx):
    t, d = u.shape
    tm = TM_S
    lat_tiles = nb * n_seq // tm
    per_lat = n_seq // tm
    per_ctx = n_ctx // tm

    def omap(i):
        j = i - lat_tiles
        b = jnp.where(i < lat_tiles, i // per_lat, j // per_ctx)
        blk = jnp.where(i < lat_tiles, i % per_lat, per_lat + j % per_ctx)
        return (b, 0, blk, 0)

    tk = n_seq + n_ctx
    return pl.pallas_call(
        _mla_prep_kernel,
        grid=(t // tm,),
        in_specs=[pl.BlockSpec((tm, d), lambda i: (i, 0)),
                  pl.BlockSpec((tm, 128), lambda i: (i, 0)),
                  pl.BlockSpec((1, Q_RANK), lambda i: (0, 0)),
                  pl.BlockSpec((1, KV_RANK), lambda i: (0, 0)),
                  _resident(w_low.shape), _resident(wq.shape), _resident(wkv.shape)],
        out_specs=(pl.BlockSpec((None, HEADS, tm, NOPE + ROPE), omap),
                   pl.BlockSpec((None, HEADS, tm, NOPE + ROPE), omap),
                   pl.BlockSpec((None, HEADS, tm, HD), omap)),
        out_shape=(jax.ShapeDtypeStruct((nb, HEADS, tk, NOPE + ROPE), BF16),
                   jax.ShapeDtypeStruct((nb, HEADS, tk, NOPE + ROPE), BF16),
                   jax.ShapeDtypeStruct((nb, HEADS, tk, HD), BF16)),
        compiler_params=_params("parallel"),
        name="mla_prep",
    )(u, cs, qn, kvn, w_low, wq, wkv)


def _attn_kernel(q_ref, k_ref, v_ref, *rest, first, kb, nkb):
    o_ref = rest[-1]
    nh, tq = q_ref.shape[0], q_ref.shape[1]
    qs = [q_ref[h] for h in range(nh)]

    def safe_update(h, sl, carry):
        m, l, acc = carry
        t = _dot_nt(qs[h], k_ref[h, sl, :])
        m_new = jnp.maximum(m, jnp.max(t, axis=-1, keepdims=True))
        alpha = jnp.exp2(m - m_new)
        pr = jnp.exp2(t - m_new)
        l = alpha * l + jnp.sum(pr, axis=-1, keepdims=True)
        return m_new, l, alpha * acc + _dot(pr.astype(BF16), v_ref[h, sl, :])

    def fast_update(j, carry):
        st = [list(c) for c in carry]
        bm = [None] * nh
        items = [(pl.ds(pl.multiple_of(j * kb + i * KSUB, KSUB), KSUB), h)
                 for i in range(kb // KSUB) for h in range(nh)]
        score = lambda it: _dot_nt(qs[it[1]], k_ref[it[1], it[0], :])
        t_next = score(items[0])
        for n, (sl, h) in enumerate(items):
            t = t_next
            if n + 1 < len(items):
                t_next = score(items[n + 1])
            pr = jnp.exp2(t - st[h][0])
            tm = jnp.max(t, axis=-1, keepdims=True)
            bm[h] = tm if bm[h] is None else jnp.maximum(bm[h], tm)
            st[h][1] = st[h][1] + jnp.sum(pr, axis=-1, keepdims=True)
            st[h][2] = st[h][2] + _dot(pr.astype(BF16), v_ref[h, sl, :])
        out = []
        for h in range(nh):
            m, l, acc, jump = st[h]
            m_new = jnp.maximum(m, bm[h])
            alpha = jnp.exp2(m - m_new)
            out.append((m_new, l * alpha, acc * alpha, jnp.maximum(jump, bm[h] - m)))
        return tuple(out)

    init = (jnp.full((tq, 1), -1e30, F32), jnp.zeros((tq, 1), F32), jnp.zeros((tq, HD), F32))
    fsl = slice(first[0], first[0] + first[1])
    start = [safe_update(h, fsl, init) for h in range(nh)]
    if nkb == 0:
        for h in range(nh):
            o_ref[:, h * HD:(h + 1) * HD] = start[h][2] / start[h][1]
        return
    done = lax.fori_loop(0, nkb, fast_update, tuple(s + (jnp.zeros((tq, 1), F32),) for s in start),
                         unroll=_pick_block(nkb, (4, 2, 1)))
    for h in range(nh):
        o_ref[:, h * HD:(h + 1) * HD] = done[h][2] / done[h][1]

    for h in range(nh):
        @pl.when(jnp.max(done[h][3]) > MAX_JUMP)
        def _(h=h):
            step = lambda j, carry: safe_update(h, pl.ds(pl.multiple_of(j * kb, kb), kb), carry)
            _, l2, acc2 = lax.fori_loop(0, nkb, step, start[h])
            o_ref[:, h * HD:(h + 1) * HD] = acc2 / l2


def _pick_block(n, cands):
    for c in cands:
        if n % c == 0:
            return c
    raise ValueError(f"no block size for {n}")


def _attention(q, k, v, o_prev, nb, n_seq, n_ctx, rows, ctx_only):
    if ctx_only:
        tq, tk = n_ctx, n_ctx
        koff = n_seq // n_ctx
        grid = (nb, HEADS // HPS, 1)
        qmap = lambda b, h, i: (b, h, koff, 0)
        kmap = lambda b, h, i: (b, h, koff, 0)
        omap = lambda b, h, i: (nb * n_seq // n_ctx + b, h)
        first, kb, nkb = (0, n_ctx), n_ctx, 0
    else:
        tq, tk = _pick_block(n_seq, (512, 256)), n_seq + n_ctx
        grid = (nb, HEADS // HPS, n_seq // tq)
        qmap = lambda b, h, i: (b, h, i, 0)
        kmap = lambda b, h, i: (b, h, 0, 0)
        omap = lambda b, h, i: (b * (n_seq // tq) + i, h)
        kb = _pick_block(n_seq, (1024, 512))
        first, nkb = (n_seq, n_ctx), n_seq // kb
    in_specs = [pl.BlockSpec((None, HPS, tq, NOPE + ROPE), qmap),
                pl.BlockSpec((None, HPS, tk, NOPE + ROPE), kmap),
                pl.BlockSpec((None, HPS, tk, HD), kmap)]
    args = [q, k, v]
    aliases = {}
    if ctx_only:
        in_specs.append(pl.BlockSpec(memory_space=pl.ANY))
        args.append(o_prev)
        aliases = {3: 0}
    return pl.pallas_call(
        functools.partial(_attn_kernel, first=first, kb=kb, nkb=nkb),
        grid=grid,
        in_specs=in_specs,
        out_specs=pl.BlockSpec((tq, HPS * HD), omap),
        out_shape=jax.ShapeDtypeStruct((rows, BW), F32),
        input_output_aliases=aliases,
        compiler_params=_params("parallel", "parallel", "parallel"),
        name="mla_attn_ctx" if ctx_only else "mla_attn",
    )(*args)


def _chunk_pos(g, lat_chunks, ncl, ncc):
    is_lat = g < lat_chunks
    pos = jnp.where(is_lat, g % ncl, (g - lat_chunks) % ncc)
    last = jnp.where(is_lat, pos == ncl - 1, pos == ncc - 1)
    return pos == 0, last


def _scan_chunk(b, s, rev, lat_chunks, ncl, ncc):
    c = jnp.where(s < ncc, s, s - ncc)
    if rev:
        c = jnp.where(s < ncc, ncc - 1 - c, ncl - 1 - c)
    return jnp.where(s < ncc, lat_chunks + b * ncc + c, b * ncl + c)


def _tri_masks(n):
    ri = lax.broadcasted_iota(jnp.int32, (n, n), 0)
    ci = lax.broadcasted_iota(jnp.int32, (n, n), 1)
    return ri, ci


def _split(x):
    hi = x.astype(BF16)
    return hi, (x - hi.astype(F32)).astype(BF16)


def _dot3(a, b):
    (ah, al), (bh, bl) = a, b
    return _dot(ah, bh) + (_dot(ah, bl) + _dot(al, bh))


def _dot_tri(tri, x):
    hi, lo = _split(x)
    lo2 = (x - hi.astype(F32) - lo.astype(F32)).astype(BF16)
    return _dot(tri, hi) + (_dot(tri, lo) + _dot(tri, lo2))


def _unit_tri_inv(mats, ri, ci, eye):
    rb, cb = ri >> 3, ci >> 3
    d8 = [jnp.where(rb == cb, a, 0.0) for a in mats]
    d8s = [_split(d) for d in d8]
    x2 = [_split(_dot3(d, d)) for d in d8s]
    ts = [eye - d for d in d8]
    x4 = [_split(_dot3(x, x)) for x in x2]
    ts = [t + _dot3(_split(t), x) for t, x in zip(ts, x2)]
    ts = [t + _dot3(_split(t), x) for t, x in zip(ts, x4)]
    for _ in range(3):
        same = rb == cb
        rb, cb = rb >> 1, cb >> 1
        off = (rb == cb) & jnp.logical_not(same)
        bs = [jnp.where(off, a, 0.0).astype(BF16) for a in mats]
        tb = [t.astype(BF16) for t in ts]
        ys = [_dot(t, b).astype(BF16) for t, b in zip(tb, bs)]
        ts = [t - _dot(y, tl) for t, y, tl in zip(ts, ys, tb)]
    return ts


def _gdn_pre_kernel(x_ref, prev_ref, next_ref, a_ref, b_ref, cw_ref, alog_ref, dtb_ref,
                    u_ref, w_ref, qg_ref, kd_ref, at_ref, dec_ref, *, lat_chunks, ncl, ncc):
    first, last = _chunk_pos(pl.program_id(0), lat_chunks, ncl, ncc)
    nrow = CPP * CHUNK
    x = x_ref[...]
    xp = jnp.where(first, 0.0, prev_ref[7:8, :])
    xn = jnp.where(last, 0.0, next_ref[0:1, :])
    row = lax.broadcasted_iota(jnp.int32, (nrow, 1), 0)
    x_dn = jnp.where(row == 0, xp, pltpu.roll(x, 1, 0))
    x_up = jnp.where(row == nrow - 1, xn, pltpu.roll(x, nrow - 1, 0))
    cw = cw_ref[...]
    s = _silu(x_dn * cw[0:1] + x * cw[1:2] + x_up * cw[2:3])

    lane = lax.broadcasted_iota(jnp.int32, (1, 128), 1)
    g = -jnp.exp(alog_ref[...]) * _softplus(a_ref[...] + dtb_ref[...])
    beta = _sigmoid(b_ref[...])
    ri, ci = _tri_masks(CHUNK)
    eye_b = ri == ci
    eye = eye_b.astype(F32)
    low = (ri >= ci).astype(BF16)
    upp = (ri <= ci).astype(BF16)
    chunk_rows = [slice(c * CHUNK, (c + 1) * CHUNK) for c in range(CPP)]
    gc, g_last = [], []
    for c, rc in enumerate(chunk_rows):
        gcc = jnp.where(lane < HEADS, _dot_tri(low, g[rc]), _dot_tri(upp, g[rc]))
        gc.append(gcc)
        g_last.append(jnp.where(lane < HEADS, gcc[CHUNK - 1:CHUNK], gcc[0:1]))
        dec_ref[c] = jnp.exp(g_last[c])

    qn, kn, vs = [], [], []
    for h in range(HEADS):
        qh = s[:, h * HD:(h + 1) * HD]
        kh = s[:, BW + h * HD:BW + (h + 1) * HD]
        vs.append(s[:, 2 * BW + h * HD:2 * BW + (h + 1) * HD])
        qn.append(qh * lax.rsqrt(jnp.sum(qh * qh, axis=-1, keepdims=True) + EPS) * HD ** -0.5)
        kn.append(kh * lax.rsqrt(jnp.sum(kh * kh, axis=-1, keepdims=True) + EPS))
    qk = [[None] * HEADS for _ in range(CPP)]
    kk = [[None] * HEADS for _ in range(CPP)]
    for c, rc in enumerate(chunk_rows):
        for h in range(HEADS):
            knb = kn[h][rc].astype(BF16)
            qk[c][h] = _dot_nt(qn[h][rc].astype(BF16), knb)
            kk[c][h] = _dot_nt(knb, knb)

    probs = [(c, d, h) for c in range(CPP) for h in range(HEADS) for d in range(2)]
    mats, rhs = [], []
    for c, d, h in probs:
        rc = chunk_rows[c]
        sl = slice(h * HD, (h + 1) * HD)
        dh = d * HEADS + h
        gcc = gc[c][:, dh:dh + 1]
        bt = beta[rc, dh:dh + 1]
        grow = jnp.sum(jnp.where(eye_b, gcc, 0.0), axis=0, keepdims=True)
        incl = (ri >= ci) if d == 0 else (ri <= ci)
        strict = (ri > ci) if d == 0 else (ri < ci)
        dm = jnp.where(incl, jnp.exp(gcc - grow), 0.0)
        eg = jnp.exp(gcc)
        knc = kn[h][rc]
        mats.append(jnp.where(strict, bt * kk[c][h] * dm, 0.0))
        rhs.append(jnp.concatenate([vs[h][rc] * bt, knc * (bt * eg)], axis=1).astype(BF16))
        qg_ref[d, rc, sl] = (qn[h][rc] * eg).astype(BF16)
        kd_ref[d, rc, sl] = (knc * jnp.exp(g_last[c][:, dh:dh + 1] - gcc)).astype(BF16)
        at_ref[d, rc, h * HD:h * HD + CHUNK] = (qk[c][h] * dm).astype(BF16)
        at_ref[d, rc, h * HD + CHUNK:(h + 1) * HD] = jnp.zeros((CHUNK, HD - CHUNK), BF16)

    for (c, d, h), t, r in zip(probs, _unit_tri_inv(mats, ri, ci, eye), rhs):
        sl = slice(h * HD, (h + 1) * HD)
        uw = _dot(t.astype(BF16), r)
        u_ref[d, chunk_rows[c], sl] = uw[:, :HD]
        w_ref[d, chunk_rows[c], sl] = uw[:, HD:].astype(BF16)


def _gdn_pre(p, conv_w, a_log, dt_bias, nb, n_seq, n_ctx):
    t = p.shape[0]
    nrow = CPP * CHUNK
    assert n_seq % nrow == 0 and n_ctx % nrow == 0
    lat_blocks, nbl, nbc = nb * n_seq // nrow, n_seq // nrow, n_ctx // nrow
    qkv_blk = P_GQKV // (3 * BW)
    pad = lambda v: jnp.pad(v.reshape(1, -1), ((0, 0), (0, 128 - v.size)))
    dir_out = lambda dt: jax.ShapeDtypeStruct((2, t, BW), dt)
    dir_spec = pl.BlockSpec((2, nrow, BW), lambda g: (0, g, 0))
    r8 = nrow // 8
    return pl.pallas_call(
        functools.partial(_gdn_pre_kernel, lat_chunks=lat_blocks, ncl=nbl, ncc=nbc),
        grid=(t // nrow,),
        in_specs=[pl.BlockSpec((nrow, 3 * BW), lambda g: (g, qkv_blk)),
                  pl.BlockSpec((8, 3 * BW), lambda g: (jnp.maximum(g * r8 - 1, 0), qkv_blk)),
                  pl.BlockSpec((8, 3 * BW), lambda g: (jnp.minimum(g * r8 + r8, t // 8 - 1), qkv_blk)),
                  pl.BlockSpec((nrow, 128), lambda g: (g, P_GA // 128)),
                  pl.BlockSpec((nrow, 128), lambda g: (g, P_GB // 128)),
                  pl.BlockSpec((3, 3 * BW), lambda g: (0, 0)),
                  pl.BlockSpec((1, 128), lambda g: (0, 0)),
                  pl.BlockSpec((1, 128), lambda g: (0, 0))],
        out_specs=(dir_spec, dir_spec, dir_spec, dir_spec, dir_spec,
                   pl.BlockSpec((CPP, 1, 128), lambda g: (g, 0, 0))),
        out_shape=(dir_out(F32), dir_out(BF16), dir_out(BF16), dir_out(BF16), dir_out(BF16),
                   jax.ShapeDtypeStruct((t // CHUNK, 1, 128), F32)),
        compiler_params=_params("parallel"),
        name="gdn_chunk_prep",
    )(p, p, p, p, p, conv_w, pad(a_log), pad(dt_bias))


def _scan_rows(d, c):
    cc = c if d == 0 else CPS - 1 - c
    return slice(cc * CHUNK, (cc + 1) * CHUNK), cc


def _gdn_scan_kernel(*refs):
    ins, (of_ref, ob_ref, s_ref) = refs[:12], refs[12:]

    @pl.when(pl.program_id(1) == 0)
    def _():
        s_ref[...] = jnp.zeros(s_ref.shape, F32)

    chains = [(d, h) for d in range(2) for h in range(HEADS)]
    outs = (of_ref, ob_ref)
    st = [s_ref[d, h] for d, h in chains]
    for c in range(CPS):
        sb = [s.astype(BF16) for s in st]
        vn = []
        for (d, h), s in zip(chains, sb):
            u_ref, w_ref = ins[6 * d], ins[6 * d + 1]
            rows, _ = _scan_rows(d, c)
            sl = slice(h * HD, (h + 1) * HD)
            vn.append((u_ref[rows, sl] - _dot(w_ref[rows, sl], s)).astype(BF16))
        for i, (d, h) in enumerate(chains):
            qg_ref, kd_ref, at_ref, dec_ref = ins[6 * d + 2:6 * d + 6]
            rows, cc = _scan_rows(d, c)
            sl = slice(h * HD, (h + 1) * HD)
            outs[d][rows, sl] = _dot(qg_ref[rows, sl], sb[i]) + _dot(at_ref[rows, h * HD:h * HD + CHUNK], vn[i])
            dec = dec_ref[cc, 0:1, d * HEADS + h:d * HEADS + h + 1]
            st[i] = st[i] * dec + _dot_tn(kd_ref[rows, sl], vn[i])
    for (d, h), s in zip(chains, st):
        s_ref[d, h] = s


def _dir_scan(kernel, arrays, dec, nb, n_seq, n_ctx, scratch, name):
    t = arrays[0].shape[1]
    step = CPS * CHUNK
    assert n_seq % step == 0 and n_ctx % step == 0
    lat_blocks, ncl, ncc = nb * n_seq // step, n_seq // step, n_ctx // step
    in_specs, args = [], []
    for d in range(2):
        cmap = functools.partial(_scan_chunk, rev=bool(d), lat_chunks=lat_blocks, ncl=ncl, ncc=ncc)
        for arr in arrays:
            if arr.ndim == 3:
                in_specs.append(pl.BlockSpec((None, step, BW), lambda b, s, d=d, cmap=cmap: (d, cmap(b, s), 0)))
            else:
                in_specs.append(pl.BlockSpec((step, BW), lambda b, s, cmap=cmap: (cmap(b, s), 0)))
            args.append(arr)
        if dec.ndim == 3:
            in_specs.append(pl.BlockSpec((CPS, 1, dec.shape[-1]), lambda b, s, cmap=cmap: (cmap(b, s), 0, 0)))
        else:
            in_specs.append(pl.BlockSpec((None, CPS, 1, dec.shape[-1]),
                                         lambda b, s, d=d, cmap=cmap: (d, cmap(b, s), 0, 0)))
        args.append(dec)
    fmap = functools.partial(_scan_chunk, rev=False, lat_chunks=lat_blocks, ncl=ncl, ncc=ncc)
    bmap = functools.partial(_scan_chunk, rev=True, lat_chunks=lat_blocks, ncl=ncl, ncc=ncc)
    return pl.pallas_call(
        kernel,
        grid=(nb, ncc + ncl),
        in_specs=in_specs,
        out_specs=(pl.BlockSpec((step, BW), lambda b, s: (fmap(b, s), 0)),
                   pl.BlockSpec((step, BW), lambda b, s: (bmap(b, s), 0))),
        out_shape=(jax.ShapeDtypeStruct((t, BW), F32), jax.ShapeDtypeStruct((t, BW), F32)),
        scratch_shapes=[scratch],
        compiler_params=_params("parallel", "arbitrary"),
        name=name,
    )(*args)


def _hg_block_decay(gc, rev):
    out = None
    for b in range(CHUNK // SUB):
        lo, hi = gc[b * SUB:b * SUB + 1], gc[(b + 1) * SUB - 1:(b + 1) * SUB]
        dcy = (hi - lo) if rev else (lo - hi)
        out = dcy if out is None else jnp.maximum(out, dcy)
    return out


def _hg_scores(q, k, gc, rev, exact):
    nblk = CHUNK // SUB
    lane = lax.broadcasted_iota(jnp.int32, (1, CHUNK), 1)
    row8 = lax.broadcasted_iota(jnp.int32, (8, 1), 0)
    blk = lambda x, b: x[b * SUB:(b + 1) * SUB]
    ref_row = lambda b: b * SUB + (0 if rev else SUB - 1)
    refs = [gc[ref_row(b):ref_row(b) + 1] for b in range(nblk)]
    kt = jnp.concatenate([blk(k, b) * jnp.exp2(refs[b] - blk(gc, b)) for b in range(nblk)], axis=0)
    srcs = list(range(nblk - 1, 0, -1)) if rev else list(range(nblk - 1))
    rows_of = (lambda j: slice(0, j * SUB)) if rev else (lambda j: slice((j + 1) * SUB, CHUNK))
    qt = jnp.concatenate([q[rows_of(j)] * jnp.exp2(gc[rows_of(j)] - refs[j]) for j in srcs], axis=0)
    seg_off, off = {}, 0
    for j in srcs:
        seg_off[j] = off
        off += (j if rev else nblk - 1 - j) * SUB
    ktb, qtb = kt.astype(BF16), qt.astype(BF16)
    cross = [_dot_nt(qtb[:, h * HD:(h + 1) * HD], ktb[:, h * HD:(h + 1) * HD]) for h in range(HEADS)]

    def assemble(own):
        out = []
        for h in range(HEADS):
            groups = []
            for g8 in range(CHUNK // 8):
                r0 = 8 * g8
                bi = r0 // SUB
                val = own[h][g8]
                for j in srcs:
                    if (j < bi and not rev) or (j > bi and rev):
                        base = seg_off[j] + (r0 if rev else r0 - (j + 1) * SUB)
                        val = jnp.where((lane >= j * SUB) & (lane < (j + 1) * SUB), cross[h][base:base + 8], val)
                groups.append(val)
            out.append(jnp.concatenate(groups, axis=0))
        return tuple(out)

    def own_exact():
        acc = [[jnp.zeros((8, CHUNK), F32) for _ in range(CHUNK // 8)] for _ in range(HEADS)]
        for j in range(CHUNK):
            b, jj = j // SUB, j % SUB
            gj, kj = gc[j:j + 1], k[j:j + 1]
            for rg in range(SUB // 8):
                lo, hi = 8 * rg, 8 * rg + 7
                if (hi < jj and not rev) or (lo > jj and rev):
                    continue
                r0 = b * SUB + lo
                w = jnp.exp2(gc[r0:r0 + 8] - gj)
                if not ((lo >= jj and not rev) or (hi <= jj and rev)):
                    w = jnp.where((row8 + lo >= jj) if not rev else (row8 + lo <= jj), w, 0.0)
                tt = q[r0:r0 + 8] * w * kj
                g8 = r0 // 8
                for h in range(HEADS):
                    col = jnp.sum(tt[:, h * HD:(h + 1) * HD], axis=-1, keepdims=True)
                    acc[h][g8] = jnp.where(lane == j, col, acc[h][g8])
        return assemble(acc)

    first_row = lambda b: b * SUB + (SUB - 1 if rev else 0)
    starts = [gc[first_row(b):first_row(b) + 1] for b in range(nblk)]

    def own_matmul():
        rfull = jnp.concatenate([jnp.broadcast_to(starts[b], (SUB, q.shape[1])) for b in range(nblk)], axis=0)
        qd = (q * jnp.exp2(gc - rfull)).astype(BF16)
        kd = (k * jnp.exp2(rfull - gc)).astype(BF16)
        ri, ci = _tri_masks(CHUNK)
        sh = SUB.bit_length() - 1
        keep = ((ri >> sh) == (ci >> sh)) & ((ri <= ci) if rev else (ri >= ci))
        own = []
        for h in range(HEADS):
            sd = jnp.where(keep, _dot_nt(qd[:, h * HD:(h + 1) * HD], kd[:, h * HD:(h + 1) * HD]), 0.0)
            own.append([sd[8 * g8:8 * g8 + 8] for g8 in range(CHUNK // 8)])
        return assemble(own)

    return own_exact() if exact else own_matmul()


def _hg_pre_kernel(q_ref, f_ref, i_ref, lbl_ref, oi_ref, qg_ref, kd_ref, vb_ref, dec_ref, *, layer):
    lbl = lbl_ref[...]
    e = jnp.exp(lbl - jnp.max(lbl, axis=0, keepdims=True))
    sm = e / jnp.sum(e, axis=0, keepdims=True)
    lb_all = sm[0]
    for l in range(1, layer + 1):
        lb_all = lb_all + sm[l]
    lb_all = lb_all - sm[0]
    q_all = q_ref[...] * HD ** -0.5
    v_all = i_ref[...].astype(BF16)
    vb_ref[...] = v_all
    ri, ci = _tri_masks(CHUNK)
    work, decay = [], None
    for d in range(2):
        lb = lb_all[d:d + 1]
        f = f_ref[:, d * BW:(d + 1) * BW]
        e = jnp.exp(-jnp.abs(f))
        r = 1.0 / (1.0 + e)
        er = e * r
        pos = f >= 0.0
        sig, sig_neg = jnp.where(pos, r, er), jnp.where(pos, er, r)
        log_f = jnp.log2(jnp.maximum(lb, LB_FLOOR) + (1.0 - lb) * sig)
        k_all = (1.0 - lb) * sig_neg
        tri = ((ri >= ci) if d == 0 else (ri <= ci)).astype(BF16)
        for c in range(HG_CPP):
            rc = slice(c * CHUNK, (c + 1) * CHUNK)
            q, k, v = q_all[rc], k_all[rc], v_all[rc]
            gc = _dot_tri(tri, log_f[rc])
            g_last = gc[CHUNK - 1:CHUNK] if d == 0 else gc[0:1]
            dec_ref[d, c] = jnp.exp2(g_last)
            qg_ref[d, rc] = (q * jnp.exp2(gc)).astype(BF16)
            kd_ref[d, rc] = (k * jnp.exp2(g_last - gc)).astype(BF16)
            work.append((d, rc, q, k, v, gc))
            dcy = _hg_block_decay(gc, rev=bool(d))
            decay = dcy if decay is None else jnp.maximum(decay, dcy)

    def intra(exact):
        for d, rc, q, k, v, gc in work:
            scores = _hg_scores(q, k, gc, rev=bool(d), exact=exact)
            for h in range(HEADS):
                sl = slice(h * HD, (h + 1) * HD)
                oi_ref[d, rc, sl] = _dot(scores[h].astype(BF16), v[:, sl])

    safe = jnp.max(decay) <= HG_MAX_DECAY
    pl.when(safe)(lambda: intra(False))
    pl.when(jnp.logical_not(safe))(lambda: intra(True))


def _hg_pre(p, lb_logits, layer):
    t = p.shape[0]
    nc = t // CHUNK
    nrow = HG_CPP * CHUNK
    assert t % nrow == 0
    dir_out = lambda dt: jax.ShapeDtypeStruct((2, t, BW), dt)
    dir_spec = pl.BlockSpec((2, nrow, BW), lambda g: (0, g, 0))
    return pl.pallas_call(
        functools.partial(_hg_pre_kernel, layer=layer),
        grid=(t // nrow,),
        in_specs=[pl.BlockSpec((nrow, BW), lambda g: (g, P_HQ // BW)),
                  pl.BlockSpec((nrow, 2 * BW), lambda g: (g, P_HF // (2 * BW))),
                  pl.BlockSpec((nrow, BW), lambda g: (g, P_HI // BW)),
                  pl.BlockSpec(lb_logits.shape, lambda g: (0, 0, 0))],
        out_specs=(dir_spec, dir_spec, dir_spec,
                   pl.BlockSpec((nrow, BW), lambda g: (g, 0)),
                   pl.BlockSpec((2, HG_CPP, 1, BW), lambda g: (0, g, 0, 0))),
        out_shape=(dir_out(F32), dir_out(BF16), dir_out(BF16),
                   jax.ShapeDtypeStruct((t, BW), BF16),
                   jax.ShapeDtypeStruct((2, nc, 1, BW), F32)),
        compiler_params=_params("parallel"),
        name="hgrn2_chunk_prep",
    )(p, p, p, lb_logits)


def _hg_scan_kernel(*refs):
    ins, (of_ref, ob_ref, s_ref) = refs[:10], refs[10:]

    @pl.when(pl.program_id(1) == 0)
    def _():
        s_ref[...] = jnp.zeros(s_ref.shape, F32)

    chains = [(d, h) for d in range(2) for h in range(HEADS)]
    outs = (of_ref, ob_ref)
    incs = []
    for d, h in chains:
        kd_ref, v_ref = ins[5 * d + 2], ins[5 * d + 3]
        sl = slice(h * HD, (h + 1) * HD)
        incs.append([_dot_tn(v_ref[_scan_rows(d, c)[0], sl], kd_ref[_scan_rows(d, c)[0], sl])
                     for c in range(CPS)])
    states = []
    for i, (d, h) in enumerate(chains):
        dec_ref = ins[5 * d + 4]
        sl = slice(h * HD, (h + 1) * HD)
        st = s_ref[d, h]
        seq = []
        for c in range(CPS):
            seq.append(st.astype(BF16))
            st = st * dec_ref[_scan_rows(d, c)[1], :, sl] + incs[i][c]
        s_ref[d, h] = st
        states.append(seq)
    for i, (d, h) in enumerate(chains):
        oi_ref, qg_ref = ins[5 * d], ins[5 * d + 1]
        sl = slice(h * HD, (h + 1) * HD)
        for c in range(CPS):
            rows, _ = _scan_rows(d, c)
            outs[d][rows, sl] = oi_ref[rows, sl] + _dot_nt(qg_ref[rows, sl], states[i][c])


def _merge_kernel(h_ref, gof_ref, gob_ref, gg_ref, yb_ref, hof_ref, hob_ref, hgg_ref, u_ref,
                  gnorm_ref, hnorm_ref, gt_ref, nw_ref, wg_ref, wb_ref, wo_ref, o_ref):
    def readout(of_ref, ob_ref, gate_ref, norm_ref):
        o = of_ref[...] + ob_ref[...]
        gate = gate_ref[...]
        parts = []
        for h in range(HEADS):
            sl = slice(h * HD, (h + 1) * HD)
            parts.append(_rms(o[:, sl], norm_ref[...]) * _silu(gate[:, sl]))
        return jnp.concatenate(parts, axis=1).astype(BF16)

    ys = (readout(gof_ref, gob_ref, gg_ref, gnorm_ref),
          yb_ref[...].astype(BF16),
          readout(hof_ref, hob_ref, hgg_ref, hnorm_ref))
    d = h_ref.shape[1]
    u = u_ref[...]
    m = None
    for j in range(3):
        gate_logits = _dot(u, wg_ref[:, j * d:(j + 1) * d])
        term = _sigmoid(gate_logits) * _dot(ys[j], wb_ref[j])
        m = term if m is None else m + term
    y = _dot(m.astype(BF16), wo_ref[...])
    o_ref[...] = h_ref[...] + gt_ref[...] * _rms(y, nw_ref[...])


def _merge(h, g_of, g_ob, yb, h_of, h_ob, p, u, gnorm, hnorm, gate_mod, nw, wg, wb, wo, n_rows, n_seq, nb):
    d = h.shape[1]
    tm = TM
    row = lambda i: (i, 0)
    bw_spec = pl.BlockSpec((tm, BW), row)
    pcol = lambda off, width: pl.BlockSpec((tm, width), lambda i: (i, off // width))
    vec = lambda n: pl.BlockSpec((1, n), lambda i: (0, 0))
    return pl.pallas_call(
        _merge_kernel,
        grid=(n_rows // tm,),
        in_specs=[pl.BlockSpec((tm, d), row), bw_spec, bw_spec, pcol(P_GGATE, BW), bw_spec,
                  bw_spec, bw_spec, pcol(P_HGATE, BW), pl.BlockSpec((tm, d), row),
                  vec(HD), vec(HD),
                  pl.BlockSpec((None, 1, d), lambda i: (jnp.minimum((i * tm) // n_seq, nb), 0, 0)),
                  vec(d), _resident(wg.shape), _resident(wb.shape), _resident(wo.shape)],
        out_specs=pl.BlockSpec((tm, d), row),
        out_shape=jax.ShapeDtypeStruct((n_rows, d), F32),
        compiler_params=_params("parallel"),
        name="branch_merge",
    )(h, g_of, g_ob, p, yb, h_of, h_ob, p, u, gnorm, hnorm, gate_mod, nw, wg, wb, wo)


def _pack_w_in(w):
    d = w.shape[0]
    w = w.astype(BF16)
    sizes = (BW, BW, BW, BW, 2 * HEADS, 2 * HEADS, Q_RANK, KV_RANK, ROPE, BW, 2 * BW, BW, BW, 3 * d)
    offs = [0]
    for s in sizes:
        offs.append(offs[-1] + s)
    part = lambda i: w[:, offs[i]:offs[i + 1]]
    gq, gk, gv, ggate, ga, gb, qa, kva, kr, hq, hf, hi, hgate, gates = (part(i) for i in range(14))
    z = lambda n: jnp.zeros((d, n), w.dtype)
    swap = jnp.concatenate([kr[:, 16:32], kr[:, 0:16], kr[:, 48:64], kr[:, 32:48]], axis=1)
    rec = jnp.concatenate([gq, gk, gv, ggate, hf, hq, hi, hgate, ga, z(120), gb, z(120)], axis=1).astype(BF16)
    mla = jnp.concatenate([qa, z(128), kva, kr, swap], axis=1).astype(BF16)
    assert rec.shape[1] == P_COLS and mla.shape[1] == M_COLS
    return rec.reshape(d, P_COLS // PROJ_TN, PROJ_TN).transpose(1, 0, 2), mla, gates.astype(BF16)


def _pack_wq(w):
    w = w.reshape(Q_RANK, HEADS, NOPE + ROPE)
    nope = w[:, :, :NOPE].reshape(Q_RANK, HEADS * NOPE)
    r = w[:, :, NOPE:]
    sw = jnp.concatenate([r[..., 16:32], r[..., 0:16], r[..., 48:64], r[..., 32:48]], axis=-1)
    rope = jnp.concatenate([r, sw], axis=-1).reshape(Q_RANK, HEADS * 2 * ROPE)
    return jnp.concatenate([nope, rope], axis=1).astype(BF16)


def _pack_wkv(w):
    w = w.reshape(KV_RANK, HEADS, NOPE + HD)
    return jnp.concatenate([w[:, :, :NOPE].reshape(KV_RANK, -1), w[:, :, NOPE:].reshape(KV_RANK, -1)],
                           axis=1).astype(BF16)


def _rope_table(nb, n_seq, n_ctx):
    nf = ROPE // 4
    rows = n_seq // GRID_W
    rpos = jnp.repeat(jnp.arange(rows, dtype=F32), GRID_W)
    cpos = jnp.tile(jnp.arange(GRID_W, dtype=F32), rows)
    inv = ROPE_BASE ** (-jnp.arange(nf, dtype=F32) / nf)
    ar, ac = rpos[:, None] * inv, cpos[:, None] * inv
    cos = jnp.concatenate([jnp.cos(ar), jnp.cos(ar), jnp.cos(ac), jnp.cos(ac)], axis=1)
    sin = jnp.concatenate([-jnp.sin(ar), jnp.sin(ar), -jnp.sin(ac), jnp.sin(ac)], axis=1)
    lat = jnp.tile(jnp.concatenate([cos, sin], axis=1), (nb, 1))
    ctx = jnp.concatenate([jnp.ones((nb * n_ctx, ROPE), F32), jnp.zeros((nb * n_ctx, ROPE), F32)], axis=1)
    return jnp.concatenate([lat, ctx], axis=0)


def kernel(x, c, ctx, c_ctx, w_ada, b_ada, norm_w, ffn_w_in, ffn_w_out, w_in, gdn_conv, gdn_a_log, gdn_dt_bias, gdn_norm, mla_q_norm, mla_kv_norm, mla_w_q_b, mla_w_kv_b, hg_lb_logits, hg_norm, w_branch, w_out):
    nb, n_seq, d = x.shape
    n_ctx = ctx.shape[1]
    depth = w_ada.shape[0]
    lat_rows, rows = nb * n_seq, nb * (n_seq + n_ctx)
    assert nb + 1 <= 8 and n_seq % (FFN_SUB * TM) == 0 and n_ctx % TM_S == 0 and (nb * n_ctx) % TM == 0
    assert n_seq % n_ctx == 0 and lat_rows % n_ctx == 0

    cc = jnp.concatenate([c, c_ctx[None], jnp.zeros((8 - nb - 1, d), F32)], axis=0)
    mods = _ada(cc, w_ada, b_ada)[:, :nb + 1].reshape(depth, nb + 1, 9, 1, d).transpose(0, 2, 1, 3, 4)
    cs = _rope_table(nb, n_seq, n_ctx)
    h = jnp.concatenate([x.reshape(lat_rows, d), ctx.reshape(nb * n_ctx, d)], axis=0)

    for l in range(depth):
        last = l == depth - 1
        md, nw = mods[l], norm_w[l][:, None, :]
        wi = [ffn_w_in[l, j].astype(BF16) for j in range(2)]
        wo = [ffn_w_out[l, j].astype(BF16) for j in range(2)]

        h, u = _ffn(h, md[0:3], nw[0:2], wi[0], wo[0], rows, n_seq, nb, mixer_mods=(nw[2], md[3], md[4]))
        w_rec, w_low, w_gates = _pack_w_in(w_in[l])
        p = _proj(u, w_rec)

        g_ops = _gdn_pre(p, gdn_conv[l], gdn_a_log[l], gdn_dt_bias[l], nb, n_seq, n_ctx)
        g_of, g_ob = _dir_scan(_gdn_scan_kernel, g_ops[:5], g_ops[5], nb, n_seq, n_ctx,
                               pltpu.VMEM((2, HEADS, HD, HD), F32), "gdn_scan")

        hoi, hqg, hkd, hvb, hdec = _hg_pre(p, hg_lb_logits, l)
        h_of, h_ob = _dir_scan(_hg_scan_kernel, (hoi, hqg, hkd, hvb), hdec, nb, n_seq, n_ctx,
                               pltpu.VMEM((2, HEADS, HD, HD), F32), "hgrn2_scan")

        q, k, v = _mla_prep(u, cs, mla_q_norm[l][None], mla_kv_norm[l][None], w_low,
                            _pack_wq(mla_w_q_b[l]), _pack_wkv(mla_w_kv_b[l]), nb, n_seq, n_ctx)
        yb = _attention(q, k, v, None, nb, n_seq, n_ctx, rows, ctx_only=False)
        if not last:
            yb = _attention(q, k, v, yb, nb, n_seq, n_ctx, rows, ctx_only=True)

        out_rows = lat_rows if last else rows
        h = _merge(h, g_of, g_ob, yb, h_of, h_ob, p, u, gdn_norm[l][None], hg_norm[l][None], md[5], nw[3],
                   w_gates, w_branch[l].astype(BF16), w_out[l].astype(BF16), out_rows, n_seq, nb)
        h = _ffn(h, md[6:9], nw[4:6], wi[1], wo[1], out_rows, n_seq, nb)
    return h.reshape(nb, n_seq, d)
```

```python
import functools

import jax
import jax.numpy as jnp
from jax import lax
from jax.experimental import pallas as pl
from jax.experimental.pallas import tpu as pltpu

F32 = jnp.float32
BF16 = jnp.bfloat16
EPS = 1e-6
LB_FLOOR = 1e-30
GRID_W = 64
ROPE_BASE = 10000.0

D_FF = 2816
HEADS = 4
HD = 128
ROPE = 64
NOPE = 128
Q_RANK = 384
KV_RANK = 256
CHUNK = 64
CPS = 4
CPP = 4
HG_CPP = 4
SUB = 16
BW = HEADS * HD

TM = 512
TM_S = 256
FF_CK = 256
FFN_SUB = 2
VMEM_LIMIT = 48 * 1024 * 1024
FFN_VMEM_LIMIT = 56 * 1024 * 1024
ATTN_LOG2_SCALE = (NOPE + ROPE) ** -0.5 * 1.4426950408889634
HPS = 2
KSUB = 256
HG_MAX_DECAY = 100.0
MAX_JUMP = 64.0

P_GQKV = 0
P_GGATE = 1536
P_HF = 2048
P_HQ = 3072
P_HI = 3584
P_HGATE = 4096
P_GA = 4608
P_GB = 4736
P_COLS = 4864
PROJ_TN = 2432
M_QA, M_KVA, M_KR, M_COLS = 0, 512, 768, 896

NT = (((1,), (1,)), ((), ()))
TN = (((0,), (0,)), ((), ()))


def _dot(a, b):
    return jnp.dot(a, b, preferred_element_type=F32)


def _dot_nt(a, b):
    return lax.dot_general(a, b, NT, preferred_element_type=F32)


def _dot_tn(a, b):
    return lax.dot_general(a, b, TN, preferred_element_type=F32)


def _sigmoid(x):
    return 1.0 / (1.0 + jnp.exp(-x))


def _silu(x):
    return x * _sigmoid(x)


def _softplus(x):
    return jnp.maximum(x, 0.0) + jnp.log(1.0 + jnp.exp(-jnp.abs(x)))


def _rms(x, w):
    return x * lax.rsqrt(jnp.mean(x * x, axis=-1, keepdims=True) + EPS) * w


def _resident(shape):
    zeros = (0,) * len(shape)
    return pl.BlockSpec(shape, lambda *_: zeros, pipeline_mode=pl.Buffered(1))


def _params(*sem):
    return pltpu.CompilerParams(dimension_semantics=sem, vmem_limit_bytes=VMEM_LIMIT)


def _ada_kernel(c_ref, w_ref, b_ref, o_ref):
    s = _silu(c_ref[...])
    o_ref[...] = _dot(s.astype(BF16), w_ref[...].astype(BF16)) + b_ref[...]


def _ada(cc, w_ada, b_ada):
    depth, d, nm = w_ada.shape
    tn = 1024
    return pl.pallas_call(
        _ada_kernel,
        grid=(depth, nm // tn),
        in_specs=[pl.BlockSpec((8, d), lambda l, j: (0, 0)),
                  pl.BlockSpec((None, d, tn), lambda l, j: (l, 0, j)),
                  pl.BlockSpec((None, 1, tn), lambda l, j: (l, 0, j))],
        out_specs=pl.BlockSpec((None, 8, tn), lambda l, j: (l, 0, j)),
        out_shape=jax.ShapeDtypeStruct((depth, 8, nm), F32),
        compiler_params=_params("parallel", "parallel"),
        name="ada_mod",
    )(cc, w_ada, b_ada.reshape(depth, 1, nm))


def _ffn_kernel(x_ref, sh_ref, sc_ref, gt_ref, prew_ref, postw_ref, wi_ref, wo_ref, *rest, emit_u, n_sub):
    if emit_u:
        nw_ref, sh2_ref, sc2_ref, o_ref, u_ref = rest
    else:
        (o_ref,) = rest
    n_ck = D_FF // FF_CK
    half = n_ck // 2

    def prologue(s):
        x = x_ref[s * TM:(s + 1) * TM, :]
        return (_rms(x, prew_ref[...]) * (1.0 + sc_ref[...]) + sh_ref[...]).astype(BF16)

    def chunks(hn, acc, lo, hi):
        for c in range(lo, hi):
            g = _dot(hn, wi_ref[:, c * FF_CK:(c + 1) * FF_CK])
            u = _dot(hn, wi_ref[:, D_FF + c * FF_CK:D_FF + (c + 1) * FF_CK])
            a = (_silu(g) * u).astype(BF16)
            acc = acc + _dot(a, wo_ref[c * FF_CK:(c + 1) * FF_CK, :])
        return acc

    def epilogue(s, acc):
        rs = slice(s * TM, (s + 1) * TM)
        out = x_ref[rs, :] + 0.5 * gt_ref[...] * _rms(acc, postw_ref[...])
        o_ref[rs, :] = out
        if emit_u:
            u_ref[rs, :] = (_rms(out, nw_ref[...]) * (1.0 + sc2_ref[...]) + sh2_ref[...]).astype(BF16)

    zero = jnp.zeros((TM, x_ref.shape[1]), F32)
    hn = prologue(0)
    acc = chunks(hn, zero, 0, half)
    for s in range(n_sub):
        hn_next = prologue(s + 1) if s + 1 < n_sub else None
        acc = chunks(hn, acc, half, n_ck)
        if hn_next is not None:
            acc_next = chunks(hn_next, zero, 0, half)
        epilogue(s, acc)
        if hn_next is not None:
            hn, acc = hn_next, acc_next


def _ffn(h, mods, nw, wi, wo, n_rows, n_seq, nb, mixer_mods=None):
    d = h.shape[1]
    emit_u = mixer_mods is not None
    vec = pl.BlockSpec((1, d), lambda i: (0, 0))
    out_shape = [jax.ShapeDtypeStruct((n_rows, d), F32)]
    if emit_u:
        out_shape.append(jax.ShapeDtypeStruct((n_rows, d), BF16))

    tm = FFN_SUB * TM
    row = lambda i: (i, 0)
    mspec = pl.BlockSpec((None, 1, d), lambda i: (jnp.minimum((i * tm) // n_seq, nb), 0, 0))
    in_specs = [pl.BlockSpec((tm, d), row), mspec, mspec, mspec, vec, vec,
                _resident(wi.shape), _resident(wo.shape)]
    args = [h, mods[0], mods[1], mods[2], nw[0], nw[1], wi, wo]
    if emit_u:
        in_specs += [vec, mspec, mspec]
        args += list(mixer_mods)
    outs = pl.pallas_call(
        functools.partial(_ffn_kernel, emit_u=emit_u, n_sub=FFN_SUB),
        grid=(pl.cdiv(n_rows, tm),),
        in_specs=in_specs,
        out_specs=tuple(pl.BlockSpec((tm, d), row) for _ in out_shape),
        out_shape=tuple(out_shape),
        compiler_params=pltpu.CompilerParams(dimension_semantics=("parallel",),
                                             vmem_limit_bytes=FFN_VMEM_LIMIT),
        name="ffn_sublayer",
    )(*args)
    return outs if emit_u else outs[0]


def _proj_kernel(x_ref, w_ref, o_ref):
    o_ref[...] = _dot(x_ref[...], w_ref[pl.program_id(1)])


def _proj(u, w):
    t, d = u.shape
    nj, _, tn = w.shape
    return pl.pallas_call(
        _proj_kernel,
        grid=(t // TM, nj),
        in_specs=[pl.BlockSpec((TM, d), lambda i, j: (i, 0)), _resident(w.shape)],
        out_specs=pl.BlockSpec((TM, tn), lambda i, j: (i, j)),
        out_shape=jax.ShapeDtypeStruct((t, nj * tn), F32),
        compiler_params=_params("parallel", "arbitrary"),
        name="in_proj",
    )(u, w)


def _mla_prep_kernel(u_ref, cs_ref, qn_ref, kvn_ref, win_ref, wq_ref, wkv_ref, q_ref, k_ref, v_ref):
    cs = cs_ref[...]
    low = _dot(u_ref[...], win_ref[...])
    qa = _rms(low[:, M_QA:M_QA + Q_RANK], qn_ref[...]).astype(BF16)
    kva = _rms(low[:, M_KVA:M_KVA + KV_RANK], kvn_ref[...]).astype(BF16)
    q2 = _dot(qa, wq_ref[...])
    kv2 = _dot(kva, wkv_ref[...])
    kr = low[:, M_KR:M_KR + 2 * ROPE] * cs
    kr = (kr + pltpu.roll(kr, ROPE, 1))[:, :ROPE].astype(BF16)
    q2 = q2 * ATTN_LOG2_SCALE
    for h in range(HEADS):
        q_ref[h, :, :NOPE] = q2[:, h * NOPE:(h + 1) * NOPE].astype(BF16)
        qr = q2[:, BW + h * 128:BW + (h + 1) * 128] * cs
        q_ref[h, :, NOPE:] = (qr + pltpu.roll(qr, ROPE, 1))[:, :ROPE].astype(BF16)
        k_ref[h, :, :NOPE] = kv2[:, h * NOPE:(h + 1) * NOPE].astype(BF16)
        k_ref[h, :, NOPE:] = kr
        v_ref[h] = kv2[:, BW + h * HD:BW + (h + 1) * HD].astype(BF16)


def _mla_prep(u, cs, qn, kvn, w_low, wq, wkv, nb, n_seq, n_ctx):
    t, d = u.shape
    tm = TM_S
    lat_tiles = nb * n_seq // tm
    per_lat = n_seq // tm
    per_ctx = n_ctx // tm

    def omap(i):
        j = i - lat_tiles
        b = jnp.where(i < lat_tiles, i // per_lat, j // per_ctx)
        blk = jnp.where(i < lat_tiles, i % per_lat, per_lat + j % per_ctx)
        return (b, 0, blk, 0)

    tk = n_seq + n_ctx
    return pl.pallas_call(
        _mla_prep_kernel,
        grid=(t // tm,),
        in_specs=[pl.BlockSpec((tm, d), lambda i: (i, 0)),
                  pl.BlockSpec((tm, 128), lambda i: (i, 0)),
                  pl.BlockSpec((1, Q_RANK), lambda i: (0, 0)),
                  pl.BlockSpec((1, KV_RANK), lambda i: (0, 0)),
                  _resident(w_low.shape), _resident(wq.shape), _resident(wkv.shape)],
        out_specs=(pl.BlockSpec((None, HEADS, tm, NOPE + ROPE), omap),
                   pl.BlockSpec((None, HEADS, tm, NOPE + ROPE), omap),
                   pl.BlockSpec((None, HEADS, tm, HD), omap)),
        out_shape=(jax.ShapeDtypeStruct((nb, HEADS, tk, NOPE + ROPE), BF16),
                   jax.ShapeDtypeStruct((nb, HEADS, tk, NOPE + ROPE), BF16),
                   jax.ShapeDtypeStruct((nb, HEADS, tk, HD), BF16)),
        compiler_params=_params("parallel"),
        name="mla_prep",
    )(u, cs, qn, kvn, w_low, wq, wkv)


def _attn_kernel(q_ref, k_ref, v_ref, *rest, first, kb, nkb):
    o_ref = rest[-1]
    nh, tq = q_ref.shape[0], q_ref.shape[1]
    qs = [q_ref[h] for h in range(nh)]

    def safe_update(h, sl, carry):
        m, l, acc = carry
        t = _dot_nt(qs[h], k_ref[h, sl, :])
        m_new = jnp.maximum(m, jnp.max(t, axis=-1, keepdims=True))
        alpha = jnp.exp2(m - m_new)
        pr = jnp.exp2(t - m_new)
        l = alpha * l + jnp.sum(pr, axis=-1, keepdims=True)
        return m_new, l, alpha * acc + _dot(pr.astype(BF16), v_ref[h, sl, :])

    def fast_update(j, carry):
        st = [list(c) for c in carry]
        bm = [None] * nh
        items = [(pl.ds(pl.multiple_of(j * kb + i * KSUB, KSUB), KSUB), h)
                 for i in range(kb // KSUB) for h in range(nh)]
        score = lambda it: _dot_nt(qs[it[1]], k_ref[it[1], it[0], :])
        t_next = score(items[0])
        for n, (sl, h) in enumerate(items):
            t = t_next
            if n + 1 < len(items):
                t_next = score(items[n + 1])
            pr = jnp.exp2(t - st[h][0])
            tm = jnp.max(t, axis=-1, keepdims=True)
            bm[h] = tm if bm[h] is None else jnp.maximum(bm[h], tm)
            st[h][1] = st[h][1] + jnp.sum(pr, axis=-1, keepdims=True)
            st[h][2] = st[h][2] + _dot(pr.astype(BF16), v_ref[h, sl, :])
        out = []
        for h in range(nh):
            m, l, acc, jump = st[h]
            m_new = jnp.maximum(m, bm[h])
            alpha = jnp.exp2(m - m_new)
            out.append((m_new, l * alpha, acc * alpha, jnp.maximum(jump, bm[h] - m)))
        return tuple(out)

    init = (jnp.full((tq, 1), -1e30, F32), jnp.zeros((tq, 1), F32), jnp.zeros((tq, HD), F32))
    fsl = slice(first[0], first[0] + first[1])
    start = [safe_update(h, fsl, init) for h in range(nh)]
    if nkb == 0:
        for h in range(nh):
            o_ref[:, h * HD:(h + 1) * HD] = start[h][2] / start[h][1]
        return
    done = lax.fori_loop(0, nkb, fast_update, tuple(s + (jnp.zeros((tq, 1), F32),) for s in start),
                         unroll=_pick_block(nkb, (4, 2, 1)))
    for h in range(nh):
        o_ref[:, h * HD:(h + 1) * HD] = done[h][2] / done[h][1]

    for h in range(nh):
        @pl.when(jnp.max(done[h][3]) > MAX_JUMP)
        def _(h=h):
            step = lambda j, carry: safe_update(h, pl.ds(pl.multiple_of(j * kb, kb), kb), carry)
            _, l2, acc2 = lax.fori_loop(0, nkb, step, start[h])
            o_ref[:, h * HD:(h + 1) * HD] = acc2 / l2


def _pick_block(n, cands):
    for c in cands:
        if n % c == 0:
            return c
    raise ValueError(f"no block size for {n}")


def _attention(q, k, v, o_prev, nb, n_seq, n_ctx, rows, ctx_only):
    if ctx_only:
        tq, tk = n_ctx, n_ctx
        koff = n_seq // n_ctx
        grid = (nb, HEADS // HPS, 1)
        qmap = lambda b, h, i: (b, h, koff, 0)
        kmap = lambda b, h, i: (b, h, koff, 0)
        omap = lambda b, h, i: (nb * n_seq // n_ctx + b, h)
        first, kb, nkb = (0, n_ctx), n_ctx, 0
    else:
        tq, tk = _pick_block(n_seq, (512, 256)), n_seq + n_ctx
        grid = (nb, HEADS // HPS, n_seq // tq)
        qmap = lambda b, h, i: (b, h, i, 0)
        kmap = lambda b, h, i: (b, h, 0, 0)
        omap = lambda b, h, i: (b * (n_seq // tq) + i, h)
        kb = _pick_block(n_seq, (1024, 512))
        first, nkb = (n_seq, n_ctx), n_seq // kb
    in_specs = [pl.BlockSpec((None, HPS, tq, NOPE + ROPE), qmap),
                pl.BlockSpec((None, HPS, tk, NOPE + ROPE), kmap),
                pl.BlockSpec((None, HPS, tk, HD), kmap)]
    args = [q, k, v]
    aliases = {}
    if ctx_only:
        in_specs.append(pl.BlockSpec(memory_space=pl.ANY))
        args.append(o_prev)
        aliases = {3: 0}
    return pl.pallas_call(
        functools.partial(_attn_kernel, first=first, kb=kb, nkb=nkb),
        grid=grid,
        in_specs=in_specs,
        out_specs=pl.BlockSpec((tq, HPS * HD), omap),
        out_shape=jax.ShapeDtypeStruct((rows, BW), F32),
        input_output_aliases=aliases,
        compiler_params=_params("parallel", "parallel", "parallel"),
        name="mla_attn_ctx" if ctx_only else "mla_attn",
    )(*args)


def _chunk_pos(g, lat_chunks, ncl, ncc):
    is_lat = g < lat_chunks
    pos = jnp.where(is_lat, g % ncl, (g - lat_chunks) % ncc)
    last = jnp.where(is_lat, pos == ncl - 1, pos == ncc - 1)
    return pos == 0, last


def _scan_chunk(b, s, rev, lat_chunks, ncl, ncc):
    c = jnp.where(s < ncc, s, s - ncc)
    if rev:
        c = jnp.where(s < ncc, ncc - 1 - c, ncl - 1 - c)
    return jnp.where(s < ncc, lat_chunks + b * ncc + c, b * ncl + c)


def _tri_masks(n):
    ri = lax.broadcasted_iota(jnp.int32, (n, n), 0)
    ci = lax.broadcasted_iota(jnp.int32, (n, n), 1)
    return ri, ci


def _split(x):
    hi = x.astype(BF16)
    return hi, (x - hi.astype(F32)).astype(BF16)


def _dot3(a, b):
    (ah, al), (bh, bl) = a, b
    return _dot(ah, bh) + (_dot(ah, bl) + _dot(al, bh))


def _dot_tri(tri, x):
    hi, lo = _split(x)
    lo2 = (x - hi.astype(F32) - lo.astype(F32)).astype(BF16)
    return _dot(tri, hi) + (_dot(tri, lo) + _dot(tri, lo2))


def _unit_tri_inv(mats, ri, ci, eye):
    rb, cb = ri >> 3, ci >> 3
    d8 = [jnp.where(rb == cb, a, 0.0) for a in mats]
    d8s = [_split(d) for d in d8]
    x2 = [_split(_dot3(d, d)) for d in d8s]
    ts = [eye - d for d in d8]
    x4 = [_split(_dot3(x, x)) for x in x2]
    ts = [t + _dot3(_split(t), x) for t, x in zip(ts, x2)]
    ts = [t + _dot3(_split(t), x) for t, x in zip(ts, x4)]
    for _ in range(3):
        same = rb == cb
        rb, cb = rb >> 1, cb >> 1
        off = (rb == cb) & jnp.logical_not(same)
        bs = [jnp.where(off, a, 0.0).astype(BF16) for a in mats]
        tb = [t.astype(BF16) for t in ts]
        ys = [_dot(t, b).astype(BF16) for t, b in zip(tb, bs)]
        ts = [t - _dot(y, tl) for t, y, tl in zip(ts, ys, tb)]
    return ts


def _gdn_pre_kernel(x_ref, prev_ref, next_ref, a_ref, b_ref, cw_ref, alog_ref, dtb_ref,
                    u_ref, w_ref, qg_ref, kd_ref, at_ref, dec_ref, *, lat_chunks, ncl, ncc):
    first, last = _chunk_pos(pl.program_id(0), lat_chunks, ncl, ncc)
    nrow = CPP * CHUNK
    x = x_ref[...]
    xp = jnp.where(first, 0.0, prev_ref[7:8, :])
    xn = jnp.where(last, 0.0, next_ref[0:1, :])
    row = lax.broadcasted_iota(jnp.int32, (nrow, 1), 0)
    x_dn = jnp.where(row == 0, xp, pltpu.roll(x, 1, 0))
    x_up = jnp.where(row == nrow - 1, xn, pltpu.roll(x, nrow - 1, 0))
    cw = cw_ref[...]
    s = _silu(x_dn * cw[0:1] + x * cw[1:2] + x_up * cw[2:3])

    lane = lax.broadcasted_iota(jnp.int32, (1, 128), 1)
    g = -jnp.exp(alog_ref[...]) * _softplus(a_ref[...] + dtb_ref[...])
    beta = _sigmoid(b_ref[...])
    ri, ci = _tri_masks(CHUNK)
    eye_b = ri == ci
    eye = eye_b.astype(F32)
    low = (ri >= ci).astype(BF16)
    upp = (ri <= ci).astype(BF16)
    chunk_rows = [slice(c * CHUNK, (c + 1) * CHUNK) for c in range(CPP)]
    gc, g_last = [], []
    for c, rc in enumerate(chunk_rows):
        gcc = jnp.where(lane < HEADS, _dot_tri(low, g[rc]), _dot_tri(upp, g[rc]))
        gc.append(gcc)
        g_last.append(jnp.where(lane < HEADS, gcc[CHUNK - 1:CHUNK], gcc[0:1]))
        dec_ref[c] = jnp.exp(g_last[c])

    qn, kn, vs = [], [], []
    for h in range(HEADS):
        qh = s[:, h * HD:(h + 1) * HD]
        kh = s[:, BW + h * HD:BW + (h + 1) * HD]
        vs.append(s[:, 2 * BW + h * HD:2 * BW + (h + 1) * HD])
        qn.append(qh * lax.rsqrt(jnp.sum(qh * qh, axis=-1, keepdims=True) + EPS) * HD ** -0.5)
        kn.append(kh * lax.rsqrt(jnp.sum(kh * kh, axis=-1, keepdims=True) + EPS))
    qk = [[None] * HEADS for _ in range(CPP)]
    kk = [[None] * HEADS for _ in range(CPP)]
    for c, rc in enumerate(chunk_rows):
        for h in range(HEADS):
            knb = kn[h][rc].astype(BF16)
            qk[c][h] = _dot_nt(qn[h][rc].astype(BF16), knb)
            kk[c][h] = _dot_nt(knb, knb)

    probs = [(c, d, h) for c in range(CPP) for h in range(HEADS) for d in range(2)]
    mats, rhs = [], []
    for c, d, h in probs:
        rc = chunk_rows[c]
        sl = slice(h * HD, (h + 1) * HD)
        dh = d * HEADS + h
        gcc = gc[c][:, dh:dh + 1]
        bt = beta[rc, dh:dh + 1]
        grow = jnp.sum(jnp.where(eye_b, gcc, 0.0), axis=0, keepdims=True)
        incl = (ri >= ci) if d == 0 else (ri <= ci)
        strict = (ri > ci) if d == 0 else (ri < ci)
        dm = jnp.where(incl, jnp.exp(gcc - grow), 0.0)
        eg = jnp.exp(gcc)
        knc = kn[h][rc]
        mats.append(jnp.where(strict, bt * kk[c][h] * dm, 0.0))
        rhs.append(jnp.concatenate([vs[h][rc] * bt, knc * (bt * eg)], axis=1).astype(BF16))
        qg_ref[d, rc, sl] = (qn[h][rc] * eg).astype(BF16)
        kd_ref[d, rc, sl] = (knc * jnp.exp(g_last[c][:, dh:dh + 1] - gcc)).astype(BF16)
        at_ref[d, rc, h * HD:h * HD + CHUNK] = (qk[c][h] * dm).astype(BF16)
        at_ref[d, rc, h * HD + CHUNK:(h + 1) * HD] = jnp.zeros((CHUNK, HD - CHUNK), BF16)

    for (c, d, h), t, r in zip(probs, _unit_tri_inv(mats, ri, ci, eye), rhs):
        sl = slice(h * HD, (h + 1) * HD)
        uw = _dot(t.astype(BF16), r)
        u_ref[d, chunk_rows[c], sl] = uw[:, :HD]
        w_ref[d, chunk_rows[c], sl] = uw[:, HD:].astype(BF16)


def _gdn_pre(p, conv_w, a_log, dt_bias, nb, n_seq, n_ctx):
    t = p.shape[0]
    nrow = CPP * CHUNK
    assert n_seq % nrow == 0 and n_ctx % nrow == 0
    lat_blocks, nbl, nbc = nb * n_seq // nrow, n_seq // nrow, n_ctx // nrow
    qkv_blk = P_GQKV // (3 * BW)
    pad = lambda v: jnp.pad(v.reshape(1, -1), ((0, 0), (0, 128 - v.size)))
    dir_out = lambda dt: jax.ShapeDtypeStruct((2, t, BW), dt)
    dir_spec = pl.BlockSpec((2, nrow, BW), lambda g: (0, g, 0))
    r8 = nrow // 8
    return pl.pallas_call(
        functools.partial(_gdn_pre_kernel, lat_chunks=lat_blocks, ncl=nbl, ncc=nbc),
        grid=(t // nrow,),
        in_specs=[pl.BlockSpec((nrow, 3 * BW), lambda g: (g, qkv_blk)),
                  pl.BlockSpec((8, 3 * BW), lambda g: (jnp.maximum(g * r8 - 1, 0), qkv_blk)),
                  pl.BlockSpec((8, 3 * BW), lambda g: (jnp.minimum(g * r8 + r8, t // 8 - 1), qkv_blk)),
                  pl.BlockSpec((nrow, 128), lambda g: (g, P_GA // 128)),
                  pl.BlockSpec((nrow, 128), lambda g: (g, P_GB // 128)),
                  pl.BlockSpec((3, 3 * BW), lambda g: (0, 0)),
                  pl.BlockSpec((1, 128), lambda g: (0, 0)),
                  pl.BlockSpec((1, 128), lambda g: (0, 0))],
        out_specs=(dir_spec, dir_spec, dir_spec, dir_spec, dir_spec,
                   pl.BlockSpec((CPP, 1, 128), lambda g: (g, 0, 0))),
        out_shape=(dir_out(F32), dir_out(BF16), dir_out(BF16), dir_out(BF16), dir_out(BF16),
                   jax.ShapeDtypeStruct((t // CHUNK, 1, 128), F32)),
        compiler_params=_params("parallel"),
        name="gdn_chunk_prep",
    )(p, p, p, p, p, conv_w, pad(a_log), pad(dt_bias))


def _scan_rows(d, c):
    cc = c if d == 0 else CPS - 1 - c
    return slice(cc * CHUNK, (cc + 1) * CHUNK), cc


def _mixer_scan_kernel(*refs):
    g_ins, h_ins = refs[:12], refs[12:22]
    g_of, g_ob, h_of, h_ob, g_state, h_state = refs[22:]

    @pl.when(pl.program_id(1) == 0)
    def _():
        g_state[...] = jnp.zeros(g_state.shape, F32)
        h_state[...] = jnp.zeros(h_state.shape, F32)

    filler = _hg_scan_body(h_ins, h_of, h_ob, h_state)
    _gdn_scan_body(g_ins, g_of, g_ob, g_state, filler)
    for _ in filler:
        pass


def _gdn_scan_body(ins, of_ref, ob_ref, s_ref, filler):
    def fill(n):
        for _ in range(n):
            next(filler, None)

    chains = [(d, h) for d in range(2) for h in range(HEADS)]
    outs = (of_ref, ob_ref)
    st = [s_ref[d, h] for d, h in chains]
    for c in range(CPS):
        sb = [s.astype(BF16) for s in st]
        vn = []
        for (d, h), s in zip(chains, sb):
            u_ref, w_ref = ins[6 * d], ins[6 * d + 1]
            rows, _ = _scan_rows(d, c)
            sl = slice(h * HD, (h + 1) * HD)
            vn.append((u_ref[rows, sl] - _dot(w_ref[rows, sl], s)).astype(BF16))
        fill(3)
        for i, (d, h) in enumerate(chains):
            qg_ref, kd_ref, at_ref, dec_ref = ins[6 * d + 2:6 * d + 6]
            rows, cc = _scan_rows(d, c)
            sl = slice(h * HD, (h + 1) * HD)
            outs[d][rows, sl] = _dot(qg_ref[rows, sl], sb[i]) + _dot(at_ref[rows, h * HD:h * HD + CHUNK], vn[i])
            dec = dec_ref[cc, 0:1, d * HEADS + h:d * HEADS + h + 1]
            st[i] = st[i] * dec + _dot_tn(kd_ref[rows, sl], vn[i])
        fill(3)
    for (d, h), s in zip(chains, st):
        s_ref[d, h] = s


def _dir_scan(kernel, groups, nb, n_seq, n_ctx, name):
    t = groups[0][0][0].shape[1]
    step = CPS * CHUNK
    assert n_seq % step == 0 and n_ctx % step == 0
    lat_blocks, ncl, ncc = nb * n_seq // step, n_seq // step, n_ctx // step
    maps = [functools.partial(_scan_chunk, rev=bool(d), lat_chunks=lat_blocks, ncl=ncl, ncc=ncc) for d in range(2)]
    in_specs, args = [], []
    for arrays, dec in groups:
        for d, cmap in enumerate(maps):
            for arr in arrays:
                if arr.ndim == 3:
                    in_specs.append(pl.BlockSpec((None, step, BW), lambda b, s, d=d, cmap=cmap: (d, cmap(b, s), 0)))
                else:
                    in_specs.append(pl.BlockSpec((step, BW), lambda b, s, cmap=cmap: (cmap(b, s), 0)))
                args.append(arr)
            if dec.ndim == 3:
                in_specs.append(pl.BlockSpec((CPS, 1, dec.shape[-1]), lambda b, s, cmap=cmap: (cmap(b, s), 0, 0)))
            else:
                in_specs.append(pl.BlockSpec((None, CPS, 1, dec.shape[-1]),
                                             lambda b, s, d=d, cmap=cmap: (d, cmap(b, s), 0, 0)))
            args.append(dec)
    out_specs = tuple(pl.BlockSpec((step, BW), lambda b, s, cmap=cmap: (cmap(b, s), 0))
                      for _ in groups for cmap in maps)
    return pl.pallas_call(
        kernel,
        grid=(nb, ncc + ncl),
        in_specs=in_specs,
        out_specs=out_specs,
        out_shape=tuple(jax.ShapeDtypeStruct((t, BW), F32) for _ in out_specs),
        scratch_shapes=[pltpu.VMEM((2, HEADS, HD, HD), F32) for _ in groups],
        compiler_params=_params("parallel", "arbitrary"),
        name=name,
    )(*args)


def _hg_block_decay(gc, rev):
    out = None
    for b in range(CHUNK // SUB):
        lo, hi = gc[b * SUB:b * SUB + 1], gc[(b + 1) * SUB - 1:(b + 1) * SUB]
        dcy = (hi - lo) if rev else (lo - hi)
        out = dcy if out is None else jnp.maximum(out, dcy)
    return out


def _hg_scores(q, k, gc, rev, exact):
    nblk = CHUNK // SUB
    lane = lax.broadcasted_iota(jnp.int32, (1, CHUNK), 1)
    row8 = lax.broadcasted_iota(jnp.int32, (8, 1), 0)
    blk = lambda x, b: x[b * SUB:(b + 1) * SUB]
    ref_row = lambda b: b * SUB + (0 if rev else SUB - 1)
    refs = [gc[ref_row(b):ref_row(b) + 1] for b in range(nblk)]
    kt = jnp.concatenate([blk(k, b) * jnp.exp2(refs[b] - blk(gc, b)) for b in range(nblk)], axis=0)
    srcs = list(range(nblk - 1, 0, -1)) if rev else list(range(nblk - 1))
    rows_of = (lambda j: slice(0, j * SUB)) if rev else (lambda j: slice((j + 1) * SUB, CHUNK))
    qt = jnp.concatenate([q[rows_of(j)] * jnp.exp2(gc[rows_of(j)] - refs[j]) for j in srcs], axis=0)
    seg_off, off = {}, 0
    for j in srcs:
        seg_off[j] = off
        off += (j if rev else nblk - 1 - j) * SUB
    ktb, qtb = kt.astype(BF16), qt.astype(BF16)
    cross = [_dot_nt(qtb[:, h * HD:(h + 1) * HD], ktb[:, h * HD:(h + 1) * HD]) for h in range(HEADS)]

    def assemble(own):
        out = []
        for h in range(HEADS):
            groups = []
            for g8 in range(CHUNK // 8):
                r0 = 8 * g8
                bi = r0 // SUB
                val = own[h][g8]
                for j in srcs:
                    if (j < bi and not rev) or (j > bi and rev):
                        base = seg_off[j] + (r0 if rev else r0 - (j + 1) * SUB)
                        val = jnp.where((lane >= j * SUB) & (lane < (j + 1) * SUB), cross[h][base:base + 8], val)
                groups.append(val)
            out.append(jnp.concatenate(groups, axis=0))
        return tuple(out)

    def own_exact():
        acc = [[jnp.zeros((8, CHUNK), F32) for _ in range(CHUNK // 8)] for _ in range(HEADS)]
        for j in range(CHUNK):
            b, jj = j // SUB, j % SUB
            gj, kj = gc[j:j + 1], k[j:j + 1]
            for rg in range(SUB // 8):
                lo, hi = 8 * rg, 8 * rg + 7
                if (hi < jj and not rev) or (lo > jj and rev):
                    continue
                r0 = b * SUB + lo
                w = jnp.exp2(gc[r0:r0 + 8] - gj)
                if not ((lo >= jj and not rev) or (hi <= jj and rev)):
                    w = jnp.where((row8 + lo >= jj) if not rev else (row8 + lo <= jj), w, 0.0)
                tt = q[r0:r0 + 8] * w * kj
                g8 = r0 // 8
                for h in range(HEADS):
                    col = jnp.sum(tt[:, h * HD:(h + 1) * HD], axis=-1, keepdims=True)
                    acc[h][g8] = jnp.where(lane == j, col, acc[h][g8])
        return assemble(acc)

    first_row = lambda b: b * SUB + (SUB - 1 if rev else 0)
    starts = [gc[first_row(b):first_row(b) + 1] for b in range(nblk)]

    def own_matmul():
        rfull = jnp.concatenate([jnp.broadcast_to(starts[b], (SUB, q.shape[1])) for b in range(nblk)], axis=0)
        qd = (q * jnp.exp2(gc - rfull)).astype(BF16)
        kd = (k * jnp.exp2(rfull - gc)).astype(BF16)
        ri, ci = _tri_masks(CHUNK)
        sh = SUB.bit_length() - 1
        keep = ((ri >> sh) == (ci >> sh)) & ((ri <= ci) if rev else (ri >= ci))
        own = []
        for h in range(HEADS):
            sd = jnp.where(keep, _dot_nt(qd[:, h * HD:(h + 1) * HD], kd[:, h * HD:(h + 1) * HD]), 0.0)
            own.append([sd[8 * g8:8 * g8 + 8] for g8 in range(CHUNK // 8)])
        return assemble(own)

    return own_exact() if exact else own_matmul()


def _hg_pre_kernel(q_ref, f_ref, i_ref, lbl_ref, oi_ref, qg_ref, kd_ref, vb_ref, dec_ref, *, layer):
    lbl = lbl_ref[...]
    e = jnp.exp(lbl - jnp.max(lbl, axis=0, keepdims=True))
    sm = e / jnp.sum(e, axis=0, keepdims=True)
    lb_all = sm[0]
    for l in range(1, layer + 1):
        lb_all = lb_all + sm[l]
    lb_all = lb_all - sm[0]
    q_all = q_ref[...] * HD ** -0.5
    v_all = i_ref[...].astype(BF16)
    vb_ref[...] = v_all
    ri, ci = _tri_masks(CHUNK)
    work, decay = [], None
    for d in range(2):
        lb = lb_all[d:d + 1]
        f = f_ref[:, d * BW:(d + 1) * BW]
        e = jnp.exp(-jnp.abs(f))
        r = 1.0 / (1.0 + e)
        er = e * r
        pos = f >= 0.0
        sig, sig_neg = jnp.where(pos, r, er), jnp.where(pos, er, r)
        log_f = jnp.log2(jnp.maximum(lb, LB_FLOOR) + (1.0 - lb) * sig)
        k_all = (1.0 - lb) * sig_neg
        tri = ((ri >= ci) if d == 0 else (ri <= ci)).astype(BF16)
        for c in range(HG_CPP):
            rc = slice(c * CHUNK, (c + 1) * CHUNK)
            q, k, v = q_all[rc], k_all[rc], v_all[rc]
            gc = _dot_tri(tri, log_f[rc])
            g_last = gc[CHUNK - 1:CHUNK] if d == 0 else gc[0:1]
            dec_ref[d, c] = jnp.exp2(g_last)
            qg_ref[d, rc] = (q * jnp.exp2(gc)).astype(BF16)
            kd_ref[d, rc] = (k * jnp.exp2(g_last - gc)).astype(BF16)
            work.append((d, rc, q, k, v, gc))
            dcy = _hg_block_decay(gc, rev=bool(d))
            decay = dcy if decay is None else jnp.maximum(decay, dcy)

    def intra(exact):
        for d, rc, q, k, v, gc in work:
            scores = _hg_scores(q, k, gc, rev=bool(d), exact=exact)
            for h in range(HEADS):
                sl = slice(h * HD, (h + 1) * HD)
                oi_ref[d, rc, sl] = _dot(scores[h].astype(BF16), v[:, sl])

    safe = jnp.max(decay) <= HG_MAX_DECAY
    pl.when(safe)(lambda: intra(False))
    pl.when(jnp.logical_not(safe))(lambda: intra(True))


def _hg_pre(p, lb_logits, layer):
    t = p.shape[0]
    nc = t // CHUNK
    nrow = HG_CPP * CHUNK
    assert t % nrow == 0
    dir_out = lambda dt: jax.ShapeDtypeStruct((2, t, BW), dt)
    dir_spec = pl.BlockSpec((2, nrow, BW), lambda g: (0, g, 0))
    return pl.pallas_call(
        functools.partial(_hg_pre_kernel, layer=layer),
        grid=(t // nrow,),
        in_specs=[pl.BlockSpec((nrow, BW), lambda g: (g, P_HQ // BW)),
                  pl.BlockSpec((nrow, 2 * BW), lambda g: (g, P_HF // (2 * BW))),
                  pl.BlockSpec((nrow, BW), lambda g: (g, P_HI // BW)),
                  pl.BlockSpec(lb_logits.shape, lambda g: (0, 0, 0))],
        out_specs=(dir_spec, dir_spec, dir_spec,
                   pl.BlockSpec((nrow, BW), lambda g: (g, 0)),
                   pl.BlockSpec((2, HG_CPP, 1, BW), lambda g: (0, g, 0, 0))),
        out_shape=(dir_out(F32), dir_out(BF16), dir_out(BF16),
                   jax.ShapeDtypeStruct((t, BW), BF16),
                   jax.ShapeDtypeStruct((2, nc, 1, BW), F32)),
        compiler_params=_params("parallel"),
        name="hgrn2_chunk_prep",
    )(p, p, p, lb_logits)


def _hg_scan_body(ins, of_ref, ob_ref, s_ref):
    chains = [(d, h) for d in range(2) for h in range(HEADS)]
    outs = (of_ref, ob_ref)
    incs = []
    for d, h in chains:
        kd_ref, v_ref = ins[5 * d + 2], ins[5 * d + 3]
        sl = slice(h * HD, (h + 1) * HD)
        incs.append([_dot_tn(v_ref[_scan_rows(d, c)[0], sl], kd_ref[_scan_rows(d, c)[0], sl])
                     for c in range(CPS)])
        yield
    states = []
    for i, (d, h) in enumerate(chains):
        dec_ref = ins[5 * d + 4]
        sl = slice(h * HD, (h + 1) * HD)
        st = s_ref[d, h]
        seq = []
        for c in range(CPS):
            seq.append(st.astype(BF16))
            st = st * dec_ref[_scan_rows(d, c)[1], :, sl] + incs[i][c]
        s_ref[d, h] = st
        states.append(seq)
        yield
    for i, (d, h) in enumerate(chains):
        oi_ref, qg_ref = ins[5 * d], ins[5 * d + 1]
        sl = slice(h * HD, (h + 1) * HD)
        for c in range(CPS):
            rows, _ = _scan_rows(d, c)
            outs[d][rows, sl] = oi_ref[rows, sl] + _dot_nt(qg_ref[rows, sl], states[i][c])
        yield


def _merge_kernel(h_ref, gof_ref, gob_ref, gg_ref, yb_ref, hof_ref, hob_ref, hgg_ref, u_ref,
                  gnorm_ref, hnorm_ref, gt_ref, nw_ref, wg_ref, wb_ref, wo_ref, o_ref):
    def readout(of_ref, ob_ref, gate_ref, norm_ref):
        o = of_ref[...] + ob_ref[...]
        gate = gate_ref[...]
        parts = []
        for h in range(HEADS):
            sl = slice(h * HD, (h + 1) * HD)
            parts.append(_rms(o[:, sl], norm_ref[...]) * _silu(gate[:, sl]))
        return jnp.concatenate(parts, axis=1).astype(BF16)

    ys = (readout(gof_ref, gob_ref, gg_ref, gnorm_ref),
          yb_ref[...].astype(BF16),
          readout(hof_ref, hob_ref, hgg_ref, hnorm_ref))
    d = h_ref.shape[1]
    u = u_ref[...]
    m = None
    for j in range(3):
        gate_logits = _dot(u, wg_ref[:, j * d:(j + 1) * d])
        term = _sigmoid(gate_logits) * _dot(ys[j], wb_ref[j])
        m = term if m is None else m + term
    y = _dot(m.astype(BF16), wo_ref[...])
    o_ref[...] = h_ref[...] + gt_ref[...] * _rms(y, nw_ref[...])


def _merge(h, g_of, g_ob, yb, h_of, h_ob, p, u, gnorm, hnorm, gate_mod, nw, wg, wb, wo, n_rows, n_seq, nb):
    d = h.shape[1]
    tm = TM
    row = lambda i: (i, 0)
    bw_spec = pl.BlockSpec((tm, BW), row)
    pcol = lambda off, width: pl.BlockSpec((tm, width), lambda i: (i, off // width))
    vec = lambda n: pl.BlockSpec((1, n), lambda i: (0, 0))
    return pl.pallas_call(
        _merge_kernel,
        grid=(n_rows // tm,),
        in_specs=[pl.BlockSpec((tm, d), row), bw_spec, bw_spec, pcol(P_GGATE, BW), bw_spec,
                  bw_spec, bw_spec, pcol(P_HGATE, BW), pl.BlockSpec((tm, d), row),
                  vec(HD), vec(HD),
                  pl.BlockSpec((None, 1, d), lambda i: (jnp.minimum((i * tm) // n_seq, nb), 0, 0)),
                  vec(d), _resident(wg.shape), _resident(wb.shape), _resident(wo.shape)],
        out_specs=pl.BlockSpec((tm, d), row),
        out_shape=jax.ShapeDtypeStruct((n_rows, d), F32),
        compiler_params=_params("parallel"),
        name="branch_merge",
    )(h, g_of, g_ob, p, yb, h_of, h_ob, p, u, gnorm, hnorm, gate_mod, nw, wg, wb, wo)


def _pack_w_in(w):
    d = w.shape[0]
    w = w.astype(BF16)
    sizes = (BW, BW, BW, BW, 2 * HEADS, 2 * HEADS, Q_RANK, KV_RANK, ROPE, BW, 2 * BW, BW, BW, 3 * d)
    offs = [0]
    for s in sizes:
        offs.append(offs[-1] + s)
    part = lambda i: w[:, offs[i]:offs[i + 1]]
    gq, gk, gv, ggate, ga, gb, qa, kva, kr, hq, hf, hi, hgate, gates = (part(i) for i in range(14))
    z = lambda n: jnp.zeros((d, n), w.dtype)
    swap = jnp.concatenate([kr[:, 16:32], kr[:, 0:16], kr[:, 48:64], kr[:, 32:48]], axis=1)
    rec = jnp.concatenate([gq, gk, gv, ggate, hf, hq, hi, hgate, ga, z(120), gb, z(120)], axis=1).astype(BF16)
    mla = jnp.concatenate([qa, z(128), kva, kr, swap], axis=1).astype(BF16)
    assert rec.shape[1] == P_COLS and mla.shape[1] == M_COLS
    return rec.reshape(d, P_COLS // PROJ_TN, PROJ_TN).transpose(1, 0, 2), mla, gates.astype(BF16)


def _pack_wq(w):
    w = w.reshape(Q_RANK, HEADS, NOPE + ROPE)
    nope = w[:, :, :NOPE].reshape(Q_RANK, HEADS * NOPE)
    r = w[:, :, NOPE:]
    sw = jnp.concatenate([r[..., 16:32], r[..., 0:16], r[..., 48:64], r[..., 32:48]], axis=-1)
    rope = jnp.concatenate([r, sw], axis=-1).reshape(Q_RANK, HEADS * 2 * ROPE)
    return jnp.concatenate([nope, rope], axis=1).astype(BF16)


def _pack_wkv(w):
    w = w.reshape(KV_RANK, HEADS, NOPE + HD)
    return jnp.concatenate([w[:, :, :NOPE].reshape(KV_RANK, -1), w[:, :, NOPE:].reshape(KV_RANK, -1)],
                           axis=1).astype(BF16)


def _rope_table(nb, n_seq, n_ctx):
    nf = ROPE // 4
    rows = n_seq // GRID_W
    rpos = jnp.repeat(jnp.arange(rows, dtype=F32), GRID_W)
    cpos = jnp.tile(jnp.arange(GRID_W, dtype=F32), rows)
    inv = ROPE_BASE ** (-jnp.arange(nf, dtype=F32) / nf)
    ar, ac = rpos[:, None] * inv, cpos[:, None] * inv
    cos = jnp.concatenate([jnp.cos(ar), jnp.cos(ar), jnp.cos(ac), jnp.cos(ac)], axis=1)
    sin = jnp.concatenate([-jnp.sin(ar), jnp.sin(ar), -jnp.sin(ac), jnp.sin(ac)], axis=1)
    lat = jnp.tile(jnp.concatenate([cos, sin], axis=1), (nb, 1))
    ctx = jnp.concatenate([jnp.ones((nb * n_ctx, ROPE), F32), jnp.zeros((nb * n_ctx, ROPE), F32)], axis=1)
    return jnp.concatenate([lat, ctx], axis=0)


def kernel(x, c, ctx, c_ctx, w_ada, b_ada, norm_w, ffn_w_in, ffn_w_out, w_in, gdn_conv, gdn_a_log, gdn_dt_bias, gdn_norm, mla_q_norm, mla_kv_norm, mla_w_q_b, mla_w_kv_b, hg_lb_logits, hg_norm, w_branch, w_out):
    nb, n_seq, d = x.shape
    n_ctx = ctx.shape[1]
    depth = w_ada.shape[0]
    lat_rows, rows = nb * n_seq, nb * (n_seq + n_ctx)
    assert nb + 1 <= 8 and n_seq % (FFN_SUB * TM) == 0 and n_ctx % TM_S == 0 and (nb * n_ctx) % TM == 0
    assert n_seq % n_ctx == 0 and lat_rows % n_ctx == 0

    cc = jnp.concatenate([c, c_ctx[None], jnp.zeros((8 - nb - 1, d), F32)], axis=0)
    mods = _ada(cc, w_ada, b_ada)[:, :nb + 1].reshape(depth, nb + 1, 9, 1, d).transpose(0, 2, 1, 3, 4)
    cs = _rope_table(nb, n_seq, n_ctx)
    h = jnp.concatenate([x.reshape(lat_rows, d), ctx.reshape(nb * n_ctx, d)], axis=0)

    for l in range(depth):
        last = l == depth - 1
        md, nw = mods[l], norm_w[l][:, None, :]
        wi = [ffn_w_in[l, j].astype(BF16) for j in range(2)]
        wo = [ffn_w_out[l, j].astype(BF16) for j in range(2)]

        h, u = _ffn(h, md[0:3], nw[0:2], wi[0], wo[0], rows, n_seq, nb, mixer_mods=(nw[2], md[3], md[4]))
        w_rec, w_low, w_gates = _pack_w_in(w_in[l])
        p = _proj(u, w_rec)

        g_ops = _gdn_pre(p, gdn_conv[l], gdn_a_log[l], gdn_dt_bias[l], nb, n_seq, n_ctx)
        hoi, hqg, hkd, hvb, hdec = _hg_pre(p, hg_lb_logits, l)
        g_of, g_ob, h_of, h_ob = _dir_scan(_mixer_scan_kernel,
                                           [(g_ops[:5], g_ops[5]), ((hoi, hqg, hkd, hvb), hdec)],
                                           nb, n_seq, n_ctx, "mixer_scan")

        q, k, v = _mla_prep(u, cs, mla_q_norm[l][None], mla_kv_norm[l][None], w_low,
                            _pack_wq(mla_w_q_b[l]), _pack_wkv(mla_w_kv_b[l]), nb, n_seq, n_ctx)
        yb = _attention(q, k, v, None, nb, n_seq, n_ctx, rows, ctx_only=False)
        if not last:
            yb = _attention(q, k, v, yb, nb, n_seq, n_ctx, rows, ctx_only=True)

        out_rows = lat_rows if last else rows
        h = _merge(h, g_of, g_ob, yb, h_of, h_ob, p, u, gdn_norm[l][None], hg_norm[l][None], md[5], nw[3],
                   w_gates, w_branch[l].astype(BF16), w_out[l].astype(BF16), out_rows, n_seq, nb)
        h = _ffn(h, md[6:9], nw[4:6], wi[1], wo[1], out_rows, n_seq, nb)
    return h.reshape(nb, n_seq, d)
```

```python
import functools

import jax
import jax.numpy as jnp
from jax import lax
from jax.experimental import pallas as pl
from jax.experimental.pallas import tpu as pltpu

F32 = jnp.float32
BF16 = jnp.bfloat16
EPS = 1e-6
LB_FLOOR = 1e-30
GRID_W = 64
ROPE_BASE = 10000.0

D_FF = 2816
HEADS = 4
HD = 128
ROPE = 64
NOPE = 128
Q_RANK = 384
KV_RANK = 256
CHUNK = 64
CPS = 4
CPP = 4
HG_CPP = 4
SUB = 16
BW = HEADS * HD

TM = 512
TM_S = 256
FF_CK = 256
MERGE_SUB = 256
FFN_SUB = 2
VMEM_LIMIT = 48 * 1024 * 1024
FFN_VMEM_LIMIT = 56 * 1024 * 1024
ATTN_LOG2_SCALE = (NOPE + ROPE) ** -0.5 * 1.4426950408889634
HPS = 2
KSUB = 256
HG_MAX_DECAY = 100.0
MAX_JUMP = 64.0

P_GQKV = 0
P_GGATE = 1536
P_HF = 2048
P_HQ = 3072
P_HI = 3584
P_HGATE = 4096
P_GA = 4608
P_GB = 4736
P_COLS = 4864
PROJ_TN = 2432
M_QA, M_KVA, M_KR, M_COLS = 0, 512, 768, 896

NT = (((1,), (1,)), ((), ()))
TN = (((0,), (0,)), ((), ()))


def _dot(a, b):
    return jnp.dot(a, b, preferred_element_type=F32)


def _dot_nt(a, b):
    return lax.dot_general(a, b, NT, preferred_element_type=F32)


def _dot_tn(a, b):
    return lax.dot_general(a, b, TN, preferred_element_type=F32)


def _sigmoid(x):
    return 1.0 / (1.0 + jnp.exp(-x))


def _silu(x):
    return x * _sigmoid(x)


def _softplus(x):
    return jnp.maximum(x, 0.0) + jnp.log(1.0 + jnp.exp(-jnp.abs(x)))


def _rms(x, w):
    return x * lax.rsqrt(jnp.mean(x * x, axis=-1, keepdims=True) + EPS) * w


def _resident(shape):
    zeros = (0,) * len(shape)
    return pl.BlockSpec(shape, lambda *_: zeros, pipeline_mode=pl.Buffered(1))


def _params(*sem):
    return pltpu.CompilerParams(dimension_semantics=sem, vmem_limit_bytes=VMEM_LIMIT)


def _ada_kernel(c_ref, w_ref, b_ref, o_ref):
    s = _silu(c_ref[...])
    o_ref[...] = _dot(s.astype(BF16), w_ref[...].astype(BF16)) + b_ref[...]


def _ada(cc, w_ada, b_ada):
    depth, d, nm = w_ada.shape
    tn = 1024
    return pl.pallas_call(
        _ada_kernel,
        grid=(depth, nm // tn),
        in_specs=[pl.BlockSpec((8, d), lambda l, j: (0, 0)),
                  pl.BlockSpec((None, d, tn), lambda l, j: (l, 0, j)),
                  pl.BlockSpec((None, 1, tn), lambda l, j: (l, 0, j))],
        out_specs=pl.BlockSpec((None, 8, tn), lambda l, j: (l, 0, j)),
        out_shape=jax.ShapeDtypeStruct((depth, 8, nm), F32),
        compiler_params=_params("parallel", "parallel"),
        name="ada_mod",
    )(cc, w_ada, b_ada.reshape(depth, 1, nm))


def _ffn_kernel(*refs, emit_u, n_sub, n_lat_steps):
    x_ref, c_ref = (refs[0], None) if n_lat_steps is None else refs[:2]
    refs = refs[1 if n_lat_steps is None else 2:]
    sh_ref, sc_ref, gt_ref, prew_ref, postw_ref, wi_ref, wo_ref = refs[:7]
    if emit_u:
        nw_ref, sh2_ref, sc2_ref, o_ref, u_ref = refs[7:]
    else:
        (o_ref,) = refs[7:]
    n_ck = D_FF // FF_CK
    half = n_ck // 2

    def load(rs):
        if c_ref is None:
            return x_ref[rs, :]
        return jnp.where(pl.program_id(0) >= n_lat_steps, c_ref[rs, :], x_ref[rs, :])

    def prologue(s):
        x = load(slice(s * TM, (s + 1) * TM))
        return (_rms(x, prew_ref[...]) * (1.0 + sc_ref[...]) + sh_ref[...]).astype(BF16)

    def chunks(hn, acc, lo, hi):
        for c in range(lo, hi):
            g = _dot(hn, wi_ref[:, c * FF_CK:(c + 1) * FF_CK])
            u = _dot(hn, wi_ref[:, D_FF + c * FF_CK:D_FF + (c + 1) * FF_CK])
            a = (_silu(g) * u).astype(BF16)
            acc = acc + _dot(a, wo_ref[c * FF_CK:(c + 1) * FF_CK, :])
        return acc

    def epilogue(s, acc):
        rs = slice(s * TM, (s + 1) * TM)
        out = load(rs) + 0.5 * gt_ref[...] * _rms(acc, postw_ref[...])
        o_ref[rs, :] = out
        if emit_u:
            u_ref[rs, :] = (_rms(out, nw_ref[...]) * (1.0 + sc2_ref[...]) + sh2_ref[...]).astype(BF16)

    zero = jnp.zeros((TM, x_ref.shape[1]), F32)
    hn = prologue(0)
    acc = chunks(hn, zero, 0, half)
    for s in range(n_sub):
        hn_next = prologue(s + 1) if s + 1 < n_sub else None
        acc = chunks(hn, acc, half, n_ck)
        if hn_next is not None:
            acc_next = chunks(hn_next, zero, 0, half)
        epilogue(s, acc)
        if hn_next is not None:
            hn, acc = hn_next, acc_next


def _ffn(h, mods, nw, wi, wo, n_rows, n_seq, nb, mixer_mods=None):
    split = isinstance(h, tuple)
    d = h[0].shape[1] if split else h.shape[1]
    emit_u = mixer_mods is not None
    vec = pl.BlockSpec((1, d), lambda i: (0, 0))
    out_shape = [jax.ShapeDtypeStruct((n_rows, d), F32)]
    if emit_u:
        out_shape.append(jax.ShapeDtypeStruct((n_rows, d), BF16))

    tm = FFN_SUB * TM
    row = lambda i: (i, 0)
    mspec = pl.BlockSpec((None, 1, d), lambda i: (jnp.minimum((i * tm) // n_seq, nb), 0, 0))
    n_lat_steps = None
    if split:
        n_lat_steps = h[0].shape[0] // tm
        assert h[0].shape[0] % tm == 0 and h[1].shape[0] <= tm
        x_specs = [pl.BlockSpec((tm, d), lambda i: (jnp.minimum(i, n_lat_steps - 1), 0)),
                   pl.BlockSpec((tm, d), lambda i: (0, 0))]
        x_args = list(h)
    else:
        x_specs, x_args = [pl.BlockSpec((tm, d), row)], [h]
    in_specs = x_specs + [mspec, mspec, mspec, vec, vec, _resident(wi.shape), _resident(wo.shape)]
    args = x_args + [mods[0], mods[1], mods[2], nw[0], nw[1], wi, wo]
    if emit_u:
        in_specs += [vec, mspec, mspec]
        args += list(mixer_mods)
    outs = pl.pallas_call(
        functools.partial(_ffn_kernel, emit_u=emit_u, n_sub=FFN_SUB, n_lat_steps=n_lat_steps),
        grid=(pl.cdiv(n_rows, tm),),
        in_specs=in_specs,
        out_specs=tuple(pl.BlockSpec((tm, d), row) for _ in out_shape),
        out_shape=tuple(out_shape),
        compiler_params=pltpu.CompilerParams(dimension_semantics=("parallel",),
                                             vmem_limit_bytes=FFN_VMEM_LIMIT),
        name="ffn_sublayer",
    )(*args)
    return outs if emit_u else outs[0]


def _proj_kernel(x_ref, w_ref, o_ref):
    o_ref[...] = _dot(x_ref[...], w_ref[pl.program_id(1)])


def _proj(u, w):
    t, d = u.shape
    nj, _, tn = w.shape
    tm = 2 * TM
    return pl.pallas_call(
        _proj_kernel,
        grid=(pl.cdiv(t, tm), nj),
        in_specs=[pl.BlockSpec((tm, d), lambda i, j: (i, 0)), _resident(w.shape)],
        out_specs=pl.BlockSpec((tm, tn), lambda i, j: (i, j)),
        out_shape=jax.ShapeDtypeStruct((t, nj * tn), F32),
        compiler_params=_params("parallel", "arbitrary"),
        name="in_proj",
    )(u, w)


def _mla_prep_kernel(u_ref, cs_ref, qn_ref, kvn_ref, win_ref, wq_ref, wkv_ref, q_ref, k_ref, v_ref):
    cs = cs_ref[...]
    low = _dot(u_ref[...], win_ref[...])
    qa = _rms(low[:, M_QA:M_QA + Q_RANK], qn_ref[...]).astype(BF16)
    kva = _rms(low[:, M_KVA:M_KVA + KV_RANK], kvn_ref[...]).astype(BF16)
    q2 = _dot(qa, wq_ref[...])
    kv2 = _dot(kva, wkv_ref[...])
    kr = low[:, M_KR:M_KR + 2 * ROPE] * cs
    kr = (kr + pltpu.roll(kr, ROPE, 1))[:, :ROPE].astype(BF16)
    q2 = q2 * ATTN_LOG2_SCALE
    for h in range(HEADS):
        q_ref[h, :, :NOPE] = q2[:, h * NOPE:(h + 1) * NOPE].astype(BF16)
        qr = q2[:, BW + h * 128:BW + (h + 1) * 128] * cs
        q_ref[h, :, NOPE:] = (qr + pltpu.roll(qr, ROPE, 1))[:, :ROPE].astype(BF16)
        k_ref[h, :, :NOPE] = kv2[:, h * NOPE:(h + 1) * NOPE].astype(BF16)
        k_ref[h, :, NOPE:] = kr
        v_ref[h] = kv2[:, BW + h * HD:BW + (h + 1) * HD].astype(BF16)


def _mla_prep(u, cs, qn, kvn, w_low, wq, wkv, nb, n_seq, n_ctx):
    t, d = u.shape
    tm = TM_S
    lat_tiles = nb * n_seq // tm
    per_lat = n_seq // tm
    per_ctx = n_ctx // tm

    def omap(i):
        j = i - lat_tiles
        b = jnp.where(i < lat_tiles, i // per_lat, j // per_ctx)
        blk = jnp.where(i < lat_tiles, i % per_lat, per_lat + j % per_ctx)
        return (b, 0, blk, 0)

    tk = n_seq + n_ctx
    return pl.pallas_call(
        _mla_prep_kernel,
        grid=(t // tm,),
        in_specs=[pl.BlockSpec((tm, d), lambda i: (i, 0)),
                  pl.BlockSpec((tm, 128), lambda i: (i, 0)),
                  pl.BlockSpec((1, Q_RANK), lambda i: (0, 0)),
                  pl.BlockSpec((1, KV_RANK), lambda i: (0, 0)),
                  _resident(w_low.shape), _resident(wq.shape), _resident(wkv.shape)],
        out_specs=(pl.BlockSpec((None, HEADS, tm, NOPE + ROPE), omap),
                   pl.BlockSpec((None, HEADS, tm, NOPE + ROPE), omap),
                   pl.BlockSpec((None, HEADS, tm, HD), omap)),
        out_shape=(jax.ShapeDtypeStruct((nb, HEADS, tk, NOPE + ROPE), BF16),
                   jax.ShapeDtypeStruct((nb, HEADS, tk, NOPE + ROPE), BF16),
                   jax.ShapeDtypeStruct((nb, HEADS, tk, HD), BF16)),
        compiler_params=_params("parallel"),
        name="mla_prep",
    )(u, cs, qn, kvn, w_low, wq, wkv)


def _attn_kernel(q_ref, k_ref, v_ref, *rest, first, kb, nkb):
    o_ref = rest[-1]
    nh, tq = q_ref.shape[0], q_ref.shape[1]
    qs = [q_ref[h] for h in range(nh)]

    def safe_update(h, sl, carry):
        m, l, acc = carry
        t = _dot_nt(qs[h], k_ref[h, sl, :])
        m_new = jnp.maximum(m, jnp.max(t, axis=-1, keepdims=True))
        alpha = jnp.exp2(m - m_new)
        pr = jnp.exp2(t - m_new)
        l = alpha * l + jnp.sum(pr, axis=-1, keepdims=True)
        return m_new, l, alpha * acc + _dot(pr.astype(BF16), v_ref[h, sl, :])

    def fast_update(j, carry):
        st = [list(c) for c in carry]
        bm = [None] * nh
        items = [(pl.ds(pl.multiple_of(j * kb + i * KSUB, KSUB), KSUB), h)
                 for i in range(kb // KSUB) for h in range(nh)]
        score = lambda it: _dot_nt(qs[it[1]], k_ref[it[1], it[0], :])
        t_next = score(items[0])
        for n, (sl, h) in enumerate(items):
            t = t_next
            if n + 1 < len(items):
                t_next = score(items[n + 1])
            pr = jnp.exp2(t - st[h][0])
            tm = jnp.max(t, axis=-1, keepdims=True)
            bm[h] = tm if bm[h] is None else jnp.maximum(bm[h], tm)
            st[h][1] = st[h][1] + jnp.sum(pr, axis=-1, keepdims=True)
            st[h][2] = st[h][2] + _dot(pr.astype(BF16), v_ref[h, sl, :])
        out = []
        for h in range(nh):
            m, l, acc, jump = st[h]
            m_new = jnp.maximum(m, bm[h])
            alpha = jnp.exp2(m - m_new)
            out.append((m_new, l * alpha, acc * alpha, jnp.maximum(jump, bm[h] - m)))
        return tuple(out)

    init = (jnp.full((tq, 1), -1e30, F32), jnp.zeros((tq, 1), F32), jnp.zeros((tq, HD), F32))
    fsl = slice(first[0], first[0] + first[1])
    start = [safe_update(h, fsl, init) for h in range(nh)]
    if nkb == 0:
        for h in range(nh):
            o_ref[:, h * HD:(h + 1) * HD] = start[h][2] / start[h][1]
        return
    done = lax.fori_loop(0, nkb, fast_update, tuple(s + (jnp.zeros((tq, 1), F32),) for s in start),
                         unroll=_pick_block(nkb, (4, 2, 1)))
    for h in range(nh):
        o_ref[:, h * HD:(h + 1) * HD] = done[h][2] / done[h][1]

    for h in range(nh):
        @pl.when(jnp.max(done[h][3]) > MAX_JUMP)
        def _(h=h):
            step = lambda j, carry: safe_update(h, pl.ds(pl.multiple_of(j * kb, kb), kb), carry)
            _, l2, acc2 = lax.fori_loop(0, nkb, step, start[h])
            o_ref[:, h * HD:(h + 1) * HD] = acc2 / l2


def _pick_block(n, cands):
    for c in cands:
        if n % c == 0:
            return c
    raise ValueError(f"no block size for {n}")


def _attention(q, k, v, o_prev, nb, n_seq, n_ctx, rows, ctx_only):
    if ctx_only:
        tq, tk = n_ctx, n_ctx
        koff = n_seq // n_ctx
        grid = (nb, HEADS // HPS, 1)
        qmap = lambda b, h, i: (b, h, koff, 0)
        kmap = lambda b, h, i: (b, h, koff, 0)
        omap = lambda b, h, i: (nb * n_seq // n_ctx + b, h)
        first, kb, nkb = (0, n_ctx), n_ctx, 0
    else:
        tq, tk = _pick_block(n_seq, (512, 256)), n_seq + n_ctx
        grid = (nb, HEADS // HPS, n_seq // tq)
        qmap = lambda b, h, i: (b, h, i, 0)
        kmap = lambda b, h, i: (b, h, 0, 0)
        omap = lambda b, h, i: (b * (n_seq // tq) + i, h)
        kb = _pick_block(n_seq, (1024, 512))
        first, nkb = (n_seq, n_ctx), n_seq // kb
    in_specs = [pl.BlockSpec((None, HPS, tq, NOPE + ROPE), qmap),
                pl.BlockSpec((None, HPS, tk, NOPE + ROPE), kmap),
                pl.BlockSpec((None, HPS, tk, HD), kmap)]
    args = [q, k, v]
    aliases = {}
    if ctx_only:
        in_specs.append(pl.BlockSpec(memory_space=pl.ANY))
        args.append(o_prev)
        aliases = {3: 0}
    return pl.pallas_call(
        functools.partial(_attn_kernel, first=first, kb=kb, nkb=nkb),
        grid=grid,
        in_specs=in_specs,
        out_specs=pl.BlockSpec((tq, HPS * HD), omap),
        out_shape=jax.ShapeDtypeStruct((rows, BW), F32),
        input_output_aliases=aliases,
        compiler_params=_params("parallel", "parallel", "parallel"),
        name="mla_attn_ctx" if ctx_only else "mla_attn",
    )(*args)


def _chunk_pos(g, lat_chunks, ncl, ncc):
    is_lat = g < lat_chunks
    pos = jnp.where(is_lat, g % ncl, (g - lat_chunks) % ncc)
    last = jnp.where(is_lat, pos == ncl - 1, pos == ncc - 1)
    return pos == 0, last


def _scan_chunk(b, s, rev, lat_chunks, ncl, ncc):
    c = jnp.where(s < ncc, s, s - ncc)
    if rev:
        c = jnp.where(s < ncc, ncc - 1 - c, ncl - 1 - c)
    return jnp.where(s < ncc, lat_chunks + b * ncc + c, b * ncl + c)


def _tri_masks(n):
    ri = lax.broadcasted_iota(jnp.int32, (n, n), 0)
    ci = lax.broadcasted_iota(jnp.int32, (n, n), 1)
    return ri, ci


def _split(x):
    hi = x.astype(BF16)
    return hi, (x - hi.astype(F32)).astype(BF16)


def _dot3(a, b):
    (ah, al), (bh, bl) = a, b
    return _dot(ah, bh) + (_dot(ah, bl) + _dot(al, bh))


def _dot_tri(tri, x):
    hi, lo = _split(x)
    lo2 = (x - hi.astype(F32) - lo.astype(F32)).astype(BF16)
    return _dot(tri, hi) + (_dot(tri, lo) + _dot(tri, lo2))


def _unit_tri_inv(mats, ri, ci, eye):
    rb, cb = ri >> 3, ci >> 3
    d8 = [jnp.where(rb == cb, a, 0.0) for a in mats]
    d8s = [_split(d) for d in d8]
    x2 = [_split(_dot3(d, d)) for d in d8s]
    ts = [eye - d for d in d8]
    x4 = [_split(_dot3(x, x)) for x in x2]
    ts = [t + _dot3(_split(t), x) for t, x in zip(ts, x2)]
    ts = [t + _dot3(_split(t), x) for t, x in zip(ts, x4)]
    for _ in range(3):
        same = rb == cb
        rb, cb = rb >> 1, cb >> 1
        off = (rb == cb) & jnp.logical_not(same)
        bs = [jnp.where(off, a, 0.0).astype(BF16) for a in mats]
        tb = [t.astype(BF16) for t in ts]
        ys = [_dot(t, b).astype(BF16) for t, b in zip(tb, bs)]
        ts = [t - _dot(y, tl) for t, y, tl in zip(ts, ys, tb)]
    return ts


def _gdn_pre_kernel(x_ref, prev_ref, next_ref, a_ref, b_ref, cw_ref, alog_ref, dtb_ref,
                    u_ref, w_ref, qg_ref, kd_ref, at_ref, dec_ref, *, lat_chunks, ncl, ncc):
    first, last = _chunk_pos(pl.program_id(0), lat_chunks, ncl, ncc)
    nrow = CPP * CHUNK
    x = x_ref[...]
    xp = jnp.where(first, 0.0, prev_ref[7:8, :])
    xn = jnp.where(last, 0.0, next_ref[0:1, :])
    row = lax.broadcasted_iota(jnp.int32, (nrow, 1), 0)
    x_dn = jnp.where(row == 0, xp, pltpu.roll(x, 1, 0))
    x_up = jnp.where(row == nrow - 1, xn, pltpu.roll(x, nrow - 1, 0))
    cw = cw_ref[...]
    s = _silu(x_dn * cw[0:1] + x * cw[1:2] + x_up * cw[2:3])

    lane = lax.broadcasted_iota(jnp.int32, (1, 128), 1)
    g = -jnp.exp(alog_ref[...]) * _softplus(a_ref[...] + dtb_ref[...])
    beta = _sigmoid(b_ref[...])
    ri, ci = _tri_masks(CHUNK)
    eye_b = ri == ci
    eye = eye_b.astype(F32)
    low = (ri >= ci).astype(BF16)
    upp = (ri <= ci).astype(BF16)
    chunk_rows = [slice(c * CHUNK, (c + 1) * CHUNK) for c in range(CPP)]
    gc, g_last = [], []
    for c, rc in enumerate(chunk_rows):
        gcc = jnp.where(lane < HEADS, _dot_tri(low, g[rc]), _dot_tri(upp, g[rc]))
        gc.append(gcc)
        g_last.append(jnp.where(lane < HEADS, gcc[CHUNK - 1:CHUNK], gcc[0:1]))
        dec_ref[c] = jnp.exp(g_last[c])

    qn, kn, vs = [], [], []
    for h in range(HEADS):
        qh = s[:, h * HD:(h + 1) * HD]
        kh = s[:, BW + h * HD:BW + (h + 1) * HD]
        vs.append(s[:, 2 * BW + h * HD:2 * BW + (h + 1) * HD])
        qn.append(qh * lax.rsqrt(jnp.sum(qh * qh, axis=-1, keepdims=True) + EPS) * HD ** -0.5)
        kn.append(kh * lax.rsqrt(jnp.sum(kh * kh, axis=-1, keepdims=True) + EPS))
    qk = [[None] * HEADS for _ in range(CPP)]
    kk = [[None] * HEADS for _ in range(CPP)]
    for c, rc in enumerate(chunk_rows):
        for h in range(HEADS):
            knb = kn[h][rc].astype(BF16)
            qk[c][h] = _dot_nt(qn[h][rc].astype(BF16), knb)
            kk[c][h] = _dot_nt(knb, knb)

    probs = [(c, d, h) for c in range(CPP) for h in range(HEADS) for d in range(2)]
    mats, rhs = [], []
    for c, d, h in probs:
        rc = chunk_rows[c]
        sl = slice(h * HD, (h + 1) * HD)
        dh = d * HEADS + h
        gcc = gc[c][:, dh:dh + 1]
        bt = beta[rc, dh:dh + 1]
        grow = jnp.sum(jnp.where(eye_b, gcc, 0.0), axis=0, keepdims=True)
        incl = (ri >= ci) if d == 0 else (ri <= ci)
        strict = (ri > ci) if d == 0 else (ri < ci)
        dm = jnp.where(incl, jnp.exp(gcc - grow), 0.0)
        eg = jnp.exp(gcc)
        knc = kn[h][rc]
        mats.append(jnp.where(strict, bt * kk[c][h] * dm, 0.0))
        rhs.append(jnp.concatenate([vs[h][rc] * bt, knc * (bt * eg)], axis=1).astype(BF16))
        qg_ref[d, rc, sl] = (qn[h][rc] * eg).astype(BF16)
        kd_ref[d, rc, sl] = (knc * jnp.exp(g_last[c][:, dh:dh + 1] - gcc)).astype(BF16)
        at_ref[d, rc, h * HD:h * HD + CHUNK] = (qk[c][h] * dm).astype(BF16)
        at_ref[d, rc, h * HD + CHUNK:(h + 1) * HD] = jnp.zeros((CHUNK, HD - CHUNK), BF16)

    for (c, d, h), t, r in zip(probs, _unit_tri_inv(mats, ri, ci, eye), rhs):
        sl = slice(h * HD, (h + 1) * HD)
        uw = _dot(t.astype(BF16), r)
        u_ref[d, chunk_rows[c], sl] = uw[:, :HD]
        w_ref[d, chunk_rows[c], sl] = uw[:, HD:].astype(BF16)


def _gdn_pre(p, conv_w, a_log, dt_bias, nb, n_seq, n_ctx):
    t = p.shape[0]
    nrow = CPP * CHUNK
    assert n_seq % nrow == 0 and n_ctx % nrow == 0
    lat_blocks, nbl, nbc = nb * n_seq // nrow, n_seq // nrow, n_ctx // nrow
    qkv_blk = P_GQKV // (3 * BW)
    pad = lambda v: jnp.pad(v.reshape(1, -1), ((0, 0), (0, 128 - v.size)))
    dir_out = lambda dt: jax.ShapeDtypeStruct((2, t, BW), dt)
    dir_spec = pl.BlockSpec((2, nrow, BW), lambda g: (0, g, 0))
    r8 = nrow // 8
    return pl.pallas_call(
        functools.partial(_gdn_pre_kernel, lat_chunks=lat_blocks, ncl=nbl, ncc=nbc),
        grid=(t // nrow,),
        in_specs=[pl.BlockSpec((nrow, 3 * BW), lambda g: (g, qkv_blk)),
                  pl.BlockSpec((8, 3 * BW), lambda g: (jnp.maximum(g * r8 - 1, 0), qkv_blk)),
                  pl.BlockSpec((8, 3 * BW), lambda g: (jnp.minimum(g * r8 + r8, t // 8 - 1), qkv_blk)),
                  pl.BlockSpec((nrow, 128), lambda g: (g, P_GA // 128)),
                  pl.BlockSpec((nrow, 128), lambda g: (g, P_GB // 128)),
                  pl.BlockSpec((3, 3 * BW), lambda g: (0, 0)),
                  pl.BlockSpec((1, 128), lambda g: (0, 0)),
                  pl.BlockSpec((1, 128), lambda g: (0, 0))],
        out_specs=(dir_spec, dir_spec, dir_spec, dir_spec, dir_spec,
                   pl.BlockSpec((CPP, 1, 128), lambda g: (g, 0, 0))),
        out_shape=(dir_out(F32), dir_out(BF16), dir_out(BF16), dir_out(BF16), dir_out(BF16),
                   jax.ShapeDtypeStruct((t // CHUNK, 1, 128), F32)),
        compiler_params=_params("parallel"),
        name="gdn_chunk_prep",
    )(p, p, p, p, p, conv_w, pad(a_log), pad(dt_bias))


def _scan_rows(d, c):
    cc = c if d == 0 else CPS - 1 - c
    return slice(cc * CHUNK, (cc + 1) * CHUNK), cc


def _mixer_scan_kernel(*refs):
    g_ins, h_ins = refs[:12], refs[12:22]
    g_of, g_ob, h_of, h_ob, g_state, h_state = refs[22:]

    @pl.when(pl.program_id(1) == 0)
    def _():
        g_state[...] = jnp.zeros(g_state.shape, F32)
        h_state[...] = jnp.zeros(h_state.shape, F32)

    filler = _hg_scan_body(h_ins, h_of, h_ob, h_state)
    _gdn_scan_body(g_ins, g_of, g_ob, g_state, filler)
    for _ in filler:
        pass


def _gdn_scan_body(ins, of_ref, ob_ref, s_ref, filler):
    def fill(n):
        for _ in range(n):
            next(filler, None)

    chains = [(d, h) for d in range(2) for h in range(HEADS)]
    outs = (of_ref, ob_ref)
    st = [s_ref[d, h] for d, h in chains]
    for c in range(CPS):
        sb = [s.astype(BF16) for s in st]
        vn = []
        for (d, h), s in zip(chains, sb):
            u_ref, w_ref = ins[6 * d], ins[6 * d + 1]
            rows, _ = _scan_rows(d, c)
            sl = slice(h * HD, (h + 1) * HD)
            vn.append((u_ref[rows, sl] - _dot(w_ref[rows, sl], s)).astype(BF16))
        fill(3)
        for i, (d, h) in enumerate(chains):
            qg_ref, kd_ref, at_ref, dec_ref = ins[6 * d + 2:6 * d + 6]
            rows, cc = _scan_rows(d, c)
            sl = slice(h * HD, (h + 1) * HD)
            outs[d][rows, sl] = _dot(qg_ref[rows, sl], sb[i]) + _dot(at_ref[rows, h * HD:h * HD + CHUNK], vn[i])
            dec = dec_ref[cc, 0:1, d * HEADS + h:d * HEADS + h + 1]
            st[i] = st[i] * dec + _dot_tn(kd_ref[rows, sl], vn[i])
        fill(3)
    for (d, h), s in zip(chains, st):
        s_ref[d, h] = s


def _dir_scan(kernel, groups, nb, n_seq, n_ctx, name):
    t = groups[0][0][0].shape[1]
    step = CPS * CHUNK
    assert n_seq % step == 0 and n_ctx % step == 0
    lat_blocks, ncl, ncc = nb * n_seq // step, n_seq // step, n_ctx // step
    maps = [functools.partial(_scan_chunk, rev=bool(d), lat_chunks=lat_blocks, ncl=ncl, ncc=ncc) for d in range(2)]
    in_specs, args = [], []
    for arrays, dec in groups:
        for d, cmap in enumerate(maps):
            for arr in arrays:
                if arr.ndim == 3:
                    in_specs.append(pl.BlockSpec((None, step, BW), lambda b, s, d=d, cmap=cmap: (d, cmap(b, s), 0)))
                else:
                    in_specs.append(pl.BlockSpec((step, BW), lambda b, s, cmap=cmap: (cmap(b, s), 0)))
                args.append(arr)
            if dec.ndim == 3:
                in_specs.append(pl.BlockSpec((CPS, 1, dec.shape[-1]), lambda b, s, cmap=cmap: (cmap(b, s), 0, 0)))
            else:
                in_specs.append(pl.BlockSpec((None, CPS, 1, dec.shape[-1]),
                                             lambda b, s, d=d, cmap=cmap: (d, cmap(b, s), 0, 0)))
            args.append(dec)
    out_specs = tuple(pl.BlockSpec((step, BW), lambda b, s, cmap=cmap: (cmap(b, s), 0))
                      for _ in groups for cmap in maps)
    return pl.pallas_call(
        kernel,
        grid=(nb, ncc + ncl),
        in_specs=in_specs,
        out_specs=out_specs,
        out_shape=tuple(jax.ShapeDtypeStruct((t, BW), F32) for _ in out_specs),
        scratch_shapes=[pltpu.VMEM((2, HEADS, HD, HD), F32) for _ in groups],
        compiler_params=_params("parallel", "arbitrary"),
        name=name,
    )(*args)


def _hg_block_decay(gc, rev):
    out = None
    for b in range(CHUNK // SUB):
        lo, hi = gc[b * SUB:b * SUB + 1], gc[(b + 1) * SUB - 1:(b + 1) * SUB]
        dcy = (hi - lo) if rev else (lo - hi)
        out = dcy if out is None else jnp.maximum(out, dcy)
    return out


def _hg_scores(q, k, gc, rev, exact):
    nblk = CHUNK // SUB
    lane = lax.broadcasted_iota(jnp.int32, (1, CHUNK), 1)
    row8 = lax.broadcasted_iota(jnp.int32, (8, 1), 0)
    blk = lambda x, b: x[b * SUB:(b + 1) * SUB]
    ref_row = lambda b: b * SUB + (0 if rev else SUB - 1)
    refs = [gc[ref_row(b):ref_row(b) + 1] for b in range(nblk)]
    kt = jnp.concatenate([blk(k, b) * jnp.exp2(refs[b] - blk(gc, b)) for b in range(nblk)], axis=0)
    srcs = list(range(nblk - 1, 0, -1)) if rev else list(range(nblk - 1))
    rows_of = (lambda j: slice(0, j * SUB)) if rev else (lambda j: slice((j + 1) * SUB, CHUNK))
    qt = jnp.concatenate([q[rows_of(j)] * jnp.exp2(gc[rows_of(j)] - refs[j]) for j in srcs], axis=0)
    seg_off, off = {}, 0
    for j in srcs:
        seg_off[j] = off
        off += (j if rev else nblk - 1 - j) * SUB
    ktb, qtb = kt.astype(BF16), qt.astype(BF16)
    cross = [_dot_nt(qtb[:, h * HD:(h + 1) * HD], ktb[:, h * HD:(h + 1) * HD]) for h in range(HEADS)]

    def assemble(own):
        out = []
        for h in range(HEADS):
            groups = []
            for g8 in range(CHUNK // 8):
                r0 = 8 * g8
                bi = r0 // SUB
                val = own[h][g8]
                for j in srcs:
                    if (j < bi and not rev) or (j > bi and rev):
                        base = seg_off[j] + (r0 if rev else r0 - (j + 1) * SUB)
                        val = jnp.where((lane >= j * SUB) & (lane < (j + 1) * SUB), cross[h][base:base + 8], val)
                groups.append(val)
            out.append(jnp.concatenate(groups, axis=0))
        return tuple(out)

    def own_exact():
        acc = [[jnp.zeros((8, CHUNK), F32) for _ in range(CHUNK // 8)] for _ in range(HEADS)]
        for j in range(CHUNK):
            b, jj = j // SUB, j % SUB
            gj, kj = gc[j:j + 1], k[j:j + 1]
            for rg in range(SUB // 8):
                lo, hi = 8 * rg, 8 * rg + 7
                if (hi < jj and not rev) or (lo > jj and rev):
                    continue
                r0 = b * SUB + lo
                w = jnp.exp2(gc[r0:r0 + 8] - gj)
                if not ((lo >= jj and not rev) or (hi <= jj and rev)):
                    w = jnp.where((row8 + lo >= jj) if not rev else (row8 + lo <= jj), w, 0.0)
                tt = q[r0:r0 + 8] * w * kj
                g8 = r0 // 8
                for h in range(HEADS):
                    col = jnp.sum(tt[:, h * HD:(h + 1) * HD], axis=-1, keepdims=True)
                    acc[h][g8] = jnp.where(lane == j, col, acc[h][g8])
        return assemble(acc)

    first_row = lambda b: b * SUB + (SUB - 1 if rev else 0)
    starts = [gc[first_row(b):first_row(b) + 1] for b in range(nblk)]

    def own_matmul():
        rfull = jnp.concatenate([jnp.broadcast_to(starts[b], (SUB, q.shape[1])) for b in range(nblk)], axis=0)
        qd = (q * jnp.exp2(gc - rfull)).astype(BF16)
        kd = (k * jnp.exp2(rfull - gc)).astype(BF16)
        ri, ci = _tri_masks(CHUNK)
        sh = SUB.bit_length() - 1
        keep = ((ri >> sh) == (ci >> sh)) & ((ri <= ci) if rev else (ri >= ci))
        own = []
        for h in range(HEADS):
            sd = jnp.where(keep, _dot_nt(qd[:, h * HD:(h + 1) * HD], kd[:, h * HD:(h + 1) * HD]), 0.0)
            own.append([sd[8 * g8:8 * g8 + 8] for g8 in range(CHUNK // 8)])
        return assemble(own)

    return own_exact() if exact else own_matmul()


def _hg_pre_kernel(q_ref, f_ref, i_ref, lbl_ref, oi_ref, qg_ref, kd_ref, vb_ref, dec_ref, *, layer):
    lbl = lbl_ref[...]
    e = jnp.exp(lbl - jnp.max(lbl, axis=0, keepdims=True))
    sm = e / jnp.sum(e, axis=0, keepdims=True)
    lb_all = sm[0]
    for l in range(1, layer + 1):
        lb_all = lb_all + sm[l]
    lb_all = lb_all - sm[0]
    q_all = q_ref[...] * HD ** -0.5
    v_all = i_ref[...].astype(BF16)
    vb_ref[...] = v_all
    ri, ci = _tri_masks(CHUNK)
    work, decay = [], None
    for d in range(2):
        lb = lb_all[d:d + 1]
        f = f_ref[:, d * BW:(d + 1) * BW]
        e = jnp.exp(-jnp.abs(f))
        r = 1.0 / (1.0 + e)
        er = e * r
        pos = f >= 0.0
        sig, sig_neg = jnp.where(pos, r, er), jnp.where(pos, er, r)
        log_f = jnp.log2(jnp.maximum(lb, LB_FLOOR) + (1.0 - lb) * sig)
        k_all = (1.0 - lb) * sig_neg
        tri = ((ri >= ci) if d == 0 else (ri <= ci)).astype(BF16)
        for c in range(HG_CPP):
            rc = slice(c * CHUNK, (c + 1) * CHUNK)
            q, k, v = q_all[rc], k_all[rc], v_all[rc]
            gc = _dot_tri(tri, log_f[rc])
            g_last = gc[CHUNK - 1:CHUNK] if d == 0 else gc[0:1]
            dec_ref[d, c] = jnp.exp2(g_last)
            qg_ref[d, rc] = (q * jnp.exp2(gc)).astype(BF16)
            kd_ref[d, rc] = (k * jnp.exp2(g_last - gc)).astype(BF16)
            work.append((d, rc, q, k, v, gc))
            dcy = _hg_block_decay(gc, rev=bool(d))
            decay = dcy if decay is None else jnp.maximum(decay, dcy)

    def intra(exact):
        for d, rc, q, k, v, gc in work:
            scores = _hg_scores(q, k, gc, rev=bool(d), exact=exact)
            for h in range(HEADS):
                sl = slice(h * HD, (h + 1) * HD)
                oi_ref[d, rc, sl] = _dot(scores[h].astype(BF16), v[:, sl])

    safe = jnp.max(decay) <= HG_MAX_DECAY
    pl.when(safe)(lambda: intra(False))
    pl.when(jnp.logical_not(safe))(lambda: intra(True))


def _hg_pre(p, lb_logits, layer):
    t = p.shape[0]
    nc = t // CHUNK
    nrow = HG_CPP * CHUNK
    assert t % nrow == 0
    dir_out = lambda dt: jax.ShapeDtypeStruct((2, t, BW), dt)
    dir_spec = pl.BlockSpec((2, nrow, BW), lambda g: (0, g, 0))
    return pl.pallas_call(
        functools.partial(_hg_pre_kernel, layer=layer),
        grid=(t // nrow,),
        in_specs=[pl.BlockSpec((nrow, BW), lambda g: (g, P_HQ // BW)),
                  pl.BlockSpec((nrow, 2 * BW), lambda g: (g, P_HF // (2 * BW))),
                  pl.BlockSpec((nrow, BW), lambda g: (g, P_HI // BW)),
                  pl.BlockSpec(lb_logits.shape, lambda g: (0, 0, 0))],
        out_specs=(dir_spec, dir_spec, dir_spec,
                   pl.BlockSpec((nrow, BW), lambda g: (g, 0)),
                   pl.BlockSpec((2, HG_CPP, 1, BW), lambda g: (0, g, 0, 0))),
        out_shape=(dir_out(F32), dir_out(BF16), dir_out(BF16),
                   jax.ShapeDtypeStruct((t, BW), BF16),
                   jax.ShapeDtypeStruct((2, nc, 1, BW), F32)),
        compiler_params=_params("parallel"),
        name="hgrn2_chunk_prep",
    )(p, p, p, lb_logits)


def _hg_scan_body(ins, of_ref, ob_ref, s_ref):
    chains = [(d, h) for d in range(2) for h in range(HEADS)]
    outs = (of_ref, ob_ref)
    incs = []
    for d, h in chains:
        kd_ref, v_ref = ins[5 * d + 2], ins[5 * d + 3]
        sl = slice(h * HD, (h + 1) * HD)
        incs.append([_dot_tn(v_ref[_scan_rows(d, c)[0], sl], kd_ref[_scan_rows(d, c)[0], sl])
                     for c in range(CPS)])
        yield
    states = []
    for i, (d, h) in enumerate(chains):
        dec_ref = ins[5 * d + 4]
        sl = slice(h * HD, (h + 1) * HD)
        st = s_ref[d, h]
        seq = []
        for c in range(CPS):
            seq.append(st.astype(BF16))
            st = st * dec_ref[_scan_rows(d, c)[1], :, sl] + incs[i][c]
        s_ref[d, h] = st
        states.append(seq)
        yield
    for i, (d, h) in enumerate(chains):
        oi_ref, qg_ref = ins[5 * d], ins[5 * d + 1]
        sl = slice(h * HD, (h + 1) * HD)
        for c in range(CPS):
            rows, _ = _scan_rows(d, c)
            outs[d][rows, sl] = oi_ref[rows, sl] + _dot_nt(qg_ref[rows, sl], states[i][c])
        yield


def _merge_kernel(h_ref, gof_ref, gob_ref, gg_ref, yb_ref, hof_ref, hob_ref, hgg_ref, u_ref,
                  gnorm_ref, hnorm_ref, gt_ref, nw_ref, wg_ref, wb_ref, wo_ref, o_ref):
    d = h_ref.shape[1]
    n_sub = h_ref.shape[0] // MERGE_SUB

    def readout(rs, of_ref, ob_ref, gate_ref, norm_ref):
        o = of_ref[rs, :] + ob_ref[rs, :]
        gate = gate_ref[rs, :]
        parts = []
        for h in range(HEADS):
            sl = slice(h * HD, (h + 1) * HD)
            parts.append(_rms(o[:, sl], norm_ref[...]) * _silu(gate[:, sl]))
        return jnp.concatenate(parts, axis=1).astype(BF16)

    def branches(rs):
        return (readout(rs, gof_ref, gob_ref, gg_ref, gnorm_ref),
                yb_ref[rs, :].astype(BF16),
                readout(rs, hof_ref, hob_ref, hgg_ref, hnorm_ref))

    def mix(rs, ys):
        u = u_ref[rs, :]
        m = None
        for j in range(3):
            gate_logits = _dot(u, wg_ref[:, j * d:(j + 1) * d])
            term = _sigmoid(gate_logits) * _dot(ys[j], wb_ref[j])
            m = term if m is None else m + term
        return m.astype(BF16)

    def finish(rs, m):
        y = _dot(m, wo_ref[...])
        o_ref[rs, :] = h_ref[rs, :] + gt_ref[...] * _rms(y, nw_ref[...])

    rows = [slice(s * MERGE_SUB, (s + 1) * MERGE_SUB) for s in range(n_sub)]
    ys = branches(rows[0])
    for s in range(n_sub):
        m = mix(rows[s], ys)
        if s + 1 < n_sub:
            ys = branches(rows[s + 1])
        finish(rows[s], m)


def _merge(h, g_of, g_ob, yb, h_of, h_ob, p, u, gnorm, hnorm, gate_mod, nw, wg, wb, wo, n_rows, n_seq, nb):
    d = h.shape[1]
    tm = TM
    row = lambda i: (i, 0)
    bw_spec = pl.BlockSpec((tm, BW), row)
    pcol = lambda off, width: pl.BlockSpec((tm, width), lambda i: (i, off // width))
    vec = lambda n: pl.BlockSpec((1, n), lambda i: (0, 0))
    return pl.pallas_call(
        _merge_kernel,
        grid=(n_rows // tm,),
        in_specs=[pl.BlockSpec((tm, d), row), bw_spec, bw_spec, pcol(P_GGATE, BW), bw_spec,
                  bw_spec, bw_spec, pcol(P_HGATE, BW), pl.BlockSpec((tm, d), row),
                  vec(HD), vec(HD),
                  pl.BlockSpec((None, 1, d), lambda i: (jnp.minimum((i * tm) // n_seq, nb), 0, 0)),
                  vec(d), _resident(wg.shape), _resident(wb.shape), _resident(wo.shape)],
        out_specs=pl.BlockSpec((tm, d), row),
        out_shape=jax.ShapeDtypeStruct((n_rows, d), F32),
        compiler_params=_params("parallel"),
        name="branch_merge",
    )(h, g_of, g_ob, p, yb, h_of, h_ob, p, u, gnorm, hnorm, gate_mod, nw, wg, wb, wo)


def _pack_w_in(w):
    d = w.shape[0]
    w = w.astype(BF16)
    sizes = (BW, BW, BW, BW, 2 * HEADS, 2 * HEADS, Q_RANK, KV_RANK, ROPE, BW, 2 * BW, BW, BW, 3 * d)
    offs = [0]
    for s in sizes:
        offs.append(offs[-1] + s)
    part = lambda i: w[:, offs[i]:offs[i + 1]]
    gq, gk, gv, ggate, ga, gb, qa, kva, kr, hq, hf, hi, hgate, gates = (part(i) for i in range(14))
    z = lambda n: jnp.zeros((d, n), w.dtype)
    swap = jnp.concatenate([kr[:, 16:32], kr[:, 0:16], kr[:, 48:64], kr[:, 32:48]], axis=1)
    rec = jnp.concatenate([gq, gk, gv, ggate, hf, hq, hi, hgate, ga, z(120), gb, z(120)], axis=1).astype(BF16)
    mla = jnp.concatenate([qa, z(128), kva, kr, swap], axis=1).astype(BF16)
    assert rec.shape[1] == P_COLS and mla.shape[1] == M_COLS
    return rec.reshape(d, P_COLS // PROJ_TN, PROJ_TN).transpose(1, 0, 2), mla, gates.astype(BF16)


def _pack_wq(w):
    w = w.reshape(Q_RANK, HEADS, NOPE + ROPE)
    nope = w[:, :, :NOPE].reshape(Q_RANK, HEADS * NOPE)
    r = w[:, :, NOPE:]
    sw = jnp.concatenate([r[..., 16:32], r[..., 0:16], r[..., 48:64], r[..., 32:48]], axis=-1)
    rope = jnp.concatenate([r, sw], axis=-1).reshape(Q_RANK, HEADS * 2 * ROPE)
    return jnp.concatenate([nope, rope], axis=1).astype(BF16)


def _pack_wkv(w):
    w = w.reshape(KV_RANK, HEADS, NOPE + HD)
    return jnp.concatenate([w[:, :, :NOPE].reshape(KV_RANK, -1), w[:, :, NOPE:].reshape(KV_RANK, -1)],
                           axis=1).astype(BF16)


def _rope_table(nb, n_seq, n_ctx):
    nf = ROPE // 4
    rows = n_seq // GRID_W
    rpos = jnp.repeat(jnp.arange(rows, dtype=F32), GRID_W)
    cpos = jnp.tile(jnp.arange(GRID_W, dtype=F32), rows)
    inv = ROPE_BASE ** (-jnp.arange(nf, dtype=F32) / nf)
    ar, ac = rpos[:, None] * inv, cpos[:, None] * inv
    cos = jnp.concatenate([jnp.cos(ar), jnp.cos(ar), jnp.cos(ac), jnp.cos(ac)], axis=1)
    sin = jnp.concatenate([-jnp.sin(ar), jnp.sin(ar), -jnp.sin(ac), jnp.sin(ac)], axis=1)
    lat = jnp.tile(jnp.concatenate([cos, sin], axis=1), (nb, 1))
    ctx = jnp.concatenate([jnp.ones((nb * n_ctx, ROPE), F32), jnp.zeros((nb * n_ctx, ROPE), F32)], axis=1)
    return jnp.concatenate([lat, ctx], axis=0)


def kernel(x, c, ctx, c_ctx, w_ada, b_ada, norm_w, ffn_w_in, ffn_w_out, w_in, gdn_conv, gdn_a_log, gdn_dt_bias, gdn_norm, mla_q_norm, mla_kv_norm, mla_w_q_b, mla_w_kv_b, hg_lb_logits, hg_norm, w_branch, w_out):
    nb, n_seq, d = x.shape
    n_ctx = ctx.shape[1]
    depth = w_ada.shape[0]
    lat_rows, rows = nb * n_seq, nb * (n_seq + n_ctx)
    assert nb + 1 <= 8 and n_seq % (FFN_SUB * TM) == 0 and n_ctx % TM_S == 0 and (nb * n_ctx) % TM == 0
    assert n_seq % n_ctx == 0 and lat_rows % n_ctx == 0

    cc = jnp.concatenate([c, c_ctx[None], jnp.zeros((8 - nb - 1, d), F32)], axis=0)
    mods = _ada(cc, w_ada, b_ada)[:, :nb + 1].reshape(depth, nb + 1, 9, 1, d).transpose(0, 2, 1, 3, 4)
    cs = _rope_table(nb, n_seq, n_ctx)
    h = (x.reshape(lat_rows, d), ctx.reshape(nb * n_ctx, d))

    for l in range(depth):
        last = l == depth - 1
        md, nw = mods[l], norm_w[l][:, None, :]
        wi = [ffn_w_in[l, j].astype(BF16) for j in range(2)]
        wo = [ffn_w_out[l, j].astype(BF16) for j in range(2)]

        h, u = _ffn(h, md[0:3], nw[0:2], wi[0], wo[0], rows, n_seq, nb, mixer_mods=(nw[2], md[3], md[4]))
        w_rec, w_low, w_gates = _pack_w_in(w_in[l])
        p = _proj(u, w_rec)

        g_ops = _gdn_pre(p, gdn_conv[l], gdn_a_log[l], gdn_dt_bias[l], nb, n_seq, n_ctx)
        hoi, hqg, hkd, hvb, hdec = _hg_pre(p, hg_lb_logits, l)
        g_of, g_ob, h_of, h_ob = _dir_scan(_mixer_scan_kernel,
                                           [(g_ops[:5], g_ops[5]), ((hoi, hqg, hkd, hvb), hdec)],
                                           nb, n_seq, n_ctx, "mixer_scan")

        q, k, v = _mla_prep(u, cs, mla_q_norm[l][None], mla_kv_norm[l][None], w_low,
                            _pack_wq(mla_w_q_b[l]), _pack_wkv(mla_w_kv_b[l]), nb, n_seq, n_ctx)
        yb = _attention(q, k, v, None, nb, n_seq, n_ctx, rows, ctx_only=False)
        if not last:
            yb = _attention(q, k, v, yb, nb, n_seq, n_ctx, rows, ctx_only=True)

        out_rows = lat_rows if last else rows
        h = _merge(h, g_of, g_ob, yb, h_of, h_ob, p, u, gdn_norm[l][None], hg_norm[l][None], md[5], nw[3],
                   w_gates, w_branch[l].astype(BF16), w_out[l].astype(BF16), out_rows, n_seq, nb)
        h = _ffn(h, md[6:9], nw[4:6], wi[1], wo[1], out_rows, n_seq, nb)
    return h.reshape(nb, n_seq, d)
```

```python
import functools

import jax
import jax.numpy as jnp
from jax import lax
from jax.experimental import pallas as pl
from jax.experimental.pallas import tpu as pltpu

F32 = jnp.float32
BF16 = jnp.bfloat16
EPS = 1e-6
LB_FLOOR = 1e-30
GRID_W = 64
ROPE_BASE = 10000.0

D_FF = 2816
HEADS = 4
HD = 128
ROPE = 64
NOPE = 128
Q_RANK = 384
KV_RANK = 256
CHUNK = 64
CPS = 4
CPP = 4
HG_CPP = 4
SUB = 16
BW = HEADS * HD

TM = 512
TM_S = 256
FF_CK = 256
MERGE_SUB = 256
FFN_SUB = 2
VMEM_LIMIT = 48 * 1024 * 1024
FFN_VMEM_LIMIT = 56 * 1024 * 1024
ATTN_LOG2_SCALE = (NOPE + ROPE) ** -0.5 * 1.4426950408889634
HPS = 2
KSUB = 256
HG_MAX_DECAY = 100.0
MAX_JUMP = 64.0

P_GQKV = 0
P_GGATE = 1536
P_HF = 2048
P_HQ = 3072
P_HI = 3584
P_HGATE = 4096
P_GA = 4608
P_GB = 4736
P_COLS = 4864
PROJ_TN = 2432
M_QA, M_KVA, M_KR, M_COLS = 0, 512, 768, 896

NT = (((1,), (1,)), ((), ()))
TN = (((0,), (0,)), ((), ()))


def _dot(a, b):
    return jnp.dot(a, b, preferred_element_type=F32)


def _dot_nt(a, b):
    return lax.dot_general(a, b, NT, preferred_element_type=F32)


def _dot_tn(a, b):
    return lax.dot_general(a, b, TN, preferred_element_type=F32)


def _sigmoid(x):
    return 1.0 / (1.0 + jnp.exp(-x))


def _silu(x):
    return x * _sigmoid(x)


def _softplus(x):
    return jnp.maximum(x, 0.0) + jnp.log(1.0 + jnp.exp(-jnp.abs(x)))


def _rms(x, w):
    return x * lax.rsqrt(jnp.mean(x * x, axis=-1, keepdims=True) + EPS) * w


def _resident(shape):
    zeros = (0,) * len(shape)
    return pl.BlockSpec(shape, lambda *_: zeros, pipeline_mode=pl.Buffered(1))


def _params(*sem):
    return pltpu.CompilerParams(dimension_semantics=sem, vmem_limit_bytes=VMEM_LIMIT)


def _ada_kernel(c_ref, w_ref, b_ref, o_ref):
    s = _silu(c_ref[...])
    o_ref[...] = _dot(s.astype(BF16), w_ref[...].astype(BF16)) + b_ref[...]


def _ada(cc, w_ada, b_ada):
    depth, d, nm = w_ada.shape
    tn = 1024
    return pl.pallas_call(
        _ada_kernel,
        grid=(depth, nm // tn),
        in_specs=[pl.BlockSpec((8, d), lambda l, j: (0, 0)),
                  pl.BlockSpec((None, d, tn), lambda l, j: (l, 0, j)),
                  pl.BlockSpec((None, 1, tn), lambda l, j: (l, 0, j))],
        out_specs=pl.BlockSpec((None, 8, tn), lambda l, j: (l, 0, j)),
        out_shape=jax.ShapeDtypeStruct((depth, 8, nm), F32),
        compiler_params=_params("parallel", "parallel"),
        name="ada_mod",
    )(cc, w_ada, b_ada.reshape(depth, 1, nm))


def _ffn_kernel(*refs, emit_u, n_sub, n_lat_steps):
    x_ref, c_ref = (refs[0], None) if n_lat_steps is None else refs[:2]
    refs = refs[1 if n_lat_steps is None else 2:]
    sh_ref, sc_ref, gt_ref, prew_ref, postw_ref, wi_ref, wo_ref = refs[:7]
    if emit_u:
        nw_ref, sh2_ref, sc2_ref, o_ref, u_ref = refs[7:]
    else:
        (o_ref,) = refs[7:]
    n_ck = D_FF // FF_CK
    half = n_ck // 2

    def load(rs):
        if c_ref is None:
            return x_ref[rs, :]
        return jnp.where(pl.program_id(0) >= n_lat_steps, c_ref[rs, :], x_ref[rs, :])

    def prologue(s):
        x = load(slice(s * TM, (s + 1) * TM))
        return (_rms(x, prew_ref[...]) * (1.0 + sc_ref[...]) + sh_ref[...]).astype(BF16)

    def chunks(hn, acc, lo, hi):
        for c in range(lo, hi):
            g = _dot(hn, wi_ref[:, c * FF_CK:(c + 1) * FF_CK])
            u = _dot(hn, wi_ref[:, D_FF + c * FF_CK:D_FF + (c + 1) * FF_CK])
            a = (_silu(g) * u).astype(BF16)
            acc = acc + _dot(a, wo_ref[c * FF_CK:(c + 1) * FF_CK, :])
        return acc

    def epilogue(s, acc):
        rs = slice(s * TM, (s + 1) * TM)
        out = load(rs) + 0.5 * gt_ref[...] * _rms(acc, postw_ref[...])
        o_ref[rs, :] = out
        if emit_u:
            u_ref[rs, :] = (_rms(out, nw_ref[...]) * (1.0 + sc2_ref[...]) + sh2_ref[...]).astype(BF16)

    zero = jnp.zeros((TM, x_ref.shape[1]), F32)
    hn = prologue(0)
    acc = chunks(hn, zero, 0, half)
    for s in range(n_sub):
        hn_next = prologue(s + 1) if s + 1 < n_sub else None
        acc = chunks(hn, acc, half, n_ck)
        if hn_next is not None:
            acc_next = chunks(hn_next, zero, 0, half)
        epilogue(s, acc)
        if hn_next is not None:
            hn, acc = hn_next, acc_next


def _ffn(h, mods, nw, wi, wo, n_rows, n_seq, nb, mixer_mods=None):
    split = isinstance(h, tuple)
    d = h[0].shape[1] if split else h.shape[1]
    emit_u = mixer_mods is not None
    vec = pl.BlockSpec((1, d), lambda i: (0, 0))
    out_shape = [jax.ShapeDtypeStruct((n_rows, d), F32)]
    if emit_u:
        out_shape.append(jax.ShapeDtypeStruct((n_rows, d), BF16))

    tm = FFN_SUB * TM
    row = lambda i: (i, 0)
    mspec = pl.BlockSpec((None, 1, d), lambda i: (jnp.minimum((i * tm) // n_seq, nb), 0, 0))
    n_lat_steps = None
    if split:
        n_lat_steps = h[0].shape[0] // tm
        assert h[0].shape[0] % tm == 0 and h[1].shape[0] <= tm
        x_specs = [pl.BlockSpec((tm, d), lambda i: (jnp.minimum(i, n_lat_steps - 1), 0)),
                   pl.BlockSpec((tm, d), lambda i: (0, 0))]
        x_args = list(h)
    else:
        x_specs, x_args = [pl.BlockSpec((tm, d), row)], [h]
    in_specs = x_specs + [mspec, mspec, mspec, vec, vec, _resident(wi.shape), _resident(wo.shape)]
    args = x_args + [mods[0], mods[1], mods[2], nw[0], nw[1], wi, wo]
    if emit_u:
        in_specs += [vec, mspec, mspec]
        args += list(mixer_mods)
    outs = pl.pallas_call(
        functools.partial(_ffn_kernel, emit_u=emit_u, n_sub=FFN_SUB, n_lat_steps=n_lat_steps),
        grid=(pl.cdiv(n_rows, tm),),
        in_specs=in_specs,
        out_specs=tuple(pl.BlockSpec((tm, d), row) for _ in out_shape),
        out_shape=tuple(out_shape),
        compiler_params=pltpu.CompilerParams(dimension_semantics=("parallel",),
                                             vmem_limit_bytes=FFN_VMEM_LIMIT),
        name="ffn_sublayer",
    )(*args)
    return outs if emit_u else outs[0]


def _proj_kernel(x_ref, w_ref, o_ref):
    o_ref[...] = _dot(x_ref[...], w_ref[pl.program_id(1)])


def _proj(u, w):
    t, d = u.shape
    nj, _, tn = w.shape
    tm = 2 * TM
    return pl.pallas_call(
        _proj_kernel,
        grid=(pl.cdiv(t, tm), nj),
        in_specs=[pl.BlockSpec((tm, d), lambda i, j: (i, 0)), _resident(w.shape)],
        out_specs=pl.BlockSpec((tm, tn), lambda i, j: (i, j)),
        out_shape=jax.ShapeDtypeStruct((t, nj * tn), F32),
        compiler_params=_params("parallel", "arbitrary"),
        name="in_proj",
    )(u, w)


def _mla_prep_kernel(u_ref, cs_ref, qn_ref, kvn_ref, win_ref, wq_ref, wkv_ref, q_ref, k_ref, v_ref):
    cs = cs_ref[...]
    low = _dot(u_ref[...], win_ref[...])
    qa = _rms(low[:, M_QA:M_QA + Q_RANK], qn_ref[...]).astype(BF16)
    kva = _rms(low[:, M_KVA:M_KVA + KV_RANK], kvn_ref[...]).astype(BF16)
    q2 = _dot(qa, wq_ref[...])
    kv2 = _dot(kva, wkv_ref[...])
    kr = low[:, M_KR:M_KR + 2 * ROPE] * cs
    kr = (kr + pltpu.roll(kr, ROPE, 1))[:, :ROPE].astype(BF16)
    q2 = q2 * ATTN_LOG2_SCALE
    for h in range(HEADS):
        q_ref[h, :, :NOPE] = q2[:, h * NOPE:(h + 1) * NOPE].astype(BF16)
        qr = q2[:, BW + h * 128:BW + (h + 1) * 128] * cs
        q_ref[h, :, NOPE:] = (qr + pltpu.roll(qr, ROPE, 1))[:, :ROPE].astype(BF16)
        k_ref[h, :, :NOPE] = kv2[:, h * NOPE:(h + 1) * NOPE].astype(BF16)
        k_ref[h, :, NOPE:] = kr
        v_ref[h] = kv2[:, BW + h * HD:BW + (h + 1) * HD].astype(BF16)


def _mla_prep(u, cs, qn, kvn, w_low, wq, wkv, nb, n_seq, n_ctx):
    t, d = u.shape
    tm = TM_S
    lat_tiles = nb * n_seq // tm
    per_lat = n_seq // tm
    per_ctx = n_ctx // tm

    def omap(i):
        j = i - lat_tiles
        b = jnp.where(i < lat_tiles, i // per_lat, j // per_ctx)
        blk = jnp.where(i < lat_tiles, i % per_lat, per_lat + j % per_ctx)
        return (b, 0, blk, 0)

    tk = n_seq + n_ctx
    return pl.pallas_call(
        _mla_prep_kernel,
        grid=(t // tm,),
        in_specs=[pl.BlockSpec((tm, d), lambda i: (i, 0)),
                  pl.BlockSpec((tm, 128), lambda i: (i, 0)),
                  pl.BlockSpec((1, Q_RANK), lambda i: (0, 0)),
                  pl.BlockSpec((1, KV_RANK), lambda i: (0, 0)),
                  _resident(w_low.shape), _resident(wq.shape), _resident(wkv.shape)],
        out_specs=(pl.BlockSpec((None, HEADS, tm, NOPE + ROPE), omap),
                   pl.BlockSpec((None, HEADS, tm, NOPE + ROPE), omap),
                   pl.BlockSpec((None, HEADS, tm, HD), omap)),
        out_shape=(jax.ShapeDtypeStruct((nb, HEADS, tk, NOPE + ROPE), BF16),
                   jax.ShapeDtypeStruct((nb, HEADS, tk, NOPE + ROPE), BF16),
                   jax.ShapeDtypeStruct((nb, HEADS, tk, HD), BF16)),
        compiler_params=_params("parallel"),
        name="mla_prep",
    )(u, cs, qn, kvn, w_low, wq, wkv)


def _attn_kernel(q_ref, k_ref, v_ref, *rest, first, kb, nkb):
    o_ref = rest[-1]
    nh, tq = q_ref.shape[0], q_ref.shape[1]
    qs = [q_ref[h] for h in range(nh)]

    def safe_update(h, sl, carry):
        m, l, acc = carry
        t = _dot_nt(qs[h], k_ref[h, sl, :])
        m_new = jnp.maximum(m, jnp.max(t, axis=-1, keepdims=True))
        alpha = jnp.exp2(m - m_new)
        pr = jnp.exp2(t - m_new)
        l = alpha * l + jnp.sum(pr, axis=-1, keepdims=True)
        return m_new, l, alpha * acc + _dot(pr.astype(BF16), v_ref[h, sl, :])

    def fast_update(j, carry):
        st = [list(c) for c in carry]
        bm = [None] * nh
        items = [(pl.ds(pl.multiple_of(j * kb + i * KSUB, KSUB), KSUB), h)
                 for i in range(kb // KSUB) for h in range(nh)]
        score = lambda it: _dot_nt(qs[it[1]], k_ref[it[1], it[0], :])
        t_next = score(items[0])
        for n, (sl, h) in enumerate(items):
            t = t_next
            if n + 1 < len(items):
                t_next = score(items[n + 1])
            pr = jnp.exp2(t - st[h][0])
            tm = jnp.max(t, axis=-1, keepdims=True)
            bm[h] = tm if bm[h] is None else jnp.maximum(bm[h], tm)
            st[h][1] = st[h][1] + jnp.sum(pr, axis=-1, keepdims=True)
            st[h][2] = st[h][2] + _dot(pr.astype(BF16), v_ref[h, sl, :])
        out = []
        for h in range(nh):
            m, l, acc, jump = st[h]
            m_new = jnp.maximum(m, bm[h])
            alpha = jnp.exp2(m - m_new)
            out.append((m_new, l * alpha, acc * alpha, jnp.maximum(jump, bm[h] - m)))
        return tuple(out)

    init = (jnp.full((tq, 1), -1e30, F32), jnp.zeros((tq, 1), F32), jnp.zeros((tq, HD), F32))
    fsl = slice(first[0], first[0] + first[1])
    start = [safe_update(h, fsl, init) for h in range(nh)]
    if nkb == 0:
        for h in range(nh):
            o_ref[:, h * HD:(h + 1) * HD] = start[h][2] / start[h][1]
        return
    done = lax.fori_loop(0, nkb, fast_update, tuple(s + (jnp.zeros((tq, 1), F32),) for s in start),
                         unroll=_pick_block(nkb, (4, 2, 1)))
    for h in range(nh):
        o_ref[:, h * HD:(h + 1) * HD] = done[h][2] / done[h][1]

    for h in range(nh):
        @pl.when(jnp.max(done[h][3]) > MAX_JUMP)
        def _(h=h):
            step = lambda j, carry: safe_update(h, pl.ds(pl.multiple_of(j * kb, kb), kb), carry)
            _, l2, acc2 = lax.fori_loop(0, nkb, step, start[h])
            o_ref[:, h * HD:(h + 1) * HD] = acc2 / l2


def _pick_block(n, cands):
    for c in cands:
        if n % c == 0:
            return c
    raise ValueError(f"no block size for {n}")


def _attention(q, k, v, o_prev, nb, n_seq, n_ctx, rows, ctx_only):
    if ctx_only:
        tq, tk = n_ctx, n_ctx
        koff = n_seq // n_ctx
        grid = (nb, HEADS // HPS, 1)
        qmap = lambda b, h, i: (b, h, koff, 0)
        kmap = lambda b, h, i: (b, h, koff, 0)
        omap = lambda b, h, i: (nb * n_seq // n_ctx + b, h)
        first, kb, nkb = (0, n_ctx), n_ctx, 0
    else:
        tq, tk = _pick_block(n_seq, (512, 256)), n_seq + n_ctx
        grid = (nb, HEADS // HPS, n_seq // tq)
        qmap = lambda b, h, i: (b, h, i, 0)
        kmap = lambda b, h, i: (b, h, 0, 0)
        omap = lambda b, h, i: (b * (n_seq // tq) + i, h)
        kb = _pick_block(n_seq, (1024, 512))
        first, nkb = (n_seq, n_ctx), n_seq // kb
    in_specs = [pl.BlockSpec((None, HPS, tq, NOPE + ROPE), qmap),
                pl.BlockSpec((None, HPS, tk, NOPE + ROPE), kmap),
                pl.BlockSpec((None, HPS, tk, HD), kmap)]
    args = [q, k, v]
    aliases = {}
    if ctx_only:
        in_specs.append(pl.BlockSpec(memory_space=pl.ANY))
        args.append(o_prev)
        aliases = {3: 0}
    return pl.pallas_call(
        functools.partial(_attn_kernel, first=first, kb=kb, nkb=nkb),
        grid=grid,
        in_specs=in_specs,
        out_specs=pl.BlockSpec((tq, HPS * HD), omap),
        out_shape=jax.ShapeDtypeStruct((rows, BW), F32),
        input_output_aliases=aliases,
        compiler_params=_params("parallel", "parallel", "parallel"),
        name="mla_attn_ctx" if ctx_only else "mla_attn",
    )(*args)


def _chunk_pos(g, lat_chunks, ncl, ncc):
    is_lat = g < lat_chunks
    pos = jnp.where(is_lat, g % ncl, (g - lat_chunks) % ncc)
    last = jnp.where(is_lat, pos == ncl - 1, pos == ncc - 1)
    return pos == 0, last


def _scan_chunk(b, s, rev, lat_chunks, ncl, ncc):
    c = jnp.where(s < ncc, s, s - ncc)
    if rev:
        c = jnp.where(s < ncc, ncc - 1 - c, ncl - 1 - c)
    return jnp.where(s < ncc, lat_chunks + b * ncc + c, b * ncl + c)


def _tri_masks(n):
    ri = lax.broadcasted_iota(jnp.int32, (n, n), 0)
    ci = lax.broadcasted_iota(jnp.int32, (n, n), 1)
    return ri, ci


def _split(x):
    hi = x.astype(BF16)
    return hi, (x - hi.astype(F32)).astype(BF16)


def _dot3(a, b):
    (ah, al), (bh, bl) = a, b
    return _dot(ah, bh) + (_dot(ah, bl) + _dot(al, bh))


def _dot_tri(tri, x):
    hi, lo = _split(x)
    lo2 = (x - hi.astype(F32) - lo.astype(F32)).astype(BF16)
    return _dot(tri, hi) + (_dot(tri, lo) + _dot(tri, lo2))


def _unit_tri_inv(mats, ri, ci, eye, fill):
    rb, cb = ri >> 3, ci >> 3
    d8 = [jnp.where(rb == cb, a, 0.0) for a in mats]
    d8s = [_split(d) for d in d8]
    x2 = [_split(_dot3(d, d)) for d in d8s]
    fill(2)
    ts = [eye - d for d in d8]
    x4 = [_split(_dot3(x, x)) for x in x2]
    ts = [t + _dot3(_split(t), x) for t, x in zip(ts, x2)]
    fill(2)
    ts = [t + _dot3(_split(t), x) for t, x in zip(ts, x4)]
    fill(2)
    for _ in range(3):
        same = rb == cb
        rb, cb = rb >> 1, cb >> 1
        off = (rb == cb) & jnp.logical_not(same)
        bs = [jnp.where(off, a, 0.0).astype(BF16) for a in mats]
        tb = [t.astype(BF16) for t in ts]
        ys = [_dot(t, b).astype(BF16) for t, b in zip(tb, bs)]
        fill(2)
        ts = [t - _dot(y, tl) for t, y, tl in zip(ts, ys, tb)]
        fill(2)
    return ts


def _gdn_pre_body(x_ref, prev_ref, next_ref, a_ref, b_ref, cw_ref, alog_ref, dtb_ref,
                  u_ref, w_ref, qg_ref, kd_ref, at_ref, dec_ref, *, lat_chunks, ncl, ncc, fill):
    first, last = _chunk_pos(pl.program_id(0), lat_chunks, ncl, ncc)
    nrow = CPP * CHUNK
    x = x_ref[...]
    xp = jnp.where(first, 0.0, prev_ref[7:8, :])
    xn = jnp.where(last, 0.0, next_ref[0:1, :])
    row = lax.broadcasted_iota(jnp.int32, (nrow, 1), 0)
    x_dn = jnp.where(row == 0, xp, pltpu.roll(x, 1, 0))
    x_up = jnp.where(row == nrow - 1, xn, pltpu.roll(x, nrow - 1, 0))
    cw = cw_ref[...]
    s = _silu(x_dn * cw[0:1] + x * cw[1:2] + x_up * cw[2:3])

    lane = lax.broadcasted_iota(jnp.int32, (1, 128), 1)
    g = -jnp.exp(alog_ref[...]) * _softplus(a_ref[...] + dtb_ref[...])
    beta = _sigmoid(b_ref[...])
    ri, ci = _tri_masks(CHUNK)
    eye_b = ri == ci
    eye = eye_b.astype(F32)
    low = (ri >= ci).astype(BF16)
    upp = (ri <= ci).astype(BF16)
    chunk_rows = [slice(c * CHUNK, (c + 1) * CHUNK) for c in range(CPP)]
    gc, g_last = [], []
    for c, rc in enumerate(chunk_rows):
        gcc = jnp.where(lane < HEADS, _dot_tri(low, g[rc]), _dot_tri(upp, g[rc]))
        gc.append(gcc)
        g_last.append(jnp.where(lane < HEADS, gcc[CHUNK - 1:CHUNK], gcc[0:1]))
        dec_ref[c] = jnp.exp(g_last[c])

    qn, kn, vs = [], [], []
    for h in range(HEADS):
        qh = s[:, h * HD:(h + 1) * HD]
        kh = s[:, BW + h * HD:BW + (h + 1) * HD]
        vs.append(s[:, 2 * BW + h * HD:2 * BW + (h + 1) * HD])
        qn.append(qh * lax.rsqrt(jnp.sum(qh * qh, axis=-1, keepdims=True) + EPS) * HD ** -0.5)
        kn.append(kh * lax.rsqrt(jnp.sum(kh * kh, axis=-1, keepdims=True) + EPS))
    qk = [[None] * HEADS for _ in range(CPP)]
    kk = [[None] * HEADS for _ in range(CPP)]
    for c, rc in enumerate(chunk_rows):
        for h in range(HEADS):
            knb = kn[h][rc].astype(BF16)
            qk[c][h] = _dot_nt(qn[h][rc].astype(BF16), knb)
            kk[c][h] = _dot_nt(knb, knb)

    probs = [(c, d, h) for c in range(CPP) for h in range(HEADS) for d in range(2)]
    mats, rhs = [], []
    for c, d, h in probs:
        rc = chunk_rows[c]
        sl = slice(h * HD, (h + 1) * HD)
        dh = d * HEADS + h
        gcc = gc[c][:, dh:dh + 1]
        bt = beta[rc, dh:dh + 1]
        grow = jnp.sum(jnp.where(eye_b, gcc, 0.0), axis=0, keepdims=True)
        incl = (ri >= ci) if d == 0 else (ri <= ci)
        strict = (ri > ci) if d == 0 else (ri < ci)
        dm = jnp.where(incl, jnp.exp(gcc - grow), 0.0)
        eg = jnp.exp(gcc)
        knc = kn[h][rc]
        mats.append(jnp.where(strict, bt * kk[c][h] * dm, 0.0))
        rhs.append(jnp.concatenate([vs[h][rc] * bt, knc * (bt * eg)], axis=1).astype(BF16))
        qg_ref[d, rc, sl] = (qn[h][rc] * eg).astype(BF16)
        kd_ref[d, rc, sl] = (knc * jnp.exp(g_last[c][:, dh:dh + 1] - gcc)).astype(BF16)
        at_ref[d, rc, h * HD:h * HD + CHUNK] = (qk[c][h] * dm).astype(BF16)
        at_ref[d, rc, h * HD + CHUNK:(h + 1) * HD] = jnp.zeros((CHUNK, HD - CHUNK), BF16)

    for (c, d, h), t, r in zip(probs, _unit_tri_inv(mats, ri, ci, eye, fill), rhs):
        sl = slice(h * HD, (h + 1) * HD)
        uw = _dot(t.astype(BF16), r)
        u_ref[d, chunk_rows[c], sl] = uw[:, :HD]
        w_ref[d, chunk_rows[c], sl] = uw[:, HD:].astype(BF16)


def _mixer_prep_kernel(*refs, layer, lat_chunks, ncl, ncc):
    g_refs, h_refs = refs[:8] + refs[12:18], refs[8:12] + refs[18:]
    pieces = _hg_pre_body(*h_refs, layer=layer)

    def fill(n):
        for _ in range(n):
            next(pieces, None)

    _gdn_pre_body(*g_refs, lat_chunks=lat_chunks, ncl=ncl, ncc=ncc, fill=fill)
    for _ in pieces:
        pass


def _mixer_prep(p, conv_w, a_log, dt_bias, lb_logits, layer, nb, n_seq, n_ctx):
    t = p.shape[0]
    assert CPP == HG_CPP
    nrow = CPP * CHUNK
    assert n_seq % nrow == 0 and n_ctx % nrow == 0
    lat_blocks, nbl, nbc = nb * n_seq // nrow, n_seq // nrow, n_ctx // nrow
    qkv_blk = P_GQKV // (3 * BW)
    pad = lambda v: jnp.pad(v.reshape(1, -1), ((0, 0), (0, 128 - v.size)))
    dir_out = lambda dt: jax.ShapeDtypeStruct((2, t, BW), dt)
    dir_spec = pl.BlockSpec((2, nrow, BW), lambda g: (0, g, 0))
    r8 = nrow // 8
    return pl.pallas_call(
        functools.partial(_mixer_prep_kernel, layer=layer, lat_chunks=lat_blocks, ncl=nbl, ncc=nbc),
        grid=(t // nrow,),
        in_specs=[pl.BlockSpec((nrow, 3 * BW), lambda g: (g, qkv_blk)),
                  pl.BlockSpec((8, 3 * BW), lambda g: (jnp.maximum(g * r8 - 1, 0), qkv_blk)),
                  pl.BlockSpec((8, 3 * BW), lambda g: (jnp.minimum(g * r8 + r8, t // 8 - 1), qkv_blk)),
                  pl.BlockSpec((nrow, 128), lambda g: (g, P_GA // 128)),
                  pl.BlockSpec((nrow, 128), lambda g: (g, P_GB // 128)),
                  pl.BlockSpec((3, 3 * BW), lambda g: (0, 0)),
                  pl.BlockSpec((1, 128), lambda g: (0, 0)),
                  pl.BlockSpec((1, 128), lambda g: (0, 0)),
                  pl.BlockSpec((nrow, BW), lambda g: (g, P_HQ // BW)),
                  pl.BlockSpec((nrow, 2 * BW), lambda g: (g, P_HF // (2 * BW))),
                  pl.BlockSpec((nrow, BW), lambda g: (g, P_HI // BW)),
                  pl.BlockSpec(lb_logits.shape, lambda g: (0, 0, 0))],
        out_specs=(dir_spec, dir_spec, dir_spec, dir_spec, dir_spec,
                   pl.BlockSpec((CPP, 1, 128), lambda g: (g, 0, 0)),
                   dir_spec, dir_spec, dir_spec,
                   pl.BlockSpec((nrow, BW), lambda g: (g, 0)),
                   pl.BlockSpec((2, HG_CPP, 1, BW), lambda g: (0, g, 0, 0))),
        out_shape=(dir_out(F32), dir_out(BF16), dir_out(BF16), dir_out(BF16), dir_out(BF16),
                   jax.ShapeDtypeStruct((t // CHUNK, 1, 128), F32),
                   dir_out(F32), dir_out(BF16), dir_out(BF16),
                   jax.ShapeDtypeStruct((t, BW), BF16),
                   jax.ShapeDtypeStruct((2, t // CHUNK, 1, BW), F32)),
        compiler_params=_params("parallel"),
        name="mixer_chunk_prep",
    )(p, p, p, p, p, conv_w, pad(a_log), pad(dt_bias), p, p, p, lb_logits)


def _scan_rows(d, c):
    cc = c if d == 0 else CPS - 1 - c
    return slice(cc * CHUNK, (cc + 1) * CHUNK), cc


def _mixer_scan_kernel(*refs):
    g_ins, h_ins = refs[:12], refs[12:22]
    g_of, g_ob, h_of, h_ob, g_state, h_state = refs[22:]

    @pl.when(pl.program_id(1) == 0)
    def _():
        g_state[...] = jnp.zeros(g_state.shape, F32)
        h_state[...] = jnp.zeros(h_state.shape, F32)

    filler = _hg_scan_body(h_ins, h_of, h_ob, h_state)
    _gdn_scan_body(g_ins, g_of, g_ob, g_state, filler)
    for _ in filler:
        pass


def _gdn_scan_body(ins, of_ref, ob_ref, s_ref, filler):
    def fill(n):
        for _ in range(n):
            next(filler, None)

    chains = [(d, h) for d in range(2) for h in range(HEADS)]
    outs = (of_ref, ob_ref)
    st = [s_ref[d, h] for d, h in chains]
    for c in range(CPS):
        sb = [s.astype(BF16) for s in st]
        vn = []
        for (d, h), s in zip(chains, sb):
            u_ref, w_ref = ins[6 * d], ins[6 * d + 1]
            rows, _ = _scan_rows(d, c)
            sl = slice(h * HD, (h + 1) * HD)
            vn.append((u_ref[rows, sl] - _dot(w_ref[rows, sl], s)).astype(BF16))
        fill(3)
        for i, (d, h) in enumerate(chains):
            qg_ref, kd_ref, at_ref, dec_ref = ins[6 * d + 2:6 * d + 6]
            rows, cc = _scan_rows(d, c)
            sl = slice(h * HD, (h + 1) * HD)
            outs[d][rows, sl] = _dot(qg_ref[rows, sl], sb[i]) + _dot(at_ref[rows, h * HD:h * HD + CHUNK], vn[i])
            dec = dec_ref[cc, 0:1, d * HEADS + h:d * HEADS + h + 1]
            st[i] = st[i] * dec + _dot_tn(kd_ref[rows, sl], vn[i])
        fill(3)
    for (d, h), s in zip(chains, st):
        s_ref[d, h] = s


def _dir_scan(kernel, groups, nb, n_seq, n_ctx, name):
    t = groups[0][0][0].shape[1]
    step = CPS * CHUNK
    assert n_seq % step == 0 and n_ctx % step == 0
    lat_blocks, ncl, ncc = nb * n_seq // step, n_seq // step, n_ctx // step
    maps = [functools.partial(_scan_chunk, rev=bool(d), lat_chunks=lat_blocks, ncl=ncl, ncc=ncc) for d in range(2)]
    in_specs, args = [], []
    for arrays, dec in groups:
        for d, cmap in enumerate(maps):
            for arr in arrays:
                if arr.ndim == 3:
                    in_specs.append(pl.BlockSpec((None, step, BW), lambda b, s, d=d, cmap=cmap: (d, cmap(b, s), 0)))
                else:
                    in_specs.append(pl.BlockSpec((step, BW), lambda b, s, cmap=cmap: (cmap(b, s), 0)))
                args.append(arr)
            if dec.ndim == 3:
                in_specs.append(pl.BlockSpec((CPS, 1, dec.shape[-1]), lambda b, s, cmap=cmap: (cmap(b, s), 0, 0)))
            else:
                in_specs.append(pl.BlockSpec((None, CPS, 1, dec.shape[-1]),
                                             lambda b, s, d=d, cmap=cmap: (d, cmap(b, s), 0, 0)))
            args.append(dec)
    out_specs = tuple(pl.BlockSpec((step, BW), lambda b, s, cmap=cmap: (cmap(b, s), 0))
                      for _ in groups for cmap in maps)
    return pl.pallas_call(
        kernel,
        grid=(nb, ncc + ncl),
        in_specs=in_specs,
        out_specs=out_specs,
        out_shape=tuple(jax.ShapeDtypeStruct((t, BW), F32) for _ in out_specs),
        scratch_shapes=[pltpu.VMEM((2, HEADS, HD, HD), F32) for _ in groups],
        compiler_params=_params("parallel", "arbitrary"),
        name=name,
    )(*args)


def _hg_block_decay(gc, rev):
    out = None
    for b in range(CHUNK // SUB):
        lo, hi = gc[b * SUB:b * SUB + 1], gc[(b + 1) * SUB - 1:(b + 1) * SUB]
        dcy = (hi - lo) if rev else (lo - hi)
        out = dcy if out is None else jnp.maximum(out, dcy)
    return out


def _hg_scores(q, k, gc, rev, exact):
    nblk = CHUNK // SUB
    lane = lax.broadcasted_iota(jnp.int32, (1, CHUNK), 1)
    row8 = lax.broadcasted_iota(jnp.int32, (8, 1), 0)
    blk = lambda x, b: x[b * SUB:(b + 1) * SUB]
    ref_row = lambda b: b * SUB + (0 if rev else SUB - 1)
    refs = [gc[ref_row(b):ref_row(b) + 1] for b in range(nblk)]
    kt = jnp.concatenate([blk(k, b) * jnp.exp2(refs[b] - blk(gc, b)) for b in range(nblk)], axis=0)
    srcs = list(range(nblk - 1, 0, -1)) if rev else list(range(nblk - 1))
    rows_of = (lambda j: slice(0, j * SUB)) if rev else (lambda j: slice((j + 1) * SUB, CHUNK))
    qt = jnp.concatenate([q[rows_of(j)] * jnp.exp2(gc[rows_of(j)] - refs[j]) for j in srcs], axis=0)
    seg_off, off = {}, 0
    for j in srcs:
        seg_off[j] = off
        off += (j if rev else nblk - 1 - j) * SUB
    ktb, qtb = kt.astype(BF16), qt.astype(BF16)
    cross = [_dot_nt(qtb[:, h * HD:(h + 1) * HD], ktb[:, h * HD:(h + 1) * HD]) for h in range(HEADS)]

    def assemble(own):
        out = []
        for h in range(HEADS):
            groups = []
            for g8 in range(CHUNK // 8):
                r0 = 8 * g8
                bi = r0 // SUB
                val = own[h][g8]
                for j in srcs:
                    if (j < bi and not rev) or (j > bi and rev):
                        base = seg_off[j] + (r0 if rev else r0 - (j + 1) * SUB)
                        val = jnp.where((lane >= j * SUB) & (lane < (j + 1) * SUB), cross[h][base:base + 8], val)
                groups.append(val)
            out.append(jnp.concatenate(groups, axis=0))
        return tuple(out)

    def own_exact():
        acc = [[jnp.zeros((8, CHUNK), F32) for _ in range(CHUNK // 8)] for _ in range(HEADS)]
        for j in range(CHUNK):
            b, jj = j // SUB, j % SUB
            gj, kj = gc[j:j + 1], k[j:j + 1]
            for rg in range(SUB // 8):
                lo, hi = 8 * rg, 8 * rg + 7
                if (hi < jj and not rev) or (lo > jj and rev):
                    continue
                r0 = b * SUB + lo
                w = jnp.exp2(gc[r0:r0 + 8] - gj)
                if not ((lo >= jj and not rev) or (hi <= jj and rev)):
                    w = jnp.where((row8 + lo >= jj) if not rev else (row8 + lo <= jj), w, 0.0)
                tt = q[r0:r0 + 8] * w * kj
                g8 = r0 // 8
                for h in range(HEADS):
                    col = jnp.sum(tt[:, h * HD:(h + 1) * HD], axis=-1, keepdims=True)
                    acc[h][g8] = jnp.where(lane == j, col, acc[h][g8])
        return assemble(acc)

    first_row = lambda b: b * SUB + (SUB - 1 if rev else 0)
    starts = [gc[first_row(b):first_row(b) + 1] for b in range(nblk)]

    def own_matmul():
        rfull = jnp.concatenate([jnp.broadcast_to(starts[b], (SUB, q.shape[1])) for b in range(nblk)], axis=0)
        qd = (q * jnp.exp2(gc - rfull)).astype(BF16)
        kd = (k * jnp.exp2(rfull - gc)).astype(BF16)
        ri, ci = _tri_masks(CHUNK)
        sh = SUB.bit_length() - 1
        keep = ((ri >> sh) == (ci >> sh)) & ((ri <= ci) if rev else (ri >= ci))
        own = []
        for h in range(HEADS):
            sd = jnp.where(keep, _dot_nt(qd[:, h * HD:(h + 1) * HD], kd[:, h * HD:(h + 1) * HD]), 0.0)
            own.append([sd[8 * g8:8 * g8 + 8] for g8 in range(CHUNK // 8)])
        return assemble(own)

    return own_exact() if exact else own_matmul()


def _hg_pre_body(q_ref, f_ref, i_ref, lbl_ref, oi_ref, qg_ref, kd_ref, vb_ref, dec_ref, *, layer):
    lbl = lbl_ref[...]
    e = jnp.exp(lbl - jnp.max(lbl, axis=0, keepdims=True))
    sm = e / jnp.sum(e, axis=0, keepdims=True)
    lb_all = sm[0]
    for l in range(1, layer + 1):
        lb_all = lb_all + sm[l]
    lb_all = lb_all - sm[0]
    q_all = q_ref[...] * HD ** -0.5
    v_all = i_ref[...].astype(BF16)
    vb_ref[...] = v_all
    ri, ci = _tri_masks(CHUNK)
    work, decay = [], None
    for d in range(2):
        lb = lb_all[d:d + 1]
        f = f_ref[:, d * BW:(d + 1) * BW]
        e = jnp.exp(-jnp.abs(f))
        r = 1.0 / (1.0 + e)
        er = e * r
        pos = f >= 0.0
        sig, sig_neg = jnp.where(pos, r, er), jnp.where(pos, er, r)
        log_f = jnp.log2(jnp.maximum(lb, LB_FLOOR) + (1.0 - lb) * sig)
        k_all = (1.0 - lb) * sig_neg
        tri = ((ri >= ci) if d == 0 else (ri <= ci)).astype(BF16)
        yield
        for c in range(HG_CPP):
            rc = slice(c * CHUNK, (c + 1) * CHUNK)
            q, k, v = q_all[rc], k_all[rc], v_all[rc]
            gc = _dot_tri(tri, log_f[rc])
            g_last = gc[CHUNK - 1:CHUNK] if d == 0 else gc[0:1]
            dec_ref[d, c] = jnp.exp2(g_last)
            qg_ref[d, rc] = (q * jnp.exp2(gc)).astype(BF16)
            kd_ref[d, rc] = (k * jnp.exp2(g_last - gc)).astype(BF16)
            work.append((d, rc, q, k, v, gc))
            dcy = _hg_block_decay(gc, rev=bool(d))
            decay = dcy if decay is None else jnp.maximum(decay, dcy)
            yield

    def intra(item, exact):
        d, rc, q, k, v, gc = item
        scores = _hg_scores(q, k, gc, rev=bool(d), exact=exact)
        for h in range(HEADS):
            sl = slice(h * HD, (h + 1) * HD)
            oi_ref[d, rc, sl] = _dot(scores[h].astype(BF16), v[:, sl])

    for item in work:
        intra(item, False)
        yield

    @pl.when(jnp.max(decay) > HG_MAX_DECAY)
    def _():
        for item in work:
            intra(item, True)


def _hg_scan_body(ins, of_ref, ob_ref, s_ref):
    chains = [(d, h) for d in range(2) for h in range(HEADS)]
    outs = (of_ref, ob_ref)
    incs = []
    for d, h in chains:
        kd_ref, v_ref = ins[5 * d + 2], ins[5 * d + 3]
        sl = slice(h * HD, (h + 1) * HD)
        incs.append([_dot_tn(v_ref[_scan_rows(d, c)[0], sl], kd_ref[_scan_rows(d, c)[0], sl])
                     for c in range(CPS)])
        yield
    states = []
    for i, (d, h) in enumerate(chains):
        dec_ref = ins[5 * d + 4]
        sl = slice(h * HD, (h + 1) * HD)
        st = s_ref[d, h]
        seq = []
        for c in range(CPS):
            seq.append(st.astype(BF16))
            st = st * dec_ref[_scan_rows(d, c)[1], :, sl] + incs[i][c]
        s_ref[d, h] = st
        states.append(seq)
        yield
    for i, (d, h) in enumerate(chains):
        oi_ref, qg_ref = ins[5 * d], ins[5 * d + 1]
        sl = slice(h * HD, (h + 1) * HD)
        for c in range(CPS):
            rows, _ = _scan_rows(d, c)
            outs[d][rows, sl] = oi_ref[rows, sl] + _dot_nt(qg_ref[rows, sl], states[i][c])
        yield


def _merge_kernel(h_ref, gof_ref, gob_ref, gg_ref, yb_ref, hof_ref, hob_ref, hgg_ref, u_ref,
                  gnorm_ref, hnorm_ref, gt_ref, nw_ref, wg_ref, wb_ref, wo_ref, o_ref):
    d = h_ref.shape[1]
    n_sub = h_ref.shape[0] // MERGE_SUB

    def readout(rs, of_ref, ob_ref, gate_ref, norm_ref):
        o = of_ref[rs, :] + ob_ref[rs, :]
        gate = gate_ref[rs, :]
        parts = []
        for h in range(HEADS):
            sl = slice(h * HD, (h + 1) * HD)
            parts.append(_rms(o[:, sl], norm_ref[...]) * _silu(gate[:, sl]))
        return jnp.concatenate(parts, axis=1).astype(BF16)

    def branches(rs):
        return (readout(rs, gof_ref, gob_ref, gg_ref, gnorm_ref),
                yb_ref[rs, :].astype(BF16),
                readout(rs, hof_ref, hob_ref, hgg_ref, hnorm_ref))

    def mix(rs, ys):
        u = u_ref[rs, :]
        m = None
        for j in range(3):
            gate_logits = _dot(u, wg_ref[:, j * d:(j + 1) * d])
            term = _sigmoid(gate_logits) * _dot(ys[j], wb_ref[j])
            m = term if m is None else m + term
        return m.astype(BF16)

    def finish(rs, m):
        y = _dot(m, wo_ref[...])
        o_ref[rs, :] = h_ref[rs, :] + gt_ref[...] * _rms(y, nw_ref[...])

    rows = [slice(s * MERGE_SUB, (s + 1) * MERGE_SUB) for s in range(n_sub)]
    ys = branches(rows[0])
    for s in range(n_sub):
        m = mix(rows[s], ys)
        if s + 1 < n_sub:
            ys = branches(rows[s + 1])
        finish(rows[s], m)


def _merge(h, g_of, g_ob, yb, h_of, h_ob, p, u, gnorm, hnorm, gate_mod, nw, wg, wb, wo, n_rows, n_seq, nb):
    d = h.shape[1]
    tm = TM
    row = lambda i: (i, 0)
    bw_spec = pl.BlockSpec((tm, BW), row)
    pcol = lambda off, width: pl.BlockSpec((tm, width), lambda i: (i, off // width))
    vec = lambda n: pl.BlockSpec((1, n), lambda i: (0, 0))
    return pl.pallas_call(
        _merge_kernel,
        grid=(n_rows // tm,),
        in_specs=[pl.BlockSpec((tm, d), row), bw_spec, bw_spec, pcol(P_GGATE, BW), bw_spec,
                  bw_spec, bw_spec, pcol(P_HGATE, BW), pl.BlockSpec((tm, d), row),
                  vec(HD), vec(HD),
                  pl.BlockSpec((None, 1, d), lambda i: (jnp.minimum((i * tm) // n_seq, nb), 0, 0)),
                  vec(d), _resident(wg.shape), _resident(wb.shape), _resident(wo.shape)],
        out_specs=pl.BlockSpec((tm, d), row),
        out_shape=jax.ShapeDtypeStruct((n_rows, d), F32),
        compiler_params=_params("parallel"),
        name="branch_merge",
    )(h, g_of, g_ob, p, yb, h_of, h_ob, p, u, gnorm, hnorm, gate_mod, nw, wg, wb, wo)


def _pack_w_in(w):
    d = w.shape[0]
    w = w.astype(BF16)
    sizes = (BW, BW, BW, BW, 2 * HEADS, 2 * HEADS, Q_RANK, KV_RANK, ROPE, BW, 2 * BW, BW, BW, 3 * d)
    offs = [0]
    for s in sizes:
        offs.append(offs[-1] + s)
    part = lambda i: w[:, offs[i]:offs[i + 1]]
    gq, gk, gv, ggate, ga, gb, qa, kva, kr, hq, hf, hi, hgate, gates = (part(i) for i in range(14))
    z = lambda n: jnp.zeros((d, n), w.dtype)
    swap = jnp.concatenate([kr[:, 16:32], kr[:, 0:16], kr[:, 48:64], kr[:, 32:48]], axis=1)
    rec = jnp.concatenate([gq, gk, gv, ggate, hf, hq, hi, hgate, ga, z(120), gb, z(120)], axis=1).astype(BF16)
    mla = jnp.concatenate([qa, z(128), kva, kr, swap], axis=1).astype(BF16)
    assert rec.shape[1] == P_COLS and mla.shape[1] == M_COLS
    return rec.reshape(d, P_COLS // PROJ_TN, PROJ_TN).transpose(1, 0, 2), mla, gates.astype(BF16)


def _pack_wq(w):
    w = w.reshape(Q_RANK, HEADS, NOPE + ROPE)
    nope = w[:, :, :NOPE].reshape(Q_RANK, HEADS * NOPE)
    r = w[:, :, NOPE:]
    sw = jnp.concatenate([r[..., 16:32], r[..., 0:16], r[..., 48:64], r[..., 32:48]], axis=-1)
    rope = jnp.concatenate([r, sw], axis=-1).reshape(Q_RANK, HEADS * 2 * ROPE)
    return jnp.concatenate([nope, rope], axis=1).astype(BF16)


def _pack_wkv(w):
    w = w.reshape(KV_RANK, HEADS, NOPE + HD)
    return jnp.concatenate([w[:, :, :NOPE].reshape(KV_RANK, -1), w[:, :, NOPE:].reshape(KV_RANK, -1)],
                           axis=1).astype(BF16)


def _rope_table(nb, n_seq, n_ctx):
    nf = ROPE // 4
    rows = n_seq // GRID_W
    rpos = jnp.repeat(jnp.arange(rows, dtype=F32), GRID_W)
    cpos = jnp.tile(jnp.arange(GRID_W, dtype=F32), rows)
    inv = ROPE_BASE ** (-jnp.arange(nf, dtype=F32) / nf)
    ar, ac = rpos[:, None] * inv, cpos[:, None] * inv
    cos = jnp.concatenate([jnp.cos(ar), jnp.cos(ar), jnp.cos(ac), jnp.cos(ac)], axis=1)
    sin = jnp.concatenate([-jnp.sin(ar), jnp.sin(ar), -jnp.sin(ac), jnp.sin(ac)], axis=1)
    lat = jnp.tile(jnp.concatenate([cos, sin], axis=1), (nb, 1))
    ctx = jnp.concatenate([jnp.ones((nb * n_ctx, ROPE), F32), jnp.zeros((nb * n_ctx, ROPE), F32)], axis=1)
    return jnp.concatenate([lat, ctx], axis=0)


def kernel(x, c, ctx, c_ctx, w_ada, b_ada, norm_w, ffn_w_in, ffn_w_out, w_in, gdn_conv, gdn_a_log, gdn_dt_bias, gdn_norm, mla_q_norm, mla_kv_norm, mla_w_q_b, mla_w_kv_b, hg_lb_logits, hg_norm, w_branch, w_out):
    nb, n_seq, d = x.shape
    n_ctx = ctx.shape[1]
    depth = w_ada.shape[0]
    lat_rows, rows = nb * n_seq, nb * (n_seq + n_ctx)
    assert nb + 1 <= 8 and n_seq % (FFN_SUB * TM) == 0 and n_ctx % TM_S == 0 and (nb * n_ctx) % TM == 0
    assert n_seq % n_ctx == 0 and lat_rows % n_ctx == 0

    cc = jnp.concatenate([c, c_ctx[None], jnp.zeros((8 - nb - 1, d), F32)], axis=0)
    mods = _ada(cc, w_ada, b_ada)[:, :nb + 1].reshape(depth, nb + 1, 9, 1, d).transpose(0, 2, 1, 3, 4)
    cs = _rope_table(nb, n_seq, n_ctx)
    h = (x.reshape(lat_rows, d), ctx.reshape(nb * n_ctx, d))

    for l in range(depth):
        last = l == depth - 1
        md, nw = mods[l], norm_w[l][:, None, :]
        wi = [ffn_w_in[l, j].astype(BF16) for j in range(2)]
        wo = [ffn_w_out[l, j].astype(BF16) for j in range(2)]

        h, u = _ffn(h, md[0:3], nw[0:2], wi[0], wo[0], rows, n_seq, nb, mixer_mods=(nw[2], md[3], md[4]))
        w_rec, w_low, w_gates = _pack_w_in(w_in[l])
        p = _proj(u, w_rec)

        prep = _mixer_prep(p, gdn_conv[l], gdn_a_log[l], gdn_dt_bias[l], hg_lb_logits, l, nb, n_seq, n_ctx)
        g_ops, (hoi, hqg, hkd, hvb, hdec) = prep[:6], prep[6:]
        g_of, g_ob, h_of, h_ob = _dir_scan(_mixer_scan_kernel,
                                           [(g_ops[:5], g_ops[5]), ((hoi, hqg, hkd, hvb), hdec)],
                                           nb, n_seq, n_ctx, "mixer_scan")

        q, k, v = _mla_prep(u, cs, mla_q_norm[l][None], mla_kv_norm[l][None], w_low,
                            _pack_wq(mla_w_q_b[l]), _pack_wkv(mla_w_kv_b[l]), nb, n_seq, n_ctx)
        yb = _attention(q, k, v, None, nb, n_seq, n_ctx, rows, ctx_only=False)
        if not last:
            yb = _attention(q, k, v, yb, nb, n_seq, n_ctx, rows, ctx_only=True)

        out_rows = lat_rows if last else rows
        h = _merge(h, g_of, g_ob, yb, h_of, h_ob, p, u, gdn_norm[l][None], hg_norm[l][None], md[5], nw[3],
                   w_gates, w_branch[l].astype(BF16), w_out[l].astype(BF16), out_rows, n_seq, nb)
        h = _ffn(h, md[6:9], nw[4:6], wi[1], wo[1], out_rows, n_seq, nb)
    return h.reshape(nb, n_seq, d)
```

```python
import functools

import jax
import jax.numpy as jnp
from jax import lax
from jax.experimental import pallas as pl
from jax.experimental.pallas import tpu as pltpu

F32 = jnp.float32
BF16 = jnp.bfloat16
EPS = 1e-6
LB_FLOOR = 1e-30
GRID_W = 64
ROPE_BASE = 10000.0

D_FF = 2816
HEADS = 4
HD = 128
ROPE = 64
NOPE = 128
Q_RANK = 384
KV_RANK = 256
CHUNK = 64
CPS = 4
CPP = 4
HG_CPP = 4
SUB = 16
BW = HEADS * HD

TM = 512
TM_S = 256
FF_CK = 256
MERGE_SUB = 256
FFN_SUB = 2
VMEM_LIMIT = 48 * 1024 * 1024
FFN_VMEM_LIMIT = 56 * 1024 * 1024
ATTN_LOG2_SCALE = (NOPE + ROPE) ** -0.5 * 1.4426950408889634
HPS = 2
KSUB = 256
HG_MAX_DECAY = 100.0
MAX_JUMP = 64.0

P_GQKV = 0
P_GGATE = 1536
P_HF = 2048
P_HQ = 3072
P_HI = 3584
P_HGATE = 4096
P_GA = 4608
P_GB = 4736
P_COLS = 4864
PROJ_TN = 2432
M_QA, M_KVA, M_KR, M_COLS = 0, 512, 768, 896

NT = (((1,), (1,)), ((), ()))
TN = (((0,), (0,)), ((), ()))


def _dot(a, b):
    return jnp.dot(a, b, preferred_element_type=F32)


def _dot_nt(a, b):
    return lax.dot_general(a, b, NT, preferred_element_type=F32)


def _dot_tn(a, b):
    return lax.dot_general(a, b, TN, preferred_element_type=F32)


def _sigmoid(x):
    return 1.0 / (1.0 + jnp.exp(-x))


def _silu(x):
    return x * _sigmoid(x)


def _softplus(x):
    return jnp.maximum(x, 0.0) + jnp.log(1.0 + jnp.exp(-jnp.abs(x)))


def _rms(x, w):
    return x * lax.rsqrt(jnp.mean(x * x, axis=-1, keepdims=True) + EPS) * w


def _resident(shape):
    zeros = (0,) * len(shape)
    return pl.BlockSpec(shape, lambda *_: zeros, pipeline_mode=pl.Buffered(1))


def _params(*sem):
    return pltpu.CompilerParams(dimension_semantics=sem, vmem_limit_bytes=VMEM_LIMIT)


def _ada_kernel(c_ref, w_ref, b_ref, o_ref):
    s = _silu(c_ref[...])
    o_ref[...] = _dot(s.astype(BF16), w_ref[...].astype(BF16)) + b_ref[...]


def _ada(cc, w_ada, b_ada):
    depth, d, nm = w_ada.shape
    tn = 1024
    return pl.pallas_call(
        _ada_kernel,
        grid=(depth, nm // tn),
        in_specs=[pl.BlockSpec((8, d), lambda l, j: (0, 0)),
                  pl.BlockSpec((None, d, tn), lambda l, j: (l, 0, j)),
                  pl.BlockSpec((None, 1, tn), lambda l, j: (l, 0, j))],
        out_specs=pl.BlockSpec((None, 8, tn), lambda l, j: (l, 0, j)),
        out_shape=jax.ShapeDtypeStruct((depth, 8, nm), F32),
        compiler_params=_params("parallel", "parallel"),
        name="ada_mod",
    )(cc, w_ada, b_ada.reshape(depth, 1, nm))


def _ffn_kernel(*refs, emit_u, n_sub, n_lat_steps):
    x_ref, c_ref = (refs[0], None) if n_lat_steps is None else refs[:2]
    refs = refs[1 if n_lat_steps is None else 2:]
    sh_ref, sc_ref, gt_ref, prew_ref, postw_ref, wi_ref, wo_ref = refs[:7]
    if emit_u:
        nw_ref, sh2_ref, sc2_ref, o_ref, u_ref = refs[7:]
    else:
        (o_ref,) = refs[7:]
    n_ck = D_FF // FF_CK
    half = n_ck // 2

    def load(rs):
        if c_ref is None:
            return x_ref[rs, :]
        return jnp.where(pl.program_id(0) >= n_lat_steps, c_ref[rs, :], x_ref[rs, :])

    def prologue(s):
        x = load(slice(s * TM, (s + 1) * TM))
        return (_rms(x, prew_ref[...]) * (1.0 + sc_ref[...]) + sh_ref[...]).astype(BF16)

    def chunks(hn, acc, lo, hi):
        for c in range(lo, hi):
            g = _dot(hn, wi_ref[:, c * FF_CK:(c + 1) * FF_CK])
            u = _dot(hn, wi_ref[:, D_FF + c * FF_CK:D_FF + (c + 1) * FF_CK])
            a = (_silu(g) * u).astype(BF16)
            acc = acc + _dot(a, wo_ref[c * FF_CK:(c + 1) * FF_CK, :])
        return acc

    def epilogue(s, acc):
        rs = slice(s * TM, (s + 1) * TM)
        out = load(rs) + 0.5 * gt_ref[...] * _rms(acc, postw_ref[...])
        o_ref[rs, :] = out
        if emit_u:
            u_ref[rs, :] = (_rms(out, nw_ref[...]) * (1.0 + sc2_ref[...]) + sh2_ref[...]).astype(BF16)

    zero = jnp.zeros((TM, x_ref.shape[1]), F32)
    hn = prologue(0)
    acc = chunks(hn, zero, 0, half)
    for s in range(n_sub):
        hn_next = prologue(s + 1) if s + 1 < n_sub else None
        acc = chunks(hn, acc, half, n_ck)
        if hn_next is not None:
            acc_next = chunks(hn_next, zero, 0, half)
        epilogue(s, acc)
        if hn_next is not None:
            hn, acc = hn_next, acc_next


def _ffn(h, mods, nw, wi, wo, n_rows, n_seq, nb, mixer_mods=None):
    split = isinstance(h, tuple)
    d = h[0].shape[1] if split else h.shape[1]
    emit_u = mixer_mods is not None
    vec = pl.BlockSpec((1, d), lambda i: (0, 0))
    out_shape = [jax.ShapeDtypeStruct((n_rows, d), F32)]
    if emit_u:
        out_shape.append(jax.ShapeDtypeStruct((n_rows, d), BF16))

    tm = FFN_SUB * TM
    row = lambda i: (i, 0)
    mspec = pl.BlockSpec((None, 1, d), lambda i: (jnp.minimum((i * tm) // n_seq, nb), 0, 0))
    n_lat_steps = None
    if split:
        n_lat_steps = h[0].shape[0] // tm
        assert h[0].shape[0] % tm == 0 and h[1].shape[0] <= tm
        x_specs = [pl.BlockSpec((tm, d), lambda i: (jnp.minimum(i, n_lat_steps - 1), 0)),
                   pl.BlockSpec((tm, d), lambda i: (0, 0))]
        x_args = list(h)
    else:
        x_specs, x_args = [pl.BlockSpec((tm, d), row)], [h]
    in_specs = x_specs + [mspec, mspec, mspec, vec, vec, _resident(wi.shape), _resident(wo.shape)]
    args = x_args + [mods[0], mods[1], mods[2], nw[0], nw[1], wi, wo]
    if emit_u:
        in_specs += [vec, mspec, mspec]
        args += list(mixer_mods)
    outs = pl.pallas_call(
        functools.partial(_ffn_kernel, emit_u=emit_u, n_sub=FFN_SUB, n_lat_steps=n_lat_steps),
        grid=(pl.cdiv(n_rows, tm),),
        in_specs=in_specs,
        out_specs=tuple(pl.BlockSpec((tm, d), row) for _ in out_shape),
        out_shape=tuple(out_shape),
        compiler_params=pltpu.CompilerParams(dimension_semantics=("parallel",),
                                             vmem_limit_bytes=FFN_VMEM_LIMIT),
        name="ffn_sublayer",
    )(*args)
    return outs if emit_u else outs[0]


def _proj_kernel(x_ref, w_ref, o_ref):
    o_ref[...] = _dot(x_ref[...], w_ref[pl.program_id(1)])


def _proj(u, w):
    t, d = u.shape
    nj, _, tn = w.shape
    tm = 2 * TM
    return pl.pallas_call(
        _proj_kernel,
        grid=(pl.cdiv(t, tm), nj),
        in_specs=[pl.BlockSpec((tm, d), lambda i, j: (i, 0)), _resident(w.shape)],
        out_specs=pl.BlockSpec((tm, tn), lambda i, j: (i, j)),
        out_shape=jax.ShapeDtypeStruct((t, nj * tn), F32),
        compiler_params=_params("parallel", "arbitrary"),
        name="in_proj",
    )(u, w)


def _mla_prep_kernel(u_ref, cs_ref, qn_ref, kvn_ref, win_ref, wq_ref, wkv_ref, q_ref, k_ref, v_ref):
    cs = cs_ref[...]
    low = _dot(u_ref[...], win_ref[...])
    qa = _rms(low[:, M_QA:M_QA + Q_RANK], qn_ref[...]).astype(BF16)
    kva = _rms(low[:, M_KVA:M_KVA + KV_RANK], kvn_ref[...]).astype(BF16)
    q2 = _dot(qa, wq_ref[...])
    kv2 = _dot(kva, wkv_ref[...])
    kr = low[:, M_KR:M_KR + 2 * ROPE] * cs
    kr = (kr + pltpu.roll(kr, ROPE, 1))[:, :ROPE].astype(BF16)
    q2 = q2 * ATTN_LOG2_SCALE
    for h in range(HEADS):
        q_ref[h, :, :NOPE] = q2[:, h * NOPE:(h + 1) * NOPE].astype(BF16)
        qr = q2[:, BW + h * 128:BW + (h + 1) * 128] * cs
        q_ref[h, :, NOPE:] = (qr + pltpu.roll(qr, ROPE, 1))[:, :ROPE].astype(BF16)
        k_ref[h, :, :NOPE] = kv2[:, h * NOPE:(h + 1) * NOPE].astype(BF16)
        k_ref[h, :, NOPE:] = kr
        v_ref[h] = kv2[:, BW + h * HD:BW + (h + 1) * HD].astype(BF16)


def _mla_prep(u, cs, qn, kvn, w_low, wq, wkv, nb, n_seq, n_ctx):
    t, d = u.shape
    tm = TM_S
    lat_tiles = nb * n_seq // tm
    per_lat = n_seq // tm
    per_ctx = n_ctx // tm

    def omap(i):
        j = i - lat_tiles
        b = jnp.where(i < lat_tiles, i // per_lat, j // per_ctx)
        blk = jnp.where(i < lat_tiles, i % per_lat, per_lat + j % per_ctx)
        return (b, 0, blk, 0)

    tk = n_seq + n_ctx
    return pl.pallas_call(
        _mla_prep_kernel,
        grid=(t // tm,),
        in_specs=[pl.BlockSpec((tm, d), lambda i: (i, 0)),
                  pl.BlockSpec((tm, 128), lambda i: (i, 0)),
                  pl.BlockSpec((1, Q_RANK), lambda i: (0, 0)),
                  pl.BlockSpec((1, KV_RANK), lambda i: (0, 0)),
                  _resident(w_low.shape), _resident(wq.shape), _resident(wkv.shape)],
        out_specs=(pl.BlockSpec((None, HEADS, tm, NOPE + ROPE), omap),
                   pl.BlockSpec((None, HEADS, tm, NOPE + ROPE), omap),
                   pl.BlockSpec((None, HEADS, tm, HD), omap)),
        out_shape=(jax.ShapeDtypeStruct((nb, HEADS, tk, NOPE + ROPE), BF16),
                   jax.ShapeDtypeStruct((nb, HEADS, tk, NOPE + ROPE), BF16),
                   jax.ShapeDtypeStruct((nb, HEADS, tk, HD), BF16)),
        compiler_params=_params("parallel"),
        name="mla_prep",
    )(u, cs, qn, kvn, w_low, wq, wkv)


def _attn_kernel(q_ref, k_ref, v_ref, *rest, first, kb, nkb):
    o_ref = rest[-1]
    nh, tq = q_ref.shape[0], q_ref.shape[1]
    qs = [q_ref[h] for h in range(nh)]

    def safe_update(h, sl, carry):
        m, l, acc = carry
        t = _dot_nt(qs[h], k_ref[h, sl, :])
        m_new = jnp.maximum(m, jnp.max(t, axis=-1, keepdims=True))
        alpha = jnp.exp2(m - m_new)
        pr = jnp.exp2(t - m_new)
        l = alpha * l + jnp.sum(pr, axis=-1, keepdims=True)
        return m_new, l, alpha * acc + _dot(pr.astype(BF16), v_ref[h, sl, :])

    def fast_update(j, carry):
        st = [list(c) for c in carry]
        bm = [None] * nh
        items = [(pl.ds(pl.multiple_of(j * kb + i * KSUB, KSUB), KSUB), h)
                 for i in range(kb // KSUB) for h in range(nh)]
        score = lambda it: _dot_nt(qs[it[1]], k_ref[it[1], it[0], :])
        t_next = score(items[0])
        for n, (sl, h) in enumerate(items):
            t = t_next
            if n + 1 < len(items):
                t_next = score(items[n + 1])
            pr = jnp.exp2(t - st[h][0])
            tm = jnp.max(t, axis=-1, keepdims=True)
            bm[h] = tm if bm[h] is None else jnp.maximum(bm[h], tm)
            st[h][1] = st[h][1] + jnp.sum(pr, axis=-1, keepdims=True)
            st[h][2] = st[h][2] + _dot(pr.astype(BF16), v_ref[h, sl, :])
        out = []
        for h in range(nh):
            m, l, acc, jump = st[h]
            m_new = jnp.maximum(m, bm[h])
            alpha = jnp.exp2(m - m_new)
            out.append((m_new, l * alpha, acc * alpha, jnp.maximum(jump, bm[h] - m)))
        return tuple(out)

    init = (jnp.full((tq, 1), -1e30, F32), jnp.zeros((tq, 1), F32), jnp.zeros((tq, HD), F32))
    fsl = slice(first[0], first[0] + first[1])
    start = [safe_update(h, fsl, init) for h in range(nh)]
    if nkb == 0:
        for h in range(nh):
            o_ref[:, h * HD:(h + 1) * HD] = start[h][2] / start[h][1]
        return
    done = lax.fori_loop(0, nkb, fast_update, tuple(s + (jnp.zeros((tq, 1), F32),) for s in start),
                         unroll=_pick_block(nkb, (2, 1)))
    for h in range(nh):
        o_ref[:, h * HD:(h + 1) * HD] = done[h][2] / done[h][1]

    for h in range(nh):
        @pl.when(jnp.max(done[h][3]) > MAX_JUMP)
        def _(h=h):
            step = lambda j, carry: safe_update(h, pl.ds(pl.multiple_of(j * kb, kb), kb), carry)
            _, l2, acc2 = lax.fori_loop(0, nkb, step, start[h])
            o_ref[:, h * HD:(h + 1) * HD] = acc2 / l2


def _pick_block(n, cands):
    for c in cands:
        if n % c == 0:
            return c
    raise ValueError(f"no block size for {n}")


def _attention(q, k, v, o_prev, nb, n_seq, n_ctx, rows, ctx_only):
    if ctx_only:
        tq, tk = n_ctx, n_ctx
        koff = n_seq // n_ctx
        grid = (nb, HEADS // HPS, 1)
        qmap = lambda b, h, i: (b, h, koff, 0)
        kmap = lambda b, h, i: (b, h, koff, 0)
        omap = lambda b, h, i: (nb * n_seq // n_ctx + b, h)
        first, kb, nkb = (0, n_ctx), n_ctx, 0
    else:
        tq, tk = _pick_block(n_seq, (512, 256)), n_seq + n_ctx
        grid = (nb, HEADS // HPS, n_seq // tq)
        qmap = lambda b, h, i: (b, h, i, 0)
        kmap = lambda b, h, i: (b, h, 0, 0)
        omap = lambda b, h, i: (b * (n_seq // tq) + i, h)
        kb = _pick_block(n_seq, (2048, 1024, 512))
        first, nkb = (n_seq, n_ctx), n_seq // kb
    in_specs = [pl.BlockSpec((None, HPS, tq, NOPE + ROPE), qmap),
                pl.BlockSpec((None, HPS, tk, NOPE + ROPE), kmap),
                pl.BlockSpec((None, HPS, tk, HD), kmap)]
    args = [q, k, v]
    aliases = {}
    if ctx_only:
        in_specs.append(pl.BlockSpec(memory_space=pl.ANY))
        args.append(o_prev)
        aliases = {3: 0}
    return pl.pallas_call(
        functools.partial(_attn_kernel, first=first, kb=kb, nkb=nkb),
        grid=grid,
        in_specs=in_specs,
        out_specs=pl.BlockSpec((tq, HPS * HD), omap),
        out_shape=jax.ShapeDtypeStruct((rows, BW), F32),
        input_output_aliases=aliases,
        compiler_params=_params("parallel", "parallel", "parallel"),
        name="mla_attn_ctx" if ctx_only else "mla_attn",
    )(*args)


def _chunk_pos(g, lat_chunks, ncl, ncc):
    is_lat = g < lat_chunks
    pos = jnp.where(is_lat, g % ncl, (g - lat_chunks) % ncc)
    last = jnp.where(is_lat, pos == ncl - 1, pos == ncc - 1)
    return pos == 0, last


def _scan_chunk(b, s, rev, lat_chunks, ncl, ncc):
    c = jnp.where(s < ncc, s, s - ncc)
    if rev:
        c = jnp.where(s < ncc, ncc - 1 - c, ncl - 1 - c)
    return jnp.where(s < ncc, lat_chunks + b * ncc + c, b * ncl + c)


def _tri_masks(n):
    ri = lax.broadcasted_iota(jnp.int32, (n, n), 0)
    ci = lax.broadcasted_iota(jnp.int32, (n, n), 1)
    return ri, ci


def _split(x):
    hi = x.astype(BF16)
    return hi, (x - hi.astype(F32)).astype(BF16)


def _dot3(a, b):
    (ah, al), (bh, bl) = a, b
    return _dot(ah, bh) + (_dot(ah, bl) + _dot(al, bh))


def _dot_tri(tri, x):
    hi, lo = _split(x)
    lo2 = (x - hi.astype(F32) - lo.astype(F32)).astype(BF16)
    return _dot(tri, hi) + (_dot(tri, lo) + _dot(tri, lo2))


def _unit_tri_inv(mats, ri, ci, eye, fill):
    rb, cb = ri >> 3, ci >> 3
    d8 = [jnp.where(rb == cb, a, 0.0) for a in mats]
    d8s = [_split(d) for d in d8]
    x2 = [_split(_dot3(d, d)) for d in d8s]
    fill(2)
    ts = [eye - d for d in d8]
    x4 = [_split(_dot3(x, x)) for x in x2]
    ts = [t + _dot3(_split(t), x) for t, x in zip(ts, x2)]
    fill(2)
    ts = [t + _dot3(_split(t), x) for t, x in zip(ts, x4)]
    fill(2)
    for _ in range(3):
        same = rb == cb
        rb, cb = rb >> 1, cb >> 1
        off = (rb == cb) & jnp.logical_not(same)
        bs = [jnp.where(off, a, 0.0).astype(BF16) for a in mats]
        tb = [t.astype(BF16) for t in ts]
        ys = [_dot(t, b).astype(BF16) for t, b in zip(tb, bs)]
        fill(2)
        ts = [t - _dot(y, tl) for t, y, tl in zip(ts, ys, tb)]
        fill(2)
    return ts


def _gdn_pre_body(x_ref, prev_ref, next_ref, a_ref, b_ref, cw_ref, alog_ref, dtb_ref,
                  u_ref, w_ref, qg_ref, kd_ref, at_ref, dec_ref, *, lat_chunks, ncl, ncc, fill):
    first, last = _chunk_pos(pl.program_id(0), lat_chunks, ncl, ncc)
    nrow = CPP * CHUNK
    x = x_ref[...]
    xp = jnp.where(first, 0.0, prev_ref[7:8, :])
    xn = jnp.where(last, 0.0, next_ref[0:1, :])
    row = lax.broadcasted_iota(jnp.int32, (nrow, 1), 0)
    x_dn = jnp.where(row == 0, xp, pltpu.roll(x, 1, 0))
    x_up = jnp.where(row == nrow - 1, xn, pltpu.roll(x, nrow - 1, 0))
    cw = cw_ref[...]
    s = _silu(x_dn * cw[0:1] + x * cw[1:2] + x_up * cw[2:3])

    lane = lax.broadcasted_iota(jnp.int32, (1, 128), 1)
    g = -jnp.exp(alog_ref[...]) * _softplus(a_ref[...] + dtb_ref[...])
    beta = _sigmoid(b_ref[...])
    ri, ci = _tri_masks(CHUNK)
    eye_b = ri == ci
    eye = eye_b.astype(F32)
    low = (ri >= ci).astype(BF16)
    upp = (ri <= ci).astype(BF16)
    chunk_rows = [slice(c * CHUNK, (c + 1) * CHUNK) for c in range(CPP)]
    gc, g_last = [], []
    for c, rc in enumerate(chunk_rows):
        gcc = jnp.where(lane < HEADS, _dot_tri(low, g[rc]), _dot_tri(upp, g[rc]))
        gc.append(gcc)
        g_last.append(jnp.where(lane < HEADS, gcc[CHUNK - 1:CHUNK], gcc[0:1]))
        dec_ref[c] = jnp.exp(g_last[c])

    qn, kn, vs = [], [], []
    for h in range(HEADS):
        qh = s[:, h * HD:(h + 1) * HD]
        kh = s[:, BW + h * HD:BW + (h + 1) * HD]
        vs.append(s[:, 2 * BW + h * HD:2 * BW + (h + 1) * HD])
        qn.append(qh * lax.rsqrt(jnp.sum(qh * qh, axis=-1, keepdims=True) + EPS) * HD ** -0.5)
        kn.append(kh * lax.rsqrt(jnp.sum(kh * kh, axis=-1, keepdims=True) + EPS))
    qk = [[None] * HEADS for _ in range(CPP)]
    kk = [[None] * HEADS for _ in range(CPP)]
    for c, rc in enumerate(chunk_rows):
        for h in range(HEADS):
            knb = kn[h][rc].astype(BF16)
            qk[c][h] = _dot_nt(qn[h][rc].astype(BF16), knb)
            kk[c][h] = _dot_nt(knb, knb)

    probs = [(c, d, h) for c in range(CPP) for h in range(HEADS) for d in range(2)]
    mats, rhs = [], []
    for c, d, h in probs:
        rc = chunk_rows[c]
        sl = slice(h * HD, (h + 1) * HD)
        dh = d * HEADS + h
        gcc = gc[c][:, dh:dh + 1]
        bt = beta[rc, dh:dh + 1]
        grow = jnp.sum(jnp.where(eye_b, gcc, 0.0), axis=0, keepdims=True)
        incl = (ri >= ci) if d == 0 else (ri <= ci)
        strict = (ri > ci) if d == 0 else (ri < ci)
        dm = jnp.where(incl, jnp.exp(gcc - grow), 0.0)
        eg = jnp.exp(gcc)
        knc = kn[h][rc]
        mats.append(jnp.where(strict, bt * kk[c][h] * dm, 0.0))
        rhs.append(jnp.concatenate([vs[h][rc] * bt, knc * (bt * eg)], axis=1).astype(BF16))
        qg_ref[d, rc, sl] = (qn[h][rc] * eg).astype(BF16)
        kd_ref[d, rc, sl] = (knc * jnp.exp(g_last[c][:, dh:dh + 1] - gcc)).astype(BF16)
        at_ref[d, rc, h * HD:h * HD + CHUNK] = (qk[c][h] * dm).astype(BF16)
        at_ref[d, rc, h * HD + CHUNK:(h + 1) * HD] = jnp.zeros((CHUNK, HD - CHUNK), BF16)

    for (c, d, h), t, r in zip(probs, _unit_tri_inv(mats, ri, ci, eye, fill), rhs):
        sl = slice(h * HD, (h + 1) * HD)
        uw = _dot(t.astype(BF16), r)
        u_ref[d, chunk_rows[c], sl] = uw[:, :HD]
        w_ref[d, chunk_rows[c], sl] = uw[:, HD:].astype(BF16)


def _mixer_prep_kernel(*refs, layer, lat_chunks, ncl, ncc):
    g_refs, h_refs = refs[:8] + refs[12:18], refs[8:12] + refs[18:]
    pieces = _hg_pre_body(*h_refs, layer=layer)

    def fill(n):
        for _ in range(n):
            next(pieces, None)

    _gdn_pre_body(*g_refs, lat_chunks=lat_chunks, ncl=ncl, ncc=ncc, fill=fill)
    for _ in pieces:
        pass


def _mixer_prep(p, conv_w, a_log, dt_bias, lb_logits, layer, nb, n_seq, n_ctx):
    t = p.shape[0]
    assert CPP == HG_CPP
    nrow = CPP * CHUNK
    assert n_seq % nrow == 0 and n_ctx % nrow == 0
    lat_blocks, nbl, nbc = nb * n_seq // nrow, n_seq // nrow, n_ctx // nrow
    qkv_blk = P_GQKV // (3 * BW)
    pad = lambda v: jnp.pad(v.reshape(1, -1), ((0, 0), (0, 128 - v.size)))
    dir_out = lambda dt: jax.ShapeDtypeStruct((2, t, BW), dt)
    dir_spec = pl.BlockSpec((2, nrow, BW), lambda g: (0, g, 0))
    r8 = nrow // 8
    return pl.pallas_call(
        functools.partial(_mixer_prep_kernel, layer=layer, lat_chunks=lat_blocks, ncl=nbl, ncc=nbc),
        grid=(t // nrow,),
        in_specs=[pl.BlockSpec((nrow, 3 * BW), lambda g: (g, qkv_blk)),
                  pl.BlockSpec((8, 3 * BW), lambda g: (jnp.maximum(g * r8 - 1, 0), qkv_blk)),
                  pl.BlockSpec((8, 3 * BW), lambda g: (jnp.minimum(g * r8 + r8, t // 8 - 1), qkv_blk)),
                  pl.BlockSpec((nrow, 128), lambda g: (g, P_GA // 128)),
                  pl.BlockSpec((nrow, 128), lambda g: (g, P_GB // 128)),
                  pl.BlockSpec((3, 3 * BW), lambda g: (0, 0)),
                  pl.BlockSpec((1, 128), lambda g: (0, 0)),
                  pl.BlockSpec((1, 128), lambda g: (0, 0)),
                  pl.BlockSpec((nrow, BW), lambda g: (g, P_HQ // BW)),
                  pl.BlockSpec((nrow, 2 * BW), lambda g: (g, P_HF // (2 * BW))),
                  pl.BlockSpec((nrow, BW), lambda g: (g, P_HI // BW)),
                  pl.BlockSpec(lb_logits.shape, lambda g: (0, 0, 0))],
        out_specs=(dir_spec, dir_spec, dir_spec, dir_spec, dir_spec,
                   pl.BlockSpec((CPP, 1, 128), lambda g: (g, 0, 0)),
                   dir_spec, dir_spec, dir_spec,
                   pl.BlockSpec((nrow, BW), lambda g: (g, 0)),
                   pl.BlockSpec((2, HG_CPP, 1, BW), lambda g: (0, g, 0, 0))),
        out_shape=(dir_out(F32), dir_out(BF16), dir_out(BF16), dir_out(BF16), dir_out(BF16),
                   jax.ShapeDtypeStruct((t // CHUNK, 1, 128), F32),
                   dir_out(F32), dir_out(BF16), dir_out(BF16),
                   jax.ShapeDtypeStruct((t, BW), BF16),
                   jax.ShapeDtypeStruct((2, t // CHUNK, 1, BW), F32)),
        compiler_params=_params("parallel"),
        name="mixer_chunk_prep",
    )(p, p, p, p, p, conv_w, pad(a_log), pad(dt_bias), p, p, p, lb_logits)


def _scan_rows(d, c):
    cc = c if d == 0 else CPS - 1 - c
    return slice(cc * CHUNK, (cc + 1) * CHUNK), cc


def _mixer_scan_kernel(*refs):
    g_ins, h_ins = refs[:12], refs[12:22]
    g_of, g_ob, h_of, h_ob, g_state, h_state = refs[22:]

    @pl.when(pl.program_id(1) == 0)
    def _():
        g_state[...] = jnp.zeros(g_state.shape, F32)
        h_state[...] = jnp.zeros(h_state.shape, F32)

    filler = _hg_scan_body(h_ins, h_of, h_ob, h_state)
    _gdn_scan_body(g_ins, g_of, g_ob, g_state, filler)
    for _ in filler:
        pass


def _gdn_scan_body(ins, of_ref, ob_ref, s_ref, filler):
    def fill(n):
        for _ in range(n):
            next(filler, None)

    chains = [(d, h) for d in range(2) for h in range(HEADS)]
    outs = (of_ref, ob_ref)
    st = [s_ref[d, h] for d, h in chains]
    for c in range(CPS):
        sb = [s.astype(BF16) for s in st]
        vn = []
        for (d, h), s in zip(chains, sb):
            u_ref, w_ref = ins[6 * d], ins[6 * d + 1]
            rows, _ = _scan_rows(d, c)
            sl = slice(h * HD, (h + 1) * HD)
            vn.append((u_ref[rows, sl] - _dot(w_ref[rows, sl], s)).astype(BF16))
        fill(3)
        for i, (d, h) in enumerate(chains):
            qg_ref, kd_ref, at_ref, dec_ref = ins[6 * d + 2:6 * d + 6]
            rows, cc = _scan_rows(d, c)
            sl = slice(h * HD, (h + 1) * HD)
            outs[d][rows, sl] = _dot(qg_ref[rows, sl], sb[i]) + _dot(at_ref[rows, h * HD:h * HD + CHUNK], vn[i])
            dec = dec_ref[cc, 0:1, d * HEADS + h:d * HEADS + h + 1]
            st[i] = st[i] * dec + _dot_tn(kd_ref[rows, sl], vn[i])
        fill(3)
    for (d, h), s in zip(chains, st):
        s_ref[d, h] = s


def _dir_scan(kernel, groups, nb, n_seq, n_ctx, name):
    t = groups[0][0][0].shape[1]
    step = CPS * CHUNK
    assert n_seq % step == 0 and n_ctx % step == 0
    lat_blocks, ncl, ncc = nb * n_seq // step, n_seq // step, n_ctx // step
    maps = [functools.partial(_scan_chunk, rev=bool(d), lat_chunks=lat_blocks, ncl=ncl, ncc=ncc) for d in range(2)]
    in_specs, args = [], []
    for arrays, dec in groups:
        for d, cmap in enumerate(maps):
            for arr in arrays:
                if arr.ndim == 3:
                    in_specs.append(pl.BlockSpec((None, step, BW), lambda b, s, d=d, cmap=cmap: (d, cmap(b, s), 0)))
                else:
                    in_specs.append(pl.BlockSpec((step, BW), lambda b, s, cmap=cmap: (cmap(b, s), 0)))
                args.append(arr)
            if dec.ndim == 3:
                in_specs.append(pl.BlockSpec((CPS, 1, dec.shape[-1]), lambda b, s, cmap=cmap: (cmap(b, s), 0, 0)))
            else:
                in_specs.append(pl.BlockSpec((None, CPS, 1, dec.shape[-1]),
                                             lambda b, s, d=d, cmap=cmap: (d, cmap(b, s), 0, 0)))
            args.append(dec)
    out_specs = tuple(pl.BlockSpec((step, BW), lambda b, s, cmap=cmap: (cmap(b, s), 0))
                      for _ in groups for cmap in maps)
    return pl.pallas_call(
        kernel,
        grid=(nb, ncc + ncl),
        in_specs=in_specs,
        out_specs=out_specs,
        out_shape=tuple(jax.ShapeDtypeStruct((t, BW), F32) for _ in out_specs),
        scratch_shapes=[pltpu.VMEM((2, HEADS, HD, HD), F32) for _ in groups],
        compiler_params=_params("parallel", "arbitrary"),
        name=name,
    )(*args)


def _hg_block_decay(gc, rev):
    out = None
    for b in range(CHUNK // SUB):
        lo, hi = gc[b * SUB:b * SUB + 1], gc[(b + 1) * SUB - 1:(b + 1) * SUB]
        dcy = (hi - lo) if rev else (lo - hi)
        out = dcy if out is None else jnp.maximum(out, dcy)
    return out


def _hg_scores(q, k, gc, rev, exact):
    nblk = CHUNK // SUB
    lane = lax.broadcasted_iota(jnp.int32, (1, CHUNK), 1)
    row8 = lax.broadcasted_iota(jnp.int32, (8, 1), 0)
    blk = lambda x, b: x[b * SUB:(b + 1) * SUB]
    ref_row = lambda b: b * SUB + (0 if rev else SUB - 1)
    refs = [gc[ref_row(b):ref_row(b) + 1] for b in range(nblk)]
    kt = jnp.concatenate([blk(k, b) * jnp.exp2(refs[b] - blk(gc, b)) for b in range(nblk)], axis=0)
    srcs = list(range(nblk - 1, 0, -1)) if rev else list(range(nblk - 1))
    rows_of = (lambda j: slice(0, j * SUB)) if rev else (lambda j: slice((j + 1) * SUB, CHUNK))
    qt = jnp.concatenate([q[rows_of(j)] * jnp.exp2(gc[rows_of(j)] - refs[j]) for j in srcs], axis=0)
    seg_off, off = {}, 0
    for j in srcs:
        seg_off[j] = off
        off += (j if rev else nblk - 1 - j) * SUB
    ktb, qtb = kt.astype(BF16), qt.astype(BF16)
    cross = [_dot_nt(qtb[:, h * HD:(h + 1) * HD], ktb[:, h * HD:(h + 1) * HD]) for h in range(HEADS)]

    def assemble(own):
        out = []
        for h in range(HEADS):
            groups = []
            for g8 in range(CHUNK // 8):
                r0 = 8 * g8
                bi = r0 // SUB
                val = own[h][g8]
                for j in srcs:
                    if (j < bi and not rev) or (j > bi and rev):
                        base = seg_off[j] + (r0 if rev else r0 - (j + 1) * SUB)
                        val = jnp.where((lane >= j * SUB) & (lane < (j + 1) * SUB), cross[h][base:base + 8], val)
                groups.append(val)
            out.append(jnp.concatenate(groups, axis=0))
        return tuple(out)

    def own_exact():
        acc = [[jnp.zeros((8, CHUNK), F32) for _ in range(CHUNK // 8)] for _ in range(HEADS)]
        for j in range(CHUNK):
            b, jj = j // SUB, j % SUB
            gj, kj = gc[j:j + 1], k[j:j + 1]
            for rg in range(SUB // 8):
                lo, hi = 8 * rg, 8 * rg + 7
                if (hi < jj and not rev) or (lo > jj and rev):
                    continue
                r0 = b * SUB + lo
                w = jnp.exp2(gc[r0:r0 + 8] - gj)
                if not ((lo >= jj and not rev) or (hi <= jj and rev)):
                    w = jnp.where((row8 + lo >= jj) if not rev else (row8 + lo <= jj), w, 0.0)
                tt = q[r0:r0 + 8] * w * kj
                g8 = r0 // 8
                for h in range(HEADS):
                    col = jnp.sum(tt[:, h * HD:(h + 1) * HD], axis=-1, keepdims=True)
                    acc[h][g8] = jnp.where(lane == j, col, acc[h][g8])
        return assemble(acc)

    first_row = lambda b: b * SUB + (SUB - 1 if rev else 0)
    starts = [gc[first_row(b):first_row(b) + 1] for b in range(nblk)]

    def own_matmul():
        rfull = jnp.concatenate([jnp.broadcast_to(starts[b], (SUB, q.shape[1])) for b in range(nblk)], axis=0)
        qd = (q * jnp.exp2(gc - rfull)).astype(BF16)
        kd = (k * jnp.exp2(rfull - gc)).astype(BF16)
        ri, ci = _tri_masks(CHUNK)
        sh = SUB.bit_length() - 1
        keep = ((ri >> sh) == (ci >> sh)) & ((ri <= ci) if rev else (ri >= ci))
        own = []
        for h in range(HEADS):
            sd = jnp.where(keep, _dot_nt(qd[:, h * HD:(h + 1) * HD], kd[:, h * HD:(h + 1) * HD]), 0.0)
            own.append([sd[8 * g8:8 * g8 + 8] for g8 in range(CHUNK // 8)])
        return assemble(own)

    return own_exact() if exact else own_matmul()


def _hg_pre_body(q_ref, f_ref, i_ref, lbl_ref, oi_ref, qg_ref, kd_ref, vb_ref, dec_ref, *, layer):
    lbl = lbl_ref[...]
    e = jnp.exp(lbl - jnp.max(lbl, axis=0, keepdims=True))
    sm = e / jnp.sum(e, axis=0, keepdims=True)
    lb_all = sm[0]
    for l in range(1, layer + 1):
        lb_all = lb_all + sm[l]
    lb_all = lb_all - sm[0]
    q_all = q_ref[...] * HD ** -0.5
    v_all = i_ref[...].astype(BF16)
    vb_ref[...] = v_all
    ri, ci = _tri_masks(CHUNK)
    work, decay = [], None
    for d in range(2):
        lb = lb_all[d:d + 1]
        f = f_ref[:, d * BW:(d + 1) * BW]
        e = jnp.exp(-jnp.abs(f))
        r = 1.0 / (1.0 + e)
        er = e * r
        pos = f >= 0.0
        sig, sig_neg = jnp.where(pos, r, er), jnp.where(pos, er, r)
        log_f = jnp.log2(jnp.maximum(lb, LB_FLOOR) + (1.0 - lb) * sig)
        k_all = (1.0 - lb) * sig_neg
        tri = ((ri >= ci) if d == 0 else (ri <= ci)).astype(BF16)
        yield
        for c in range(HG_CPP):
            rc = slice(c * CHUNK, (c + 1) * CHUNK)
            q, k, v = q_all[rc], k_all[rc], v_all[rc]
            gc = _dot_tri(tri, log_f[rc])
            g_last = gc[CHUNK - 1:CHUNK] if d == 0 else gc[0:1]
            dec_ref[d, c] = jnp.exp2(g_last)
            qg_ref[d, rc] = (q * jnp.exp2(gc)).astype(BF16)
            kd_ref[d, rc] = (k * jnp.exp2(g_last - gc)).astype(BF16)
            work.append((d, rc, q, k, v, gc))
            dcy = _hg_block_decay(gc, rev=bool(d))
            decay = dcy if decay is None else jnp.maximum(decay, dcy)
            yield

    def intra(item, exact):
        d, rc, q, k, v, gc = item
        scores = _hg_scores(q, k, gc, rev=bool(d), exact=exact)
        for h in range(HEADS):
            sl = slice(h * HD, (h + 1) * HD)
            oi_ref[d, rc, sl] = _dot(scores[h].astype(BF16), v[:, sl])

    for item in work:
        intra(item, False)
        yield

    @pl.when(jnp.max(decay) > HG_MAX_DECAY)
    def _():
        for item in work:
            intra(item, True)


def _hg_scan_body(ins, of_ref, ob_ref, s_ref):
    chains = [(d, h) for d in range(2) for h in range(HEADS)]
    outs = (of_ref, ob_ref)
    incs = []
    for d, h in chains:
        kd_ref, v_ref = ins[5 * d + 2], ins[5 * d + 3]
        sl = slice(h * HD, (h + 1) * HD)
        incs.append([_dot_tn(v_ref[_scan_rows(d, c)[0], sl], kd_ref[_scan_rows(d, c)[0], sl])
                     for c in range(CPS)])
        yield
    states = []
    for i, (d, h) in enumerate(chains):
        dec_ref = ins[5 * d + 4]
        sl = slice(h * HD, (h + 1) * HD)
        st = s_ref[d, h]
        seq = []
        for c in range(CPS):
            seq.append(st.astype(BF16))
            st = st * dec_ref[_scan_rows(d, c)[1], :, sl] + incs[i][c]
        s_ref[d, h] = st
        states.append(seq)
        yield
    for i, (d, h) in enumerate(chains):
        oi_ref, qg_ref = ins[5 * d], ins[5 * d + 1]
        sl = slice(h * HD, (h + 1) * HD)
        for c in range(CPS):
            rows, _ = _scan_rows(d, c)
            outs[d][rows, sl] = oi_ref[rows, sl] + _dot_nt(qg_ref[rows, sl], states[i][c])
        yield


def _merge_kernel(h_ref, gof_ref, gob_ref, gg_ref, yb_ref, hof_ref, hob_ref, hgg_ref, u_ref,
                  gnorm_ref, hnorm_ref, gt_ref, nw_ref, wg_ref, wb_ref, wo_ref, o_ref):
    d = h_ref.shape[1]
    n_sub = h_ref.shape[0] // MERGE_SUB

    def readout(rs, of_ref, ob_ref, gate_ref, norm_ref):
        o = of_ref[rs, :] + ob_ref[rs, :]
        gate = gate_ref[rs, :]
        parts = []
        for h in range(HEADS):
            sl = slice(h * HD, (h + 1) * HD)
            parts.append(_rms(o[:, sl], norm_ref[...]) * _silu(gate[:, sl]))
        return jnp.concatenate(parts, axis=1).astype(BF16)

    def branches(rs):
        return (readout(rs, gof_ref, gob_ref, gg_ref, gnorm_ref),
                yb_ref[rs, :].astype(BF16),
                readout(rs, hof_ref, hob_ref, hgg_ref, hnorm_ref))

    def mix(rs, ys):
        u = u_ref[rs, :]
        m = None
        for j in range(3):
            gate_logits = _dot(u, wg_ref[:, j * d:(j + 1) * d])
            term = _sigmoid(gate_logits) * _dot(ys[j], wb_ref[j])
            m = term if m is None else m + term
        return m.astype(BF16)

    def finish(rs, m):
        y = _dot(m, wo_ref[...])
        o_ref[rs, :] = h_ref[rs, :] + gt_ref[...] * _rms(y, nw_ref[...])

    rows = [slice(s * MERGE_SUB, (s + 1) * MERGE_SUB) for s in range(n_sub)]
    ys = branches(rows[0])
    for s in range(n_sub):
        m = mix(rows[s], ys)
        if s + 1 < n_sub:
            ys = branches(rows[s + 1])
        finish(rows[s], m)


def _merge(h, g_of, g_ob, yb, h_of, h_ob, p, u, gnorm, hnorm, gate_mod, nw, wg, wb, wo, n_rows, n_seq, nb):
    d = h.shape[1]
    tm = TM
    row = lambda i: (i, 0)
    bw_spec = pl.BlockSpec((tm, BW), row)
    pcol = lambda off, width: pl.BlockSpec((tm, width), lambda i: (i, off // width))
    vec = lambda n: pl.BlockSpec((1, n), lambda i: (0, 0))
    return pl.pallas_call(
        _merge_kernel,
        grid=(n_rows // tm,),
        in_specs=[pl.BlockSpec((tm, d), row), bw_spec, bw_spec, pcol(P_GGATE, BW), bw_spec,
                  bw_spec, bw_spec, pcol(P_HGATE, BW), pl.BlockSpec((tm, d), row),
                  vec(HD), vec(HD),
                  pl.BlockSpec((None, 1, d), lambda i: (jnp.minimum((i * tm) // n_seq, nb), 0, 0)),
                  vec(d), _resident(wg.shape), _resident(wb.shape), _resident(wo.shape)],
        out_specs=pl.BlockSpec((tm, d), row),
        out_shape=jax.ShapeDtypeStruct((n_rows, d), F32),
        compiler_params=_params("parallel"),
        name="branch_merge",
    )(h, g_of, g_ob, p, yb, h_of, h_ob, p, u, gnorm, hnorm, gate_mod, nw, wg, wb, wo)


def _pack_w_in(w):
    d = w.shape[0]
    sizes = (BW, BW, BW, BW, 2 * HEADS, 2 * HEADS, Q_RANK, KV_RANK, ROPE, BW, 2 * BW, BW, BW, 3 * d)
    offs = [0]
    for s in sizes:
        offs.append(offs[-1] + s)
    part = lambda i: w[:, offs[i]:offs[i + 1]]
    gq, gk, gv, ggate, ga, gb, qa, kva, kr, hq, hf, hi, hgate, gates = (part(i) for i in range(14))
    z = lambda n: jnp.zeros((d, n), w.dtype)
    swap = jnp.concatenate([kr[:, 16:32], kr[:, 0:16], kr[:, 48:64], kr[:, 32:48]], axis=1)
    rec = jnp.concatenate([gq, gk, gv, ggate, hf, hq, hi, hgate, ga, z(120), gb, z(120)], axis=1).astype(BF16)
    mla = jnp.concatenate([qa, z(128), kva, kr, swap], axis=1).astype(BF16)
    assert rec.shape[1] == P_COLS and mla.shape[1] == M_COLS
    return rec.reshape(d, P_COLS // PROJ_TN, PROJ_TN).transpose(1, 0, 2), mla, gates.astype(BF16)


def _pack_wq(w):
    w = w.reshape(Q_RANK, HEADS, NOPE + ROPE)
    nope = w[:, :, :NOPE].reshape(Q_RANK, HEADS * NOPE)
    r = w[:, :, NOPE:]
    sw = jnp.concatenate([r[..., 16:32], r[..., 0:16], r[..., 48:64], r[..., 32:48]], axis=-1)
    rope = jnp.concatenate([r, sw], axis=-1).reshape(Q_RANK, HEADS * 2 * ROPE)
    return jnp.concatenate([nope, rope], axis=1).astype(BF16)


def _pack_wkv(w):
    w = w.reshape(KV_RANK, HEADS, NOPE + HD)
    return jnp.concatenate([w[:, :, :NOPE].reshape(KV_RANK, -1), w[:, :, NOPE:].reshape(KV_RANK, -1)],
                           axis=1).astype(BF16)


def _rope_table(nb, n_seq, n_ctx):
    nf = ROPE // 4
    rows = n_seq // GRID_W
    rpos = jnp.repeat(jnp.arange(rows, dtype=F32), GRID_W)
    cpos = jnp.tile(jnp.arange(GRID_W, dtype=F32), rows)
    inv = ROPE_BASE ** (-jnp.arange(nf, dtype=F32) / nf)
    ar, ac = rpos[:, None] * inv, cpos[:, None] * inv
    cos = jnp.concatenate([jnp.cos(ar), jnp.cos(ar), jnp.cos(ac), jnp.cos(ac)], axis=1)
    sin = jnp.concatenate([-jnp.sin(ar), jnp.sin(ar), -jnp.sin(ac), jnp.sin(ac)], axis=1)
    lat = jnp.tile(jnp.concatenate([cos, sin], axis=1), (nb, 1))
    ctx = jnp.concatenate([jnp.ones((nb * n_ctx, ROPE), F32), jnp.zeros((nb * n_ctx, ROPE), F32)], axis=1)
    return jnp.concatenate([lat, ctx], axis=0)


def kernel(x, c, ctx, c_ctx, w_ada, b_ada, norm_w, ffn_w_in, ffn_w_out, w_in, gdn_conv, gdn_a_log, gdn_dt_bias, gdn_norm, mla_q_norm, mla_kv_norm, mla_w_q_b, mla_w_kv_b, hg_lb_logits, hg_norm, w_branch, w_out):
    nb, n_seq, d = x.shape
    n_ctx = ctx.shape[1]
    depth = w_ada.shape[0]
    lat_rows, rows = nb * n_seq, nb * (n_seq + n_ctx)
    assert nb + 1 <= 8 and n_seq % (FFN_SUB * TM) == 0 and n_ctx % TM_S == 0 and (nb * n_ctx) % TM == 0
    assert n_seq % n_ctx == 0 and lat_rows % n_ctx == 0

    cc = jnp.concatenate([c, c_ctx[None], jnp.zeros((8 - nb - 1, d), F32)], axis=0)
    mods = _ada(cc, w_ada, b_ada)[:, :nb + 1].reshape(depth, nb + 1, 9, 1, d).transpose(0, 2, 1, 3, 4)
    cs = _rope_table(nb, n_seq, n_ctx)
    h = (x.reshape(lat_rows, d), ctx.reshape(nb * n_ctx, d))

    for l in range(depth):
        last = l == depth - 1
        md, nw = mods[l], norm_w[l][:, None, :]
        wi = [ffn_w_in[l, j].astype(BF16) for j in range(2)]
        wo = [ffn_w_out[l, j].astype(BF16) for j in range(2)]

        h, u = _ffn(h, md[0:3], nw[0:2], wi[0], wo[0], rows, n_seq, nb, mixer_mods=(nw[2], md[3], md[4]))
        w_rec, w_low, w_gates = _pack_w_in(w_in[l])
        p = _proj(u, w_rec)

        prep = _mixer_prep(p, gdn_conv[l], gdn_a_log[l], gdn_dt_bias[l], hg_lb_logits, l, nb, n_seq, n_ctx)
        g_ops, (hoi, hqg, hkd, hvb, hdec) = prep[:6], prep[6:]
        g_of, g_ob, h_of, h_ob = _dir_scan(_mixer_scan_kernel,
                                           [(g_ops[:5], g_ops[5]), ((hoi, hqg, hkd, hvb), hdec)],
                                           nb, n_seq, n_ctx, "mixer_scan")

        q, k, v = _mla_prep(u, cs, mla_q_norm[l][None], mla_kv_norm[l][None], w_low,
                            _pack_wq(mla_w_q_b[l]), _pack_wkv(mla_w_kv_b[l]), nb, n_seq, n_ctx)
        yb = _attention(q, k, v, None, nb, n_seq, n_ctx, rows, ctx_only=False)
        if not last:
            yb = _attention(q, k, v, yb, nb, n_seq, n_ctx, rows, ctx_only=True)

        out_rows = lat_rows if last else rows
        h = _merge(h, g_of, g_ob, yb, h_of, h_ob, p, u, gdn_norm[l][None], hg_norm[l][None], md[5], nw[3],
                   w_gates, w_branch[l].astype(BF16), w_out[l].astype(BF16), out_rows, n_seq, nb)
        h = _ffn(h, md[6:9], nw[4:6], wi[1], wo[1], out_rows, n_seq, nb)
    return h.reshape(nb, n_seq, d)
```

```python
import functools

import jax
import jax.numpy as jnp
from jax import lax
from jax.experimental import pallas as pl
from jax.experimental.pallas import tpu as pltpu

F32 = jnp.float32
BF16 = jnp.bfloat16
EPS = 1e-6
LB_FLOOR = 1e-30
GRID_W = 64
ROPE_BASE = 10000.0

D_FF = 2816
HEADS = 4
HD = 128
ROPE = 64
NOPE = 128
Q_RANK = 384
KV_RANK = 256
CHUNK = 64
CPS = 4
CPP = 4
HG_CPP = 4
SUB = 16
BW = HEADS * HD

TM = 512
TM_S = 256
FF_CK = 256
MERGE_SUB = 256
FFN_SUB = 2
VMEM_LIMIT = 48 * 1024 * 1024
FFN_VMEM_LIMIT = 56 * 1024 * 1024
ATTN_LOG2_SCALE = (NOPE + ROPE) ** -0.5 * 1.4426950408889634
HPS = 2
KSUB = 256
HG_MAX_DECAY = 100.0
MAX_JUMP = 64.0

P_GQKV = 0
P_GGATE = 1536
P_HF = 2048
P_HQ = 3072
P_HI = 3584
P_HGATE = 4096
P_GA = 4608
P_GB = 4736
P_COLS = 4864
PROJ_TN = 2432
M_QA, M_KVA, M_KR, M_COLS = 0, 512, 768, 896

NT = (((1,), (1,)), ((), ()))
TN = (((0,), (0,)), ((), ()))


def _dot(a, b):
    return jnp.dot(a, b, preferred_element_type=F32)


def _dot_nt(a, b):
    return lax.dot_general(a, b, NT, preferred_element_type=F32)


def _dot_tn(a, b):
    return lax.dot_general(a, b, TN, preferred_element_type=F32)


def _sigmoid(x):
    return 1.0 / (1.0 + jnp.exp(-x))


def _silu(x):
    return x * _sigmoid(x)


def _softplus(x):
    return jnp.maximum(x, 0.0) + jnp.log(1.0 + jnp.exp(-jnp.abs(x)))


def _rms(x, w):
    return x * lax.rsqrt(jnp.mean(x * x, axis=-1, keepdims=True) + EPS) * w


def _resident(shape):
    zeros = (0,) * len(shape)
    return pl.BlockSpec(shape, lambda *_: zeros, pipeline_mode=pl.Buffered(1))


def _params(*sem):
    return pltpu.CompilerParams(dimension_semantics=sem, vmem_limit_bytes=VMEM_LIMIT)


def _ada_kernel(c_ref, w_ref, b_ref, o_ref):
    s = _silu(c_ref[...])
    o_ref[...] = _dot(s.astype(BF16), w_ref[...].astype(BF16)) + b_ref[...]


def _ada(cc, w_ada, b_ada):
    depth, d, nm = w_ada.shape
    tn = 1024
    return pl.pallas_call(
        _ada_kernel,
        grid=(depth, nm // tn),
        in_specs=[pl.BlockSpec((8, d), lambda l, j: (0, 0)),
                  pl.BlockSpec((None, d, tn), lambda l, j: (l, 0, j)),
                  pl.BlockSpec((None, 1, tn), lambda l, j: (l, 0, j))],
        out_specs=pl.BlockSpec((None, 8, tn), lambda l, j: (l, 0, j)),
        out_shape=jax.ShapeDtypeStruct((depth, 8, nm), F32),
        compiler_params=_params("parallel", "parallel"),
        name="ada_mod",
    )(cc, w_ada, b_ada.reshape(depth, 1, nm))


def _ffn_kernel(*refs, emit_u, n_sub, n_lat_steps):
    x_ref, c_ref = (refs[0], None) if n_lat_steps is None else refs[:2]
    refs = refs[1 if n_lat_steps is None else 2:]
    sh_ref, sc_ref, gt_ref, prew_ref, postw_ref, wi_ref, wo_ref = refs[:7]
    if emit_u:
        nw_ref, sh2_ref, sc2_ref, o_ref, u_ref = refs[7:]
    else:
        (o_ref,) = refs[7:]
    n_ck = D_FF // FF_CK
    half = n_ck // 2

    def load(rs):
        if c_ref is None:
            return x_ref[rs, :]
        return jnp.where(pl.program_id(0) >= n_lat_steps, c_ref[rs, :], x_ref[rs, :])

    def prologue(s):
        x = load(slice(s * TM, (s + 1) * TM))
        return (_rms(x, prew_ref[...]) * (1.0 + sc_ref[...]) + sh_ref[...]).astype(BF16)

    def chunks(hn, acc, lo, hi):
        for c in range(lo, hi):
            g = _dot(hn, wi_ref[:, c * FF_CK:(c + 1) * FF_CK])
            u = _dot(hn, wi_ref[:, D_FF + c * FF_CK:D_FF + (c + 1) * FF_CK])
            a = (_silu(g) * u).astype(BF16)
            acc = acc + _dot(a, wo_ref[c * FF_CK:(c + 1) * FF_CK, :])
        return acc

    def epilogue(s, acc):
        rs = slice(s * TM, (s + 1) * TM)
        out = load(rs) + 0.5 * gt_ref[...] * _rms(acc, postw_ref[...])
        o_ref[rs, :] = out
        if emit_u:
            u_ref[rs, :] = (_rms(out, nw_ref[...]) * (1.0 + sc2_ref[...]) + sh2_ref[...]).astype(BF16)

    zero = jnp.zeros((TM, x_ref.shape[1]), F32)
    hn = prologue(0)
    acc = chunks(hn, zero, 0, half)
    for s in range(n_sub):
        hn_next = prologue(s + 1) if s + 1 < n_sub else None
        acc = chunks(hn, acc, half, n_ck)
        if hn_next is not None:
            acc_next = chunks(hn_next, zero, 0, half)
        epilogue(s, acc)
        if hn_next is not None:
            hn, acc = hn_next, acc_next


def _ffn(h, mods, nw, wi, wo, n_rows, n_seq, nb, mixer_mods=None):
    split = isinstance(h, tuple)
    d = h[0].shape[1] if split else h.shape[1]
    emit_u = mixer_mods is not None
    vec = pl.BlockSpec((1, d), lambda i: (0, 0))
    out_shape = [jax.ShapeDtypeStruct((n_rows, d), F32)]
    if emit_u:
        out_shape.append(jax.ShapeDtypeStruct((n_rows, d), BF16))

    tm = FFN_SUB * TM
    row = lambda i: (i, 0)
    mspec = pl.BlockSpec((None, 1, d), lambda i: (jnp.minimum((i * tm) // n_seq, nb), 0, 0))
    n_lat_steps = None
    if split:
        n_lat_steps = h[0].shape[0] // tm
        assert h[0].shape[0] % tm == 0 and h[1].shape[0] <= tm
        x_specs = [pl.BlockSpec((tm, d), lambda i: (jnp.minimum(i, n_lat_steps - 1), 0)),
                   pl.BlockSpec((tm, d), lambda i: (0, 0))]
        x_args = list(h)
    else:
        x_specs, x_args = [pl.BlockSpec((tm, d), row)], [h]
    in_specs = x_specs + [mspec, mspec, mspec, vec, vec, _resident(wi.shape), _resident(wo.shape)]
    args = x_args + [mods[0], mods[1], mods[2], nw[0], nw[1], wi, wo]
    if emit_u:
        in_specs += [vec, mspec, mspec]
        args += list(mixer_mods)
    outs = pl.pallas_call(
        functools.partial(_ffn_kernel, emit_u=emit_u, n_sub=FFN_SUB, n_lat_steps=n_lat_steps),
        grid=(pl.cdiv(n_rows, tm),),
        in_specs=in_specs,
        out_specs=tuple(pl.BlockSpec((tm, d), row) for _ in out_shape),
        out_shape=tuple(out_shape),
        compiler_params=pltpu.CompilerParams(dimension_semantics=("parallel",),
                                             vmem_limit_bytes=FFN_VMEM_LIMIT),
        name="ffn_sublayer",
    )(*args)
    return outs if emit_u else outs[0]


def _proj_kernel(x_ref, w_ref, o_ref):
    o_ref[...] = _dot(x_ref[...], w_ref[pl.program_id(1)])


def _proj(u, w):
    t, d = u.shape
    nj, _, tn = w.shape
    tm = 2 * TM
    return pl.pallas_call(
        _proj_kernel,
        grid=(pl.cdiv(t, tm), nj),
        in_specs=[pl.BlockSpec((tm, d), lambda i, j: (i, 0)), _resident(w.shape)],
        out_specs=pl.BlockSpec((tm, tn), lambda i, j: (i, j)),
        out_shape=jax.ShapeDtypeStruct((t, nj * tn), F32),
        compiler_params=_params("parallel", "arbitrary"),
        name="in_proj",
    )(u, w)


def _mla_prep_kernel(u_ref, cs_ref, qn_ref, kvn_ref, win_ref, wq_ref, wkv_ref, q_ref, k_ref, v_ref):
    cs = cs_ref[...]
    low = _dot(u_ref[...], win_ref[...])
    qa = _rms(low[:, M_QA:M_QA + Q_RANK], qn_ref[...]).astype(BF16)
    kva = _rms(low[:, M_KVA:M_KVA + KV_RANK], kvn_ref[...]).astype(BF16)
    q2 = _dot(qa, wq_ref[...])
    kv2 = _dot(kva, wkv_ref[...])
    kr = low[:, M_KR:M_KR + 2 * ROPE] * cs
    kr = (kr + pltpu.roll(kr, ROPE, 1))[:, :ROPE].astype(BF16)
    q2 = q2 * ATTN_LOG2_SCALE
    for h in range(HEADS):
        q_ref[h, :, :NOPE] = q2[:, h * NOPE:(h + 1) * NOPE].astype(BF16)
        qr = q2[:, BW + h * 128:BW + (h + 1) * 128] * cs
        q_ref[h, :, NOPE:] = (qr + pltpu.roll(qr, ROPE, 1))[:, :ROPE].astype(BF16)
        k_ref[h, :, :NOPE] = kv2[:, h * NOPE:(h + 1) * NOPE].astype(BF16)
        k_ref[h, :, NOPE:] = kr
        v_ref[h] = kv2[:, BW + h * HD:BW + (h + 1) * HD].astype(BF16)


def _mla_prep(u, cs, qn, kvn, w_low, wq, wkv, nb, n_seq, n_ctx):
    t, d = u.shape
    tm = TM_S
    lat_tiles = nb * n_seq // tm
    per_lat = n_seq // tm
    per_ctx = n_ctx // tm

    def omap(i):
        j = i - lat_tiles
        b = jnp.where(i < lat_tiles, i // per_lat, j // per_ctx)
        blk = jnp.where(i < lat_tiles, i % per_lat, per_lat + j % per_ctx)
        return (b, 0, blk, 0)

    tk = n_seq + n_ctx
    return pl.pallas_call(
        _mla_prep_kernel,
        grid=(t // tm,),
        in_specs=[pl.BlockSpec((tm, d), lambda i: (i, 0)),
                  pl.BlockSpec((tm, 128), lambda i: (i, 0)),
                  pl.BlockSpec((1, Q_RANK), lambda i: (0, 0)),
                  pl.BlockSpec((1, KV_RANK), lambda i: (0, 0)),
                  _resident(w_low.shape), _resident(wq.shape), _resident(wkv.shape)],
        out_specs=(pl.BlockSpec((None, HEADS, tm, NOPE + ROPE), omap),
                   pl.BlockSpec((None, HEADS, tm, NOPE + ROPE), omap),
                   pl.BlockSpec((None, HEADS, tm, HD), omap)),
        out_shape=(jax.ShapeDtypeStruct((nb, HEADS, tk, NOPE + ROPE), BF16),
                   jax.ShapeDtypeStruct((nb, HEADS, tk, NOPE + ROPE), BF16),
                   jax.ShapeDtypeStruct((nb, HEADS, tk, HD), BF16)),
        compiler_params=_params("parallel"),
        name="mla_prep",
    )(u, cs, qn, kvn, w_low, wq, wkv)


def _attn_kernel(q_ref, k_ref, v_ref, *rest, first, kb, nkb):
    o_ref = rest[-1]
    nh, tq = q_ref.shape[0], q_ref.shape[1]
    qs = [q_ref[h] for h in range(nh)]

    def safe_update(h, sl, carry):
        m, l, acc = carry
        t = _dot_nt(qs[h], k_ref[h, sl, :])
        m_new = jnp.maximum(m, jnp.max(t, axis=-1, keepdims=True))
        alpha = jnp.exp2(m - m_new)
        pr = jnp.exp2(t - m_new)
        l = alpha * l + jnp.sum(pr, axis=-1, keepdims=True)
        return m_new, l, alpha * acc + _dot(pr.astype(BF16), v_ref[h, sl, :])

    def fast_update(j, carry):
        st = [list(c) for c in carry]
        bm = [None] * nh
        items = [(pl.ds(pl.multiple_of(j * kb + i * KSUB, KSUB), KSUB), h)
                 for i in range(kb // KSUB) for h in range(nh)]
        score = lambda it: _dot_nt(qs[it[1]], k_ref[it[1], it[0], :])
        t_next = score(items[0])
        for n, (sl, h) in enumerate(items):
            t = t_next
            if n + 1 < len(items):
                t_next = score(items[n + 1])
            pr = jnp.exp2(t - st[h][0])
            tm = jnp.max(t, axis=-1, keepdims=True)
            bm[h] = tm if bm[h] is None else jnp.maximum(bm[h], tm)
            st[h][1] = st[h][1] + jnp.sum(pr, axis=-1, keepdims=True)
            st[h][2] = st[h][2] + _dot(pr.astype(BF16), v_ref[h, sl, :])
        out = []
        for h in range(nh):
            m, l, acc, jump = st[h]
            m_new = jnp.maximum(m, bm[h])
            alpha = jnp.exp2(m - m_new)
            out.append((m_new, l * alpha, acc * alpha, jnp.maximum(jump, bm[h] - m)))
        return tuple(out)

    init = (jnp.full((tq, 1), -1e30, F32), jnp.zeros((tq, 1), F32), jnp.zeros((tq, HD), F32))
    fsl = slice(first[0], first[0] + first[1])
    start = [safe_update(h, fsl, init) for h in range(nh)]
    if nkb == 0:
        for h in range(nh):
            o_ref[:, h * HD:(h + 1) * HD] = start[h][2] / start[h][1]
        return
    done = lax.fori_loop(0, nkb, fast_update, tuple(s + (jnp.zeros((tq, 1), F32),) for s in start),
                         unroll=_pick_block(nkb, (2, 1)))
    for h in range(nh):
        o_ref[:, h * HD:(h + 1) * HD] = done[h][2] / done[h][1]

    for h in range(nh):
        @pl.when(jnp.max(done[h][3]) > MAX_JUMP)
        def _(h=h):
            step = lambda j, carry: safe_update(h, pl.ds(pl.multiple_of(j * kb, kb), kb), carry)
            _, l2, acc2 = lax.fori_loop(0, nkb, step, start[h])
            o_ref[:, h * HD:(h + 1) * HD] = acc2 / l2


def _pick_block(n, cands):
    for c in cands:
        if n % c == 0:
            return c
    raise ValueError(f"no block size for {n}")


def _attention(q, k, v, o_prev, nb, n_seq, n_ctx, rows, ctx_only):
    if ctx_only:
        tq, tk = n_ctx, n_ctx
        koff = n_seq // n_ctx
        grid = (nb, HEADS // HPS, 1)
        qmap = lambda b, h, i: (b, h, koff, 0)
        kmap = lambda b, h, i: (b, h, koff, 0)
        omap = lambda b, h, i: (nb * n_seq // n_ctx + b, h)
        first, kb, nkb = (0, n_ctx), n_ctx, 0
    else:
        tq, tk = _pick_block(n_seq, (512, 256)), n_seq + n_ctx
        grid = (nb, HEADS // HPS, n_seq // tq)
        qmap = lambda b, h, i: (b, h, i, 0)
        kmap = lambda b, h, i: (b, h, 0, 0)
        omap = lambda b, h, i: (b * (n_seq // tq) + i, h)
        kb = _pick_block(n_seq, (2048, 1024, 512))
        first, nkb = (n_seq, n_ctx), n_seq // kb
    in_specs = [pl.BlockSpec((None, HPS, tq, NOPE + ROPE), qmap),
                pl.BlockSpec((None, HPS, tk, NOPE + ROPE), kmap),
                pl.BlockSpec((None, HPS, tk, HD), kmap)]
    args = [q, k, v]
    aliases = {}
    if ctx_only:
        in_specs.append(pl.BlockSpec(memory_space=pl.ANY))
        args.append(o_prev)
        aliases = {3: 0}
    return pl.pallas_call(
        functools.partial(_attn_kernel, first=first, kb=kb, nkb=nkb),
        grid=grid,
        in_specs=in_specs,
        out_specs=pl.BlockSpec((tq, HPS * HD), omap),
        out_shape=jax.ShapeDtypeStruct((rows, BW), F32),
        input_output_aliases=aliases,
        compiler_params=_params("parallel", "parallel", "parallel"),
        name="mla_attn_ctx" if ctx_only else "mla_attn",
    )(*args)


def _chunk_pos(g, lat_chunks, ncl, ncc):
    is_lat = g < lat_chunks
    pos = jnp.where(is_lat, g % ncl, (g - lat_chunks) % ncc)
    last = jnp.where(is_lat, pos == ncl - 1, pos == ncc - 1)
    return pos == 0, last


def _scan_chunk(b, s, rev, lat_chunks, ncl, ncc):
    c = jnp.where(s < ncc, s, s - ncc)
    if rev:
        c = jnp.where(s < ncc, ncc - 1 - c, ncl - 1 - c)
    return jnp.where(s < ncc, lat_chunks + b * ncc + c, b * ncl + c)


def _tri_masks(n):
    ri = lax.broadcasted_iota(jnp.int32, (n, n), 0)
    ci = lax.broadcasted_iota(jnp.int32, (n, n), 1)
    return ri, ci


def _split(x):
    hi = x.astype(BF16)
    return hi, (x - hi.astype(F32)).astype(BF16)


def _dot3(a, b):
    (ah, al), (bh, bl) = a, b
    return _dot(ah, bh) + (_dot(ah, bl) + _dot(al, bh))


def _dot_tri(tri, x):
    hi, lo = _split(x)
    lo2 = (x - hi.astype(F32) - lo.astype(F32)).astype(BF16)
    return _dot(tri, hi) + (_dot(tri, lo) + _dot(tri, lo2))


def _unit_tri_inv(mats, ri, ci, eye, fill):
    rb, cb = ri >> 3, ci >> 3
    d8 = [jnp.where(rb == cb, a, 0.0) for a in mats]
    d8s = [_split(d) for d in d8]
    x2f = [_dot3(d, d) for d in d8s]
    x2 = [_split(x) for x in x2f]
    fill(2)
    x4 = [_split(_dot3(x, x)) for x in x2]
    ts = [eye - d + xf - _dot3(ds, x) for d, ds, xf, x in zip(d8, d8s, x2f, x2)]
    fill(2)
    ts = [t + _dot3(_split(t), x) for t, x in zip(ts, x4)]
    fill(2)
    for _ in range(3):
        same = rb == cb
        rb, cb = rb >> 1, cb >> 1
        off = (rb == cb) & jnp.logical_not(same)
        bs = [jnp.where(off, a, 0.0).astype(BF16) for a in mats]
        tb = [t.astype(BF16) for t in ts]
        ys = [_dot(t, b).astype(BF16) for t, b in zip(tb, bs)]
        fill(2)
        ts = [t - _dot(y, tl) for t, y, tl in zip(ts, ys, tb)]
        fill(2)
    return ts


def _gdn_pre_body(x_ref, prev_ref, next_ref, a_ref, b_ref, cw_ref, alog_ref, dtb_ref,
                  u_ref, w_ref, qg_ref, kd_ref, at_ref, dec_ref, *, lat_chunks, ncl, ncc, fill):
    first, last = _chunk_pos(pl.program_id(0), lat_chunks, ncl, ncc)
    nrow = CPP * CHUNK
    x = x_ref[...]
    xp = jnp.where(first, 0.0, prev_ref[7:8, :])
    xn = jnp.where(last, 0.0, next_ref[0:1, :])
    row = lax.broadcasted_iota(jnp.int32, (nrow, 1), 0)
    x_dn = jnp.where(row == 0, xp, pltpu.roll(x, 1, 0))
    x_up = jnp.where(row == nrow - 1, xn, pltpu.roll(x, nrow - 1, 0))
    cw = cw_ref[...]
    s = _silu(x_dn * cw[0:1] + x * cw[1:2] + x_up * cw[2:3])

    lane = lax.broadcasted_iota(jnp.int32, (1, 128), 1)
    g = -jnp.exp(alog_ref[...]) * _softplus(a_ref[...] + dtb_ref[...])
    beta = _sigmoid(b_ref[...])
    ri, ci = _tri_masks(CHUNK)
    eye_b = ri == ci
    eye = eye_b.astype(F32)
    low = (ri >= ci).astype(BF16)
    upp = (ri <= ci).astype(BF16)
    chunk_rows = [slice(c * CHUNK, (c + 1) * CHUNK) for c in range(CPP)]
    gc, g_last = [], []
    for c, rc in enumerate(chunk_rows):
        gcc = jnp.where(lane < HEADS, _dot_tri(low, g[rc]), _dot_tri(upp, g[rc]))
        gc.append(gcc)
        g_last.append(jnp.where(lane < HEADS, gcc[CHUNK - 1:CHUNK], gcc[0:1]))
        dec_ref[c] = jnp.exp(g_last[c])

    qn, kn, vs = [], [], []
    for h in range(HEADS):
        qh = s[:, h * HD:(h + 1) * HD]
        kh = s[:, BW + h * HD:BW + (h + 1) * HD]
        vs.append(s[:, 2 * BW + h * HD:2 * BW + (h + 1) * HD])
        qn.append(qh * lax.rsqrt(jnp.sum(qh * qh, axis=-1, keepdims=True) + EPS) * HD ** -0.5)
        kn.append(kh * lax.rsqrt(jnp.sum(kh * kh, axis=-1, keepdims=True) + EPS))
    qk = [[None] * HEADS for _ in range(CPP)]
    kk = [[None] * HEADS for _ in range(CPP)]
    for c, rc in enumerate(chunk_rows):
        for h in range(HEADS):
            knb = kn[h][rc].astype(BF16)
            qk[c][h] = _dot_nt(qn[h][rc].astype(BF16), knb)
            kk[c][h] = _dot_nt(knb, knb)

    probs = [(c, d, h) for c in range(CPP) for h in range(HEADS) for d in range(2)]
    mats, rhs = [], []
    for c, d, h in probs:
        rc = chunk_rows[c]
        sl = slice(h * HD, (h + 1) * HD)
        dh = d * HEADS + h
        gcc = gc[c][:, dh:dh + 1]
        bt = beta[rc, dh:dh + 1]
        grow = jnp.sum(jnp.where(eye_b, gcc, 0.0), axis=0, keepdims=True)
        incl = (ri >= ci) if d == 0 else (ri <= ci)
        strict = (ri > ci) if d == 0 else (ri < ci)
        dm = jnp.where(incl, jnp.exp(gcc - grow), 0.0)
        eg = jnp.exp(gcc)
        knc = kn[h][rc]
        mats.append(jnp.where(strict, bt * kk[c][h] * dm, 0.0))
        rhs.append(jnp.concatenate([vs[h][rc] * bt, knc * (bt * eg)], axis=1).astype(BF16))
        qg_ref[d, rc, sl] = (qn[h][rc] * eg).astype(BF16)
        kd_ref[d, rc, sl] = (knc * jnp.exp(g_last[c][:, dh:dh + 1] - gcc)).astype(BF16)
        at_ref[d, rc, h * HD:h * HD + CHUNK] = (qk[c][h] * dm).astype(BF16)
        at_ref[d, rc, h * HD + CHUNK:(h + 1) * HD] = jnp.zeros((CHUNK, HD - CHUNK), BF16)

    for (c, d, h), t, r in zip(probs, _unit_tri_inv(mats, ri, ci, eye, fill), rhs):
        sl = slice(h * HD, (h + 1) * HD)
        uw = _dot(t.astype(BF16), r)
        u_ref[d, chunk_rows[c], sl] = uw[:, :HD]
        w_ref[d, chunk_rows[c], sl] = uw[:, HD:].astype(BF16)


def _mixer_prep_kernel(*refs, layer, lat_chunks, ncl, ncc):
    g_refs, h_refs = refs[:8] + refs[12:18], refs[8:12] + refs[18:]
    pieces = _hg_pre_body(*h_refs, layer=layer)

    def fill(n):
        for _ in range(n):
            next(pieces, None)

    _gdn_pre_body(*g_refs, lat_chunks=lat_chunks, ncl=ncl, ncc=ncc, fill=fill)
    for _ in pieces:
        pass


def _mixer_prep(p, conv_w, a_log, dt_bias, lb_logits, layer, nb, n_seq, n_ctx):
    t = p.shape[0]
    assert CPP == HG_CPP
    nrow = CPP * CHUNK
    assert n_seq % nrow == 0 and n_ctx % nrow == 0
    lat_blocks, nbl, nbc = nb * n_seq // nrow, n_seq // nrow, n_ctx // nrow
    qkv_blk = P_GQKV // (3 * BW)
    pad = lambda v: jnp.pad(v.reshape(1, -1), ((0, 0), (0, 128 - v.size)))
    dir_out = lambda dt: jax.ShapeDtypeStruct((2, t, BW), dt)
    dir_spec = pl.BlockSpec((2, nrow, BW), lambda g: (0, g, 0))
    r8 = nrow // 8
    return pl.pallas_call(
        functools.partial(_mixer_prep_kernel, layer=layer, lat_chunks=lat_blocks, ncl=nbl, ncc=nbc),
        grid=(t // nrow,),
        in_specs=[pl.BlockSpec((nrow, 3 * BW), lambda g: (g, qkv_blk)),
                  pl.BlockSpec((8, 3 * BW), lambda g: (jnp.maximum(g * r8 - 1, 0), qkv_blk)),
                  pl.BlockSpec((8, 3 * BW), lambda g: (jnp.minimum(g * r8 + r8, t // 8 - 1), qkv_blk)),
                  pl.BlockSpec((nrow, 128), lambda g: (g, P_GA // 128)),
                  pl.BlockSpec((nrow, 128), lambda g: (g, P_GB // 128)),
                  pl.BlockSpec((3, 3 * BW), lambda g: (0, 0)),
                  pl.BlockSpec((1, 128), lambda g: (0, 0)),
                  pl.BlockSpec((1, 128), lambda g: (0, 0)),
                  pl.BlockSpec((nrow, BW), lambda g: (g, P_HQ // BW)),
                  pl.BlockSpec((nrow, 2 * BW), lambda g: (g, P_HF // (2 * BW))),
                  pl.BlockSpec((nrow, BW), lambda g: (g, P_HI // BW)),
                  pl.BlockSpec(lb_logits.shape, lambda g: (0, 0, 0))],
        out_specs=(dir_spec, dir_spec, dir_spec, dir_spec, dir_spec,
                   pl.BlockSpec((CPP, 1, 128), lambda g: (g, 0, 0)),
                   dir_spec, dir_spec, dir_spec,
                   pl.BlockSpec((nrow, BW), lambda g: (g, 0)),
                   pl.BlockSpec((2, HG_CPP, 1, BW), lambda g: (0, g, 0, 0))),
        out_shape=(dir_out(F32), dir_out(BF16), dir_out(BF16), dir_out(BF16), dir_out(BF16),
                   jax.ShapeDtypeStruct((t // CHUNK, 1, 128), F32),
                   dir_out(F32), dir_out(BF16), dir_out(BF16),
                   jax.ShapeDtypeStruct((t, BW), BF16),
                   jax.ShapeDtypeStruct((2, t // CHUNK, 1, BW), F32)),
        compiler_params=_params("parallel"),
        name="mixer_chunk_prep",
    )(p, p, p, p, p, conv_w, pad(a_log), pad(dt_bias), p, p, p, lb_logits)


def _scan_rows(d, c):
    cc = c if d == 0 else CPS - 1 - c
    return slice(cc * CHUNK, (cc + 1) * CHUNK), cc


def _mixer_scan_kernel(*refs):
    g_ins, h_ins = refs[:12], refs[12:22]
    g_of, g_ob, h_of, h_ob, g_state, h_state = refs[22:]

    @pl.when(pl.program_id(1) == 0)
    def _():
        g_state[...] = jnp.zeros(g_state.shape, F32)
        h_state[...] = jnp.zeros(h_state.shape, F32)

    filler = _hg_scan_body(h_ins, h_of, h_ob, h_state)
    _gdn_scan_body(g_ins, g_of, g_ob, g_state, filler)
    for _ in filler:
        pass


def _gdn_scan_body(ins, of_ref, ob_ref, s_ref, filler):
    def fill(n):
        for _ in range(n):
            next(filler, None)

    chains = [(d, h) for d in range(2) for h in range(HEADS)]
    outs = (of_ref, ob_ref)
    st = [s_ref[d, h] for d, h in chains]
    for c in range(CPS):
        sb = [s.astype(BF16) for s in st]
        vn = []
        for (d, h), s in zip(chains, sb):
            u_ref, w_ref = ins[6 * d], ins[6 * d + 1]
            rows, _ = _scan_rows(d, c)
            sl = slice(h * HD, (h + 1) * HD)
            vn.append((u_ref[rows, sl] - _dot(w_ref[rows, sl], s)).astype(BF16))
        fill(3)
        for i, (d, h) in enumerate(chains):
            qg_ref, kd_ref, at_ref, dec_ref = ins[6 * d + 2:6 * d + 6]
            rows, cc = _scan_rows(d, c)
            sl = slice(h * HD, (h + 1) * HD)
            outs[d][rows, sl] = _dot(qg_ref[rows, sl], sb[i]) + _dot(at_ref[rows, h * HD:h * HD + CHUNK], vn[i])
            dec = dec_ref[cc, 0:1, d * HEADS + h:d * HEADS + h + 1]
            st[i] = st[i] * dec + _dot_tn(kd_ref[rows, sl], vn[i])
        fill(3)
    for (d, h), s in zip(chains, st):
        s_ref[d, h] = s


def _dir_scan(kernel, groups, nb, n_seq, n_ctx, name):
    t = groups[0][0][0].shape[1]
    step = CPS * CHUNK
    assert n_seq % step == 0 and n_ctx % step == 0
    lat_blocks, ncl, ncc = nb * n_seq // step, n_seq // step, n_ctx // step
    maps = [functools.partial(_scan_chunk, rev=bool(d), lat_chunks=lat_blocks, ncl=ncl, ncc=ncc) for d in range(2)]
    in_specs, args = [], []
    for arrays, dec in groups:
        for d, cmap in enumerate(maps):
            for arr in arrays:
                if arr.ndim == 3:
                    in_specs.append(pl.BlockSpec((None, step, BW), lambda b, s, d=d, cmap=cmap: (d, cmap(b, s), 0)))
                else:
                    in_specs.append(pl.BlockSpec((step, BW), lambda b, s, cmap=cmap: (cmap(b, s), 0)))
                args.append(arr)
            if dec.ndim == 3:
                in_specs.append(pl.BlockSpec((CPS, 1, dec.shape[-1]), lambda b, s, cmap=cmap: (cmap(b, s), 0, 0)))
            else:
                in_specs.append(pl.BlockSpec((None, CPS, 1, dec.shape[-1]),
                                             lambda b, s, d=d, cmap=cmap: (d, cmap(b, s), 0, 0)))
            args.append(dec)
    out_specs = tuple(pl.BlockSpec((step, BW), lambda b, s, cmap=cmap: (cmap(b, s), 0))
                      for _ in groups for cmap in maps)
    return pl.pallas_call(
        kernel,
        grid=(nb, ncc + ncl),
        in_specs=in_specs,
        out_specs=out_specs,
        out_shape=tuple(jax.ShapeDtypeStruct((t, BW), F32) for _ in out_specs),
        scratch_shapes=[pltpu.VMEM((2, HEADS, HD, HD), F32) for _ in groups],
        compiler_params=_params("parallel", "arbitrary"),
        name=name,
    )(*args)


def _hg_block_decay(gc, rev):
    out = None
    for b in range(CHUNK // SUB):
        lo, hi = gc[b * SUB:b * SUB + 1], gc[(b + 1) * SUB - 1:(b + 1) * SUB]
        dcy = (hi - lo) if rev else (lo - hi)
        out = dcy if out is None else jnp.maximum(out, dcy)
    return out


def _hg_scores(q, k, gc, rev, exact):
    nblk = CHUNK // SUB
    lane = lax.broadcasted_iota(jnp.int32, (1, CHUNK), 1)
    row8 = lax.broadcasted_iota(jnp.int32, (8, 1), 0)
    blk = lambda x, b: x[b * SUB:(b + 1) * SUB]
    ref_row = lambda b: b * SUB + (0 if rev else SUB - 1)
    refs = [gc[ref_row(b):ref_row(b) + 1] for b in range(nblk)]
    kt = jnp.concatenate([blk(k, b) * jnp.exp2(refs[b] - blk(gc, b)) for b in range(nblk)], axis=0)
    srcs = list(range(nblk - 1, 0, -1)) if rev else list(range(nblk - 1))
    rows_of = (lambda j: slice(0, j * SUB)) if rev else (lambda j: slice((j + 1) * SUB, CHUNK))
    qt = jnp.concatenate([q[rows_of(j)] * jnp.exp2(gc[rows_of(j)] - refs[j]) for j in srcs], axis=0)
    seg_off, off = {}, 0
    for j in srcs:
        seg_off[j] = off
        off += (j if rev else nblk - 1 - j) * SUB
    ktb, qtb = kt.astype(BF16), qt.astype(BF16)
    cross = [_dot_nt(qtb[:, h * HD:(h + 1) * HD], ktb[:, h * HD:(h + 1) * HD]) for h in range(HEADS)]

    def assemble(own):
        out = []
        for h in range(HEADS):
            groups = []
            for g8 in range(CHUNK // 8):
                r0 = 8 * g8
                bi = r0 // SUB
                val = own[h][g8]
                for j in srcs:
                    if (j < bi and not rev) or (j > bi and rev):
                        base = seg_off[j] + (r0 if rev else r0 - (j + 1) * SUB)
                        val = jnp.where((lane >= j * SUB) & (lane < (j + 1) * SUB), cross[h][base:base + 8], val)
                groups.append(val)
            out.append(jnp.concatenate(groups, axis=0))
        return tuple(out)

    def own_exact():
        acc = [[jnp.zeros((8, CHUNK), F32) for _ in range(CHUNK // 8)] for _ in range(HEADS)]
        for j in range(CHUNK):
            b, jj = j // SUB, j % SUB
            gj, kj = gc[j:j + 1], k[j:j + 1]
            for rg in range(SUB // 8):
                lo, hi = 8 * rg, 8 * rg + 7
                if (hi < jj and not rev) or (lo > jj and rev):
                    continue
                r0 = b * SUB + lo
                w = jnp.exp2(gc[r0:r0 + 8] - gj)
                if not ((lo >= jj and not rev) or (hi <= jj and rev)):
                    w = jnp.where((row8 + lo >= jj) if not rev else (row8 + lo <= jj), w, 0.0)
                tt = q[r0:r0 + 8] * w * kj
                g8 = r0 // 8
                for h in range(HEADS):
                    col = jnp.sum(tt[:, h * HD:(h + 1) * HD], axis=-1, keepdims=True)
                    acc[h][g8] = jnp.where(lane == j, col, acc[h][g8])
        return assemble(acc)

    first_row = lambda b: b * SUB + (SUB - 1 if rev else 0)
    starts = [gc[first_row(b):first_row(b) + 1] for b in range(nblk)]

    def own_matmul():
        rfull = jnp.concatenate([jnp.broadcast_to(starts[b], (SUB, q.shape[1])) for b in range(nblk)], axis=0)
        qd = (q * jnp.exp2(gc - rfull)).astype(BF16)
        kd = (k * jnp.exp2(rfull - gc)).astype(BF16)
        ri, ci = _tri_masks(CHUNK)
        sh = SUB.bit_length() - 1
        keep = ((ri >> sh) == (ci >> sh)) & ((ri <= ci) if rev else (ri >= ci))
        own = []
        for h in range(HEADS):
            sd = jnp.where(keep, _dot_nt(qd[:, h * HD:(h + 1) * HD], kd[:, h * HD:(h + 1) * HD]), 0.0)
            own.append([sd[8 * g8:8 * g8 + 8] for g8 in range(CHUNK // 8)])
        return assemble(own)

    return own_exact() if exact else own_matmul()


def _hg_pre_body(q_ref, f_ref, i_ref, lbl_ref, oi_ref, qg_ref, kd_ref, vb_ref, dec_ref, *, layer):
    lbl = lbl_ref[...]
    e = jnp.exp(lbl - jnp.max(lbl, axis=0, keepdims=True))
    sm = e / jnp.sum(e, axis=0, keepdims=True)
    lb_all = sm[0]
    for l in range(1, layer + 1):
        lb_all = lb_all + sm[l]
    lb_all = lb_all - sm[0]
    q_all = q_ref[...] * HD ** -0.5
    v_all = i_ref[...].astype(BF16)
    vb_ref[...] = v_all
    ri, ci = _tri_masks(CHUNK)
    work, decay = [], None
    for d in range(2):
        lb = lb_all[d:d + 1]
        f = f_ref[:, d * BW:(d + 1) * BW]
        e = jnp.exp(-jnp.abs(f))
        r = 1.0 / (1.0 + e)
        er = e * r
        pos = f >= 0.0
        sig, sig_neg = jnp.where(pos, r, er), jnp.where(pos, er, r)
        log_f = jnp.log2(jnp.maximum(lb, LB_FLOOR) + (1.0 - lb) * sig)
        k_all = (1.0 - lb) * sig_neg
        tri = ((ri >= ci) if d == 0 else (ri <= ci)).astype(BF16)
        yield
        for c in range(HG_CPP):
            rc = slice(c * CHUNK, (c + 1) * CHUNK)
            q, k, v = q_all[rc], k_all[rc], v_all[rc]
            gc = _dot_tri(tri, log_f[rc])
            g_last = gc[CHUNK - 1:CHUNK] if d == 0 else gc[0:1]
            dec_ref[d, c] = jnp.exp2(g_last)
            qg_ref[d, rc] = (q * jnp.exp2(gc)).astype(BF16)
            kd_ref[d, rc] = (k * jnp.exp2(g_last - gc)).astype(BF16)
            work.append((d, rc, q, k, v, gc))
            dcy = _hg_block_decay(gc, rev=bool(d))
            decay = dcy if decay is None else jnp.maximum(decay, dcy)
            yield

    def intra(item, exact):
        d, rc, q, k, v, gc = item
        scores = _hg_scores(q, k, gc, rev=bool(d), exact=exact)
        for h in range(HEADS):
            sl = slice(h * HD, (h + 1) * HD)
            oi_ref[d, rc, sl] = _dot(scores[h].astype(BF16), v[:, sl])

    for item in work:
        intra(item, False)
        yield

    @pl.when(jnp.max(decay) > HG_MAX_DECAY)
    def _():
        for item in work:
            intra(item, True)


def _hg_scan_body(ins, of_ref, ob_ref, s_ref):
    chains = [(d, h) for d in range(2) for h in range(HEADS)]
    outs = (of_ref, ob_ref)
    incs = []
    for d, h in chains:
        kd_ref, v_ref = ins[5 * d + 2], ins[5 * d + 3]
        sl = slice(h * HD, (h + 1) * HD)
        incs.append([_dot_tn(v_ref[_scan_rows(d, c)[0], sl], kd_ref[_scan_rows(d, c)[0], sl])
                     for c in range(CPS)])
        yield
    states = []
    for i, (d, h) in enumerate(chains):
        dec_ref = ins[5 * d + 4]
        sl = slice(h * HD, (h + 1) * HD)
        st = s_ref[d, h]
        seq = []
        for c in range(CPS):
            seq.append(st.astype(BF16))
            st = st * dec_ref[_scan_rows(d, c)[1], :, sl] + incs[i][c]
        s_ref[d, h] = st
        states.append(seq)
        yield
    for i, (d, h) in enumerate(chains):
        oi_ref, qg_ref = ins[5 * d], ins[5 * d + 1]
        sl = slice(h * HD, (h + 1) * HD)
        for c in range(CPS):
            rows, _ = _scan_rows(d, c)
            outs[d][rows, sl] = oi_ref[rows, sl] + _dot_nt(qg_ref[rows, sl], states[i][c])
        yield


def _merge_kernel(h_ref, gof_ref, gob_ref, gg_ref, yb_ref, hof_ref, hob_ref, hgg_ref, u_ref,
                  gnorm_ref, hnorm_ref, gt_ref, nw_ref, wg_ref, wb_ref, wo_ref, o_ref):
    d = h_ref.shape[1]
    n_sub = h_ref.shape[0] // MERGE_SUB

    def readout(rs, of_ref, ob_ref, gate_ref, norm_ref):
        o = of_ref[rs, :] + ob_ref[rs, :]
        gate = gate_ref[rs, :]
        parts = []
        for h in range(HEADS):
            sl = slice(h * HD, (h + 1) * HD)
            parts.append(_rms(o[:, sl], norm_ref[...]) * _silu(gate[:, sl]))
        return jnp.concatenate(parts, axis=1).astype(BF16)

    def branches(rs):
        return (readout(rs, gof_ref, gob_ref, gg_ref, gnorm_ref),
                yb_ref[rs, :].astype(BF16),
                readout(rs, hof_ref, hob_ref, hgg_ref, hnorm_ref))

    def mix(rs, ys):
        u = u_ref[rs, :]
        m = None
        for j in range(3):
            gate_logits = _dot(u, wg_ref[:, j * d:(j + 1) * d])
            term = _sigmoid(gate_logits) * _dot(ys[j], wb_ref[j])
            m = term if m is None else m + term
        return m.astype(BF16)

    def finish(rs, m):
        y = _dot(m, wo_ref[...])
        o_ref[rs, :] = h_ref[rs, :] + gt_ref[...] * _rms(y, nw_ref[...])

    rows = [slice(s * MERGE_SUB, (s + 1) * MERGE_SUB) for s in range(n_sub)]
    ys = branches(rows[0])
    for s in range(n_sub):
        m = mix(rows[s], ys)
        if s + 1 < n_sub:
            ys = branches(rows[s + 1])
        finish(rows[s], m)


def _merge(h, g_of, g_ob, yb, h_of, h_ob, p, u, gnorm, hnorm, gate_mod, nw, wg, wb, wo, n_rows, n_seq, nb):
    d = h.shape[1]
    tm = TM
    row = lambda i: (i, 0)
    bw_spec = pl.BlockSpec((tm, BW), row)
    pcol = lambda off, width: pl.BlockSpec((tm, width), lambda i: (i, off // width))
    vec = lambda n: pl.BlockSpec((1, n), lambda i: (0, 0))
    return pl.pallas_call(
        _merge_kernel,
        grid=(n_rows // tm,),
        in_specs=[pl.BlockSpec((tm, d), row), bw_spec, bw_spec, pcol(P_GGATE, BW), bw_spec,
                  bw_spec, bw_spec, pcol(P_HGATE, BW), pl.BlockSpec((tm, d), row),
                  vec(HD), vec(HD),
                  pl.BlockSpec((None, 1, d), lambda i: (jnp.minimum((i * tm) // n_seq, nb), 0, 0)),
                  vec(d), _resident(wg.shape), _resident(wb.shape), _resident(wo.shape)],
        out_specs=pl.BlockSpec((tm, d), row),
        out_shape=jax.ShapeDtypeStruct((n_rows, d), F32),
        compiler_params=_params("parallel"),
        name="branch_merge",
    )(h, g_of, g_ob, p, yb, h_of, h_ob, p, u, gnorm, hnorm, gate_mod, nw, wg, wb, wo)


def _pack_w_in(w):
    d = w.shape[0]
    w = w.astype(BF16)
    sizes = (BW, BW, BW, BW, 2 * HEADS, 2 * HEADS, Q_RANK, KV_RANK, ROPE, BW, 2 * BW, BW, BW, 3 * d)
    offs = [0]
    for s in sizes:
        offs.append(offs[-1] + s)
    part = lambda i: w[:, offs[i]:offs[i + 1]]
    gq, gk, gv, ggate, ga, gb, qa, kva, kr, hq, hf, hi, hgate, gates = (part(i) for i in range(14))
    z = lambda n: jnp.zeros((d, n), w.dtype)
    swap = jnp.concatenate([kr[:, 16:32], kr[:, 0:16], kr[:, 48:64], kr[:, 32:48]], axis=1)
    rec = jnp.concatenate([gq, gk, gv, ggate, hf, hq, hi, hgate, ga, z(120), gb, z(120)], axis=1).astype(BF16)
    mla = jnp.concatenate([qa, z(128), kva, kr, swap], axis=1).astype(BF16)
    assert rec.shape[1] == P_COLS and mla.shape[1] == M_COLS
    return rec.reshape(d, P_COLS // PROJ_TN, PROJ_TN).transpose(1, 0, 2), mla, gates.astype(BF16)


def _pack_wq(w):
    w = w.reshape(Q_RANK, HEADS, NOPE + ROPE)
    nope = w[:, :, :NOPE].reshape(Q_RANK, HEADS * NOPE)
    r = w[:, :, NOPE:]
    sw = jnp.concatenate([r[..., 16:32], r[..., 0:16], r[..., 48:64], r[..., 32:48]], axis=-1)
    rope = jnp.concatenate([r, sw], axis=-1).reshape(Q_RANK, HEADS * 2 * ROPE)
    return jnp.concatenate([nope, rope], axis=1).astype(BF16)


def _pack_wkv(w):
    w = w.reshape(KV_RANK, HEADS, NOPE + HD)
    return jnp.concatenate([w[:, :, :NOPE].reshape(KV_RANK, -1), w[:, :, NOPE:].reshape(KV_RANK, -1)],
                           axis=1).astype(BF16)


def _rope_table(nb, n_seq, n_ctx):
    nf = ROPE // 4
    rows = n_seq // GRID_W
    rpos = jnp.repeat(jnp.arange(rows, dtype=F32), GRID_W)
    cpos = jnp.tile(jnp.arange(GRID_W, dtype=F32), rows)
    inv = ROPE_BASE ** (-jnp.arange(nf, dtype=F32) / nf)
    ar, ac = rpos[:, None] * inv, cpos[:, None] * inv
    cos = jnp.concatenate([jnp.cos(ar), jnp.cos(ar), jnp.cos(ac), jnp.cos(ac)], axis=1)
    sin = jnp.concatenate([-jnp.sin(ar), jnp.sin(ar), -jnp.sin(ac), jnp.sin(ac)], axis=1)
    lat = jnp.tile(jnp.concatenate([cos, sin], axis=1), (nb, 1))
    ctx = jnp.concatenate([jnp.ones((nb * n_ctx, ROPE), F32), jnp.zeros((nb * n_ctx, ROPE), F32)], axis=1)
    return jnp.concatenate([lat, ctx], axis=0)


def kernel(x, c, ctx, c_ctx, w_ada, b_ada, norm_w, ffn_w_in, ffn_w_out, w_in, gdn_conv, gdn_a_log, gdn_dt_bias, gdn_norm, mla_q_norm, mla_kv_norm, mla_w_q_b, mla_w_kv_b, hg_lb_logits, hg_norm, w_branch, w_out):
    nb, n_seq, d = x.shape
    n_ctx = ctx.shape[1]
    depth = w_ada.shape[0]
    lat_rows, rows = nb * n_seq, nb * (n_seq + n_ctx)
    assert nb + 1 <= 8 and n_seq % (FFN_SUB * TM) == 0 and n_ctx % TM_S == 0 and (nb * n_ctx) % TM == 0
    assert n_seq % n_ctx == 0 and lat_rows % n_ctx == 0

    cc = jnp.concatenate([c, c_ctx[None], jnp.zeros((8 - nb - 1, d), F32)], axis=0)
    mods = _ada(cc, w_ada, b_ada)[:, :nb + 1].reshape(depth, nb + 1, 9, 1, d).transpose(0, 2, 1, 3, 4)
    cs = _rope_table(nb, n_seq, n_ctx)
    h = (x.reshape(lat_rows, d), ctx.reshape(nb * n_ctx, d))

    for l in range(depth):
        last = l == depth - 1
        md, nw = mods[l], norm_w[l][:, None, :]
        wi = [ffn_w_in[l, j].astype(BF16) for j in range(2)]
        wo = [ffn_w_out[l, j].astype(BF16) for j in range(2)]

        h, u = _ffn(h, md[0:3], nw[0:2], wi[0], wo[0], rows, n_seq, nb, mixer_mods=(nw[2], md[3], md[4]))
        w_rec, w_low, w_gates = _pack_w_in(w_in[l])
        p = _proj(u, w_rec)

        prep = _mixer_prep(p, gdn_conv[l], gdn_a_log[l], gdn_dt_bias[l], hg_lb_logits, l, nb, n_seq, n_ctx)
        g_ops, (hoi, hqg, hkd, hvb, hdec) = prep[:6], prep[6:]
        g_of, g_ob, h_of, h_ob = _dir_scan(_mixer_scan_kernel,
                                           [(g_ops[:5], g_ops[5]), ((hoi, hqg, hkd, hvb), hdec)],
                                           nb, n_seq, n_ctx, "mixer_scan")

        q, k, v = _mla_prep(u, cs, mla_q_norm[l][None], mla_kv_norm[l][None], w_low,
                            _pack_wq(mla_w_q_b[l]), _pack_wkv(mla_w_kv_b[l]), nb, n_seq, n_ctx)
        yb = _attention(q, k, v, None, nb, n_seq, n_ctx, rows, ctx_only=False)
        if not last:
            yb = _attention(q, k, v, yb, nb, n_seq, n_ctx, rows, ctx_only=True)

        out_rows = lat_rows if last else rows
        h = _merge(h, g_of, g_ob, yb, h_of, h_ob, p, u, gdn_norm[l][None], hg_norm[l][None], md[5], nw[3],
                   w_gates, w_branch[l].astype(BF16), w_out[l].astype(BF16), out_rows, n_seq, nb)
        h = _ffn(h, md[6:9], nw[4:6], wi[1], wo[1], out_rows, n_seq, nb)
    return h.reshape(nb, n_seq, d)
```
